```python
import math
import jax, jax.numpy as jnp
from jax import lax
import numpy as np

D_MODEL = 1024
BATCH = 8
SEQ = 2048
DEPTH = 2
DEC_BATCH = 32
DEC_SEQ = 1
PAST_LEN = 16384
PAGE_SIZE = 128

EPS = 1e-6
A_GROUPS = ((128, 1), (512, 4), (2048, 16))
N_GROUPS = 3
A_HEADS = 8
A_DH = 64
A_WIDTH = A_HEADS * A_DH
N_BUCKETS = 32
MAX_DIST = 2048
B_HEADS = 4
B_DK = 64
B_DV = 128
B_RANK = 16
B_TAU = 16.0
B_CHUNK = 64
C_HEADS = 4
C_DK = 128
C_DV = 128
CONV_W = 4
C_CONV_CH = 2 * C_HEADS * C_DK + C_HEADS * C_DV
C_CHUNK = 64
BRANCH_W = 512
D_FF = 2816
N_EXPERTS = 8
TOP_K = 2
D_FF_EXPERT = 1792
N_DENSE = (DEPTH + 1) // 2
N_MOE = DEPTH // 2
IN_SIZES = (N_GROUPS * A_WIDTH, N_GROUPS * A_WIDTH, N_GROUPS * A_WIDTH,
            B_HEADS * B_DK, B_HEADS * B_DK, B_HEADS * B_DV, B_HEADS * B_DV, B_RANK,
            C_CONV_CH, C_HEADS * C_DV, C_HEADS, C_HEADS, 3 * D_MODEL)
N_IN = sum(IN_SIZES)

kernel_name = 'hybrid_dilated_gla_gdn_decode_step'


def split_points():
    pts, acc = [], 0
    for s in IN_SIZES[:-1]:
        acc += s
        pts.append(acc)
    return pts


def rms_norm(x, w):
    xf = x.astype(jnp.float32)
    y = xf * lax.rsqrt(jnp.mean(xf * xf, axis=-1, keepdims=True) + EPS)
    return (y * w.astype(jnp.float32)).astype(x.dtype)


def l2_norm(x):
    xf = x.astype(jnp.float32)
    return (xf * lax.rsqrt(jnp.sum(xf * xf, axis=-1, keepdims=True) + EPS)).astype(x.dtype)


def t5_bucket(dist):
    max_exact = N_BUCKETS // 2
    d = jnp.maximum(dist.astype(jnp.float32), 1.0)
    large = max_exact + (jnp.log(d / max_exact) / math.log(MAX_DIST / max_exact)
                         * (N_BUCKETS - max_exact)).astype(jnp.int32)
    large = jnp.minimum(large, N_BUCKETS - 1)
    return jnp.where(dist < max_exact, dist, large).astype(jnp.int32)


def dilated_prompt(q, k, v, bias_g, window, dil):
    B, S, H, Dh = q.shape
    span = window // dil
    L = S // dil
    nb = -(-L // span)
    pad_back = nb * span - L

    def by_residue(t):
        return t.reshape(B, L, dil, H, Dh).transpose(0, 2, 1, 3, 4)

    qb = jnp.pad(by_residue(q), ((0, 0), (0, 0), (0, pad_back), (0, 0), (0, 0))).reshape(B, dil, nb, span, H, Dh)

    def key_blocks(t):
        tp = jnp.pad(by_residue(t), ((0, 0), (0, 0), (span, pad_back), (0, 0), (0, 0)))
        tp = tp.reshape(B, dil, nb + 1, span, H, Dh)
        return jnp.concatenate([tp[:, :, :-1], tp[:, :, 1:]], axis=3)

    kb = key_blocks(k)
    vb = key_blocks(v)
    i = jnp.arange(span)[:, None]
    c = jnp.arange(2 * span)[None, :]
    dist = i + span - c
    mk = jnp.arange(nb)[:, None, None] * span + c[None] - span
    valid = (dist >= 0)[None] & (dist <= span)[None] & (mk >= 0)
    bias = bias_g[t5_bucket(jnp.maximum(dist, 0) * dil)].transpose(2, 0, 1).astype(jnp.float32)
    logits = jnp.einsum('brnihd,brnjhd->brnhij', qb, kb, preferred_element_type=jnp.float32) + bias
    logits = jnp.where(valid[None, None, :, None], logits, -jnp.inf)
    m = jnp.max(logits, axis=-1, keepdims=True)
    p = jnp.exp(logits - m)
    s = jnp.sum(p, axis=-1)
    s_t = s.transpose(0, 1, 2, 4, 3)
    o = jnp.einsum('brnhij,brnjhd->brnihd', p, vb.astype(jnp.float32)) / s_t[..., None]
    lse = (m[..., 0] + jnp.log(s)).transpose(0, 1, 2, 4, 3)

    def back(t):
        t = t.reshape((B, dil, nb * span) + t.shape[4:])[:, :, :L]
        return jnp.moveaxis(t, 1, 2).reshape((B, S) + t.shape[3:])

    return back(o), back(lse)


def dilated_sample(q, k, v, buf, bias_g, window, dil):
    Bd, T, H, Dh = q.shape
    Wb = buf.shape[1]
    span = window // dil
    k_all = jnp.concatenate([buf[:, :, 0].astype(k.dtype), k], axis=1)
    v_all = jnp.concatenate([buf[:, :, 1].astype(v.dtype), v], axis=1)
    j = jnp.arange(span + 1)
    idx = jnp.arange(T)[:, None] + Wb - dil * j[None, :]
    valid = idx >= 0
    idx = jnp.maximum(idx, 0)
    kg = k_all[:, idx]
    vg = v_all[:, idx]
    bias = bias_g[t5_bucket(dil * j)].T.astype(jnp.float32)
    logits = jnp.einsum('bthd,btjhd->bthj', q, kg, preferred_element_type=jnp.float32) + bias
    logits = jnp.where(valid[None, :, None, :], logits, -jnp.inf)
    m = jnp.max(logits, axis=-1, keepdims=True)
    p = jnp.exp(logits - m)
    s = jnp.sum(p, axis=-1)
    o = jnp.einsum('bthj,btjhd->bthd', p, vg.astype(jnp.float32)) / s[..., None]
    lse = m[..., 0] + jnp.log(s)
    new_buf = jnp.stack([k_all[:, k_all.shape[1] - Wb:], v_all[:, v_all.shape[1] - Wb:]], axis=2)
    return o, lse, new_buf


def to_chunks(t, C):
    B, T = t.shape[0], t.shape[1]
    t = t.astype(jnp.float32).reshape((B, T // C, C) + t.shape[2:])
    return jnp.moveaxis(t, 3, 2).swapaxes(0, 1)


def from_chunks(t):
    t = jnp.moveaxis(t.swapaxes(0, 1), 2, 3)
    return t.reshape((t.shape[0], t.shape[1] * t.shape[2]) + t.shape[3:])


def gla_chunked(q, k, v, log_a, s0):
    T = q.shape[1]
    C = math.gcd(T, B_CHUNK)
    causal = jnp.tril(jnp.ones((C, C), dtype=bool))

    def step(S, inp):
        qc, kc, vc, ac = inp
        b = jnp.cumsum(ac, axis=2)
        diff = b[:, :, :, None, :] - b[:, :, None, :, :]
        dec = jnp.exp(jnp.where(causal[:, :, None], diff, -jnp.inf))
        att = jnp.einsum('bhtd,bhsd,bhtsd->bhts', qc, kc, dec)
        o = jnp.einsum('bhtd,bhdv->bhtv', qc * jnp.exp(b), S) + jnp.einsum('bhts,bhsv->bhtv', att, vc)
        b_last = b[:, :, -1:, :]
        S = jnp.exp(b_last[:, :, 0, :, None]) * S + jnp.einsum('bhsd,bhsv->bhdv', kc * jnp.exp(b_last - b), vc)
        return S, o

    S, o = lax.scan(step, s0.astype(jnp.float32),
                    (to_chunks(q, C), to_chunks(k, C), to_chunks(v, C), to_chunks(log_a, C)))
    return from_chunks(o), S


def gdn_chunked(q, k, v, g, beta, s0):
    T = q.shape[1]
    C = math.gcd(T, C_CHUNK)
    incl = jnp.tril(jnp.ones((C, C), dtype=bool))
    strict = jnp.tril(jnp.ones((C, C), dtype=bool), -1)
    eye = jnp.eye(C, dtype=jnp.float32)

    def step(S, inp):
        qc, kc, vc, gc, bc = inp
        b = jnp.cumsum(gc, axis=-1)
        dec = jnp.exp(jnp.where(incl, b[..., :, None] - b[..., None, :], -jnp.inf))
        kk = jnp.einsum('bhtd,bhsd->bhts', kc, kc)
        mat = eye + jnp.where(strict, bc[..., :, None] * dec * kk, 0.0)
        rhs = bc[..., None] * (vc - jnp.exp(b)[..., None] * jnp.einsum('bhtd,bhdv->bhtv', kc, S))
        u = lax.linalg.triangular_solve(mat, rhs, left_side=True, lower=True, unit_diagonal=True)
        qk = jnp.einsum('bhtd,bhsd->bhts', qc, kc)
        o = jnp.exp(b)[..., None] * jnp.einsum('bhtd,bhdv->bhtv', qc, S) + jnp.einsum('bhts,bhsv->bhtv', dec * qk, u)
        b_last = b[..., -1:]
        S = jnp.exp(b_last)[..., None] * S + jnp.einsum('bhsd,bhsv->bhdv', kc * jnp.exp(b_last - b)[..., None], u)
        return S, o

    S, o = lax.scan(step, s0.astype(jnp.float32),
                    (to_chunks(q, C), to_chunks(k, C), to_chunks(v, C), to_chunks(g, C), to_chunks(beta, C)))
    return from_chunks(o), S


def token_mixer(x, l, P, cache):
    B, T, _ = x.shape
    h = rms_norm(x, P['norm_mix'][l])
    z = jnp.einsum('btd,dn->btn', h, P['w_in'][l])
    a_q, a_k, a_v, b_q, b_k, b_v, b_g, b_lr, c_qkv, c_z, c_b, c_a, gates = jnp.split(z, split_points(), axis=-1)

    shp = (B, T, N_GROUPS, A_HEADS, A_DH)
    q = rms_norm(a_q.reshape(shp), P['a_q_norm'][l]) * (A_DH ** -0.5)
    k = rms_norm(a_k.reshape(shp), P['a_k_norm'][l])
    v = a_v.reshape(shp)
    outs, lses, bufs = [], [], []
    for g, (win, dil) in enumerate(A_GROUPS):
        bias_g = P['rel_bias'][:, g * A_HEADS:(g + 1) * A_HEADS]
        qg, kg, vg = q[:, :, g], k[:, :, g], v[:, :, g]
        if cache is None:
            o, lse = dilated_prompt(qg, kg, vg, bias_g, win, dil)
            wp = min(win, T)
            buf = jnp.stack([kg[:, T - wp:], vg[:, T - wp:]], axis=2)
        else:
            o, lse, buf = dilated_sample(qg, kg, vg, cache[0][g], bias_g, win, dil)
        outs.append(o)
        lses.append(lse)
        bufs.append(buf)
    wts = jax.nn.softmax(jnp.stack(lses, axis=0), axis=0)
    o_a = jnp.sum(wts[..., None] * jnp.stack(outs, axis=0), axis=0).reshape(B, T, A_WIDTH).astype(x.dtype)

    log_a = jax.nn.log_sigmoid((jnp.einsum('btr,rk->btk', b_lr, P['b_w_lr'][l]) + P['b_lr_bias'][l]).astype(jnp.float32)) / B_TAU
    s_b0 = jnp.zeros((B, B_HEADS, B_DK, B_DV), jnp.float32) if cache is None else cache[1]
    o_b, s_b = gla_chunked(b_q.reshape(B, T, B_HEADS, B_DK) * (B_DK ** -0.5), b_k.reshape(B, T, B_HEADS, B_DK),
                           b_v.reshape(B, T, B_HEADS, B_DV), log_a.reshape(B, T, B_HEADS, B_DK), s_b0)
    o_b = (rms_norm(o_b, P['b_out_norm'][l]) * jax.nn.silu(b_g.reshape(B, T, B_HEADS, B_DV).astype(jnp.float32)))
    o_b = o_b.reshape(B, T, BRANCH_W).astype(x.dtype)

    conv_buf = jnp.zeros((B, CONV_W - 1, C_CONV_CH), x.dtype) if cache is None else cache[3].astype(x.dtype)
    cat = jnp.concatenate([conv_buf, c_qkv], axis=1)
    w_conv = P['c_conv'][l]
    conv = jax.nn.silu(sum(cat[:, j:j + T] * w_conv[j] for j in range(CONV_W)))
    new_conv = cat[:, T:]
    cq, ck, cv = jnp.split(conv, [C_HEADS * C_DK, 2 * C_HEADS * C_DK], axis=-1)
    cq = l2_norm(cq.reshape(B, T, C_HEADS, C_DK)) * (C_DK ** -0.5)
    ck = l2_norm(ck.reshape(B, T, C_HEADS, C_DK))
    cv = cv.reshape(B, T, C_HEADS, C_DV)
    gdec = -jnp.exp(P['c_a_log'][l]) * jax.nn.softplus(c_a.astype(jnp.float32) + P['c_dt_bias'][l])
    beta = jax.nn.sigmoid(c_b.astype(jnp.float32))
    s_c0 = jnp.zeros((B, C_HEADS, C_DK, C_DV), jnp.float32) if cache is None else cache[2]
    o_c, s_c = gdn_chunked(cq, ck, cv, gdec, beta, s_c0)
    o_c = rms_norm(o_c, P['c_out_norm'][l]) * jax.nn.silu(c_z.reshape(B, T, C_HEADS, C_DV).astype(jnp.float32))
    o_c = o_c.reshape(B, T, BRANCH_W).astype(x.dtype)

    g_a, g_b, g_c = jnp.split(jax.nn.sigmoid(gates), 3, axis=-1)
    wb = P['w_branch'][l]
    merged = (g_a * jnp.einsum('btc,cd->btd', o_a, wb[0]) + g_b * jnp.einsum('btc,cd->btd', o_b, wb[1])
              + g_c * jnp.einsum('btc,cd->btd', o_c, wb[2]))
    out = jnp.einsum('btd,de->bte', merged, P['w_out'][l])
    return out, (tuple(bufs), s_b, s_c, new_conv)


def channel_mixer(x, l, P):
    h = rms_norm(x, P['norm_ffn'][l])
    i = l // 2
    if l % 2 == 0:
        a = jax.nn.silu(jnp.einsum('btd,df->btf', h, P['ffn_w_gate'][i])) * jnp.einsum('btd,df->btf', h, P['ffn_w_up'][i])
        return jnp.einsum('btf,fd->btd', a, P['ffn_w_down'][i])
    logits = jnp.einsum('btd,de->bte', h, P['moe_router'][i], preferred_element_type=jnp.float32)
    probs = jax.nn.softmax(logits, axis=-1)
    top_p, top_i = lax.top_k(probs, TOP_K)
    top_p = top_p / jnp.sum(top_p, axis=-1, keepdims=True)
    gate = jnp.sum(jax.nn.one_hot(top_i, N_EXPERTS, dtype=jnp.float32) * top_p[..., None], axis=-2)
    hg = jnp.einsum('btd,edf->btef', h, P['moe_w_gate'][i])
    hu = jnp.einsum('btd,edf->btef', h, P['moe_w_up'][i])
    a = jax.nn.silu(hg) * hu * gate[..., None].astype(h.dtype)
    return jnp.einsum('btef,efd->btd', a, P['moe_w_down'][i])


def setup_inputs(seed: int = 0) -> dict:
    key = jax.random.key(seed)
    ks = jax.random.split(key, 32)
    D = D_MODEL

    def nrm(k, shape, scale):
        return scale * jax.random.normal(k, shape, jnp.float32)

    def gain(k, shape):
        return 1.0 + 0.05 * jax.random.normal(k, shape, jnp.float32)

    def win_shape(w):
        return (DEPTH, DEC_BATCH, min(w, PAST_LEN), 2, A_HEADS, A_DH)

    dt = jnp.exp(jax.random.uniform(ks[18], (DEPTH, C_HEADS), jnp.float32, math.log(1e-3), math.log(1e-1)))
    return {
        'x_prompt': nrm(ks[0], (BATCH, SEQ, D), 1.0),
        'x_sample': nrm(ks[1], (DEC_BATCH, DEC_SEQ, D), 1.0),
        'cache_win128_kv': nrm(ks[2], win_shape(128), 1.0),
        'cache_win512_kv': nrm(ks[3], win_shape(512), 1.0),
        'cache_win2048_kv': nrm(ks[4], win_shape(2048), 1.0),
        'state_gla': nrm(ks[5], (DEPTH, DEC_BATCH, B_HEADS, B_DK, B_DV), 0.1),
        'state_gdn': nrm(ks[6], (DEPTH, DEC_BATCH, C_HEADS, C_DK, C_DV), 0.1),
        'state_conv': nrm(ks[7], (DEPTH, DEC_BATCH, CONV_W - 1, C_CONV_CH), 1.0),
        'norm_mix': gain(ks[8], (DEPTH, D)),
        'w_in': nrm(ks[9], (DEPTH, D, N_IN), D ** -0.5),
        'a_q_norm': gain(ks[10], (DEPTH, A_DH)),
        'a_k_norm': gain(ks[11], (DEPTH, A_DH)),
        'rel_bias': nrm(ks[12], (N_BUCKETS, N_GROUPS * A_HEADS), 0.5),
        'b_w_lr': nrm(ks[13], (DEPTH, B_RANK, B_HEADS * B_DK), B_RANK ** -0.5),
        'b_lr_bias': nrm(ks[14], (DEPTH, B_HEADS * B_DK), 0.1),
        'b_out_norm': gain(ks[15], (DEPTH, B_DV)),
        'c_conv': nrm(ks[16], (DEPTH, CONV_W, C_CONV_CH), CONV_W ** -0.5),
        'c_a_log': jnp.log(jax.random.uniform(ks[17], (DEPTH, C_HEADS), jnp.float32, 1.0, 16.0)),
        'c_dt_bias': dt + jnp.log(-jnp.expm1(-dt)),
        'c_out_norm': gain(ks[19], (DEPTH, C_DV)),
        'w_branch': nrm(ks[20], (DEPTH, 3, BRANCH_W, D), BRANCH_W ** -0.5),
        'w_out': nrm(ks[21], (DEPTH, D, D), D ** -0.5),
        'norm_ffn': gain(ks[22], (DEPTH, D)),
        'ffn_w_gate': nrm(ks[23], (N_DENSE, D, D_FF), D ** -0.5),
        'ffn_w_up': nrm(ks[24], (N_DENSE, D, D_FF), D ** -0.5),
        'ffn_w_down': nrm(ks[25], (N_DENSE, D_FF, D), D_FF ** -0.5),
        'moe_router': nrm(ks[26], (N_MOE, D, N_EXPERTS), D ** -0.5),
        'moe_w_gate': nrm(ks[27], (N_MOE, N_EXPERTS, D, D_FF_EXPERT), D ** -0.5),
        'moe_w_up': nrm(ks[28], (N_MOE, N_EXPERTS, D, D_FF_EXPERT), D ** -0.5),
        'moe_w_down': nrm(ks[29], (N_MOE, N_EXPERTS, D_FF_EXPERT, D), D_FF_EXPERT ** -0.5),
    }


def reference(x_prompt, x_sample, cache_win128_kv, cache_win512_kv, cache_win2048_kv, state_gla, state_gdn,
              state_conv, norm_mix, w_in, a_q_norm, a_k_norm, rel_bias, b_w_lr, b_lr_bias, b_out_norm, c_conv,
              c_a_log, c_dt_bias, c_out_norm, w_branch, w_out, norm_ffn, ffn_w_gate, ffn_w_up, ffn_w_down,
              moe_router, moe_w_gate, moe_w_up, moe_w_down):
    P = dict(norm_mix=norm_mix, w_in=w_in, a_q_norm=a_q_norm, a_k_norm=a_k_norm, rel_bias=rel_bias,
             b_w_lr=b_w_lr, b_lr_bias=b_lr_bias, b_out_norm=b_out_norm, c_conv=c_conv, c_a_log=c_a_log,
             c_dt_bias=c_dt_bias, c_out_norm=c_out_norm, w_branch=w_branch, w_out=w_out, norm_ffn=norm_ffn,
             ffn_w_gate=ffn_w_gate, ffn_w_up=ffn_w_up, ffn_w_down=ffn_w_down, moe_router=moe_router,
             moe_w_gate=moe_w_gate, moe_w_up=moe_w_up, moe_w_down=moe_w_down)
    xp, xs = x_prompt, x_sample
    new_p, new_s = [], []
    for l in range(DEPTH):
        out_p, st_p = token_mixer(xp, l, P, None)
        xp = xp + out_p
        xp = xp + channel_mixer(xp, l, P)
        new_p.append(st_p)
        cache_l = ((cache_win128_kv[l], cache_win512_kv[l], cache_win2048_kv[l]), state_gla[l], state_gdn[l], state_conv[l])
        out_s, st_s = token_mixer(xs, l, P, cache_l)
        xs = xs + out_s
        xs = xs + channel_mixer(xs, l, P)
        new_s.append(st_s)
    win128_p = jnp.stack([st[0][0] for st in new_p])
    win128_s = jnp.stack([st[0][0] for st in new_s])
    win512_p = jnp.stack([st[0][1] for st in new_p])
    win512_s = jnp.stack([st[0][1] for st in new_s])
    win2048_p = jnp.stack([st[0][2] for st in new_p])
    win2048_s = jnp.stack([st[0][2] for st in new_s])
    gla_p = jnp.stack([st[1] for st in new_p])
    gla_s = jnp.stack([st[1] for st in new_s])
    gdn_p = jnp.stack([st[2] for st in new_p])
    gdn_s = jnp.stack([st[2] for st in new_s])
    conv_p = jnp.stack([st[3] for st in new_p])
    conv_s = jnp.stack([st[3] for st in new_s])
    return (xp, xs, win128_p, win128_s, win512_p, win512_s, win2048_p, win2048_s,
            gla_p, gla_s, gdn_p, gdn_s, conv_p, conv_s)
```

```python
import functools
import math

import jax
import jax.numpy as jnp
import numpy as np
from jax import lax
from jax.experimental import pallas as pl
from jax.experimental.pallas import tpu as pltpu

F32 = jnp.float32
BF16 = jnp.bfloat16
HI = lax.Precision.HIGHEST
NEG = -1e30

D_MODEL = 1024
DEPTH = 2
EPS = 1e-6
A_GROUPS = ((128, 1), (512, 4), (2048, 16))
N_GROUPS = 3
A_HEADS = 8
A_DH = 64
A_WIDTH = A_HEADS * A_DH
SPAN = 128
LPR = 2 * A_WIDTH // 128
N_BUCKETS = 32
MAX_DIST = 2048
B_HEADS = 4
B_DK = 64
B_DV = 128
B_RANK = 16
B_TAU = 16.0
C_HEADS = 4
C_DK = 128
C_DV = 128
CONV_W = 4
C_CONV_CH = 2 * C_HEADS * C_DK + C_HEADS * C_DV
CHUNK = 64
SUB = 16
BRANCH_W = 512
D_FF = 2816
N_EXPERTS = 8
D_FF_EXPERT = 1792
IN_SIZES = (N_GROUPS * A_WIDTH, N_GROUPS * A_WIDTH, N_GROUPS * A_WIDTH,
            B_HEADS * B_DK, B_HEADS * B_DK, B_HEADS * B_DV, B_HEADS * B_DV, B_RANK,
            C_CONV_CH, C_HEADS * C_DV, C_HEADS, C_HEADS, 3 * D_MODEL)

Z_AQ, Z_AK, Z_AV = 0, 1536, 3072
Z_BQ, Z_BK, Z_BV, Z_BG = 4608, 4864, 5120, 5632
Z_CQKV, Z_CZ, Z_GATES, Z_SMALL = 6144, 7680, 8192, 11264
SMALL_W = 256
NZ = Z_SMALL + SMALL_W
SM_CB, SM_CA = B_RANK, B_RANK + C_HEADS

VMEM_LIMIT = 56 * 1024 * 1024


def _cparams(sem):
    return pltpu.CompilerParams(dimension_semantics=sem, vmem_limit_bytes=VMEM_LIMIT)


def _dot(a, b):
    return jnp.dot(a.astype(BF16), b.astype(BF16), preferred_element_type=F32)


def _dot_nt(a, b):
    return lax.dot_general(a.astype(BF16), b.astype(BF16), (((1,), (1,)), ((), ())), preferred_element_type=F32)


def _dot_tn(a, b):
    return lax.dot_general(a.astype(BF16), b.astype(BF16), (((0,), (0,)), ((), ())), preferred_element_type=F32)


def _dot_hi(a, b):
    return jnp.dot(a, b, precision=HI, preferred_element_type=F32)


def _r16(x):
    return x.astype(BF16).astype(F32)


def _sigmoid(x):
    return jax.nn.sigmoid(x)


def _silu(x):
    return x * jax.nn.sigmoid(x)


def _softplus(x):
    return jnp.maximum(x, 0.0) + jnp.log1p(jnp.exp(-jnp.abs(x)))


def _log_sigmoid(x):
    return jnp.minimum(x, 0.0) - jnp.log1p(jnp.exp(-jnp.abs(x)))


def _rms(x, gain):
    return x * lax.rsqrt(jnp.mean(x * x, axis=-1, keepdims=True) + EPS) * gain


def _norm_matmul_kernel(x_ref, g_ref, w_ref, o_ref, h_ref):
    @pl.when(pl.program_id(1) == 0)
    def _():
        h_ref[...] = _rms(x_ref[...], g_ref[...]).astype(BF16)

    o_ref[...] = jnp.dot(h_ref[...], w_ref[...], preferred_element_type=F32)


def _norm_matmul(x, gain, w, tm, tn):
    m, d = x.shape
    n = w.shape[1]
    return pl.pallas_call(
        _norm_matmul_kernel,
        grid=(m // tm, n // tn),
        in_specs=[pl.BlockSpec((tm, d), lambda i, j: (i, 0)),
                  pl.BlockSpec((1, d), lambda i, j: (0, 0)),
                  pl.BlockSpec((d, tn), lambda i, j: (0, j))],
        out_specs=pl.BlockSpec((tm, tn), lambda i, j: (i, j)),
        out_shape=jax.ShapeDtypeStruct((m, n), F32),
        scratch_shapes=[pltpu.VMEM((tm, d), BF16)],
        compiler_params=_cparams(("parallel", "arbitrary")),
        name="norm_matmul",
    )(x, gain.reshape(1, d), w)


def _attn_prompt_kernel(q1, k1, v1, q2, k2, v2, q3, k3, v3, qg_ref, kg_ref, pavg_ref, bias_ref,
                        o_ref, kn1, kn2, kn3, qt, qs, ks, vs, og, lg, to, tl):
    t_len = o_ref.shape[0]
    head0 = lax.broadcasted_iota(jnp.int32, (1, 2 * A_DH), 1) < A_DH
    first_half = lax.broadcasted_iota(jnp.int32, (1, 2 * SPAN), 1) < SPAN
    groups = ((q1, k1, v1, kn1), (q2, k2, v2, kn2), (q3, k3, v3, kn3))
    for g, (qr, kr, vr, knr) in enumerate(groups):
        dil = A_GROUPS[g][1]
        length = t_len // dil
        nb = length // SPAN
        q = qr[...]
        qt[...] = q * lax.rsqrt(_dot_hi(q * q, pavg_ref[...]) + EPS) * qg_ref[...] * (A_DH ** -0.5)
        k = kr[...]
        knr[...] = k * lax.rsqrt(_dot_hi(k * k, pavg_ref[...]) + EPS) * kg_ref[...]
        for r in range(dil):
            rows = pl.ds(r, length, stride=dil) if dil > 1 else pl.ds(0, length)
            dst = pl.ds(r * length, length)
            qs[dst, :] = qt[rows, :].astype(BF16)
            ks[dst, :] = knr[rows, :].astype(BF16)
            vs[dst, :] = vr[rows, :].astype(BF16)

        def block(j, carry, g=g, nb=nb):
            row0 = pl.multiple_of(j * SPAN, SPAN)
            qb = qs[pl.ds(row0, SPAN), :]
            if nb > 1:
                prow = pl.multiple_of(jnp.maximum(row0 - SPAN, 0), SPAN)
                k2 = jnp.concatenate([ks[pl.ds(prow, SPAN), :], ks[pl.ds(row0, SPAN), :]], axis=0)
                v2 = jnp.concatenate([vs[pl.ds(prow, SPAN), :], vs[pl.ds(row0, SPAN), :]], axis=0)
                pen = jnp.where(first_half, jnp.where(j % nb == 0, NEG, 0.0), 0.0)
            else:
                k2 = ks[pl.ds(row0, SPAN), :]
                v2 = vs[pl.ds(row0, SPAN), :]
            outs, lses = [], []
            for h in range(2):
                qh = jnp.where(head0 if h == 0 else jnp.logical_not(head0), qb, jnp.zeros_like(qb))
                logits = lax.dot_general(qh, k2, (((1,), (1,)), ((), ())), preferred_element_type=F32)
                if nb > 1:
                    logits = logits + bias_ref[g, h] + pen
                else:
                    logits = logits + bias_ref[g, h, :, SPAN:]
                m = jnp.max(logits, axis=-1, keepdims=True)
                p = jnp.exp(logits - m)
                s = jnp.sum(p, axis=-1, keepdims=True)
                acc = jnp.dot(p.astype(BF16), v2, preferred_element_type=F32)
                outs.append(acc / s)
                lses.append(jnp.broadcast_to(m + jnp.log(s), (SPAN, 2 * A_DH)))
            to[pl.ds(row0, SPAN), :] = jnp.where(head0, outs[0], outs[1])
            tl[pl.ds(row0, SPAN), :] = jnp.where(head0, lses[0], lses[1])
            return carry

        lax.fori_loop(0, t_len // SPAN, block, 0)
        for r in range(dil):
            rows = pl.ds(r, length, stride=dil) if dil > 1 else pl.ds(0, length)
            src = pl.ds(r * length, length)
            og[g, rows, :] = to[src, :]
            lg[g, rows, :] = tl[src, :]
    l_1, l_2, l_3 = lg[0], lg[1], lg[2]
    mx = jnp.maximum(jnp.maximum(l_1, l_2), l_3)
    e_1, e_2, e_3 = jnp.exp(l_1 - mx), jnp.exp(l_2 - mx), jnp.exp(l_3 - mx)
    o_ref[...] = (e_1 * og[0] + e_2 * og[1] + e_3 * og[2]) / (e_1 + e_2 + e_3)


def _attn_prompt(z3, qgain, kgain, pavg, bias):
    b, t, _ = z3.shape
    hp_blocks = A_WIDTH // 128

    def col(base, g):
        return lambda i, hp: (i, 0, base // 128 + g * hp_blocks + hp)

    in_specs = []
    for g in range(N_GROUPS):
        for base in (Z_AQ, Z_AK, Z_AV):
            in_specs.append(pl.BlockSpec((None, t, 128), col(base, g)))
    in_specs += [pl.BlockSpec((1, 128), lambda i, hp: (0, 0)),
                 pl.BlockSpec((1, 128), lambda i, hp: (0, 0)),
                 pl.BlockSpec((128, 128), lambda i, hp: (0, 0)),
                 pl.BlockSpec((N_GROUPS, 2, SPAN, 2 * SPAN), lambda i, hp: (0, hp, 0, 0))]
    out_spec = pl.BlockSpec((None, t, 128), lambda i, hp: (i, 0, hp))
    shp = jax.ShapeDtypeStruct((b, t, A_WIDTH), F32)
    return pl.pallas_call(
        _attn_prompt_kernel,
        grid=(b, hp_blocks),
        in_specs=in_specs,
        out_specs=[out_spec] * 4,
        out_shape=[shp] * 4,
        scratch_shapes=[pltpu.VMEM((t, 128), F32),
                        pltpu.VMEM((t, 128), BF16), pltpu.VMEM((t, 128), BF16), pltpu.VMEM((t, 128), BF16),
                        pltpu.VMEM((N_GROUPS, t, 128), F32), pltpu.VMEM((N_GROUPS, t, 128), F32),
                        pltpu.VMEM((t, 128), F32), pltpu.VMEM((t, 128), F32)],
        compiler_params=_cparams(("parallel", "parallel")),
        name="attn_prompt",
    )(*([z3] * 9), qgain, kgain, pavg, bias)


def _gla_prompt_kernel(q_ref, k_ref, v_ref, sm_ref, wlr_ref, lrb_ref, o_ref, st_ref, s_scr, la_scr):
    tb = q_ref.shape[0]
    hk = B_HEADS * B_DK
    hv = B_HEADS * B_DV

    @pl.when(pl.program_id(1) == 0)
    def _():
        s_scr[...] = jnp.zeros_like(s_scr)

    la_scr[...] = _log_sigmoid(_dot(sm_ref[...], wlr_ref[...]) + lrb_ref[...]) * (1.0 / B_TAU)

    rr = lax.broadcasted_iota(jnp.int32, (CHUNK, CHUNK), 0)
    cc = lax.broadcasted_iota(jnp.int32, (CHUNK, CHUNK), 1)
    causal = rr >= cc
    ltri = causal.astype(F32)
    lane_head = lax.broadcasted_iota(jnp.int32, (1, hk), 1) // B_DK
    row = lax.broadcasted_iota(jnp.int32, (CHUNK, 1), 0)
    same_head = (lax.broadcasted_iota(jnp.int32, (hv, hk), 0) // B_DV
                 == lax.broadcasted_iota(jnp.int32, (hv, hk), 1) // B_DK)

    def chunk(c, carry):
        r0 = pl.multiple_of(c * CHUNK, CHUNK)
        b = _dot_hi(ltri, la_scr[pl.ds(r0, CHUNK), :])
        q = q_ref[pl.ds(r0, CHUNK), :] * (B_DK ** -0.5)
        k = k_ref[pl.ds(r0, CHUNK), :]
        vb = v_ref[pl.ds(r0, CHUNK), :].astype(BF16)
        s = s_scr[...]
        b_last = b[CHUNK - 1:CHUNK, :]
        o_inter = _dot_nt(q * jnp.exp(b), s)
        atts = []
        for i in range(CHUNK // SUB):
            b_ref0 = b[SUB * i:SUB * i + 1, :]
            qi = q[SUB * i:SUB * (i + 1), :] * jnp.exp(b[SUB * i:SUB * (i + 1), :] - b_ref0)
            qst = jnp.concatenate([jnp.where(lane_head == h, qi, 0.0) for h in range(B_HEADS)], axis=0)
            kt = k * jnp.exp(jnp.where(row < SUB * (i + 1), b_ref0 - b, 0.0))
            atts.append(_dot_nt(qst, kt))
        o_parts = []
        for h in range(B_HEADS):
            att = jnp.concatenate([a[SUB * h:SUB * (h + 1), :] for a in atts], axis=0)
            att = jnp.where(causal, att, 0.0)
            o_parts.append(_dot(att, vb[:, B_DV * h:B_DV * (h + 1)]))
        o_ref[pl.ds(r0, CHUNK), :] = o_inter + jnp.concatenate(o_parts, axis=1)
        upd = _dot_tn(vb, k * jnp.exp(b_last - b))
        s_scr[...] = s * jnp.exp(b_last) + jnp.where(same_head, upd, 0.0)
        return carry

    lax.fori_loop(0, tb // CHUNK, chunk, 0)

    @pl.when(pl.program_id(1) == pl.num_programs(1) - 1)
    def _():
        st_ref[...] = s_scr[...]


def _gla_prompt(z3, wlr_pad, lr_bias, tb):
    b, t, _ = z3.shape
    hk, hv = B_HEADS * B_DK, B_HEADS * B_DV
    return pl.pallas_call(
        _gla_prompt_kernel,
        grid=(b, t // tb),
        in_specs=[pl.BlockSpec((None, tb, hk), lambda i, j: (i, j, Z_BQ // hk)),
                  pl.BlockSpec((None, tb, hk), lambda i, j: (i, j, Z_BK // hk)),
                  pl.BlockSpec((None, tb, hv), lambda i, j: (i, j, Z_BV // hv)),
                  pl.BlockSpec((None, tb, SMALL_W), lambda i, j: (i, j, Z_SMALL // SMALL_W)),
                  pl.BlockSpec((SMALL_W, hk), lambda i, j: (0, 0)),
                  pl.BlockSpec((1, hk), lambda i, j: (0, 0))],
        out_specs=[pl.BlockSpec((None, tb, hv), lambda i, j: (i, j, 0)),
                   pl.BlockSpec((None, hv, hk), lambda i, j: (i, 0, 0))],
        out_shape=[jax.ShapeDtypeStruct((b, t, hv), F32), jax.ShapeDtypeStruct((b, hv, hk), F32)],
        scratch_shapes=[pltpu.VMEM((hv, hk), F32), pltpu.VMEM((tb, hk), F32)],
        compiler_params=_cparams(("parallel", "arbitrary")),
        name="gla_prompt",
    )(z3, z3, z3, z3, wlr_pad, lr_bias)


def _gdn_prompt_kernel(x_ref, sm_ref, cw_ref, selb_ref, sela_ref, dtb_ref, alog_ref, o_ref, s_ref,
                       xpad, cv, gsc, bsc, s_scr):
    tb = x_ref.shape[0]
    hd = C_HEADS * C_DK
    pad = 8

    @pl.when(pl.program_id(1) == 0)
    def _():
        s_scr[...] = jnp.zeros_like(s_scr)
        xpad[0:pad, :] = jnp.zeros((pad, C_CONV_CH), F32)

    xpad[pad:pad + tb, :] = x_ref[...]
    rb = 128
    for i in range(tb // rb):
        acc = None
        for j in range(CONV_W):
            lo = pad - (CONV_W - 1) + j + i * rb
            term = xpad[lo:lo + rb, :] * cw_ref[j:j + 1, :]
            acc = term if acc is None else acc + term
        y = _silu(acc)
        rows = slice(i * rb, (i + 1) * rb)
        for h in range(C_HEADS):
            qh = y[:, C_DK * h:C_DK * (h + 1)]
            cv[rows, C_DK * h:C_DK * (h + 1)] = (
                qh * lax.rsqrt(jnp.sum(qh * qh, axis=-1, keepdims=True) + EPS) * (C_DK ** -0.5))
            kh = y[:, hd + C_DK * h:hd + C_DK * (h + 1)]
            cv[rows, hd + C_DK * h:hd + C_DK * (h + 1)] = (
                kh * lax.rsqrt(jnp.sum(kh * kh, axis=-1, keepdims=True) + EPS))
        cv[rows, 2 * hd:] = y[:, 2 * hd:]
    xpad[pad - (CONV_W - 1):pad, :] = xpad[pad + tb - (CONV_W - 1):pad + tb, :]

    sm = sm_ref[...]
    gsc[...] = -jnp.exp(alog_ref[...]) * _softplus(_dot_hi(sm, sela_ref[...]) + dtb_ref[...])
    bsc[...] = _sigmoid(_dot_hi(sm, selb_ref[...]))

    n = C_HEADS * CHUNK
    rr = lax.broadcasted_iota(jnp.int32, (CHUNK, CHUNK), 0)
    cc = lax.broadcasted_iota(jnp.int32, (CHUNK, CHUNK), 1)
    ltri = (rr >= cc).astype(F32)
    ri = lax.broadcasted_iota(jnp.int32, (n, n), 0)
    ci = lax.broadcasted_iota(jnp.int32, (n, n), 1)
    same = (ri // CHUNK) == (ci // CHUNK)
    incl = jnp.logical_and(same, ci <= ri)
    strict = jnp.logical_and(same, ci < ri)
    eye = (ri == ci).astype(F32)

    def stack(x):
        return jnp.concatenate([x[:, C_DK * h:C_DK * (h + 1)] for h in range(C_HEADS)], axis=0)

    def chunk(c, carry):
        r0 = pl.multiple_of(c * CHUNK, CHUNK)
        bst = stack(_dot_hi(ltri, gsc[pl.ds(r0, CHUNK), :]))
        beta = stack(bsc[pl.ds(r0, CHUNK), :])
        qst = stack(cv[pl.ds(r0, CHUNK), 0:hd])
        kst = stack(cv[pl.ds(r0, CHUNK), hd:2 * hd])
        vst = stack(cv[pl.ds(r0, CHUNK), 2 * hd:3 * hd])
        kb = kst.astype(BF16)
        kk = lax.dot_general(kb, kb, (((1,), (1,)), ((), ())), preferred_element_type=F32)
        qk = lax.dot_general(qst.astype(BF16), kb, (((1,), (1,)), ((), ())), preferred_element_type=F32)
        col_b = jnp.concatenate([bst, bst], axis=1)
        bst_t = bst.T
        row_b = jnp.concatenate([bst_t, bst_t], axis=0)
        dec = jnp.exp(jnp.where(incl, col_b - row_b, NEG))
        a = jnp.where(strict, jnp.concatenate([beta, beta], axis=1) * dec * kk, 0.0)
        aqk = dec * qk
        inv = eye - a
        pw = a
        for _ in range(int(math.log2(CHUNK)) - 1):
            pw = _dot(pw, pw)
            inv = inv + _dot(inv, pw)
        eb = jnp.exp(bst)
        wu = _dot(inv, jnp.concatenate([beta * eb * kst, beta * vst], axis=1))
        w, uv = wu[:, :C_DK], wu[:, C_DK:]
        us, qss = [], []
        for h in range(C_HEADS):
            sl = slice(CHUNK * h, CHUNK * (h + 1))
            ws = _dot(jnp.concatenate([w[sl], qst[sl]], axis=0), s_scr[h])
            us.append(uv[sl] - ws[:CHUNK])
            qss.append(ws[CHUNK:])
        ust = jnp.concatenate(us, axis=0)
        o = eb * jnp.concatenate(qss, axis=0) + _dot(aqk, ust)
        for h in range(C_HEADS):
            sl = slice(CHUNK * h, CHUNK * (h + 1))
            o_ref[pl.ds(r0, CHUNK), C_DV * h:C_DV * (h + 1)] = o[sl]
            b_last = bst[CHUNK * (h + 1) - 1:CHUNK * (h + 1), :]
            s_scr[h] = jnp.exp(b_last) * s_scr[h] + _dot_tn(kst[sl] * jnp.exp(b_last - bst[sl]), us[h])
        return carry

    lax.fori_loop(0, tb // CHUNK, chunk, 0)

    @pl.when(pl.program_id(1) == pl.num_programs(1) - 1)
    def _():
        s_ref[...] = s_scr[...]


def _gdn_prompt(z3, conv_w, selb, sela, dtb_bc, alog_bc, tb):
    b, t, _ = z3.shape
    hd = C_HEADS * C_DV
    const = lambda shape: pl.BlockSpec(shape, lambda i, j: (0,) * len(shape))
    return pl.pallas_call(
        _gdn_prompt_kernel,
        grid=(b, t // tb),
        in_specs=[pl.BlockSpec((None, tb, C_CONV_CH), lambda i, j: (i, j, Z_CQKV // C_CONV_CH)),
                  pl.BlockSpec((None, tb, SMALL_W), lambda i, j: (i, j, Z_SMALL // SMALL_W)),
                  const((CONV_W, C_CONV_CH)), const((SMALL_W, hd)), const((SMALL_W, hd)),
                  const((1, hd)), const((1, hd))],
        out_specs=[pl.BlockSpec((None, tb, hd), lambda i, j: (i, j, 0)),
                   pl.BlockSpec((None, C_HEADS, C_DK, C_DV), lambda i, j: (i, 0, 0, 0))],
        out_shape=[jax.ShapeDtypeStruct((b, t, hd), F32),
                   jax.ShapeDtypeStruct((b, C_HEADS, C_DK, C_DV), F32)],
        scratch_shapes=[pltpu.VMEM((tb + 8, C_CONV_CH), F32), pltpu.VMEM((tb, C_CONV_CH), F32),
                        pltpu.VMEM((tb, hd), F32), pltpu.VMEM((tb, hd), F32),
                        pltpu.VMEM((C_HEADS, C_DK, C_DV), F32)],
        compiler_params=_cparams(("parallel", "arbitrary")),
        name="gdn_prompt",
    )(z3, z3, conv_w, selb, sela, dtb_bc, alog_bc)


def _head_norm(o, gain, width):
    parts = []
    for h in range(o.shape[1] // width):
        oh = o[:, width * h:width * (h + 1)]
        parts.append(oh * lax.rsqrt(jnp.mean(oh * oh, axis=-1, keepdims=True) + EPS))
    return jnp.concatenate(parts, axis=1) * gain


def _merge_kernel(oa_ref, ob_ref, bg_ref, oc_ref, cz_ref, ga_ref, gb_ref, gc_ref, x_ref,
                  bon_ref, con_ref, wb_ref, wo_ref, y_ref):
    ob = _head_norm(ob_ref[...], bon_ref[...], B_DV) * _silu(bg_ref[...])
    oc = _head_norm(oc_ref[...], con_ref[...], C_DV) * _silu(cz_ref[...])
    merged = (_sigmoid(ga_ref[...]) * _dot(oa_ref[...], wb_ref[0])
              + _sigmoid(gb_ref[...]) * _dot(ob, wb_ref[1])
              + _sigmoid(gc_ref[...]) * _dot(oc, wb_ref[2]))
    y_ref[...] = x_ref[...] + _dot(merged, wo_ref[...])


def _merge(z, oa, ob, oc, x, bon, con, wb, wo, tm):
    m = x.shape[0]
    w = BRANCH_W
    row = lambda width, blk: pl.BlockSpec((tm, width), lambda i: (i, blk))
    const = lambda shape: pl.BlockSpec(shape, lambda i: (0,) * len(shape))
    return pl.pallas_call(
        _merge_kernel,
        grid=(m // tm,),
        in_specs=[row(w, 0), row(w, 0), row(w, Z_BG // w), row(w, 0), row(w, Z_CZ // w),
                  row(D_MODEL, Z_GATES // D_MODEL), row(D_MODEL, Z_GATES // D_MODEL + 1),
                  row(D_MODEL, Z_GATES // D_MODEL + 2), row(D_MODEL, 0),
                  const((1, w)), const((1, w)), const((3, w, D_MODEL)), const((D_MODEL, D_MODEL))],
        out_specs=row(D_MODEL, 0),
        out_shape=jax.ShapeDtypeStruct((m, D_MODEL), F32),
        compiler_params=_cparams(("parallel",)),
        name="merge",
    )(oa, ob, z, oc, z, z, z, z, x, bon, con, wb, wo)


def _ffn_kernel(x_ref, g_ref, wg_ref, wu_ref, wd_ref, y_ref, h_ref, acc_ref):
    f = pl.program_id(1)

    @pl.when(f == 0)
    def _():
        h_ref[...] = _rms(x_ref[...], g_ref[...]).astype(BF16)
        acc_ref[...] = jnp.zeros_like(acc_ref)

    h = h_ref[...]
    a = _silu(jnp.dot(h, wg_ref[...], preferred_element_type=F32)) * jnp.dot(h, wu_ref[...],
                                                                              preferred_element_type=F32)
    acc_ref[...] += jnp.dot(a.astype(BF16), wd_ref[...], preferred_element_type=F32)

    @pl.when(f == pl.num_programs(1) - 1)
    def _():
        y_ref[...] = x_ref[...] + acc_ref[...]


def _ffn(x, gain, wg, wu, wd, tm, tf):
    m, d = x.shape
    ff = wg.shape[1]
    return pl.pallas_call(
        _ffn_kernel,
        grid=(m // tm, ff // tf),
        in_specs=[pl.BlockSpec((tm, d), lambda i, f: (i, 0)),
                  pl.BlockSpec((1, d), lambda i, f: (0, 0)),
                  pl.BlockSpec((d, tf), lambda i, f: (0, f)),
                  pl.BlockSpec((d, tf), lambda i, f: (0, f)),
                  pl.BlockSpec((tf, d), lambda i, f: (f, 0))],
        out_specs=pl.BlockSpec((tm, d), lambda i, f: (i, 0)),
        out_shape=jax.ShapeDtypeStruct((m, d), F32),
        scratch_shapes=[pltpu.VMEM((tm, d), BF16), pltpu.VMEM((tm, d), F32)],
        compiler_params=_cparams(("parallel", "arbitrary")),
        name="ffn",
    )(x, gain.reshape(1, d), wg, wu, wd)


def _moe_kernel(x_ref, g_ref, rt_ref, wg_ref, wu_ref, wd_ref, y_ref, h_ref, acc_ref, gate_ref):
    e = pl.program_id(1)
    f = pl.program_id(2)
    lane = lax.broadcasted_iota(jnp.int32, (1, 128), 1).astype(F32)

    @pl.when(jnp.logical_and(e == 0, f == 0))
    def _():
        hf = _rms(x_ref[...], g_ref[...])
        h_ref[...] = hf.astype(BF16)
        acc_ref[...] = jnp.zeros_like(acc_ref)
        valid = lane < N_EXPERTS
        logits = jnp.where(valid, _dot(hf, rt_ref[...]), NEG)
        ex = jnp.exp(logits - jnp.max(logits, axis=-1, keepdims=True))
        probs = ex / jnp.sum(ex, axis=-1, keepdims=True)
        m1 = jnp.max(probs, axis=-1, keepdims=True)
        i1 = jnp.min(jnp.where(jnp.logical_and(probs == m1, valid), lane, 128.0), axis=-1, keepdims=True)
        hot1 = lane == i1
        rest = jnp.where(jnp.logical_or(hot1, jnp.logical_not(valid)), -1.0, probs)
        m2 = jnp.max(rest, axis=-1, keepdims=True)
        i2 = jnp.min(jnp.where(rest == m2, lane, 128.0), axis=-1, keepdims=True)
        hot2 = lane == i2
        den = m1 + m2
        gate_ref[...] = jnp.where(hot1, m1 / den, 0.0) + jnp.where(hot2, m2 / den, 0.0)

    h = h_ref[...]
    ge = jnp.sum(jnp.where(lane == e.astype(F32), gate_ref[...], 0.0), axis=-1, keepdims=True)
    a = (_silu(jnp.dot(h, wg_ref[...], preferred_element_type=F32))
         * jnp.dot(h, wu_ref[...], preferred_element_type=F32) * ge)
    acc_ref[...] += jnp.dot(a.astype(BF16), wd_ref[...], preferred_element_type=F32)

    @pl.when(jnp.logical_and(e == pl.num_programs(1) - 1, f == pl.num_programs(2) - 1))
    def _():
        y_ref[...] = x_ref[...] + acc_ref[...]


def _moe(x, gain, router_pad, wg, wu, wd, tm, tf):
    m, d = x.shape
    ne, _, ff = wg.shape
    return pl.pallas_call(
        _moe_kernel,
        grid=(m // tm, ne, ff // tf),
        in_specs=[pl.BlockSpec((tm, d), lambda i, e, f: (i, 0)),
                  pl.BlockSpec((1, d), lambda i, e, f: (0, 0)),
                  pl.BlockSpec((d, 128), lambda i, e, f: (0, 0)),
                  pl.BlockSpec((None, d, tf), lambda i, e, f: (e, 0, f)),
                  pl.BlockSpec((None, d, tf), lambda i, e, f: (e, 0, f)),
                  pl.BlockSpec((None, tf, d), lambda i, e, f: (e, f, 0))],
        out_specs=pl.BlockSpec((tm, d), lambda i, e, f: (i, 0)),
        out_shape=jax.ShapeDtypeStruct((m, d), F32),
        scratch_shapes=[pltpu.VMEM((tm, d), BF16), pltpu.VMEM((tm, d), F32), pltpu.VMEM((tm, 128), F32)],
        compiler_params=_cparams(("parallel", "arbitrary", "arbitrary")),
        name="moe",
    )(x, gain.reshape(1, d), router_pad, wg, wu, wd)


def _attn_sample_kernel(q_ref, k_ref, v_ref, qg_ref, kg_ref, pavg_ref, bias_ref, bias0_ref, buf_ref,
                        o_ref, l_ref, nb_ref, *, dil):
    nrows = buf_ref.shape[0]
    q = q_ref[...]
    qn = q * lax.rsqrt(_dot_hi(q * q, pavg_ref[...]) + EPS) * qg_ref[...] * (A_DH ** -0.5)
    k = k_ref[...]
    kn = k * lax.rsqrt(_dot_hi(k * k, pavg_ref[...]) + EPS) * kg_ref[...]
    v = v_ref[...]
    chunks = [buf_ref[pl.ds(c, SPAN, stride=dil * LPR), :] for c in range(LPR)]
    kc = jnp.concatenate(chunks[:LPR // 2], axis=1)
    vc = jnp.concatenate(chunks[LPR // 2:], axis=1)
    sel = (lax.broadcasted_iota(jnp.int32, (A_HEADS, A_WIDTH), 1) // A_DH
           == lax.broadcasted_iota(jnp.int32, (A_HEADS, A_WIDTH), 0))
    qbd = jnp.where(sel, jnp.broadcast_to(qn, (A_HEADS, A_WIDTH)), 0.0)
    lc = _dot_nt(qbd, kc) + bias_ref[...]
    l0 = jnp.sum(_r16(qbd) * _r16(kn), axis=1, keepdims=True) + bias0_ref[...]
    m = jnp.maximum(jnp.max(lc, axis=1, keepdims=True), l0)
    pc = jnp.exp(lc - m)
    p0 = jnp.exp(l0 - m)
    s = jnp.sum(pc, axis=1, keepdims=True) + p0
    o8 = (_dot(pc, vc) + _r16(p0) * _r16(v)) / s
    o_ref[...] = jnp.sum(jnp.where(sel, o8, 0.0), axis=0, keepdims=True)
    l_ref[...] = jnp.sum(jnp.where(sel, m + jnp.log(s), 0.0), axis=0, keepdims=True)
    nb_ref[0:nrows - LPR, :] = buf_ref[LPR:nrows, :]
    new = jnp.concatenate([kn, v], axis=1)
    nb_ref[nrows - LPR:nrows, :] = jnp.concatenate(
        [new[:, 128 * c:128 * (c + 1)] for c in range(LPR)], axis=0)


def _attn_sample(zs3, buf, g, qgain, kgain, pavg, bias, bias0):
    bd, nrows, _ = buf.shape
    dil = A_GROUPS[g][1]
    aw = A_WIDTH
    const = lambda shape: pl.BlockSpec(shape, lambda i: (0,) * len(shape))
    vec = pl.BlockSpec((None, 1, aw), lambda i: (i, 0, 0))
    return pl.pallas_call(
        functools.partial(_attn_sample_kernel, dil=dil),
        grid=(bd,),
        in_specs=[pl.BlockSpec((None, 1, aw), lambda i: (i, 0, Z_AQ // aw + g)),
                  pl.BlockSpec((None, 1, aw), lambda i: (i, 0, Z_AK // aw + g)),
                  pl.BlockSpec((None, 1, aw), lambda i: (i, 0, Z_AV // aw + g)),
                  const((1, aw)), const((1, aw)), const((aw, aw)),
                  const((A_HEADS, SPAN)), const((A_HEADS, 1)),
                  pl.BlockSpec((None, nrows, 128), lambda i: (i, 0, 0))],
        out_specs=[vec, vec, pl.BlockSpec((None, nrows, 128), lambda i: (i, 0, 0))],
        out_shape=[jax.ShapeDtypeStruct((bd, 1, aw), F32), jax.ShapeDtypeStruct((bd, 1, aw), F32),
                   jax.ShapeDtypeStruct((bd, nrows, 128), F32)],
        compiler_params=_cparams(("parallel",)),
        name=f"attn_sample_{g}",
    )(zs3, zs3, zs3, qgain, kgain, pavg, bias, bias0, buf)


def _to_col(row, n):
    eye = lax.broadcasted_iota(jnp.int32, (n, n), 0) == lax.broadcasted_iota(jnp.int32, (n, n), 1)
    return jnp.sum(jnp.where(eye, jnp.broadcast_to(row, (n, n)), 0.0), axis=1, keepdims=True)


def _sample_mix_kernel(bq_ref, bk_ref, bv_ref, cqkv_ref, sm_ref, cbuf_ref, sgla_ref, sgdn_ref,
                       o1_ref, l1_ref, o2_ref, l2_ref, o3_ref, l3_ref,
                       wlr_ref, lrb_ref, cw_ref, dtb_ref, alog_ref,
                       oa_ref, ob_ref, oc_ref, nconv_ref, ngla_ref, ngdn_ref):
    l_1, l_2, l_3 = l1_ref[...], l2_ref[...], l3_ref[...]
    mx = jnp.maximum(jnp.maximum(l_1, l_2), l_3)
    e_1, e_2, e_3 = jnp.exp(l_1 - mx), jnp.exp(l_2 - mx), jnp.exp(l_3 - mx)
    oa_ref[...] = (e_1 * o1_ref[...] + e_2 * o2_ref[...] + e_3 * o3_ref[...]) / (e_1 + e_2 + e_3)

    sm = sm_ref[...]
    la = _log_sigmoid(_dot(sm, wlr_ref[...]) + lrb_ref[...]) * (1.0 / B_TAU)
    q = bq_ref[...] * (B_DK ** -0.5)
    k = bk_ref[...]
    v = _r16(bv_ref[...])
    for h in range(B_HEADS):
        lk = slice(B_DK * h, B_DK * (h + 1))
        lv = slice(B_DV * h, B_DV * (h + 1))
        s_old = sgla_ref[h]
        decay = jnp.exp(la[:, lk])
        ngla_ref[h] = _to_col(decay, B_DK) * s_old + _to_col(_r16(k[:, lk]), B_DK) * v[:, lv]
        att = jnp.sum(q[:, lk] * k[:, lk], axis=-1, keepdims=True)
        ob_ref[:, lv] = (jnp.sum(_to_col(_r16(q[:, lk] * decay), B_DK) * _r16(s_old), axis=0, keepdims=True)
                         + _r16(att) * v[:, lv])

    cat = jnp.concatenate([cbuf_ref[...], cqkv_ref[...]], axis=0)
    acc = None
    for j in range(CONV_W):
        term = cat[j:j + 1, :] * cw_ref[j:j + 1, :]
        acc = term if acc is None else acc + term
    y = _silu(acc)
    nconv_ref[...] = cat[1:CONV_W, :]
    hd = C_HEADS * C_DK
    for h in range(C_HEADS):
        ld = slice(C_DK * h, C_DK * (h + 1))
        qh = y[:, C_DK * h:C_DK * (h + 1)]
        qh = qh * lax.rsqrt(jnp.sum(qh * qh, axis=-1, keepdims=True) + EPS) * (C_DK ** -0.5)
        kh = y[:, hd + C_DK * h:hd + C_DK * (h + 1)]
        kh = kh * lax.rsqrt(jnp.sum(kh * kh, axis=-1, keepdims=True) + EPS)
        vh = y[:, 2 * hd + C_DV * h:2 * hd + C_DV * (h + 1)]
        beta = _sigmoid(sm[:, SM_CB + h:SM_CB + h + 1])
        g = -jnp.exp(alog_ref[:, ld]) * _softplus(sm[:, SM_CA + h:SM_CA + h + 1] + dtb_ref[:, ld])
        eg = jnp.exp(g)
        s = sgdn_ref[h]
        s16 = _r16(s)
        qh, kh = _r16(qh), _r16(kh)
        kcol = _to_col(kh, C_DK)
        u = beta * (vh - eg * jnp.sum(kcol * s16, axis=0, keepdims=True))
        qs = jnp.sum(_to_col(qh, C_DK) * s16, axis=0, keepdims=True)
        u16 = _r16(u)
        oc_ref[:, ld] = eg * qs + _r16(jnp.sum(qh * kh, axis=-1, keepdims=True)) * u16
        ngdn_ref[h] = eg * s + kcol * u16


def _sample_mix(zs3, cbuf, sgla, sgdn, attn, wlr_pad, lr_bias, conv_w, dtb_bc, alog_bc):
    bd = zs3.shape[0]
    hk, hv, hd = B_HEADS * B_DK, B_HEADS * B_DV, C_HEADS * C_DV
    const = lambda shape: pl.BlockSpec(shape, lambda i: (0,) * len(shape))
    zrow = lambda width, off: pl.BlockSpec((None, 1, width), lambda i: (i, 0, off // width))
    vec = lambda width: pl.BlockSpec((None, 1, width), lambda i: (i, 0, 0))
    st = lambda shape: pl.BlockSpec((None,) + shape, lambda i: (i,) + (0,) * len(shape))
    return pl.pallas_call(
        _sample_mix_kernel,
        grid=(bd,),
        in_specs=[zrow(hk, Z_BQ), zrow(hk, Z_BK), zrow(hv, Z_BV), zrow(C_CONV_CH, Z_CQKV), zrow(SMALL_W, Z_SMALL),
                  st((CONV_W - 1, C_CONV_CH)), st((B_HEADS, B_DK, B_DV)), st((C_HEADS, C_DK, C_DV))]
                 + [vec(A_WIDTH)] * 6
                 + [const((SMALL_W, hk)), const((1, hk)), const((CONV_W, C_CONV_CH)), const((1, hd)), const((1, hd))],
        out_specs=[vec(A_WIDTH), vec(hv), vec(hd), st((CONV_W - 1, C_CONV_CH)),
                   st((B_HEADS, B_DK, B_DV)), st((C_HEADS, C_DK, C_DV))],
        out_shape=[jax.ShapeDtypeStruct((bd, 1, A_WIDTH), F32), jax.ShapeDtypeStruct((bd, 1, hv), F32),
                   jax.ShapeDtypeStruct((bd, 1, hd), F32), jax.ShapeDtypeStruct((bd, CONV_W - 1, C_CONV_CH), F32),
                   jax.ShapeDtypeStruct((bd, B_HEADS, B_DK, B_DV), F32),
                   jax.ShapeDtypeStruct((bd, C_HEADS, C_DK, C_DV), F32)],
        compiler_params=_cparams(("parallel",)),
        name="sample_mix",
    )(zs3, zs3, zs3, zs3, zs3, cbuf, sgla, sgdn, *attn, wlr_pad, lr_bias, conv_w, dtb_bc, alog_bc)


def _t5_bucket(dist):
    max_exact = N_BUCKETS // 2
    d = jnp.maximum(dist.astype(F32), 1.0)
    large = max_exact + (jnp.log(d / max_exact) / math.log(MAX_DIST / max_exact)
                         * (N_BUCKETS - max_exact)).astype(jnp.int32)
    large = jnp.minimum(large, N_BUCKETS - 1)
    return jnp.where(dist < max_exact, dist, large).astype(jnp.int32)


def _bias_tables(rel_bias):
    i = jnp.arange(SPAN)[:, None]
    c = jnp.arange(2 * SPAN)[None, :]
    dist = i + SPAN - c
    valid = (dist >= 0) & (dist <= SPAN)
    prompt, cached, new = [], [], []
    for g, (_, dil) in enumerate(A_GROUPS):
        bias_g = rel_bias[:, g * A_HEADS:(g + 1) * A_HEADS]
        tbl = bias_g[_t5_bucket(jnp.maximum(dist, 0) * dil)].transpose(2, 0, 1).astype(F32)
        prompt.append(jnp.where(valid[None], tbl, NEG))
        j = SPAN - jnp.arange(SPAN)
        cached.append(bias_g[_t5_bucket(dil * j)].T.astype(F32))
        new.append(bias_g[_t5_bucket(jnp.zeros((1,), jnp.int32))].T.astype(F32))
    return jnp.stack(prompt), jnp.stack(cached), jnp.stack(new)


def _prep_w_in(w):
    offs = np.concatenate([[0], np.cumsum(IN_SIZES)])
    seg = lambda i: w[:, offs[i]:offs[i + 1]]
    order = (0, 1, 2, 3, 4, 5, 6, 8, 9, 12, 7, 10, 11)
    used = sum(IN_SIZES)
    parts = [seg(i) for i in order] + [jnp.zeros((w.shape[0], NZ - used), w.dtype)]
    return jnp.concatenate(parts, axis=1).astype(BF16)


def _selector(offset):
    sel = np.zeros((SMALL_W, C_HEADS * C_DV), np.float32)
    for h in range(C_HEADS):
        sel[offset + h, C_DV * h:C_DV * (h + 1)] = 1.0
    return jnp.asarray(sel)


def _block_avg(width, group):
    idx = np.arange(width) // group
    return jnp.asarray((idx[:, None] == idx[None, :]).astype(np.float32) / group)


def kernel(x_prompt, x_sample, cache_win128_kv, cache_win512_kv, cache_win2048_kv, state_gla, state_gdn, state_conv, norm_mix, w_in, a_q_norm, a_k_norm, rel_bias, b_w_lr, b_lr_bias, b_out_norm, c_conv, c_a_log, c_dt_bias, c_out_norm, w_branch, w_out, norm_ffn, ffn_w_gate, ffn_w_up, ffn_w_down, moe_router, moe_w_gate, moe_w_up, moe_w_down):
    bp, t, d = x_prompt.shape
    bs = x_sample.shape[0]
    mp = bp * t
    caches = (cache_win128_kv, cache_win512_kv, cache_win2048_kv)
    bias_p, bias_c, bias_n = _bias_tables(rel_bias)
    pavg128 = _block_avg(2 * A_DH, A_DH)
    pavg512 = _block_avg(A_WIDTH, A_DH)
    selb, sela = _selector(SM_CB), _selector(SM_CA)

    xp = x_prompt.reshape(mp, d)
    xs = x_sample.reshape(bs, d)
    outs_p = {k: [] for k in ("w0", "w1", "w2", "gla", "gdn", "conv")}
    outs_s = {k: [] for k in ("w0", "w1", "w2", "gla", "gdn", "conv")}
    for l in range(DEPTH):
        w_in_l = _prep_w_in(w_in[l])
        wb = w_branch[l].astype(BF16)
        wo = w_out[l].astype(BF16)
        qg128 = jnp.tile(a_q_norm[l], 2).reshape(1, 2 * A_DH)
        kg128 = jnp.tile(a_k_norm[l], 2).reshape(1, 2 * A_DH)
        qg512 = jnp.tile(a_q_norm[l], A_HEADS).reshape(1, A_WIDTH)
        kg512 = jnp.tile(a_k_norm[l], A_HEADS).reshape(1, A_WIDTH)
        wlr_pad = jnp.zeros((SMALL_W, B_HEADS * B_DK), F32).at[:B_RANK].set(b_w_lr[l])
        lr_bias = b_lr_bias[l].reshape(1, -1)
        bon = jnp.tile(b_out_norm[l], B_HEADS).reshape(1, -1)
        con = jnp.tile(c_out_norm[l], C_HEADS).reshape(1, -1)
        dtb_bc = jnp.repeat(c_dt_bias[l], C_DV).reshape(1, -1)
        alog_bc = jnp.repeat(c_a_log[l], C_DV).reshape(1, -1)

        z = _norm_matmul(xp, norm_mix[l], w_in_l, tm=1024, tn=1280)
        z3 = z.reshape(bp, t, NZ)
        o_a, kn0, kn1, kn2 = _attn_prompt(z3, qg128, kg128, pavg128, bias_p)
        o_b, gla_t = _gla_prompt(z3, wlr_pad, lr_bias, tb=512)
        o_c, gdn_s = _gdn_prompt(z3, c_conv[l], selb, sela, dtb_bc, alog_bc, tb=512)
        xp = _merge(z, o_a.reshape(mp, -1), o_b.reshape(mp, -1), o_c.reshape(mp, -1), xp, bon, con, wb, wo, tm=256)
        for g, (kn, (win, _)) in enumerate(zip((kn0, kn1, kn2), A_GROUPS)):
            wp = min(win, t)
            kk = kn[:, t - wp:].reshape(bp, wp, A_HEADS, A_DH)
            vv = z3[:, t - wp:, Z_AV + g * A_WIDTH:Z_AV + (g + 1) * A_WIDTH].reshape(bp, wp, A_HEADS, A_DH)
            outs_p[f"w{g}"].append(jnp.stack([kk, vv], axis=2))
        gla = gla_t.reshape(bp, B_HEADS, B_DV, B_HEADS, B_DK)
        gla = jnp.stack([gla[:, h, :, h, :] for h in range(B_HEADS)], axis=1)
        outs_p["gla"].append(jnp.swapaxes(gla, 2, 3))
        outs_p["gdn"].append(gdn_s)
        outs_p["conv"].append(z3[:, t - (CONV_W - 1):, Z_CQKV:Z_CQKV + C_CONV_CH])

        zs = _norm_matmul(xs, norm_mix[l], w_in_l, tm=bs, tn=1280)
        zs3 = zs.reshape(bs, 1, NZ)
        attn = []
        for g in range(N_GROUPS):
            buf = caches[g][l].reshape(bs, caches[g].shape[2] * LPR, 128)
            o_g, l_g, nbuf = _attn_sample(zs3, buf, g, qg512, kg512, pavg512, bias_c[g], bias_n[g])
            attn += [o_g, l_g]
            outs_s[f"w{g}"].append(nbuf.reshape(caches[g].shape[1:]))
        oa_s, ob_s, oc_s, nconv, ngla, ngdn = _sample_mix(
            zs3, state_conv[l], state_gla[l], state_gdn[l], attn, wlr_pad, lr_bias, c_conv[l], dtb_bc, alog_bc)
        xs = _merge(zs, oa_s.reshape(bs, -1), ob_s.reshape(bs, -1), oc_s.reshape(bs, -1), xs, bon, con, wb, wo, tm=bs)
        outs_s["gla"].append(ngla)
        outs_s["gdn"].append(ngdn)
        outs_s["conv"].append(nconv)

        i = l // 2
        if l % 2 == 0:
            wg, wu, wd = ffn_w_gate[i].astype(BF16), ffn_w_up[i].astype(BF16), ffn_w_down[i].astype(BF16)
            xp = _ffn(xp, norm_ffn[l], wg, wu, wd, tm=512, tf=1408)
            xs = _ffn(xs, norm_ffn[l], wg, wu, wd, tm=bs, tf=1408)
        else:
            wg, wu, wd = moe_w_gate[i].astype(BF16), moe_w_up[i].astype(BF16), moe_w_down[i].astype(BF16)
            router_pad = jnp.zeros((d, 128), F32).at[:, :N_EXPERTS].set(moe_router[i])
            xp = _moe(xp, norm_ffn[l], router_pad, wg, wu, wd, tm=512, tf=896)
            xs = _moe(xs, norm_ffn[l], router_pad, wg, wu, wd, tm=bs, tf=896)

    st = lambda name, d_: jnp.stack(d_[name])
    return (xp.reshape(bp, t, d), xs.reshape(bs, 1, d),
            st("w0", outs_p), st("w0", outs_s), st("w1", outs_p), st("w1", outs_s), st("w2", outs_p), st("w2", outs_s),
            st("gla", outs_p), st("gla", outs_s), st("gdn", outs_p), st("gdn", outs_s),
            st("conv", outs_p), st("conv", outs_s))
```

```python
import functools
import math

import jax
import jax.numpy as jnp
import numpy as np
from jax import lax
from jax.experimental import pallas as pl
from jax.experimental.pallas import tpu as pltpu

F32 = jnp.float32
BF16 = jnp.bfloat16
HI = lax.Precision.HIGHEST
NEG = -1e30

D_MODEL = 1024
DEPTH = 2
EPS = 1e-6
A_GROUPS = ((128, 1), (512, 4), (2048, 16))
N_GROUPS = 3
A_HEADS = 8
A_DH = 64
A_WIDTH = A_HEADS * A_DH
SPAN = 128
LPR = 2 * A_WIDTH // 128
N_BUCKETS = 32
MAX_DIST = 2048
B_HEADS = 4
B_DK = 64
B_DV = 128
B_RANK = 16
B_TAU = 16.0
C_HEADS = 4
C_DK = 128
C_DV = 128
CONV_W = 4
C_CONV_CH = 2 * C_HEADS * C_DK + C_HEADS * C_DV
CHUNK = 64
SUB = 16
BRANCH_W = 512
D_FF = 2816
N_EXPERTS = 8
D_FF_EXPERT = 1792
IN_SIZES = (N_GROUPS * A_WIDTH, N_GROUPS * A_WIDTH, N_GROUPS * A_WIDTH,
            B_HEADS * B_DK, B_HEADS * B_DK, B_HEADS * B_DV, B_HEADS * B_DV, B_RANK,
            C_CONV_CH, C_HEADS * C_DV, C_HEADS, C_HEADS, 3 * D_MODEL)

Z_AQ, Z_AK, Z_AV = 0, 1536, 3072
Z_BQ, Z_BK, Z_BV, Z_BG = 4608, 4864, 5120, 5632
Z_CQKV, Z_CZ, Z_GATES, Z_SMALL = 6144, 7680, 8192, 11264
SMALL_W = 256
NZ = Z_SMALL + SMALL_W
SM_CB, SM_CA = B_RANK, B_RANK + C_HEADS

VMEM_LIMIT = 56 * 1024 * 1024


def _cparams(sem):
    return pltpu.CompilerParams(dimension_semantics=sem, vmem_limit_bytes=VMEM_LIMIT)


def _dot(a, b):
    return jnp.dot(a.astype(BF16), b.astype(BF16), preferred_element_type=F32)


def _dot_nt(a, b):
    return lax.dot_general(a.astype(BF16), b.astype(BF16), (((1,), (1,)), ((), ())), preferred_element_type=F32)


def _dot_tn(a, b):
    return lax.dot_general(a.astype(BF16), b.astype(BF16), (((0,), (0,)), ((), ())), preferred_element_type=F32)


def _dot_hi(a, b):
    return jnp.dot(a, b, precision=HI, preferred_element_type=F32)


def _r16(x):
    return x.astype(BF16).astype(F32)


def _sigmoid(x):
    return jax.nn.sigmoid(x)


def _silu(x):
    return x * jax.nn.sigmoid(x)


def _softplus(x):
    return jnp.maximum(x, 0.0) + jnp.log1p(jnp.exp(-jnp.abs(x)))


def _log_sigmoid(x):
    return jnp.minimum(x, 0.0) - jnp.log1p(jnp.exp(-jnp.abs(x)))


def _rms(x, gain):
    return x * lax.rsqrt(jnp.mean(x * x, axis=-1, keepdims=True) + EPS) * gain


def _norm_matmul_kernel(x_ref, g_ref, w_ref, o_ref, h_ref):
    @pl.when(pl.program_id(1) == 0)
    def _():
        h_ref[...] = _rms(x_ref[...], g_ref[...]).astype(BF16)

    o_ref[...] = jnp.dot(h_ref[...], w_ref[...], preferred_element_type=F32)


def _norm_matmul(x, gain, w, tm, tn):
    m, d = x.shape
    n = w.shape[1]
    return pl.pallas_call(
        _norm_matmul_kernel,
        grid=(m // tm, n // tn),
        in_specs=[pl.BlockSpec((tm, d), lambda i, j: (i, 0)),
                  pl.BlockSpec((1, d), lambda i, j: (0, 0)),
                  pl.BlockSpec((d, tn), lambda i, j: (0, j))],
        out_specs=pl.BlockSpec((tm, tn), lambda i, j: (i, j)),
        out_shape=jax.ShapeDtypeStruct((m, n), F32),
        scratch_shapes=[pltpu.VMEM((tm, d), BF16)],
        compiler_params=_cparams(("parallel", "arbitrary")),
        name="norm_matmul",
    )(x, gain.reshape(1, d), w)


def _attn_prompt_kernel(q1, k1, v1, q2, k2, v2, q3, k3, v3, qg_ref, kg_ref, pavg_ref, bias_ref,
                        o_ref, kn1, kn2, kn3, qt, qs, ks, vs, og, lg, to, tl):
    t_len = o_ref.shape[0]
    head0 = lax.broadcasted_iota(jnp.int32, (1, 2 * A_DH), 1) < A_DH
    first_half = lax.broadcasted_iota(jnp.int32, (1, 2 * SPAN), 1) < SPAN
    groups = ((q1, k1, v1, kn1), (q2, k2, v2, kn2), (q3, k3, v3, kn3))
    for g, (qr, kr, vr, knr) in enumerate(groups):
        dil = A_GROUPS[g][1]
        length = t_len // dil
        nb = length // SPAN
        q = qr[...]
        qt[...] = q * lax.rsqrt(_dot_hi(q * q, pavg_ref[...]) + EPS) * qg_ref[...] * (A_DH ** -0.5)
        k = kr[...]
        knr[...] = k * lax.rsqrt(_dot_hi(k * k, pavg_ref[...]) + EPS) * kg_ref[...]
        for r in range(dil):
            rows = pl.ds(r, length, stride=dil) if dil > 1 else pl.ds(0, length)
            dst = pl.ds(r * length, length)
            qs[dst, :] = qt[rows, :].astype(BF16)
            ks[dst, :] = knr[rows, :].astype(BF16)
            vs[dst, :] = vr[rows, :].astype(BF16)

        def block(j, carry, g=g, nb=nb):
            row0 = pl.multiple_of(j * SPAN, SPAN)
            qb = qs[pl.ds(row0, SPAN), :]
            if nb > 1:
                prow = pl.multiple_of(jnp.maximum(row0 - SPAN, 0), SPAN)
                k2 = jnp.concatenate([ks[pl.ds(prow, SPAN), :], ks[pl.ds(row0, SPAN), :]], axis=0)
                v2 = jnp.concatenate([vs[pl.ds(prow, SPAN), :], vs[pl.ds(row0, SPAN), :]], axis=0)
                pen = jnp.where(first_half, jnp.where(j % nb == 0, NEG, 0.0), 0.0)
            else:
                k2 = ks[pl.ds(row0, SPAN), :]
                v2 = vs[pl.ds(row0, SPAN), :]
            outs, lses = [], []
            for h in range(2):
                qh = jnp.where(head0 if h == 0 else jnp.logical_not(head0), qb, jnp.zeros_like(qb))
                logits = lax.dot_general(qh, k2, (((1,), (1,)), ((), ())), preferred_element_type=F32)
                if nb > 1:
                    logits = logits + bias_ref[g, h] + pen
                else:
                    logits = logits + bias_ref[g, h, :, SPAN:]
                m = jnp.max(logits, axis=-1, keepdims=True)
                p = jnp.exp(logits - m)
                s = jnp.sum(p, axis=-1, keepdims=True)
                acc = jnp.dot(p.astype(BF16), v2, preferred_element_type=F32)
                outs.append(acc / s)
                lses.append(jnp.broadcast_to(m + jnp.log(s), (SPAN, 2 * A_DH)))
            to[pl.ds(row0, SPAN), :] = jnp.where(head0, outs[0], outs[1])
            tl[pl.ds(row0, SPAN), :] = jnp.where(head0, lses[0], lses[1])
            return carry

        lax.fori_loop(0, t_len // SPAN, block, 0)
        for r in range(dil):
            rows = pl.ds(r, length, stride=dil) if dil > 1 else pl.ds(0, length)
            src = pl.ds(r * length, length)
            og[g, rows, :] = to[src, :]
            lg[g, rows, :] = tl[src, :]
    l_1, l_2, l_3 = lg[0], lg[1], lg[2]
    mx = jnp.maximum(jnp.maximum(l_1, l_2), l_3)
    e_1, e_2, e_3 = jnp.exp(l_1 - mx), jnp.exp(l_2 - mx), jnp.exp(l_3 - mx)
    o_ref[...] = (e_1 * og[0] + e_2 * og[1] + e_3 * og[2]) / (e_1 + e_2 + e_3)


def _attn_prompt(z3, qgain, kgain, pavg, bias):
    b, t, _ = z3.shape
    hp_blocks = A_WIDTH // 128

    def col(base, g):
        return lambda i, hp: (i, 0, base // 128 + g * hp_blocks + hp)

    in_specs = []
    for g in range(N_GROUPS):
        for base in (Z_AQ, Z_AK, Z_AV):
            in_specs.append(pl.BlockSpec((None, t, 128), col(base, g)))
    in_specs += [pl.BlockSpec((1, 128), lambda i, hp: (0, 0)),
                 pl.BlockSpec((1, 128), lambda i, hp: (0, 0)),
                 pl.BlockSpec((128, 128), lambda i, hp: (0, 0)),
                 pl.BlockSpec((N_GROUPS, 2, SPAN, 2 * SPAN), lambda i, hp: (0, hp, 0, 0))]
    out_spec = pl.BlockSpec((None, t, 128), lambda i, hp: (i, 0, hp))
    shp = jax.ShapeDtypeStruct((b, t, A_WIDTH), F32)
    return pl.pallas_call(
        _attn_prompt_kernel,
        grid=(b, hp_blocks),
        in_specs=in_specs,
        out_specs=[out_spec] * 4,
        out_shape=[shp] * 4,
        scratch_shapes=[pltpu.VMEM((t, 128), F32),
                        pltpu.VMEM((t, 128), BF16), pltpu.VMEM((t, 128), BF16), pltpu.VMEM((t, 128), BF16),
                        pltpu.VMEM((N_GROUPS, t, 128), F32), pltpu.VMEM((N_GROUPS, t, 128), F32),
                        pltpu.VMEM((t, 128), F32), pltpu.VMEM((t, 128), F32)],
        compiler_params=_cparams(("parallel", "parallel")),
        name="attn_prompt",
    )(*([z3] * 9), qgain, kgain, pavg, bias)


def _gla_prompt_kernel(q_ref, k_ref, v_ref, sm_ref, wlr_ref, lrb_ref, o_ref, st_ref, s_scr, la_scr):
    tb = q_ref.shape[0]
    hk = B_HEADS * B_DK
    hv = B_HEADS * B_DV

    @pl.when(pl.program_id(1) == 0)
    def _():
        s_scr[...] = jnp.zeros_like(s_scr)

    la_scr[...] = _log_sigmoid(_dot(sm_ref[...], wlr_ref[...]) + lrb_ref[...]) * (1.0 / B_TAU)

    rr = lax.broadcasted_iota(jnp.int32, (CHUNK, CHUNK), 0)
    cc = lax.broadcasted_iota(jnp.int32, (CHUNK, CHUNK), 1)
    causal = rr >= cc
    ltri = causal.astype(F32)
    lane_head = lax.broadcasted_iota(jnp.int32, (1, hk), 1) // B_DK
    row = lax.broadcasted_iota(jnp.int32, (CHUNK, 1), 0)
    same_head = (lax.broadcasted_iota(jnp.int32, (hv, hk), 0) // B_DV
                 == lax.broadcasted_iota(jnp.int32, (hv, hk), 1) // B_DK)

    def chunk(c, carry):
        r0 = pl.multiple_of(c * CHUNK, CHUNK)
        b = _dot_hi(ltri, la_scr[pl.ds(r0, CHUNK), :])
        q = q_ref[pl.ds(r0, CHUNK), :] * (B_DK ** -0.5)
        k = k_ref[pl.ds(r0, CHUNK), :]
        vb = v_ref[pl.ds(r0, CHUNK), :].astype(BF16)
        s = s_scr[...]
        b_last = b[CHUNK - 1:CHUNK, :]
        o_inter = _dot_nt(q * jnp.exp(b), s)
        atts = []
        for i in range(CHUNK // SUB):
            b_ref0 = b[SUB * i:SUB * i + 1, :]
            qi = q[SUB * i:SUB * (i + 1), :] * jnp.exp(b[SUB * i:SUB * (i + 1), :] - b_ref0)
            qst = jnp.concatenate([jnp.where(lane_head == h, qi, 0.0) for h in range(B_HEADS)], axis=0)
            kt = k * jnp.exp(jnp.where(row < SUB * (i + 1), b_ref0 - b, 0.0))
            atts.append(_dot_nt(qst, kt))
        o_parts = []
        for h in range(B_HEADS):
            att = jnp.concatenate([a[SUB * h:SUB * (h + 1), :] for a in atts], axis=0)
            att = jnp.where(causal, att, 0.0)
            o_parts.append(_dot(att, vb[:, B_DV * h:B_DV * (h + 1)]))
        o_ref[pl.ds(r0, CHUNK), :] = o_inter + jnp.concatenate(o_parts, axis=1)
        upd = _dot_tn(vb, k * jnp.exp(b_last - b))
        s_scr[...] = s * jnp.exp(b_last) + jnp.where(same_head, upd, 0.0)
        return carry

    lax.fori_loop(0, tb // CHUNK, chunk, 0)

    @pl.when(pl.program_id(1) == pl.num_programs(1) - 1)
    def _():
        st_ref[...] = s_scr[...]


def _gla_prompt(z3, wlr_pad, lr_bias, tb):
    b, t, _ = z3.shape
    hk, hv = B_HEADS * B_DK, B_HEADS * B_DV
    return pl.pallas_call(
        _gla_prompt_kernel,
        grid=(b, t // tb),
        in_specs=[pl.BlockSpec((None, tb, hk), lambda i, j: (i, j, Z_BQ // hk)),
                  pl.BlockSpec((None, tb, hk), lambda i, j: (i, j, Z_BK // hk)),
                  pl.BlockSpec((None, tb, hv), lambda i, j: (i, j, Z_BV // hv)),
                  pl.BlockSpec((None, tb, SMALL_W), lambda i, j: (i, j, Z_SMALL // SMALL_W)),
                  pl.BlockSpec((SMALL_W, hk), lambda i, j: (0, 0)),
                  pl.BlockSpec((1, hk), lambda i, j: (0, 0))],
        out_specs=[pl.BlockSpec((None, tb, hv), lambda i, j: (i, j, 0)),
                   pl.BlockSpec((None, hv, hk), lambda i, j: (i, 0, 0))],
        out_shape=[jax.ShapeDtypeStruct((b, t, hv), F32), jax.ShapeDtypeStruct((b, hv, hk), F32)],
        scratch_shapes=[pltpu.VMEM((hv, hk), F32), pltpu.VMEM((tb, hk), F32)],
        compiler_params=_cparams(("parallel", "arbitrary")),
        name="gla_prompt",
    )(z3, z3, z3, z3, wlr_pad, lr_bias)


def _gdn_prompt_kernel(x_ref, sm_ref, cw_ref, selb_ref, sela_ref, dtb_ref, alog_ref, o_ref, s_ref,
                       xpad, cv, gsc, bsc, s_scr):
    tb = x_ref.shape[0]
    hd = C_HEADS * C_DK
    pad = 8

    @pl.when(pl.program_id(1) == 0)
    def _():
        s_scr[...] = jnp.zeros_like(s_scr)
        xpad[0:pad, :] = jnp.zeros((pad, C_CONV_CH), F32)

    xpad[pad:pad + tb, :] = x_ref[...]
    rb = 128
    for i in range(tb // rb):
        acc = None
        for j in range(CONV_W):
            lo = pad - (CONV_W - 1) + j + i * rb
            term = xpad[lo:lo + rb, :] * cw_ref[j:j + 1, :]
            acc = term if acc is None else acc + term
        y = _silu(acc)
        rows = slice(i * rb, (i + 1) * rb)
        for h in range(C_HEADS):
            qh = y[:, C_DK * h:C_DK * (h + 1)]
            cv[rows, C_DK * h:C_DK * (h + 1)] = (
                qh * lax.rsqrt(jnp.sum(qh * qh, axis=-1, keepdims=True) + EPS) * (C_DK ** -0.5))
            kh = y[:, hd + C_DK * h:hd + C_DK * (h + 1)]
            cv[rows, hd + C_DK * h:hd + C_DK * (h + 1)] = (
                kh * lax.rsqrt(jnp.sum(kh * kh, axis=-1, keepdims=True) + EPS))
        cv[rows, 2 * hd:] = y[:, 2 * hd:]
    xpad[pad - (CONV_W - 1):pad, :] = xpad[pad + tb - (CONV_W - 1):pad + tb, :]

    sm = sm_ref[...]
    gsc[...] = -jnp.exp(alog_ref[...]) * _softplus(_dot_hi(sm, sela_ref[...]) + dtb_ref[...])
    bsc[...] = _sigmoid(_dot_hi(sm, selb_ref[...]))

    n = C_HEADS * CHUNK
    rr = lax.broadcasted_iota(jnp.int32, (CHUNK, CHUNK), 0)
    cc = lax.broadcasted_iota(jnp.int32, (CHUNK, CHUNK), 1)
    ltri = (rr >= cc).astype(F32)
    ri = lax.broadcasted_iota(jnp.int32, (n, n), 0)
    ci = lax.broadcasted_iota(jnp.int32, (n, n), 1)
    same = (ri // CHUNK) == (ci // CHUNK)
    incl = jnp.logical_and(same, ci <= ri)
    strict = jnp.logical_and(same, ci < ri)
    eye = (ri == ci).astype(F32)

    def stack(x):
        return jnp.concatenate([x[:, C_DK * h:C_DK * (h + 1)] for h in range(C_HEADS)], axis=0)

    def chunk(c, carry):
        r0 = pl.multiple_of(c * CHUNK, CHUNK)
        bst = stack(_dot_hi(ltri, gsc[pl.ds(r0, CHUNK), :]))
        beta = stack(bsc[pl.ds(r0, CHUNK), :])
        qst = stack(cv[pl.ds(r0, CHUNK), 0:hd])
        kst = stack(cv[pl.ds(r0, CHUNK), hd:2 * hd])
        vst = stack(cv[pl.ds(r0, CHUNK), 2 * hd:3 * hd])
        kb = kst.astype(BF16)
        kk = lax.dot_general(kb, kb, (((1,), (1,)), ((), ())), preferred_element_type=F32)
        qk = lax.dot_general(qst.astype(BF16), kb, (((1,), (1,)), ((), ())), preferred_element_type=F32)
        col_b = jnp.concatenate([bst, bst], axis=1)
        bst_t = bst.T
        row_b = jnp.concatenate([bst_t, bst_t], axis=0)
        dec = jnp.exp(jnp.where(incl, col_b - row_b, NEG))
        a = jnp.where(strict, jnp.concatenate([beta, beta], axis=1) * dec * kk, 0.0)
        aqk = dec * qk
        inv = eye - a
        pw = a
        for _ in range(int(math.log2(CHUNK)) - 1):
            pw = _dot(pw, pw)
            inv = inv + _dot(inv, pw)
        eb = jnp.exp(bst)
        wu = _dot(inv, jnp.concatenate([beta * eb * kst, beta * vst], axis=1))
        w, uv = wu[:, :C_DK], wu[:, C_DK:]
        us, qss = [], []
        for h in range(C_HEADS):
            sl = slice(CHUNK * h, CHUNK * (h + 1))
            ws = _dot(jnp.concatenate([w[sl], qst[sl]], axis=0), s_scr[h])
            us.append(uv[sl] - ws[:CHUNK])
            qss.append(ws[CHUNK:])
        ust = jnp.concatenate(us, axis=0)
        o = eb * jnp.concatenate(qss, axis=0) + _dot(aqk, ust)
        for h in range(C_HEADS):
            sl = slice(CHUNK * h, CHUNK * (h + 1))
            o_ref[pl.ds(r0, CHUNK), C_DV * h:C_DV * (h + 1)] = o[sl]
            b_last = bst[CHUNK * (h + 1) - 1:CHUNK * (h + 1), :]
            s_scr[h] = jnp.exp(b_last) * s_scr[h] + _dot_tn(kst[sl] * jnp.exp(b_last - bst[sl]), us[h])
        return carry

    lax.fori_loop(0, tb // CHUNK, chunk, 0)

    @pl.when(pl.program_id(1) == pl.num_programs(1) - 1)
    def _():
        s_ref[...] = s_scr[...]


def _gdn_prompt(z3, conv_w, selb, sela, dtb_bc, alog_bc, tb):
    b, t, _ = z3.shape
    hd = C_HEADS * C_DV
    const = lambda shape: pl.BlockSpec(shape, lambda i, j: (0,) * len(shape))
    return pl.pallas_call(
        _gdn_prompt_kernel,
        grid=(b, t // tb),
        in_specs=[pl.BlockSpec((None, tb, C_CONV_CH), lambda i, j: (i, j, Z_CQKV // C_CONV_CH)),
                  pl.BlockSpec((None, tb, SMALL_W), lambda i, j: (i, j, Z_SMALL // SMALL_W)),
                  const((CONV_W, C_CONV_CH)), const((SMALL_W, hd)), const((SMALL_W, hd)),
                  const((1, hd)), const((1, hd))],
        out_specs=[pl.BlockSpec((None, tb, hd), lambda i, j: (i, j, 0)),
                   pl.BlockSpec((None, C_HEADS, C_DK, C_DV), lambda i, j: (i, 0, 0, 0))],
        out_shape=[jax.ShapeDtypeStruct((b, t, hd), F32),
                   jax.ShapeDtypeStruct((b, C_HEADS, C_DK, C_DV), F32)],
        scratch_shapes=[pltpu.VMEM((tb + 8, C_CONV_CH), F32), pltpu.VMEM((tb, C_CONV_CH), F32),
                        pltpu.VMEM((tb, hd), F32), pltpu.VMEM((tb, hd), F32),
                        pltpu.VMEM((C_HEADS, C_DK, C_DV), F32)],
        compiler_params=_cparams(("parallel", "arbitrary")),
        name="gdn_prompt",
    )(z3, z3, conv_w, selb, sela, dtb_bc, alog_bc)


def _head_norm(o, gain, width):
    parts = []
    for h in range(o.shape[1] // width):
        oh = o[:, width * h:width * (h + 1)]
        parts.append(oh * lax.rsqrt(jnp.mean(oh * oh, axis=-1, keepdims=True) + EPS))
    return jnp.concatenate(parts, axis=1) * gain


def _merge_kernel(oa_ref, ob_ref, bg_ref, oc_ref, cz_ref, ga_ref, gb_ref, gc_ref, x_ref,
                  bon_ref, con_ref, wb_ref, wo_ref, y_ref):
    ob = _head_norm(ob_ref[...], bon_ref[...], B_DV) * _silu(bg_ref[...])
    oc = _head_norm(oc_ref[...], con_ref[...], C_DV) * _silu(cz_ref[...])
    merged = (_sigmoid(ga_ref[...]) * _dot(oa_ref[...], wb_ref[0])
              + _sigmoid(gb_ref[...]) * _dot(ob, wb_ref[1])
              + _sigmoid(gc_ref[...]) * _dot(oc, wb_ref[2]))
    y_ref[...] = x_ref[...] + _dot(merged, wo_ref[...])


def _merge(z, oa, ob, oc, x, bon, con, wb, wo, tm):
    m = x.shape[0]
    w = BRANCH_W
    row = lambda width, blk: pl.BlockSpec((tm, width), lambda i: (i, blk))
    const = lambda shape: pl.BlockSpec(shape, lambda i: (0,) * len(shape))
    return pl.pallas_call(
        _merge_kernel,
        grid=(m // tm,),
        in_specs=[row(w, 0), row(w, 0), row(w, Z_BG // w), row(w, 0), row(w, Z_CZ // w),
                  row(D_MODEL, Z_GATES // D_MODEL), row(D_MODEL, Z_GATES // D_MODEL + 1),
                  row(D_MODEL, Z_GATES // D_MODEL + 2), row(D_MODEL, 0),
                  const((1, w)), const((1, w)), const((3, w, D_MODEL)), const((D_MODEL, D_MODEL))],
        out_specs=row(D_MODEL, 0),
        out_shape=jax.ShapeDtypeStruct((m, D_MODEL), F32),
        compiler_params=_cparams(("parallel",)),
        name="merge",
    )(oa, ob, z, oc, z, z, z, z, x, bon, con, wb, wo)


def _ffn_kernel(x_ref, g_ref, wg_ref, wu_ref, wd_ref, y_ref, h_ref, acc_ref):
    f = pl.program_id(1)

    @pl.when(f == 0)
    def _():
        h_ref[...] = _rms(x_ref[...], g_ref[...]).astype(BF16)
        acc_ref[...] = jnp.zeros_like(acc_ref)

    h = h_ref[...]
    a = _silu(jnp.dot(h, wg_ref[...], preferred_element_type=F32)) * jnp.dot(h, wu_ref[...],
                                                                              preferred_element_type=F32)
    acc_ref[...] += jnp.dot(a.astype(BF16), wd_ref[...], preferred_element_type=F32)

    @pl.when(f == pl.num_programs(1) - 1)
    def _():
        y_ref[...] = x_ref[...] + acc_ref[...]


def _ffn(x, gain, wg, wu, wd, tm, tf):
    m, d = x.shape
    ff = wg.shape[1]
    return pl.pallas_call(
        _ffn_kernel,
        grid=(m // tm, ff // tf),
        in_specs=[pl.BlockSpec((tm, d), lambda i, f: (i, 0)),
                  pl.BlockSpec((1, d), lambda i, f: (0, 0)),
                  pl.BlockSpec((d, tf), lambda i, f: (0, f)),
                  pl.BlockSpec((d, tf), lambda i, f: (0, f)),
                  pl.BlockSpec((tf, d), lambda i, f: (f, 0))],
        out_specs=pl.BlockSpec((tm, d), lambda i, f: (i, 0)),
        out_shape=jax.ShapeDtypeStruct((m, d), F32),
        scratch_shapes=[pltpu.VMEM((tm, d), BF16), pltpu.VMEM((tm, d), F32)],
        compiler_params=_cparams(("parallel", "arbitrary")),
        name="ffn",
    )(x, gain.reshape(1, d), wg, wu, wd)


def _moe_kernel(x_ref, g_ref, rt_ref, wg_ref, wu_ref, wd_ref, y_ref, h_ref, acc_ref, gate_ref):
    e = pl.program_id(1)
    f = pl.program_id(2)
    lane = lax.broadcasted_iota(jnp.int32, (1, 128), 1).astype(F32)

    @pl.when(jnp.logical_and(e == 0, f == 0))
    def _():
        hf = _rms(x_ref[...], g_ref[...])
        h_ref[...] = hf.astype(BF16)
        acc_ref[...] = jnp.zeros_like(acc_ref)
        valid = lane < N_EXPERTS
        logits = jnp.where(valid, _dot(hf, rt_ref[...]), NEG)
        ex = jnp.exp(logits - jnp.max(logits, axis=-1, keepdims=True))
        probs = ex / jnp.sum(ex, axis=-1, keepdims=True)
        m1 = jnp.max(probs, axis=-1, keepdims=True)
        i1 = jnp.min(jnp.where(jnp.logical_and(probs == m1, valid), lane, 128.0), axis=-1, keepdims=True)
        hot1 = lane == i1
        rest = jnp.where(jnp.logical_or(hot1, jnp.logical_not(valid)), -1.0, probs)
        m2 = jnp.max(rest, axis=-1, keepdims=True)
        i2 = jnp.min(jnp.where(rest == m2, lane, 128.0), axis=-1, keepdims=True)
        hot2 = lane == i2
        den = m1 + m2
        gate_ref[...] = jnp.where(hot1, m1 / den, 0.0) + jnp.where(hot2, m2 / den, 0.0)

    h = h_ref[...]
    ge = jnp.sum(jnp.where(lane == e.astype(F32), gate_ref[...], 0.0), axis=-1, keepdims=True)
    a = (_silu(jnp.dot(h, wg_ref[...], preferred_element_type=F32))
         * jnp.dot(h, wu_ref[...], preferred_element_type=F32) * ge)
    acc_ref[...] += jnp.dot(a.astype(BF16), wd_ref[...], preferred_element_type=F32)

    @pl.when(jnp.logical_and(e == pl.num_programs(1) - 1, f == pl.num_programs(2) - 1))
    def _():
        y_ref[...] = x_ref[...] + acc_ref[...]


def _moe(x, gain, router_pad, wg, wu, wd, tm, tf):
    m, d = x.shape
    ne, _, ff = wg.shape
    return pl.pallas_call(
        _moe_kernel,
        grid=(m // tm, ne, ff // tf),
        in_specs=[pl.BlockSpec((tm, d), lambda i, e, f: (i, 0)),
                  pl.BlockSpec((1, d), lambda i, e, f: (0, 0)),
                  pl.BlockSpec((d, 128), lambda i, e, f: (0, 0)),
                  pl.BlockSpec((None, d, tf), lambda i, e, f: (e, 0, f)),
                  pl.BlockSpec((None, d, tf), lambda i, e, f: (e, 0, f)),
                  pl.BlockSpec((None, tf, d), lambda i, e, f: (e, f, 0))],
        out_specs=pl.BlockSpec((tm, d), lambda i, e, f: (i, 0)),
        out_shape=jax.ShapeDtypeStruct((m, d), F32),
        scratch_shapes=[pltpu.VMEM((tm, d), BF16), pltpu.VMEM((tm, d), F32), pltpu.VMEM((tm, 128), F32)],
        compiler_params=_cparams(("parallel", "arbitrary", "arbitrary")),
        name="moe",
    )(x, gain.reshape(1, d), router_pad, wg, wu, wd)


def _cache_roll_kernel(c0, c1, c2, o0, o1, o2, sem):
    copies = []
    for i, (c, o) in enumerate(((c0, o0), (c1, o1), (c2, o2))):
        w = c.shape[2]
        for l in range(c.shape[0]):
            copies.append(pltpu.make_async_copy(c.at[l, :, pl.ds(1, w - 1)], o.at[l, :, pl.ds(0, w - 1)],
                                                sem.at[i, l, 0]))
            copies.append(pltpu.make_async_copy(c.at[l, :, pl.ds(w - 1, 1)], o.at[l, :, pl.ds(w - 1, 1)],
                                                sem.at[i, l, 1]))
    for cp in copies:
        cp.start()
    for cp in copies:
        cp.wait()


def _cache_roll(caches):
    anyspec = pl.BlockSpec(memory_space=pl.ANY)
    return pl.pallas_call(
        _cache_roll_kernel,
        in_specs=[anyspec] * N_GROUPS,
        out_specs=[anyspec] * N_GROUPS,
        out_shape=[jax.ShapeDtypeStruct(c.shape, c.dtype) for c in caches],
        scratch_shapes=[pltpu.SemaphoreType.DMA((N_GROUPS, DEPTH, 2))],
        name="cache_roll",
    )(*caches)


def _attn_sample_kernel(qkv_ref, qg_ref, kg_ref, bias_ref, bias0_ref, c0_ref, c1_ref, c2_ref,
                        r0_ref, r1_ref, r2_ref, o_ref, n0_ref, n1_ref, n2_ref):
    del r0_ref, r1_ref, r2_ref
    outs, lses = [], []
    for g, (c_ref, n_ref) in enumerate(((c0_ref, n0_ref), (c1_ref, n1_ref), (c2_ref, n2_ref))):
        q = qkv_ref[:, g]
        k = qkv_ref[:, N_GROUPS + g]
        v = qkv_ref[:, 2 * N_GROUPS + g]
        qn = _rms(q, qg_ref[...]) * (A_DH ** -0.5)
        kn = _rms(k, kg_ref[...])
        n_ref[:, 0] = kn
        n_ref[:, 1] = v
        q16 = _r16(qn)
        lc = jnp.sum(q16[:, None] * _r16(c_ref[:, :, 0]), axis=-1, keepdims=True) + bias_ref[g]
        l0 = jnp.sum(q16 * _r16(kn), axis=-1, keepdims=True) + bias0_ref[g]
        m = jnp.maximum(jnp.max(lc, axis=1), l0)
        pc = jnp.exp(lc - m[:, None])
        p0 = jnp.exp(l0 - m)
        s = jnp.sum(pc, axis=1) + p0
        acc = jnp.sum(_r16(pc) * _r16(c_ref[:, :, 1]), axis=1) + _r16(p0) * _r16(v)
        outs.append(acc / s)
        lses.append(m + jnp.log(s))
    mx = jnp.maximum(jnp.maximum(lses[0], lses[1]), lses[2])
    e = [jnp.exp(l - mx) for l in lses]
    o_ref[...] = (e[0] * outs[0] + e[1] * outs[1] + e[2] * outs[2]) / (e[0] + e[1] + e[2])


def _attn_sample(qkv, caches, rolled, layer, qgain, kgain, bias, bias0, bblk):
    bd = qkv.shape[0]
    const = lambda shape: pl.BlockSpec(shape, lambda i: (0,) * len(shape))
    kv = (2, A_HEADS, A_DH)
    cache_specs, new_specs, views = [], [], []
    for c, (win, dil) in zip(caches, A_GROUPS):
        views.append(c.reshape(c.shape[:2] + (SPAN, dil) + kv))
        cache_specs.append(pl.BlockSpec((None, bblk, SPAN, None) + kv, lambda i: (layer, i, 0, 0, 0, 0, 0)))
        new_specs.append(pl.BlockSpec((None, bblk, None) + kv, functools.partial(
            lambda last, i: (layer, i, last, 0, 0, 0), c.shape[2] - 1)))
    n_in = 5 + N_GROUPS
    return pl.pallas_call(
        _attn_sample_kernel,
        grid=(bd // bblk,),
        in_specs=[pl.BlockSpec((bblk, 3 * N_GROUPS, A_HEADS, A_DH), lambda i: (i, 0, 0, 0)),
                  const((1, A_DH)), const((1, A_DH)),
                  const((N_GROUPS, SPAN, A_HEADS, 1)), const((N_GROUPS, A_HEADS, 1))]
                 + cache_specs + [pl.BlockSpec(memory_space=pl.ANY)] * N_GROUPS,
        out_specs=[pl.BlockSpec((bblk, A_HEADS, A_DH), lambda i: (i, 0, 0))] + new_specs,
        out_shape=[jax.ShapeDtypeStruct((bd, A_HEADS, A_DH), F32)]
                  + [jax.ShapeDtypeStruct(r.shape, r.dtype) for r in rolled],
        input_output_aliases={n_in + g: 1 + g for g in range(N_GROUPS)},
        compiler_params=_cparams(("parallel",)),
        name="attn_sample",
    )(qkv, qgain, kgain, bias, bias0, *views, *rolled)


def _to_col(row, n):
    eye = lax.broadcasted_iota(jnp.int32, (n, n), 0) == lax.broadcasted_iota(jnp.int32, (n, n), 1)
    return jnp.sum(jnp.where(eye, jnp.broadcast_to(row, (n, n)), 0.0), axis=1, keepdims=True)


def _sample_mix_kernel(bq_ref, bk_ref, bv_ref, cqkv_ref, sm_ref, cbuf_ref, sgla_ref, sgdn_ref,
                       wlr_ref, lrb_ref, cw_ref, dtb_ref, alog_ref,
                       ob_ref, oc_ref, nconv_ref, ngla_ref, ngdn_ref):
    sm = sm_ref[...]
    la = _log_sigmoid(_dot(sm, wlr_ref[...]) + lrb_ref[...]) * (1.0 / B_TAU)
    q = bq_ref[...] * (B_DK ** -0.5)
    k = bk_ref[...]
    v = _r16(bv_ref[...])
    for h in range(B_HEADS):
        lk = slice(B_DK * h, B_DK * (h + 1))
        lv = slice(B_DV * h, B_DV * (h + 1))
        s_old = sgla_ref[h]
        decay = jnp.exp(la[:, lk])
        ngla_ref[h] = _to_col(decay, B_DK) * s_old + _to_col(_r16(k[:, lk]), B_DK) * v[:, lv]
        att = jnp.sum(q[:, lk] * k[:, lk], axis=-1, keepdims=True)
        ob_ref[:, lv] = (jnp.sum(_to_col(_r16(q[:, lk] * decay), B_DK) * _r16(s_old), axis=0, keepdims=True)
                         + _r16(att) * v[:, lv])

    cat = jnp.concatenate([cbuf_ref[...], cqkv_ref[...]], axis=0)
    acc = None
    for j in range(CONV_W):
        term = cat[j:j + 1, :] * cw_ref[j:j + 1, :]
        acc = term if acc is None else acc + term
    y = _silu(acc)
    nconv_ref[...] = cat[1:CONV_W, :]
    hd = C_HEADS * C_DK
    for h in range(C_HEADS):
        ld = slice(C_DK * h, C_DK * (h + 1))
        qh = y[:, C_DK * h:C_DK * (h + 1)]
        qh = qh * lax.rsqrt(jnp.sum(qh * qh, axis=-1, keepdims=True) + EPS) * (C_DK ** -0.5)
        kh = y[:, hd + C_DK * h:hd + C_DK * (h + 1)]
        kh = kh * lax.rsqrt(jnp.sum(kh * kh, axis=-1, keepdims=True) + EPS)
        vh = y[:, 2 * hd + C_DV * h:2 * hd + C_DV * (h + 1)]
        beta = _sigmoid(sm[:, SM_CB + h:SM_CB + h + 1])
        g = -jnp.exp(alog_ref[:, ld]) * _softplus(sm[:, SM_CA + h:SM_CA + h + 1] + dtb_ref[:, ld])
        eg = jnp.exp(g)
        s = sgdn_ref[h]
        s16 = _r16(s)
        qh, kh = _r16(qh), _r16(kh)
        kcol = _to_col(kh, C_DK)
        u = beta * (vh - eg * jnp.sum(kcol * s16, axis=0, keepdims=True))
        qs = jnp.sum(_to_col(qh, C_DK) * s16, axis=0, keepdims=True)
        u16 = _r16(u)
        oc_ref[:, ld] = eg * qs + _r16(jnp.sum(qh * kh, axis=-1, keepdims=True)) * u16
        ngdn_ref[h] = eg * s + kcol * u16


def _sample_mix(zs3, cbuf, sgla, sgdn, wlr_pad, lr_bias, conv_w, dtb_bc, alog_bc):
    bd = zs3.shape[0]
    hk, hv, hd = B_HEADS * B_DK, B_HEADS * B_DV, C_HEADS * C_DV
    const = lambda shape: pl.BlockSpec(shape, lambda i: (0,) * len(shape))
    zrow = lambda width, off: pl.BlockSpec((None, 1, width), lambda i: (i, 0, off // width))
    vec = lambda width: pl.BlockSpec((None, 1, width), lambda i: (i, 0, 0))
    st = lambda shape: pl.BlockSpec((None,) + shape, lambda i: (i,) + (0,) * len(shape))
    return pl.pallas_call(
        _sample_mix_kernel,
        grid=(bd,),
        in_specs=[zrow(hk, Z_BQ), zrow(hk, Z_BK), zrow(hv, Z_BV), zrow(C_CONV_CH, Z_CQKV), zrow(SMALL_W, Z_SMALL),
                  st((CONV_W - 1, C_CONV_CH)), st((B_HEADS, B_DK, B_DV)), st((C_HEADS, C_DK, C_DV))]
                 + [const((SMALL_W, hk)), const((1, hk)), const((CONV_W, C_CONV_CH)), const((1, hd)), const((1, hd))],
        out_specs=[vec(hv), vec(hd), st((CONV_W - 1, C_CONV_CH)),
                   st((B_HEADS, B_DK, B_DV)), st((C_HEADS, C_DK, C_DV))],
        out_shape=[jax.ShapeDtypeStruct((bd, 1, hv), F32),
                   jax.ShapeDtypeStruct((bd, 1, hd), F32), jax.ShapeDtypeStruct((bd, CONV_W - 1, C_CONV_CH), F32),
                   jax.ShapeDtypeStruct((bd, B_HEADS, B_DK, B_DV), F32),
                   jax.ShapeDtypeStruct((bd, C_HEADS, C_DK, C_DV), F32)],
        compiler_params=_cparams(("parallel",)),
        name="sample_mix",
    )(zs3, zs3, zs3, zs3, zs3, cbuf, sgla, sgdn, wlr_pad, lr_bias, conv_w, dtb_bc, alog_bc)


def _t5_bucket(dist):
    max_exact = N_BUCKETS // 2
    d = jnp.maximum(dist.astype(F32), 1.0)
    large = max_exact + (jnp.log(d / max_exact) / math.log(MAX_DIST / max_exact)
                         * (N_BUCKETS - max_exact)).astype(jnp.int32)
    large = jnp.minimum(large, N_BUCKETS - 1)
    return jnp.where(dist < max_exact, dist, large).astype(jnp.int32)


def _bias_tables(rel_bias):
    i = jnp.arange(SPAN)[:, None]
    c = jnp.arange(2 * SPAN)[None, :]
    dist = i + SPAN - c
    valid = (dist >= 0) & (dist <= SPAN)

    def lookup(bias_g, idx):
        hot = (idx[..., None] == jnp.arange(N_BUCKETS)).astype(F32)
        return jnp.einsum("...k,kh->...h", hot, bias_g.astype(F32), precision=HI)

    prompt, cached, new = [], [], []
    for g, (_, dil) in enumerate(A_GROUPS):
        bias_g = rel_bias[:, g * A_HEADS:(g + 1) * A_HEADS]
        tbl = lookup(bias_g, _t5_bucket(jnp.maximum(dist, 0) * dil)).transpose(2, 0, 1)
        prompt.append(jnp.where(valid[None], tbl, NEG))
        j = SPAN - jnp.arange(SPAN)
        cached.append(lookup(bias_g, _t5_bucket(dil * j))[..., None])
        new.append(lookup(bias_g, _t5_bucket(jnp.zeros((1,), jnp.int32))).T)
    return jnp.stack(prompt), jnp.stack(cached), jnp.stack(new)


def _prep_w_in(w):
    offs = np.concatenate([[0], np.cumsum(IN_SIZES)])
    seg = lambda i: w[:, offs[i]:offs[i + 1]]
    order = (0, 1, 2, 3, 4, 5, 6, 8, 9, 12, 7, 10, 11)
    used = sum(IN_SIZES)
    parts = [seg(i) for i in order] + [jnp.zeros((w.shape[0], NZ - used), w.dtype)]
    return jnp.concatenate(parts, axis=1).astype(BF16)


def _selector(offset):
    sel = np.zeros((SMALL_W, C_HEADS * C_DV), np.float32)
    for h in range(C_HEADS):
        sel[offset + h, C_DV * h:C_DV * (h + 1)] = 1.0
    return jnp.asarray(sel)


def _block_avg(width, group):
    idx = np.arange(width) // group
    return jnp.asarray((idx[:, None] == idx[None, :]).astype(np.float32) / group)


def kernel(x_prompt, x_sample, cache_win128_kv, cache_win512_kv, cache_win2048_kv, state_gla, state_gdn, state_conv, norm_mix, w_in, a_q_norm, a_k_norm, rel_bias, b_w_lr, b_lr_bias, b_out_norm, c_conv, c_a_log, c_dt_bias, c_out_norm, w_branch, w_out, norm_ffn, ffn_w_gate, ffn_w_up, ffn_w_down, moe_router, moe_w_gate, moe_w_up, moe_w_down):
    bp, t, d = x_prompt.shape
    bs = x_sample.shape[0]
    mp = bp * t
    caches = (cache_win128_kv, cache_win512_kv, cache_win2048_kv)
    bias_p, bias_c, bias_n = _bias_tables(rel_bias)
    pavg128 = _block_avg(2 * A_DH, A_DH)
    selb, sela = _selector(SM_CB), _selector(SM_CA)
    rolled = _cache_roll(caches)

    xp = x_prompt.reshape(mp, d)
    xs = x_sample.reshape(bs, d)
    outs_p = {k: [] for k in ("w0", "w1", "w2", "gla", "gdn", "conv")}
    outs_s = {k: [] for k in ("w0", "w1", "w2", "gla", "gdn", "conv")}
    for l in range(DEPTH):
        w_in_l = _prep_w_in(w_in[l])
        wb = w_branch[l].astype(BF16)
        wo = w_out[l].astype(BF16)
        qg128 = jnp.tile(a_q_norm[l], 2).reshape(1, 2 * A_DH)
        kg128 = jnp.tile(a_k_norm[l], 2).reshape(1, 2 * A_DH)
        wlr_pad = jnp.zeros((SMALL_W, B_HEADS * B_DK), F32).at[:B_RANK].set(b_w_lr[l])
        lr_bias = b_lr_bias[l].reshape(1, -1)
        bon = jnp.tile(b_out_norm[l], B_HEADS).reshape(1, -1)
        con = jnp.tile(c_out_norm[l], C_HEADS).reshape(1, -1)
        dtb_bc = jnp.repeat(c_dt_bias[l], C_DV).reshape(1, -1)
        alog_bc = jnp.repeat(c_a_log[l], C_DV).reshape(1, -1)

        z = _norm_matmul(xp, norm_mix[l], w_in_l, tm=1024, tn=1280)
        z3 = z.reshape(bp, t, NZ)
        o_a, kn0, kn1, kn2 = _attn_prompt(z3, qg128, kg128, pavg128, bias_p)
        o_b, gla_t = _gla_prompt(z3, wlr_pad, lr_bias, tb=512)
        o_c, gdn_s = _gdn_prompt(z3, c_conv[l], selb, sela, dtb_bc, alog_bc, tb=512)
        xp = _merge(z, o_a.reshape(mp, -1), o_b.reshape(mp, -1), o_c.reshape(mp, -1), xp, bon, con, wb, wo, tm=256)
        for g, (kn, (win, _)) in enumerate(zip((kn0, kn1, kn2), A_GROUPS)):
            wp = min(win, t)
            kk = kn[:, t - wp:].reshape(bp, wp, A_HEADS, A_DH)
            vv = z3[:, t - wp:, Z_AV + g * A_WIDTH:Z_AV + (g + 1) * A_WIDTH].reshape(bp, wp, A_HEADS, A_DH)
            outs_p[f"w{g}"].append(jnp.stack([kk, vv], axis=2))
        gla = gla_t.reshape(bp, B_HEADS, B_DV, B_HEADS, B_DK)
        gla = jnp.stack([gla[:, h, :, h, :] for h in range(B_HEADS)], axis=1)
        outs_p["gla"].append(jnp.swapaxes(gla, 2, 3))
        outs_p["gdn"].append(gdn_s)
        outs_p["conv"].append(z3[:, t - (CONV_W - 1):, Z_CQKV:Z_CQKV + C_CONV_CH])

        zs = _norm_matmul(xs, norm_mix[l], w_in_l, tm=bs, tn=1280)
        zs3 = zs.reshape(bs, 1, NZ)
        qkv_s = zs[:, Z_AQ:Z_AQ + 3 * N_GROUPS * A_WIDTH].reshape(bs, 3 * N_GROUPS, A_HEADS, A_DH)
        oa_s, *rolled = _attn_sample(qkv_s, caches, rolled, l, a_q_norm[l].reshape(1, A_DH),
                                     a_k_norm[l].reshape(1, A_DH), bias_c, bias_n, bblk=4)
        ob_s, oc_s, nconv, ngla, ngdn = _sample_mix(
            zs3, state_conv[l], state_gla[l], state_gdn[l], wlr_pad, lr_bias, c_conv[l], dtb_bc, alog_bc)
        xs = _merge(zs, oa_s.reshape(bs, -1), ob_s.reshape(bs, -1), oc_s.reshape(bs, -1), xs, bon, con, wb, wo, tm=bs)
        outs_s["gla"].append(ngla)
        outs_s["gdn"].append(ngdn)
        outs_s["conv"].append(nconv)

        i = l // 2
        if l % 2 == 0:
            wg, wu, wd = ffn_w_gate[i].astype(BF16), ffn_w_up[i].astype(BF16), ffn_w_down[i].astype(BF16)
            xp = _ffn(xp, norm_ffn[l], wg, wu, wd, tm=512, tf=1408)
            xs = _ffn(xs, norm_ffn[l], wg, wu, wd, tm=bs, tf=1408)
        else:
            wg, wu, wd = moe_w_gate[i].astype(BF16), moe_w_up[i].astype(BF16), moe_w_down[i].astype(BF16)
            router_pad = jnp.zeros((d, 128), F32).at[:, :N_EXPERTS].set(moe_router[i])
            xp = _moe(xp, norm_ffn[l], router_pad, wg, wu, wd, tm=512, tf=896)
            xs = _moe(xs, norm_ffn[l], router_pad, wg, wu, wd, tm=bs, tf=896)

    st = lambda name, d_: jnp.stack(d_[name])
    return (xp.reshape(bp, t, d), xs.reshape(bs, 1, d),
            st("w0", outs_p), rolled[0], st("w1", outs_p), rolled[1], st("w2", outs_p), rolled[2],
            st("gla", outs_p), st("gla", outs_s), st("gdn", outs_p), st("gdn", outs_s),
            st("conv", outs_p), st("conv", outs_s))
```

```python
import functools
import math

import jax
import jax.numpy as jnp
import numpy as np
from jax import lax
from jax.experimental import pallas as pl
from jax.experimental.pallas import tpu as pltpu

F32 = jnp.float32
BF16 = jnp.bfloat16
HI = lax.Precision.HIGHEST
NEG = -1e30

D_MODEL = 1024
DEPTH = 2
EPS = 1e-6
A_GROUPS = ((128, 1), (512, 4), (2048, 16))
N_GROUPS = 3
A_HEADS = 8
A_DH = 64
A_WIDTH = A_HEADS * A_DH
SPAN = 128
LPR = 2 * A_WIDTH // 128
N_BUCKETS = 32
MAX_DIST = 2048
B_HEADS = 4
B_DK = 64
B_DV = 128
B_RANK = 16
B_TAU = 16.0
C_HEADS = 4
C_DK = 128
C_DV = 128
CONV_W = 4
C_CONV_CH = 2 * C_HEADS * C_DK + C_HEADS * C_DV
CHUNK = 64
SUB = 16
BRANCH_W = 512
D_FF = 2816
N_EXPERTS = 8
D_FF_EXPERT = 1792
IN_SIZES = (N_GROUPS * A_WIDTH, N_GROUPS * A_WIDTH, N_GROUPS * A_WIDTH,
            B_HEADS * B_DK, B_HEADS * B_DK, B_HEADS * B_DV, B_HEADS * B_DV, B_RANK,
            C_CONV_CH, C_HEADS * C_DV, C_HEADS, C_HEADS, 3 * D_MODEL)

Z_AQ, Z_AK, Z_AV = 0, 1536, 3072
Z_BQ, Z_BK, Z_BV, Z_BG = 4608, 4864, 5120, 5632
Z_CQKV, Z_CZ, Z_GATES, Z_SMALL = 6144, 7680, 8192, 11264
SMALL_W = 256
NZ = Z_SMALL + SMALL_W
SM_CB, SM_CA = B_RANK, B_RANK + C_HEADS

VMEM_LIMIT = 56 * 1024 * 1024


def _cparams(sem):
    return pltpu.CompilerParams(dimension_semantics=sem, vmem_limit_bytes=VMEM_LIMIT)


def _dot(a, b):
    return jnp.dot(a.astype(BF16), b.astype(BF16), preferred_element_type=F32)


def _dot_nt(a, b):
    return lax.dot_general(a.astype(BF16), b.astype(BF16), (((1,), (1,)), ((), ())), preferred_element_type=F32)


def _dot_tn(a, b):
    return lax.dot_general(a.astype(BF16), b.astype(BF16), (((0,), (0,)), ((), ())), preferred_element_type=F32)


def _dot_hi(a, b):
    return jnp.dot(a, b, precision=HI, preferred_element_type=F32)


def _r16(x):
    return x.astype(BF16).astype(F32)


def _sigmoid(x):
    return jax.nn.sigmoid(x)


def _silu(x):
    return x * jax.nn.sigmoid(x)


def _softplus(x):
    return jnp.maximum(x, 0.0) + jnp.log1p(jnp.exp(-jnp.abs(x)))


def _log_sigmoid(x):
    return jnp.minimum(x, 0.0) - jnp.log1p(jnp.exp(-jnp.abs(x)))


def _rms(x, gain):
    return x * lax.rsqrt(jnp.mean(x * x, axis=-1, keepdims=True) + EPS) * gain


def _norm_matmul_kernel(x_ref, g_ref, w_ref, o_ref, h_ref):
    @pl.when(pl.program_id(1) == 0)
    def _():
        h_ref[...] = _rms(x_ref[...], g_ref[...]).astype(BF16)

    o_ref[...] = jnp.dot(h_ref[...], w_ref[...], preferred_element_type=F32)


def _norm_matmul(x, gain, w, tm, tn):
    m, d = x.shape
    n = w.shape[1]
    return pl.pallas_call(
        _norm_matmul_kernel,
        grid=(m // tm, n // tn),
        in_specs=[pl.BlockSpec((tm, d), lambda i, j: (i, 0)),
                  pl.BlockSpec((1, d), lambda i, j: (0, 0)),
                  pl.BlockSpec((d, tn), lambda i, j: (0, j))],
        out_specs=pl.BlockSpec((tm, tn), lambda i, j: (i, j)),
        out_shape=jax.ShapeDtypeStruct((m, n), F32),
        scratch_shapes=[pltpu.VMEM((tm, d), BF16)],
        compiler_params=_cparams(("parallel", "arbitrary")),
        name="norm_matmul",
    )(x, gain.reshape(1, d), w)


def _pair_rms(x, gain, head0):
    sq = x * x
    s0 = jnp.sum(jnp.where(head0, sq, 0.0), axis=-1, keepdims=True)
    s1 = jnp.sum(sq, axis=-1, keepdims=True) - s0
    return x * lax.rsqrt(jnp.where(head0, s0, s1) * (1.0 / A_DH) + EPS) * gain


def _attn_prompt_kernel(q1, k1, v1, q2, k2, v2, q3, k3, v3, qg_ref, kg_ref, bias_ref,
                        o_ref, w1, w2, w3, qt, kt, qs, ks, vs, og, lg, to, tl, *, slots):
    t_len = o_ref.shape[0]
    head0 = lax.broadcasted_iota(jnp.int32, (1, 2 * A_DH), 1) < A_DH
    first_half = lax.broadcasted_iota(jnp.int32, (1, 2 * SPAN), 1) < SPAN
    groups = ((q1, k1, v1, w1), (q2, k2, v2, w2), (q3, k3, v3, w3))
    for g, (qr, kr, vr, wr) in enumerate(groups):
        dil = A_GROUPS[g][1]
        length = t_len // dil
        nb = length // SPAN
        qt[...] = _pair_rms(qr[...], qg_ref[...], head0) * (A_DH ** -0.5)
        kt[...] = _pair_rms(kr[...], kg_ref[...], head0)
        wlen = wr.shape[-1]
        k_win = kt[t_len - wlen:, :].T
        v_win = vr[t_len - wlen:, :].T
        for s in range(slots):
            wr[s, 0] = k_win
            wr[s, 1] = v_win
        knr = kt
        for r in range(dil):
            rows = pl.ds(r, length, stride=dil) if dil > 1 else pl.ds(0, length)
            dst = pl.ds(r * length, length)
            qs[dst, :] = qt[rows, :].astype(BF16)
            ks[dst, :] = knr[rows, :].astype(BF16)
            vs[dst, :] = vr[rows, :].astype(BF16)

        def block(j, carry, g=g, nb=nb):
            row0 = pl.multiple_of(j * SPAN, SPAN)
            qb = qs[pl.ds(row0, SPAN), :]
            if nb > 1:
                prow = pl.multiple_of(jnp.maximum(row0 - SPAN, 0), SPAN)
                k2 = jnp.concatenate([ks[pl.ds(prow, SPAN), :], ks[pl.ds(row0, SPAN), :]], axis=0)
                v2 = jnp.concatenate([vs[pl.ds(prow, SPAN), :], vs[pl.ds(row0, SPAN), :]], axis=0)
                pen = jnp.where(first_half, jnp.where(j % nb == 0, NEG, 0.0), 0.0)
            else:
                k2 = ks[pl.ds(row0, SPAN), :]
                v2 = vs[pl.ds(row0, SPAN), :]
            outs, lses = [], []
            for h in range(2):
                qh = jnp.where(head0 if h == 0 else jnp.logical_not(head0), qb, jnp.zeros_like(qb))
                logits = lax.dot_general(qh, k2, (((1,), (1,)), ((), ())), preferred_element_type=F32)
                if nb > 1:
                    logits = logits + bias_ref[g, h] + pen
                else:
                    logits = logits + bias_ref[g, h, :, SPAN:]
                m = jnp.max(logits, axis=-1, keepdims=True)
                p = jnp.exp(logits - m)
                s = jnp.sum(p, axis=-1, keepdims=True)
                acc = jnp.dot(p.astype(BF16), v2, preferred_element_type=F32)
                outs.append(acc / s)
                lses.append(jnp.broadcast_to(m + jnp.log(s), (SPAN, 2 * A_DH)))
            to[pl.ds(row0, SPAN), :] = jnp.where(head0, outs[0], outs[1])
            tl[pl.ds(row0, SPAN), :] = jnp.where(head0, lses[0], lses[1])
            return carry

        lax.fori_loop(0, t_len // SPAN, block, 0, unroll=2)
        for r in range(dil):
            rows = pl.ds(r, length, stride=dil) if dil > 1 else pl.ds(0, length)
            src = pl.ds(r * length, length)
            og[g, rows, :] = to[src, :]
            lg[g, rows, :] = tl[src, :]
    l_1, l_2, l_3 = lg[0], lg[1], lg[2]
    mx = jnp.maximum(jnp.maximum(l_1, l_2), l_3)
    e_1, e_2, e_3 = jnp.exp(l_1 - mx), jnp.exp(l_2 - mx), jnp.exp(l_3 - mx)
    o_ref[...] = (e_1 * og[0] + e_2 * og[1] + e_3 * og[2]) / (e_1 + e_2 + e_3)


def _attn_prompt(z3, qgain, kgain, bias, layer, windows):
    b, t, _ = z3.shape
    hp_blocks = A_WIDTH // 128

    def col(base, g):
        return lambda i, hp: (i, 0, base // 128 + g * hp_blocks + hp)

    in_specs = []
    for g in range(N_GROUPS):
        for base in (Z_AQ, Z_AK, Z_AV):
            in_specs.append(pl.BlockSpec((None, t, 128), col(base, g)))
    in_specs += [pl.BlockSpec((1, 128), lambda i, hp: (0, 0)),
                 pl.BlockSpec((1, 128), lambda i, hp: (0, 0)),
                 pl.BlockSpec((N_GROUPS, 2, SPAN, 2 * SPAN), lambda i, hp: (0, hp, 0, 0))]
    args = [z3] * 9 + [qgain, kgain, bias]
    slots = DEPTH if windows is None else 1
    win_specs, win_shapes = [], []
    for win, _ in A_GROUPS:
        wp = min(win, t)
        win_specs.append(pl.BlockSpec((slots, None, 2, 128, wp), lambda i, hp: (layer // slots, i, 0, hp, 0)))
        win_shapes.append(jax.ShapeDtypeStruct((DEPTH, b, 2, A_WIDTH, wp), F32))
    aliases = {}
    if windows is not None:
        aliases = {len(args) + g: 1 + g for g in range(N_GROUPS)}
        in_specs += [pl.BlockSpec(memory_space=pl.ANY)] * N_GROUPS
        args += list(windows)
    kern = functools.partial(_attn_prompt_kernel, slots=slots)
    if windows is not None:
        kern = functools.partial(_drop_refs, kern, len(args) - N_GROUPS, N_GROUPS)
    return pl.pallas_call(
        kern,
        grid=(b, hp_blocks),
        in_specs=in_specs,
        out_specs=[pl.BlockSpec((None, t, 128), lambda i, hp: (i, 0, hp))] + win_specs,
        out_shape=[jax.ShapeDtypeStruct((b, t, A_WIDTH), F32)] + win_shapes,
        scratch_shapes=[pltpu.VMEM((t, 128), F32), pltpu.VMEM((t, 128), F32),
                        pltpu.VMEM((t, 128), BF16), pltpu.VMEM((t, 128), BF16), pltpu.VMEM((t, 128), BF16),
                        pltpu.VMEM((N_GROUPS, t, 128), F32), pltpu.VMEM((N_GROUPS, t, 128), F32),
                        pltpu.VMEM((t, 128), F32), pltpu.VMEM((t, 128), F32)],
        input_output_aliases=aliases,
        compiler_params=_cparams(("parallel", "parallel")),
        name="attn_prompt",
    )(*args)


def _drop_refs(kern, start, count, *refs):
    return kern(*refs[:start], *refs[start + count:])


def _gla_prompt_kernel(q_ref, k_ref, v_ref, sm_ref, wlr_ref, lrb_ref, o_ref, st_ref, s_scr, la_scr):
    tb = q_ref.shape[0]
    hk = B_HEADS * B_DK
    hv = B_HEADS * B_DV

    @pl.when(pl.program_id(1) == 0)
    def _():
        s_scr[...] = jnp.zeros_like(s_scr)

    la_scr[...] = _log_sigmoid(_dot(sm_ref[...], wlr_ref[...]) + lrb_ref[...]) * (1.0 / B_TAU)

    rr = lax.broadcasted_iota(jnp.int32, (CHUNK, CHUNK), 0)
    cc = lax.broadcasted_iota(jnp.int32, (CHUNK, CHUNK), 1)
    causal = rr >= cc
    ltri = causal.astype(F32)
    lane_head = lax.broadcasted_iota(jnp.int32, (1, hk), 1) // B_DK
    row = lax.broadcasted_iota(jnp.int32, (CHUNK, 1), 0)
    same_head = (lax.broadcasted_iota(jnp.int32, (hv, hk), 0) // B_DV
                 == lax.broadcasted_iota(jnp.int32, (hv, hk), 1) // B_DK)

    def chunk(c, carry):
        r0 = pl.multiple_of(c * CHUNK, CHUNK)
        b = _dot_hi(ltri, la_scr[pl.ds(r0, CHUNK), :])
        q = q_ref[pl.ds(r0, CHUNK), :] * (B_DK ** -0.5)
        k = k_ref[pl.ds(r0, CHUNK), :]
        vb = v_ref[pl.ds(r0, CHUNK), :].astype(BF16)
        s = s_scr[...]
        b_last = b[CHUNK - 1:CHUNK, :]
        o_inter = _dot_nt(q * jnp.exp(b), s)
        atts = []
        for i in range(CHUNK // SUB):
            b_ref0 = b[SUB * i:SUB * i + 1, :]
            qi = q[SUB * i:SUB * (i + 1), :] * jnp.exp(b[SUB * i:SUB * (i + 1), :] - b_ref0)
            qst = jnp.concatenate([jnp.where(lane_head == h, qi, 0.0) for h in range(B_HEADS)], axis=0)
            kt = k * jnp.exp(jnp.where(row < SUB * (i + 1), b_ref0 - b, 0.0))
            atts.append(_dot_nt(qst, kt))
        o_parts = []
        for h in range(B_HEADS):
            att = jnp.concatenate([a[SUB * h:SUB * (h + 1), :] for a in atts], axis=0)
            att = jnp.where(causal, att, 0.0)
            o_parts.append(_dot(att, vb[:, B_DV * h:B_DV * (h + 1)]))
        o_ref[pl.ds(r0, CHUNK), :] = o_inter + jnp.concatenate(o_parts, axis=1)
        upd = _dot_tn(vb, k * jnp.exp(b_last - b))
        s_scr[...] = s * jnp.exp(b_last) + jnp.where(same_head, upd, 0.0)
        return carry

    lax.fori_loop(0, tb // CHUNK, chunk, 0)

    @pl.when(pl.program_id(1) == pl.num_programs(1) - 1)
    def _():
        st_ref[...] = s_scr[...]


def _gla_prompt(z3, wlr_pad, lr_bias, tb):
    b, t, _ = z3.shape
    hk, hv = B_HEADS * B_DK, B_HEADS * B_DV
    return pl.pallas_call(
        _gla_prompt_kernel,
        grid=(b, t // tb),
        in_specs=[pl.BlockSpec((None, tb, hk), lambda i, j: (i, j, Z_BQ // hk)),
                  pl.BlockSpec((None, tb, hk), lambda i, j: (i, j, Z_BK // hk)),
                  pl.BlockSpec((None, tb, hv), lambda i, j: (i, j, Z_BV // hv)),
                  pl.BlockSpec((None, tb, SMALL_W), lambda i, j: (i, j, Z_SMALL // SMALL_W)),
                  pl.BlockSpec((SMALL_W, hk), lambda i, j: (0, 0)),
                  pl.BlockSpec((1, hk), lambda i, j: (0, 0))],
        out_specs=[pl.BlockSpec((None, tb, hv), lambda i, j: (i, j, 0)),
                   pl.BlockSpec((None, hv, hk), lambda i, j: (i, 0, 0))],
        out_shape=[jax.ShapeDtypeStruct((b, t, hv), F32), jax.ShapeDtypeStruct((b, hv, hk), F32)],
        scratch_shapes=[pltpu.VMEM((hv, hk), F32), pltpu.VMEM((tb, hk), F32)],
        compiler_params=_cparams(("parallel", "arbitrary")),
        name="gla_prompt",
    )(z3, z3, z3, z3, wlr_pad, lr_bias)


def _gdn_prompt_kernel(x_ref, sm_ref, cw_ref, selb_ref, sela_ref, dtb_ref, alog_ref, o_ref, s_ref,
                       xpad, cv, gsc, bsc, s_scr):
    tb = x_ref.shape[0]
    hd = C_HEADS * C_DK
    pad = 8

    @pl.when(pl.program_id(1) == 0)
    def _():
        s_scr[...] = jnp.zeros_like(s_scr)
        xpad[0:pad, :] = jnp.zeros((pad, C_CONV_CH), F32)

    xpad[pad:pad + tb, :] = x_ref[...]
    rb = 128
    for i in range(tb // rb):
        acc = None
        for j in range(CONV_W):
            lo = pad - (CONV_W - 1) + j + i * rb
            term = xpad[lo:lo + rb, :] * cw_ref[j:j + 1, :]
            acc = term if acc is None else acc + term
        y = _silu(acc)
        rows = slice(i * rb, (i + 1) * rb)
        for h in range(C_HEADS):
            qh = y[:, C_DK * h:C_DK * (h + 1)]
            cv[rows, C_DK * h:C_DK * (h + 1)] = (
                qh * lax.rsqrt(jnp.sum(qh * qh, axis=-1, keepdims=True) + EPS) * (C_DK ** -0.5))
            kh = y[:, hd + C_DK * h:hd + C_DK * (h + 1)]
            cv[rows, hd + C_DK * h:hd + C_DK * (h + 1)] = (
                kh * lax.rsqrt(jnp.sum(kh * kh, axis=-1, keepdims=True) + EPS))
        cv[rows, 2 * hd:] = y[:, 2 * hd:]
    xpad[pad - (CONV_W - 1):pad, :] = xpad[pad + tb - (CONV_W - 1):pad + tb, :]

    sm = sm_ref[...]
    gsc[...] = -jnp.exp(alog_ref[...]) * _softplus(_dot_hi(sm, sela_ref[...]) + dtb_ref[...])
    bsc[...] = _sigmoid(_dot_hi(sm, selb_ref[...]))

    n = C_HEADS * CHUNK
    rr = lax.broadcasted_iota(jnp.int32, (CHUNK, CHUNK), 0)
    cc = lax.broadcasted_iota(jnp.int32, (CHUNK, CHUNK), 1)
    ltri = (rr >= cc).astype(F32)
    ri = lax.broadcasted_iota(jnp.int32, (n, n), 0)
    ci = lax.broadcasted_iota(jnp.int32, (n, n), 1)
    same = (ri // CHUNK) == (ci // CHUNK)
    incl = jnp.logical_and(same, ci <= ri)
    strict = jnp.logical_and(same, ci < ri)
    eye = (ri == ci).astype(F32)

    def stack(x):
        return jnp.concatenate([x[:, C_DK * h:C_DK * (h + 1)] for h in range(C_HEADS)], axis=0)

    def chunk(c, carry):
        r0 = pl.multiple_of(c * CHUNK, CHUNK)
        bst = stack(_dot_hi(ltri, gsc[pl.ds(r0, CHUNK), :]))
        beta = stack(bsc[pl.ds(r0, CHUNK), :])
        qst = stack(cv[pl.ds(r0, CHUNK), 0:hd])
        kst = stack(cv[pl.ds(r0, CHUNK), hd:2 * hd])
        vst = stack(cv[pl.ds(r0, CHUNK), 2 * hd:3 * hd])
        kb = kst.astype(BF16)
        kk = lax.dot_general(kb, kb, (((1,), (1,)), ((), ())), preferred_element_type=F32)
        qk = lax.dot_general(qst.astype(BF16), kb, (((1,), (1,)), ((), ())), preferred_element_type=F32)
        col_b = jnp.concatenate([bst, bst], axis=1)
        bst_t = bst.T
        row_b = jnp.concatenate([bst_t, bst_t], axis=0)
        dec = jnp.exp(jnp.where(incl, col_b - row_b, NEG))
        a = jnp.where(strict, jnp.concatenate([beta, beta], axis=1) * dec * kk, 0.0)
        aqk = dec * qk
        inv = eye - a
        pw = a
        for _ in range(int(math.log2(CHUNK)) - 1):
            pw = _dot(pw, pw)
            inv = inv + _dot(inv, pw)
        eb = jnp.exp(bst)
        wu = _dot(inv, jnp.concatenate([beta * eb * kst, beta * vst], axis=1))
        w, uv = wu[:, :C_DK], wu[:, C_DK:]
        us, qss = [], []
        for h in range(C_HEADS):
            sl = slice(CHUNK * h, CHUNK * (h + 1))
            ws = _dot(jnp.concatenate([w[sl], qst[sl]], axis=0), s_scr[h])
            us.append(uv[sl] - ws[:CHUNK])
            qss.append(ws[CHUNK:])
        ust = jnp.concatenate(us, axis=0)
        o = eb * jnp.concatenate(qss, axis=0) + _dot(aqk, ust)
        for h in range(C_HEADS):
            sl = slice(CHUNK * h, CHUNK * (h + 1))
            o_ref[pl.ds(r0, CHUNK), C_DV * h:C_DV * (h + 1)] = o[sl]
            b_last = bst[CHUNK * (h + 1) - 1:CHUNK * (h + 1), :]
            s_scr[h] = jnp.exp(b_last) * s_scr[h] + _dot_tn(kst[sl] * jnp.exp(b_last - bst[sl]), us[h])
        return carry

    lax.fori_loop(0, tb // CHUNK, chunk, 0)

    @pl.when(pl.program_id(1) == pl.num_programs(1) - 1)
    def _():
        s_ref[...] = s_scr[...]


def _gdn_prompt(z3, conv_w, selb, sela, dtb_bc, alog_bc, tb):
    b, t, _ = z3.shape
    hd = C_HEADS * C_DV
    const = lambda shape: pl.BlockSpec(shape, lambda i, j: (0,) * len(shape))
    return pl.pallas_call(
        _gdn_prompt_kernel,
        grid=(b, t // tb),
        in_specs=[pl.BlockSpec((None, tb, C_CONV_CH), lambda i, j: (i, j, Z_CQKV // C_CONV_CH)),
                  pl.BlockSpec((None, tb, SMALL_W), lambda i, j: (i, j, Z_SMALL // SMALL_W)),
                  const((CONV_W, C_CONV_CH)), const((SMALL_W, hd)), const((SMALL_W, hd)),
                  const((1, hd)), const((1, hd))],
        out_specs=[pl.BlockSpec((None, tb, hd), lambda i, j: (i, j, 0)),
                   pl.BlockSpec((None, C_HEADS, C_DK, C_DV), lambda i, j: (i, 0, 0, 0))],
        out_shape=[jax.ShapeDtypeStruct((b, t, hd), F32),
                   jax.ShapeDtypeStruct((b, C_HEADS, C_DK, C_DV), F32)],
        scratch_shapes=[pltpu.VMEM((tb + 8, C_CONV_CH), F32), pltpu.VMEM((tb, C_CONV_CH), F32),
                        pltpu.VMEM((tb, hd), F32), pltpu.VMEM((tb, hd), F32),
                        pltpu.VMEM((C_HEADS, C_DK, C_DV), F32)],
        compiler_params=_cparams(("parallel", "arbitrary")),
        name="gdn_prompt",
    )(z3, z3, conv_w, selb, sela, dtb_bc, alog_bc)


def _head_norm(o, gain, width):
    parts = []
    for h in range(o.shape[1] // width):
        oh = o[:, width * h:width * (h + 1)]
        parts.append(oh * lax.rsqrt(jnp.mean(oh * oh, axis=-1, keepdims=True) + EPS))
    return jnp.concatenate(parts, axis=1) * gain


def _merge_kernel(oa_ref, ob_ref, bg_ref, oc_ref, cz_ref, ga_ref, gb_ref, gc_ref, x_ref,
                  bon_ref, con_ref, wb_ref, wo_ref, y_ref):
    ob = _head_norm(ob_ref[...], bon_ref[...], B_DV) * _silu(bg_ref[...])
    oc = _head_norm(oc_ref[...], con_ref[...], C_DV) * _silu(cz_ref[...])
    merged = (_sigmoid(ga_ref[...]) * _dot(oa_ref[...], wb_ref[0])
              + _sigmoid(gb_ref[...]) * _dot(ob, wb_ref[1])
              + _sigmoid(gc_ref[...]) * _dot(oc, wb_ref[2]))
    y_ref[...] = x_ref[...] + _dot(merged, wo_ref[...])


def _merge(z, oa, ob, oc, x, bon, con, wb, wo, tm):
    m = x.shape[0]
    w = BRANCH_W
    row = lambda width, blk: pl.BlockSpec((tm, width), lambda i: (i, blk))
    const = lambda shape: pl.BlockSpec(shape, lambda i: (0,) * len(shape))
    return pl.pallas_call(
        _merge_kernel,
        grid=(m // tm,),
        in_specs=[row(w, 0), row(w, 0), row(w, Z_BG // w), row(w, 0), row(w, Z_CZ // w),
                  row(D_MODEL, Z_GATES // D_MODEL), row(D_MODEL, Z_GATES // D_MODEL + 1),
                  row(D_MODEL, Z_GATES // D_MODEL + 2), row(D_MODEL, 0),
                  const((1, w)), const((1, w)), const((3, w, D_MODEL)), const((D_MODEL, D_MODEL))],
        out_specs=row(D_MODEL, 0),
        out_shape=jax.ShapeDtypeStruct((m, D_MODEL), F32),
        compiler_params=_cparams(("parallel",)),
        name="merge",
    )(oa, ob, z, oc, z, z, z, z, x, bon, con, wb, wo)


def _ffn_kernel(x_ref, g_ref, wg_ref, wu_ref, wd_ref, y_ref, h_ref, acc_ref):
    f = pl.program_id(1)

    @pl.when(f == 0)
    def _():
        h_ref[...] = _rms(x_ref[...], g_ref[...]).astype(BF16)
        acc_ref[...] = jnp.zeros_like(acc_ref)

    h = h_ref[...]
    a = _silu(jnp.dot(h, wg_ref[...], preferred_element_type=F32)) * jnp.dot(h, wu_ref[...],
                                                                              preferred_element_type=F32)
    acc_ref[...] += jnp.dot(a.astype(BF16), wd_ref[...], preferred_element_type=F32)

    @pl.when(f == pl.num_programs(1) - 1)
    def _():
        y_ref[...] = x_ref[...] + acc_ref[...]


def _ffn(x, gain, wg, wu, wd, tm, tf):
    m, d = x.shape
    ff = wg.shape[1]
    return pl.pallas_call(
        _ffn_kernel,
        grid=(m // tm, ff // tf),
        in_specs=[pl.BlockSpec((tm, d), lambda i, f: (i, 0)),
                  pl.BlockSpec((1, d), lambda i, f: (0, 0)),
                  pl.BlockSpec((d, tf), lambda i, f: (0, f)),
                  pl.BlockSpec((d, tf), lambda i, f: (0, f)),
                  pl.BlockSpec((tf, d), lambda i, f: (f, 0))],
        out_specs=pl.BlockSpec((tm, d), lambda i, f: (i, 0)),
        out_shape=jax.ShapeDtypeStruct((m, d), F32),
        scratch_shapes=[pltpu.VMEM((tm, d), BF16), pltpu.VMEM((tm, d), F32)],
        compiler_params=_cparams(("parallel", "arbitrary")),
        name="ffn",
    )(x, gain.reshape(1, d), wg, wu, wd)


def _moe_kernel(x_ref, g_ref, rt_ref, wg_ref, wu_ref, wd_ref, y_ref, h_ref, acc_ref, gate_ref):
    e = pl.program_id(1)
    f = pl.program_id(2)
    lane = lax.broadcasted_iota(jnp.int32, (1, 128), 1).astype(F32)

    @pl.when(jnp.logical_and(e == 0, f == 0))
    def _():
        hf = _rms(x_ref[...], g_ref[...])
        h_ref[...] = hf.astype(BF16)
        acc_ref[...] = jnp.zeros_like(acc_ref)
        valid = lane < N_EXPERTS
        logits = jnp.where(valid, _dot(hf, rt_ref[...]), NEG)
        ex = jnp.exp(logits - jnp.max(logits, axis=-1, keepdims=True))
        probs = ex / jnp.sum(ex, axis=-1, keepdims=True)
        m1 = jnp.max(probs, axis=-1, keepdims=True)
        i1 = jnp.min(jnp.where(jnp.logical_and(probs == m1, valid), lane, 128.0), axis=-1, keepdims=True)
        hot1 = lane == i1
        rest = jnp.where(jnp.logical_or(hot1, jnp.logical_not(valid)), -1.0, probs)
        m2 = jnp.max(rest, axis=-1, keepdims=True)
        i2 = jnp.min(jnp.where(rest == m2, lane, 128.0), axis=-1, keepdims=True)
        hot2 = lane == i2
        den = m1 + m2
        gate_ref[...] = jnp.where(hot1, m1 / den, 0.0) + jnp.where(hot2, m2 / den, 0.0)

    h = h_ref[...]
    ge = jnp.sum(jnp.where(lane == e.astype(F32), gate_ref[...], 0.0), axis=-1, keepdims=True)
    a = (_silu(jnp.dot(h, wg_ref[...], preferred_element_type=F32))
         * jnp.dot(h, wu_ref[...], preferred_element_type=F32) * ge)
    acc_ref[...] += jnp.dot(a.astype(BF16), wd_ref[...], preferred_element_type=F32)

    @pl.when(jnp.logical_and(e == pl.num_programs(1) - 1, f == pl.num_programs(2) - 1))
    def _():
        y_ref[...] = x_ref[...] + acc_ref[...]


def _moe(x, gain, router_pad, wg, wu, wd, tm, tf):
    m, d = x.shape
    ne, _, ff = wg.shape
    return pl.pallas_call(
        _moe_kernel,
        grid=(m // tm, ne, ff // tf),
        in_specs=[pl.BlockSpec((tm, d), lambda i, e, f: (i, 0)),
                  pl.BlockSpec((1, d), lambda i, e, f: (0, 0)),
                  pl.BlockSpec((d, 128), lambda i, e, f: (0, 0)),
                  pl.BlockSpec((None, d, tf), lambda i, e, f: (e, 0, f)),
                  pl.BlockSpec((None, d, tf), lambda i, e, f: (e, 0, f)),
                  pl.BlockSpec((None, tf, d), lambda i, e, f: (e, f, 0))],
        out_specs=pl.BlockSpec((tm, d), lambda i, e, f: (i, 0)),
        out_shape=jax.ShapeDtypeStruct((m, d), F32),
        scratch_shapes=[pltpu.VMEM((tm, d), BF16), pltpu.VMEM((tm, d), F32), pltpu.VMEM((tm, 128), F32)],
        compiler_params=_cparams(("parallel", "arbitrary", "arbitrary")),
        name="moe",
    )(x, gain.reshape(1, d), router_pad, wg, wu, wd)


def _attn_sample_kernel(*refs, layer, full_roll):
    zq_ref, qg_ref, kg_ref, pavg_ref, b0_ref, b1_ref, b2_ref, bias0_ref, c0_ref, c1_ref, c2_ref = refs[:11]
    if full_roll:
        tails = (None,) * N_GROUPS
        rest = refs[11:]
    else:
        tails = refs[11:14]
        rest = refs[14:]
    o_ref, n0_ref, n1_ref, n2_ref, p0_scr, p1_scr, p2_scr, st_scr, oacc_scr = rest
    half = pl.program_id(1)
    active = (pl.program_id(2) == layer) if full_roll else True
    aw = A_WIDTH
    sel = (lax.broadcasted_iota(jnp.int32, (A_HEADS, aw), 1) // A_DH
           == lax.broadcasted_iota(jnp.int32, (A_HEADS, aw), 0))
    last_lane = lax.broadcasted_iota(jnp.int32, (1, 128), 1) == 127
    groups = ((c0_ref, n0_ref, b0_ref, p0_scr, tails[0]), (c1_ref, n1_ref, b1_ref, p1_scr, tails[1]),
              (c2_ref, n2_ref, b2_ref, p2_scr, tails[2]))
    for g, (c_ref, n_ref, b_ref, p_scr, t_ref) in enumerate(groups):
        wlen = c_ref.shape[1]
        if full_roll:
            n_ref[...] = pltpu.roll(c_ref[...], wlen - 1, axis=1)

        def newest(col, n_ref=n_ref, t_ref=t_ref, wlen=wlen):
            old = n_ref[:, wlen - 128:] if full_roll else t_ref[...]
            dst = n_ref.at[:, wlen - 128:] if full_roll else n_ref
            dst[...] = jnp.where(last_lane, col, old)

        lo = g * aw

        @pl.when(jnp.logical_and(active, half == 0))
        def _(g=g, c_ref=c_ref, b_ref=b_ref, p_scr=p_scr, lo=lo, newest=newest):
            q = zq_ref[:, Z_AQ + lo:Z_AQ + lo + aw]
            k = zq_ref[:, Z_AK + lo:Z_AK + lo + aw]
            qn = q * lax.rsqrt(_dot_hi(q * q, pavg_ref[...]) + EPS) * qg_ref[...] * (A_DH ** -0.5)
            kn = k * lax.rsqrt(_dot_hi(k * k, pavg_ref[...]) + EPS) * kg_ref[...]
            qbd = jnp.where(sel, jnp.broadcast_to(qn, (A_HEADS, aw)), 0.0)
            lc = _dot(qbd, c_ref[...]) + b_ref[...]
            l0 = jnp.sum(_r16(qbd) * _r16(kn), axis=1, keepdims=True) + bias0_ref[g]
            m = jnp.maximum(jnp.max(lc, axis=1, keepdims=True), l0)
            pc = jnp.exp(lc - m)
            p0 = jnp.exp(l0 - m)
            p_scr[...] = pc
            st_scr[g, 0] = jnp.broadcast_to(p0, (A_HEADS, 128))
            st_scr[g, 1] = jnp.broadcast_to(jnp.sum(pc, axis=1, keepdims=True) + p0, (A_HEADS, 128))
            st_scr[g, 2] = jnp.broadcast_to(m, (A_HEADS, 128))
            newest(_to_col(kn, aw))

        @pl.when(jnp.logical_and(active, half == 1))
        def _(g=g, c_ref=c_ref, p_scr=p_scr, lo=lo, newest=newest):
            v = zq_ref[:, Z_AV + lo:Z_AV + lo + aw]
            p0 = st_scr[g, 0][:, 0:1]
            s = st_scr[g, 1][:, 0:1]
            m = st_scr[g, 2][:, 0:1]
            o8 = (_dot_nt(p_scr[...], c_ref[...]) + _r16(p0) * _r16(v)) / s
            oacc_scr[g, 0:1, :] = jnp.sum(jnp.where(sel, o8, 0.0), axis=0, keepdims=True)
            oacc_scr[g, 1:2, :] = jnp.sum(jnp.where(sel, m + jnp.log(s), 0.0), axis=0, keepdims=True)
            newest(_to_col(v, aw))

    @pl.when(jnp.logical_and(active, half == 1))
    def _():
        l_1, l_2, l_3 = oacc_scr[0, 1:2, :], oacc_scr[1, 1:2, :], oacc_scr[2, 1:2, :]
        mx = jnp.maximum(jnp.maximum(l_1, l_2), l_3)
        e_1, e_2, e_3 = jnp.exp(l_1 - mx), jnp.exp(l_2 - mx), jnp.exp(l_3 - mx)
        o_ref[...] = ((e_1 * oacc_scr[0, 0:1, :] + e_2 * oacc_scr[1, 0:1, :] + e_3 * oacc_scr[2, 0:1, :])
                      / (e_1 + e_2 + e_3))


def _attn_sample(zs3, caches_t, rolled, layer, qgain, kgain, pavg, biases, bias0):
    bd = zs3.shape[0]
    full_roll = rolled is None
    aw = A_WIDTH
    if full_roll:
        grid = (bd, 2, DEPTH)
        const = lambda shape: pl.BlockSpec(shape, lambda i, h, l: (0,) * len(shape))
        zspec = pl.BlockSpec((None, 1, Z_AV + N_GROUPS * aw), lambda i, h, l: (i, 0, 0))
        ospec = pl.BlockSpec((None, 1, aw), lambda i, h, l: (i, 0, 0))
        cache_specs = [pl.BlockSpec((None, None, None, aw, c.shape[-1]), lambda i, h, l: (l, i, h, 0, 0))
                       for c in caches_t]
        new_specs = cache_specs
        extra_specs, extra_args, aliases = [], [], {}
        sem = ("parallel", "arbitrary", "arbitrary")
    else:
        grid = (bd, 2)
        const = lambda shape: pl.BlockSpec(shape, lambda i, h: (0,) * len(shape))
        zspec = pl.BlockSpec((None, 1, Z_AV + N_GROUPS * aw), lambda i, h: (i, 0, 0))
        ospec = pl.BlockSpec((None, 1, aw), lambda i, h: (i, 0, 0))
        cache_specs = [pl.BlockSpec((None, None, None, aw, c.shape[-1]), lambda i, h: (layer, i, h, 0, 0))
                       for c in caches_t]
        new_specs = [pl.BlockSpec((None, None, None, aw, 128), functools.partial(
            lambda last, i, h: (layer, i, h, 0, last), c.shape[-1] // 128 - 1)) for c in caches_t]
        extra_specs, extra_args = new_specs, list(rolled)
        aliases = {11 + g: 1 + g for g in range(N_GROUPS)}
        sem = ("parallel", "arbitrary")
    return pl.pallas_call(
        functools.partial(_attn_sample_kernel, layer=layer, full_roll=full_roll),
        grid=grid,
        in_specs=[zspec, const((1, aw)), const((1, aw)), const((aw, aw))]
                 + [const((A_HEADS, c.shape[-1])) for c in caches_t] + [const((N_GROUPS, A_HEADS, 1))]
                 + cache_specs + extra_specs,
        out_specs=[ospec] + new_specs,
        out_shape=[jax.ShapeDtypeStruct((bd, 1, aw), F32)]
                  + [jax.ShapeDtypeStruct(c.shape, c.dtype) for c in caches_t],
        scratch_shapes=[pltpu.VMEM((A_HEADS, c.shape[-1]), F32) for c in caches_t]
                       + [pltpu.VMEM((N_GROUPS, 3, A_HEADS, 128), F32), pltpu.VMEM((N_GROUPS, 8, aw), F32)],
        input_output_aliases=aliases,
        compiler_params=_cparams(sem),
        name="attn_sample",
    )(zs3, qgain, kgain, pavg, *biases, bias0, *caches_t, *extra_args)


def _to_col(row, n):
    eye = lax.broadcasted_iota(jnp.int32, (n, n), 0) == lax.broadcasted_iota(jnp.int32, (n, n), 1)
    return jnp.sum(jnp.where(eye, jnp.broadcast_to(row, (n, n)), 0.0), axis=1, keepdims=True)


def _sample_mix_kernel(bq_ref, bk_ref, bv_ref, cqkv_ref, sm_ref, cbuf_ref, sgla_ref, sgdn_ref,
                       wlr_ref, lrb_ref, cw_ref, dtb_ref, alog_ref,
                       ob_ref, oc_ref, nconv_ref, ngla_ref, ngdn_ref):
    sm = sm_ref[...]
    la = _log_sigmoid(_dot(sm, wlr_ref[...]) + lrb_ref[...]) * (1.0 / B_TAU)
    q = bq_ref[...] * (B_DK ** -0.5)
    k = bk_ref[...]
    v = _r16(bv_ref[...])
    for h in range(B_HEADS):
        lk = slice(B_DK * h, B_DK * (h + 1))
        lv = slice(B_DV * h, B_DV * (h + 1))
        s_old = sgla_ref[h]
        decay = jnp.exp(la[:, lk])
        ngla_ref[h] = _to_col(decay, B_DK) * s_old + _to_col(_r16(k[:, lk]), B_DK) * v[:, lv]
        att = jnp.sum(q[:, lk] * k[:, lk], axis=-1, keepdims=True)
        ob_ref[:, lv] = (jnp.sum(_to_col(_r16(q[:, lk] * decay), B_DK) * _r16(s_old), axis=0, keepdims=True)
                         + _r16(att) * v[:, lv])

    cat = jnp.concatenate([cbuf_ref[...], cqkv_ref[...]], axis=0)
    acc = None
    for j in range(CONV_W):
        term = cat[j:j + 1, :] * cw_ref[j:j + 1, :]
        acc = term if acc is None else acc + term
    y = _silu(acc)
    nconv_ref[...] = cat[1:CONV_W, :]
    hd = C_HEADS * C_DK
    for h in range(C_HEADS):
        ld = slice(C_DK * h, C_DK * (h + 1))
        qh = y[:, C_DK * h:C_DK * (h + 1)]
        qh = qh * lax.rsqrt(jnp.sum(qh * qh, axis=-1, keepdims=True) + EPS) * (C_DK ** -0.5)
        kh = y[:, hd + C_DK * h:hd + C_DK * (h + 1)]
        kh = kh * lax.rsqrt(jnp.sum(kh * kh, axis=-1, keepdims=True) + EPS)
        vh = y[:, 2 * hd + C_DV * h:2 * hd + C_DV * (h + 1)]
        beta = _sigmoid(sm[:, SM_CB + h:SM_CB + h + 1])
        g = -jnp.exp(alog_ref[:, ld]) * _softplus(sm[:, SM_CA + h:SM_CA + h + 1] + dtb_ref[:, ld])
        eg = jnp.exp(g)
        s = sgdn_ref[h]
        s16 = _r16(s)
        qh, kh = _r16(qh), _r16(kh)
        kcol = _to_col(kh, C_DK)
        u = beta * (vh - eg * jnp.sum(kcol * s16, axis=0, keepdims=True))
        qs = jnp.sum(_to_col(qh, C_DK) * s16, axis=0, keepdims=True)
        u16 = _r16(u)
        oc_ref[:, ld] = eg * qs + _r16(jnp.sum(qh * kh, axis=-1, keepdims=True)) * u16
        ngdn_ref[h] = eg * s + kcol * u16


def _sample_mix(zs3, cbuf, sgla, sgdn, wlr_pad, lr_bias, conv_w, dtb_bc, alog_bc):
    bd = zs3.shape[0]
    hk, hv, hd = B_HEADS * B_DK, B_HEADS * B_DV, C_HEADS * C_DV
    const = lambda shape: pl.BlockSpec(shape, lambda i: (0,) * len(shape))
    zrow = lambda width, off: pl.BlockSpec((None, 1, width), lambda i: (i, 0, off // width))
    vec = lambda width: pl.BlockSpec((None, 1, width), lambda i: (i, 0, 0))
    st = lambda shape: pl.BlockSpec((None,) + shape, lambda i: (i,) + (0,) * len(shape))
    return pl.pallas_call(
        _sample_mix_kernel,
        grid=(bd,),
        in_specs=[zrow(hk, Z_BQ), zrow(hk, Z_BK), zrow(hv, Z_BV), zrow(C_CONV_CH, Z_CQKV), zrow(SMALL_W, Z_SMALL),
                  st((CONV_W - 1, C_CONV_CH)), st((B_HEADS, B_DK, B_DV)), st((C_HEADS, C_DK, C_DV))]
                 + [const((SMALL_W, hk)), const((1, hk)), const((CONV_W, C_CONV_CH)), const((1, hd)), const((1, hd))],
        out_specs=[vec(hv), vec(hd), st((CONV_W - 1, C_CONV_CH)),
                   st((B_HEADS, B_DK, B_DV)), st((C_HEADS, C_DK, C_DV))],
        out_shape=[jax.ShapeDtypeStruct((bd, 1, hv), F32),
                   jax.ShapeDtypeStruct((bd, 1, hd), F32), jax.ShapeDtypeStruct((bd, CONV_W - 1, C_CONV_CH), F32),
                   jax.ShapeDtypeStruct((bd, B_HEADS, B_DK, B_DV), F32),
                   jax.ShapeDtypeStruct((bd, C_HEADS, C_DK, C_DV), F32)],
        compiler_params=_cparams(("parallel",)),
        name="sample_mix",
    )(zs3, zs3, zs3, zs3, zs3, cbuf, sgla, sgdn, wlr_pad, lr_bias, conv_w, dtb_bc, alog_bc)


def _t5_bucket(dist):
    max_exact = N_BUCKETS // 2
    d = jnp.maximum(dist.astype(F32), 1.0)
    large = max_exact + (jnp.log(d / max_exact) / math.log(MAX_DIST / max_exact)
                         * (N_BUCKETS - max_exact)).astype(jnp.int32)
    large = jnp.minimum(large, N_BUCKETS - 1)
    return jnp.where(dist < max_exact, dist, large).astype(jnp.int32)


def _bias_tables(rel_bias):
    i = jnp.arange(SPAN)[:, None]
    c = jnp.arange(2 * SPAN)[None, :]
    dist = i + SPAN - c
    valid = (dist >= 0) & (dist <= SPAN)

    def lookup(bias_g, idx):
        hot = (idx[..., None] == jnp.arange(N_BUCKETS)).astype(F32)
        return jnp.einsum("...k,kh->...h", hot, bias_g.astype(F32), precision=HI)

    prompt, cached, new = [], [], []
    for g, (win, dil) in enumerate(A_GROUPS):
        bias_g = rel_bias[:, g * A_HEADS:(g + 1) * A_HEADS]
        tbl = lookup(bias_g, _t5_bucket(jnp.maximum(dist, 0) * dil)).transpose(2, 0, 1)
        prompt.append(jnp.where(valid[None], tbl, NEG))
        w = jnp.arange(win)
        tbl_s = lookup(bias_g, _t5_bucket(win - w)).T
        cached.append(jnp.where((w % dil == 0)[None], tbl_s, NEG))
        new.append(lookup(bias_g, _t5_bucket(jnp.zeros((1,), jnp.int32))).T)
    return jnp.stack(prompt), cached, jnp.stack(new)


def _prep_w_in(w):
    offs = np.concatenate([[0], np.cumsum(IN_SIZES)])
    seg = lambda i: w[:, offs[i]:offs[i + 1]]
    order = (0, 1, 2, 3, 4, 5, 6, 8, 9, 12, 7, 10, 11)
    used = sum(IN_SIZES)
    parts = [seg(i) for i in order] + [jnp.zeros((w.shape[0], NZ - used), w.dtype)]
    return jnp.concatenate(parts, axis=1).astype(BF16)


def _selector(offset):
    sel = np.zeros((SMALL_W, C_HEADS * C_DV), np.float32)
    for h in range(C_HEADS):
        sel[offset + h, C_DV * h:C_DV * (h + 1)] = 1.0
    return jnp.asarray(sel)


def _block_avg(width, group):
    idx = np.arange(width) // group
    return jnp.asarray((idx[:, None] == idx[None, :]).astype(np.float32) / group)


def kernel(x_prompt, x_sample, cache_win128_kv, cache_win512_kv, cache_win2048_kv, state_gla, state_gdn, state_conv, norm_mix, w_in, a_q_norm, a_k_norm, rel_bias, b_w_lr, b_lr_bias, b_out_norm, c_conv, c_a_log, c_dt_bias, c_out_norm, w_branch, w_out, norm_ffn, ffn_w_gate, ffn_w_up, ffn_w_down, moe_router, moe_w_gate, moe_w_up, moe_w_down):
    bp, t, d = x_prompt.shape
    bs = x_sample.shape[0]
    mp = bp * t
    caches_t = [jnp.transpose(c, (0, 1, 3, 4, 5, 2)).reshape(c.shape[:2] + (2, A_WIDTH, c.shape[2]))
                for c in (cache_win128_kv, cache_win512_kv, cache_win2048_kv)]
    bias_p, bias_c, bias_n = _bias_tables(rel_bias)
    pavg512 = _block_avg(A_WIDTH, A_DH)
    selb, sela = _selector(SM_CB), _selector(SM_CA)
    rolled = windows = None

    xp = x_prompt.reshape(mp, d)
    xs = x_sample.reshape(bs, d)
    outs_p = {k: [] for k in ("w0", "w1", "w2", "gla", "gdn", "conv")}
    outs_s = {k: [] for k in ("w0", "w1", "w2", "gla", "gdn", "conv")}
    for l in range(DEPTH):
        w_in_l = _prep_w_in(w_in[l])
        wb = w_branch[l].astype(BF16)
        wo = w_out[l].astype(BF16)
        qg128 = jnp.tile(a_q_norm[l], 2).reshape(1, 2 * A_DH)
        kg128 = jnp.tile(a_k_norm[l], 2).reshape(1, 2 * A_DH)
        qg512 = jnp.tile(a_q_norm[l], A_HEADS).reshape(1, A_WIDTH)
        kg512 = jnp.tile(a_k_norm[l], A_HEADS).reshape(1, A_WIDTH)
        wlr_pad = jnp.zeros((SMALL_W, B_HEADS * B_DK), F32).at[:B_RANK].set(b_w_lr[l])
        lr_bias = b_lr_bias[l].reshape(1, -1)
        bon = jnp.tile(b_out_norm[l], B_HEADS).reshape(1, -1)
        con = jnp.tile(c_out_norm[l], C_HEADS).reshape(1, -1)
        dtb_bc = jnp.repeat(c_dt_bias[l], C_DV).reshape(1, -1)
        alog_bc = jnp.repeat(c_a_log[l], C_DV).reshape(1, -1)

        z = _norm_matmul(xp, norm_mix[l], w_in_l, tm=1024, tn=1280)
        z3 = z.reshape(bp, t, NZ)
        o_a, *windows = _attn_prompt(z3, qg128, kg128, bias_p, l, windows)
        o_b, gla_t = _gla_prompt(z3, wlr_pad, lr_bias, tb=512)
        o_c, gdn_s = _gdn_prompt(z3, c_conv[l], selb, sela, dtb_bc, alog_bc, tb=512)
        xp = _merge(z, o_a.reshape(mp, -1), o_b.reshape(mp, -1), o_c.reshape(mp, -1), xp, bon, con, wb, wo, tm=256)
        gla = gla_t.reshape(bp, B_HEADS, B_DV, B_HEADS, B_DK)
        gla = jnp.stack([gla[:, h, :, h, :] for h in range(B_HEADS)], axis=1)
        outs_p["gla"].append(jnp.swapaxes(gla, 2, 3))
        outs_p["gdn"].append(gdn_s)
        outs_p["conv"].append(z3[:, t - (CONV_W - 1):, Z_CQKV:Z_CQKV + C_CONV_CH])

        zs = _norm_matmul(xs, norm_mix[l], w_in_l, tm=bs, tn=1280)
        zs3 = zs.reshape(bs, 1, NZ)
        oa_s, *rolled = _attn_sample(zs3, caches_t, rolled, l, qg512, kg512, pavg512, bias_c, bias_n)
        ob_s, oc_s, nconv, ngla, ngdn = _sample_mix(
            zs3, state_conv[l], state_gla[l], state_gdn[l], wlr_pad, lr_bias, c_conv[l], dtb_bc, alog_bc)
        xs = _merge(zs, oa_s.reshape(bs, -1), ob_s.reshape(bs, -1), oc_s.reshape(bs, -1), xs, bon, con, wb, wo, tm=bs)
        outs_s["gla"].append(ngla)
        outs_s["gdn"].append(ngdn)
        outs_s["conv"].append(nconv)

        i = l // 2
        if l % 2 == 0:
            wg, wu, wd = ffn_w_gate[i].astype(BF16), ffn_w_up[i].astype(BF16), ffn_w_down[i].astype(BF16)
            xp = _ffn(xp, norm_ffn[l], wg, wu, wd, tm=512, tf=1408)
            xs = _ffn(xs, norm_ffn[l], wg, wu, wd, tm=bs, tf=1408)
        else:
            wg, wu, wd = moe_w_gate[i].astype(BF16), moe_w_up[i].astype(BF16), moe_w_down[i].astype(BF16)
            router_pad = jnp.zeros((d, 128), F32).at[:, :N_EXPERTS].set(moe_router[i])
            xp = _moe(xp, norm_ffn[l], router_pad, wg, wu, wd, tm=512, tf=896)
            xs = _moe(xs, norm_ffn[l], router_pad, wg, wu, wd, tm=bs, tf=896)

    st = lambda name, d_: jnp.stack(d_[name])

    def window_out(w):
        w = w.reshape(w.shape[:3] + (A_HEADS, A_DH, w.shape[-1]))
        return jnp.transpose(w, (0, 1, 5, 2, 3, 4))

    return (xp.reshape(bp, t, d), xs.reshape(bs, 1, d),
            window_out(windows[0]), window_out(rolled[0]), window_out(windows[1]), window_out(rolled[1]),
            window_out(windows[2]), window_out(rolled[2]),
            st("gla", outs_p), st("gla", outs_s), st("gdn", outs_p), st("gdn", outs_s),
            st("conv", outs_p), st("conv", outs_s))
```

```python
import functools
import math

import jax
import jax.numpy as jnp
import numpy as np
from jax import lax
from jax.experimental import pallas as pl
from jax.experimental.pallas import tpu as pltpu

F32 = jnp.float32
BF16 = jnp.bfloat16
HI = lax.Precision.HIGHEST
NEG = -1e30

D_MODEL = 1024
DEPTH = 2
EPS = 1e-6
A_GROUPS = ((128, 1), (512, 4), (2048, 16))
N_GROUPS = 3
A_HEADS = 8
A_DH = 64
A_WIDTH = A_HEADS * A_DH
SPAN = 128
LPR = 2 * A_WIDTH // 128
N_BUCKETS = 32
MAX_DIST = 2048
B_HEADS = 4
B_DK = 64
B_DV = 128
B_RANK = 16
B_TAU = 16.0
C_HEADS = 4
C_DK = 128
C_DV = 128
CONV_W = 4
C_CONV_CH = 2 * C_HEADS * C_DK + C_HEADS * C_DV
CHUNK = 64
SUB = 16
BRANCH_W = 512
D_FF = 2816
N_EXPERTS = 8
D_FF_EXPERT = 1792
IN_SIZES = (N_GROUPS * A_WIDTH, N_GROUPS * A_WIDTH, N_GROUPS * A_WIDTH,
            B_HEADS * B_DK, B_HEADS * B_DK, B_HEADS * B_DV, B_HEADS * B_DV, B_RANK,
            C_CONV_CH, C_HEADS * C_DV, C_HEADS, C_HEADS, 3 * D_MODEL)

Z_AQ, Z_AK, Z_AV = 0, 1536, 3072
Z_BQ, Z_BK, Z_BV, Z_BG = 4608, 4864, 5120, 5632
Z_CQKV, Z_CZ, Z_GATES, Z_SMALL = 6144, 7680, 8192, 11264
SMALL_W = 256
NZ = Z_SMALL + SMALL_W
SM_CB, SM_CA = B_RANK, B_RANK + C_HEADS

VMEM_LIMIT = 56 * 1024 * 1024


def _cparams(sem):
    return pltpu.CompilerParams(dimension_semantics=sem, vmem_limit_bytes=VMEM_LIMIT)


def _dot(a, b):
    return jnp.dot(a.astype(BF16), b.astype(BF16), preferred_element_type=F32)


def _dot_nt(a, b):
    return lax.dot_general(a.astype(BF16), b.astype(BF16), (((1,), (1,)), ((), ())), preferred_element_type=F32)


def _dot_tn(a, b):
    return lax.dot_general(a.astype(BF16), b.astype(BF16), (((0,), (0,)), ((), ())), preferred_element_type=F32)


def _dot_hi(a, b):
    return jnp.dot(a, b, precision=HI, preferred_element_type=F32)


def _r16(x):
    return x.astype(BF16).astype(F32)


def _sigmoid(x):
    return jax.nn.sigmoid(x)


def _silu(x):
    return x * jax.nn.sigmoid(x)


def _softplus(x):
    return jnp.maximum(x, 0.0) + jnp.log1p(jnp.exp(-jnp.abs(x)))


def _log_sigmoid(x):
    return jnp.minimum(x, 0.0) - jnp.log1p(jnp.exp(-jnp.abs(x)))


def _rms(x, gain):
    return x * lax.rsqrt(jnp.mean(x * x, axis=-1, keepdims=True) + EPS) * gain


def _norm_matmul_kernel(x_ref, g_ref, w_ref, o_ref, h_ref):
    @pl.when(pl.program_id(1) == 0)
    def _():
        h_ref[...] = _rms(x_ref[...], g_ref[...]).astype(BF16)

    o_ref[...] = jnp.dot(h_ref[...], w_ref[...], preferred_element_type=F32)


def _norm_matmul(x, gain, w, tm, tn):
    m, d = x.shape
    n = w.shape[1]
    return pl.pallas_call(
        _norm_matmul_kernel,
        grid=(m // tm, n // tn),
        in_specs=[pl.BlockSpec((tm, d), lambda i, j: (i, 0)),
                  pl.BlockSpec((1, d), lambda i, j: (0, 0)),
                  pl.BlockSpec((d, tn), lambda i, j: (0, j))],
        out_specs=pl.BlockSpec((tm, tn), lambda i, j: (i, j)),
        out_shape=jax.ShapeDtypeStruct((m, n), F32),
        scratch_shapes=[pltpu.VMEM((tm, d), BF16)],
        compiler_params=_cparams(("parallel", "arbitrary")),
        name="norm_matmul",
    )(x, gain.reshape(1, d), w)


def _pair_rms(x, gain, head0):
    sq = x * x
    s0 = jnp.sum(jnp.where(head0, sq, 0.0), axis=-1, keepdims=True)
    s1 = jnp.sum(sq, axis=-1, keepdims=True) - s0
    return x * lax.rsqrt(jnp.where(head0, s0, s1) * (1.0 / A_DH) + EPS) * gain


def _attn_prompt_kernel(q1, k1, v1, q2, k2, v2, q3, k3, v3, qg_ref, kg_ref, bias_ref,
                        o_ref, w1, w2, w3, qt, kt, qs, ks, vs, og, lg, to, tl, *, slots):
    t_len = o_ref.shape[0]
    head0 = lax.broadcasted_iota(jnp.int32, (1, 2 * A_DH), 1) < A_DH
    first_half = lax.broadcasted_iota(jnp.int32, (1, 2 * SPAN), 1) < SPAN
    groups = ((q1, k1, v1, w1), (q2, k2, v2, w2), (q3, k3, v3, w3))
    for g, (qr, kr, vr, wr) in enumerate(groups):
        dil = A_GROUPS[g][1]
        length = t_len // dil
        nb = length // SPAN
        qt[...] = _pair_rms(qr[...], qg_ref[...], head0) * (A_DH ** -0.5)
        kt[...] = _pair_rms(kr[...], kg_ref[...], head0)
        wlen = wr.shape[-1]
        k_win = kt[t_len - wlen:, :].T
        v_win = vr[t_len - wlen:, :].T
        for s in range(slots):
            wr[s, 0] = k_win
            wr[s, 1] = v_win
        knr = kt
        for r in range(dil):
            rows = pl.ds(r, length, stride=dil) if dil > 1 else pl.ds(0, length)
            dst = pl.ds(r * length, length)
            qs[dst, :] = qt[rows, :].astype(BF16)
            ks[dst, :] = knr[rows, :].astype(BF16)
            vs[dst, :] = vr[rows, :].astype(BF16)

        def block(j, carry, g=g, nb=nb):
            row0 = pl.multiple_of(j * SPAN, SPAN)
            qb = qs[pl.ds(row0, SPAN), :]
            if nb > 1:
                prow = pl.multiple_of(jnp.maximum(row0 - SPAN, 0), SPAN)
                k2 = jnp.concatenate([ks[pl.ds(prow, SPAN), :], ks[pl.ds(row0, SPAN), :]], axis=0)
                v2 = jnp.concatenate([vs[pl.ds(prow, SPAN), :], vs[pl.ds(row0, SPAN), :]], axis=0)
                pen = jnp.where(first_half, jnp.where(j % nb == 0, NEG, 0.0), 0.0)
            else:
                k2 = ks[pl.ds(row0, SPAN), :]
                v2 = vs[pl.ds(row0, SPAN), :]
            outs, lses = [], []
            for h in range(2):
                qh = jnp.where(head0 if h == 0 else jnp.logical_not(head0), qb, jnp.zeros_like(qb))
                logits = lax.dot_general(qh, k2, (((1,), (1,)), ((), ())), preferred_element_type=F32)
                if nb > 1:
                    logits = logits + bias_ref[g, h] + pen
                else:
                    logits = logits + bias_ref[g, h, :, SPAN:]
                m = jnp.max(logits, axis=-1, keepdims=True)
                p = jnp.exp(logits - m)
                s = jnp.sum(p, axis=-1, keepdims=True)
                acc = jnp.dot(p.astype(BF16), v2, preferred_element_type=F32)
                outs.append(acc / s)
                lses.append(jnp.broadcast_to(m + jnp.log(s), (SPAN, 2 * A_DH)))
            to[pl.ds(row0, SPAN), :] = jnp.where(head0, outs[0], outs[1])
            tl[pl.ds(row0, SPAN), :] = jnp.where(head0, lses[0], lses[1])
            return carry

        lax.fori_loop(0, t_len // SPAN, block, 0, unroll=2)
        for r in range(dil):
            rows = pl.ds(r, length, stride=dil) if dil > 1 else pl.ds(0, length)
            src = pl.ds(r * length, length)
            og[g, rows, :] = to[src, :]
            lg[g, rows, :] = tl[src, :]
    l_1, l_2, l_3 = lg[0], lg[1], lg[2]
    mx = jnp.maximum(jnp.maximum(l_1, l_2), l_3)
    e_1, e_2, e_3 = jnp.exp(l_1 - mx), jnp.exp(l_2 - mx), jnp.exp(l_3 - mx)
    o_ref[...] = (e_1 * og[0] + e_2 * og[1] + e_3 * og[2]) / (e_1 + e_2 + e_3)


def _attn_prompt(z3, qgain, kgain, bias, layer, windows):
    b, t, _ = z3.shape
    hp_blocks = A_WIDTH // 128

    def col(base, g):
        return lambda i, hp: (i, 0, base // 128 + g * hp_blocks + hp)

    in_specs = []
    for g in range(N_GROUPS):
        for base in (Z_AQ, Z_AK, Z_AV):
            in_specs.append(pl.BlockSpec((None, t, 128), col(base, g)))
    in_specs += [pl.BlockSpec((1, 128), lambda i, hp: (0, 0)),
                 pl.BlockSpec((1, 128), lambda i, hp: (0, 0)),
                 pl.BlockSpec((N_GROUPS, 2, SPAN, 2 * SPAN), lambda i, hp: (0, hp, 0, 0))]
    args = [z3] * 9 + [qgain, kgain, bias]
    slots = DEPTH if windows is None else 1
    win_specs, win_shapes = [], []
    for win, _ in A_GROUPS:
        wp = min(win, t)
        win_specs.append(pl.BlockSpec((slots, None, 2, 128, wp), lambda i, hp: (layer // slots, i, 0, hp, 0)))
        win_shapes.append(jax.ShapeDtypeStruct((DEPTH, b, 2, A_WIDTH, wp), F32))
    aliases = {}
    if windows is not None:
        aliases = {len(args) + g: 1 + g for g in range(N_GROUPS)}
        in_specs += [pl.BlockSpec(memory_space=pl.ANY)] * N_GROUPS
        args += list(windows)
    kern = functools.partial(_attn_prompt_kernel, slots=slots)
    if windows is not None:
        kern = functools.partial(_drop_refs, kern, len(args) - N_GROUPS, N_GROUPS)
    return pl.pallas_call(
        kern,
        grid=(b, hp_blocks),
        in_specs=in_specs,
        out_specs=[pl.BlockSpec((None, t, 128), lambda i, hp: (i, 0, hp))] + win_specs,
        out_shape=[jax.ShapeDtypeStruct((b, t, A_WIDTH), F32)] + win_shapes,
        scratch_shapes=[pltpu.VMEM((t, 128), F32), pltpu.VMEM((t, 128), F32),
                        pltpu.VMEM((t, 128), BF16), pltpu.VMEM((t, 128), BF16), pltpu.VMEM((t, 128), BF16),
                        pltpu.VMEM((N_GROUPS, t, 128), F32), pltpu.VMEM((N_GROUPS, t, 128), F32),
                        pltpu.VMEM((t, 128), F32), pltpu.VMEM((t, 128), F32)],
        input_output_aliases=aliases,
        compiler_params=_cparams(("parallel", "parallel")),
        name="attn_prompt",
    )(*args)


def _drop_refs(kern, start, count, *refs):
    return kern(*refs[:start], *refs[start + count:])


def _gla_prompt_kernel(q_ref, k_ref, v_ref, sm_ref, wlr_ref, lrb_ref, o_ref, st_ref, s_scr, la_scr):
    tb = q_ref.shape[0]
    hk = B_HEADS * B_DK
    hv = B_HEADS * B_DV

    @pl.when(pl.program_id(1) == 0)
    def _():
        s_scr[...] = jnp.zeros_like(s_scr)

    la_scr[...] = _log_sigmoid(_dot(sm_ref[...], wlr_ref[...]) + lrb_ref[...]) * (1.0 / B_TAU)

    rr = lax.broadcasted_iota(jnp.int32, (CHUNK, CHUNK), 0)
    cc = lax.broadcasted_iota(jnp.int32, (CHUNK, CHUNK), 1)
    causal = rr >= cc
    ltri = causal.astype(F32)
    lane_head = lax.broadcasted_iota(jnp.int32, (1, hk), 1) // B_DK
    row = lax.broadcasted_iota(jnp.int32, (CHUNK, 1), 0)
    same_head = (lax.broadcasted_iota(jnp.int32, (hv, hk), 0) // B_DV
                 == lax.broadcasted_iota(jnp.int32, (hv, hk), 1) // B_DK)

    def chunk(c, carry):
        r0 = pl.multiple_of(c * CHUNK, CHUNK)
        b = _dot_hi(ltri, la_scr[pl.ds(r0, CHUNK), :])
        q = q_ref[pl.ds(r0, CHUNK), :] * (B_DK ** -0.5)
        k = k_ref[pl.ds(r0, CHUNK), :]
        vb = v_ref[pl.ds(r0, CHUNK), :].astype(BF16)
        s = s_scr[...]
        b_last = b[CHUNK - 1:CHUNK, :]
        o_inter = _dot_nt(q * jnp.exp(b), s)
        atts = []
        for i in range(CHUNK // SUB):
            b_ref0 = b[SUB * i:SUB * i + 1, :]
            qi = q[SUB * i:SUB * (i + 1), :] * jnp.exp(b[SUB * i:SUB * (i + 1), :] - b_ref0)
            qst = jnp.concatenate([jnp.where(lane_head == h, qi, 0.0) for h in range(B_HEADS)], axis=0)
            kt = k * jnp.exp(jnp.where(row < SUB * (i + 1), b_ref0 - b, 0.0))
            atts.append(_dot_nt(qst, kt))
        o_parts = []
        for h in range(B_HEADS):
            att = jnp.concatenate([a[SUB * h:SUB * (h + 1), :] for a in atts], axis=0)
            att = jnp.where(causal, att, 0.0)
            o_parts.append(_dot(att, vb[:, B_DV * h:B_DV * (h + 1)]))
        o_ref[pl.ds(r0, CHUNK), :] = o_inter + jnp.concatenate(o_parts, axis=1)
        upd = _dot_tn(vb, k * jnp.exp(b_last - b))
        s_scr[...] = s * jnp.exp(b_last) + jnp.where(same_head, upd, 0.0)
        return carry

    lax.fori_loop(0, tb // CHUNK, chunk, 0, unroll=2)

    @pl.when(pl.program_id(1) == pl.num_programs(1) - 1)
    def _():
        st_ref[...] = s_scr[...]


def _gla_prompt(z3, wlr_pad, lr_bias, tb):
    b, t, _ = z3.shape
    hk, hv = B_HEADS * B_DK, B_HEADS * B_DV
    return pl.pallas_call(
        _gla_prompt_kernel,
        grid=(b, t // tb),
        in_specs=[pl.BlockSpec((None, tb, hk), lambda i, j: (i, j, Z_BQ // hk)),
                  pl.BlockSpec((None, tb, hk), lambda i, j: (i, j, Z_BK // hk)),
                  pl.BlockSpec((None, tb, hv), lambda i, j: (i, j, Z_BV // hv)),
                  pl.BlockSpec((None, tb, SMALL_W), lambda i, j: (i, j, Z_SMALL // SMALL_W)),
                  pl.BlockSpec((SMALL_W, hk), lambda i, j: (0, 0)),
                  pl.BlockSpec((1, hk), lambda i, j: (0, 0))],
        out_specs=[pl.BlockSpec((None, tb, hv), lambda i, j: (i, j, 0)),
                   pl.BlockSpec((None, hv, hk), lambda i, j: (i, 0, 0))],
        out_shape=[jax.ShapeDtypeStruct((b, t, hv), F32), jax.ShapeDtypeStruct((b, hv, hk), F32)],
        scratch_shapes=[pltpu.VMEM((hv, hk), F32), pltpu.VMEM((tb, hk), F32)],
        compiler_params=_cparams(("parallel", "arbitrary")),
        name="gla_prompt",
    )(z3, z3, z3, z3, wlr_pad, lr_bias)


def _gdn_prompt_kernel(x_ref, sm_ref, cw_ref, selb_ref, sela_ref, dtb_ref, alog_ref, o_ref, s_ref,
                       xpad, cv, gsc, bsc, s_scr):
    tb = x_ref.shape[0]
    hd = C_HEADS * C_DK
    pad = 8

    @pl.when(pl.program_id(1) == 0)
    def _():
        s_scr[...] = jnp.zeros_like(s_scr)
        xpad[0:pad, :] = jnp.zeros((pad, C_CONV_CH), F32)

    xpad[pad:pad + tb, :] = x_ref[...]
    rb = 128
    for i in range(tb // rb):
        acc = None
        for j in range(CONV_W):
            lo = pad - (CONV_W - 1) + j + i * rb
            term = xpad[lo:lo + rb, :] * cw_ref[j:j + 1, :]
            acc = term if acc is None else acc + term
        y = _silu(acc)
        rows = slice(i * rb, (i + 1) * rb)
        for h in range(C_HEADS):
            qh = y[:, C_DK * h:C_DK * (h + 1)]
            cv[rows, C_DK * h:C_DK * (h + 1)] = (
                qh * lax.rsqrt(jnp.sum(qh * qh, axis=-1, keepdims=True) + EPS) * (C_DK ** -0.5))
            kh = y[:, hd + C_DK * h:hd + C_DK * (h + 1)]
            cv[rows, hd + C_DK * h:hd + C_DK * (h + 1)] = (
                kh * lax.rsqrt(jnp.sum(kh * kh, axis=-1, keepdims=True) + EPS))
        cv[rows, 2 * hd:] = y[:, 2 * hd:]
    xpad[pad - (CONV_W - 1):pad, :] = xpad[pad + tb - (CONV_W - 1):pad + tb, :]

    sm = sm_ref[...]
    gsc[...] = -jnp.exp(alog_ref[...]) * _softplus(_dot_hi(sm, sela_ref[...]) + dtb_ref[...])
    bsc[...] = _sigmoid(_dot_hi(sm, selb_ref[...]))

    n = C_HEADS * CHUNK
    rr = lax.broadcasted_iota(jnp.int32, (CHUNK, CHUNK), 0)
    cc = lax.broadcasted_iota(jnp.int32, (CHUNK, CHUNK), 1)
    ltri = (rr >= cc).astype(F32)
    ri = lax.broadcasted_iota(jnp.int32, (n, n), 0)
    ci = lax.broadcasted_iota(jnp.int32, (n, n), 1)
    same = (ri // CHUNK) == (ci // CHUNK)
    incl = jnp.logical_and(same, ci <= ri)
    strict = jnp.logical_and(same, ci < ri)
    eye = (ri == ci).astype(F32)

    def stack(x):
        return jnp.concatenate([x[:, C_DK * h:C_DK * (h + 1)] for h in range(C_HEADS)], axis=0)

    def chunk(c, carry):
        r0 = pl.multiple_of(c * CHUNK, CHUNK)
        bst = stack(_dot_hi(ltri, gsc[pl.ds(r0, CHUNK), :]))
        beta = stack(bsc[pl.ds(r0, CHUNK), :])
        qst = stack(cv[pl.ds(r0, CHUNK), 0:hd])
        kst = stack(cv[pl.ds(r0, CHUNK), hd:2 * hd])
        vst = stack(cv[pl.ds(r0, CHUNK), 2 * hd:3 * hd])
        kb = kst.astype(BF16)
        kk = lax.dot_general(kb, kb, (((1,), (1,)), ((), ())), preferred_element_type=F32)
        qk = lax.dot_general(qst.astype(BF16), kb, (((1,), (1,)), ((), ())), preferred_element_type=F32)
        col_b = jnp.concatenate([bst, bst], axis=1)
        bst_t = bst.T
        row_b = jnp.concatenate([bst_t, bst_t], axis=0)
        dec = jnp.exp(jnp.where(incl, col_b - row_b, NEG))
        a = jnp.where(strict, jnp.concatenate([beta, beta], axis=1) * dec * kk, 0.0)
        aqk = dec * qk
        inv = eye - a
        pw = a
        for _ in range(int(math.log2(CHUNK)) - 1):
            pw = _dot(pw, pw)
            inv = inv + _dot(inv, pw)
        eb = jnp.exp(bst)
        wu = _dot(inv, jnp.concatenate([beta * eb * kst, beta * vst], axis=1))
        w, uv = wu[:, :C_DK], wu[:, C_DK:]
        us, qss = [], []
        for h in range(C_HEADS):
            sl = slice(CHUNK * h, CHUNK * (h + 1))
            ws = _dot(jnp.concatenate([w[sl], qst[sl]], axis=0), s_scr[h])
            us.append(uv[sl] - ws[:CHUNK])
            qss.append(ws[CHUNK:])
        ust = jnp.concatenate(us, axis=0)
        o = eb * jnp.concatenate(qss, axis=0) + _dot(aqk, ust)
        for h in range(C_HEADS):
            sl = slice(CHUNK * h, CHUNK * (h + 1))
            o_ref[pl.ds(r0, CHUNK), C_DV * h:C_DV * (h + 1)] = o[sl]
            b_last = bst[CHUNK * (h + 1) - 1:CHUNK * (h + 1), :]
            s_scr[h] = jnp.exp(b_last) * s_scr[h] + _dot_tn(kst[sl] * jnp.exp(b_last - bst[sl]), us[h])
        return carry

    lax.fori_loop(0, tb // CHUNK, chunk, 0, unroll=2)

    @pl.when(pl.program_id(1) == pl.num_programs(1) - 1)
    def _():
        s_ref[...] = s_scr[...]


def _gdn_prompt(z3, conv_w, selb, sela, dtb_bc, alog_bc, tb):
    b, t, _ = z3.shape
    hd = C_HEADS * C_DV
    const = lambda shape: pl.BlockSpec(shape, lambda i, j: (0,) * len(shape))
    return pl.pallas_call(
        _gdn_prompt_kernel,
        grid=(b, t // tb),
        in_specs=[pl.BlockSpec((None, tb, C_CONV_CH), lambda i, j: (i, j, Z_CQKV // C_CONV_CH)),
                  pl.BlockSpec((None, tb, SMALL_W), lambda i, j: (i, j, Z_SMALL // SMALL_W)),
                  const((CONV_W, C_CONV_CH)), const((SMALL_W, hd)), const((SMALL_W, hd)),
                  const((1, hd)), const((1, hd))],
        out_specs=[pl.BlockSpec((None, tb, hd), lambda i, j: (i, j, 0)),
                   pl.BlockSpec((None, C_HEADS, C_DK, C_DV), lambda i, j: (i, 0, 0, 0))],
        out_shape=[jax.ShapeDtypeStruct((b, t, hd), F32),
                   jax.ShapeDtypeStruct((b, C_HEADS, C_DK, C_DV), F32)],
        scratch_shapes=[pltpu.VMEM((tb + 8, C_CONV_CH), F32), pltpu.VMEM((tb, C_CONV_CH), F32),
                        pltpu.VMEM((tb, hd), F32), pltpu.VMEM((tb, hd), F32),
                        pltpu.VMEM((C_HEADS, C_DK, C_DV), F32)],
        compiler_params=_cparams(("parallel", "arbitrary")),
        name="gdn_prompt",
    )(z3, z3, conv_w, selb, sela, dtb_bc, alog_bc)


def _head_norm(o, gain, width):
    parts = []
    for h in range(o.shape[1] // width):
        oh = o[:, width * h:width * (h + 1)]
        parts.append(oh * lax.rsqrt(jnp.mean(oh * oh, axis=-1, keepdims=True) + EPS))
    return jnp.concatenate(parts, axis=1) * gain


def _merge_kernel(oa_ref, ob_ref, bg_ref, oc_ref, cz_ref, ga_ref, gb_ref, gc_ref, x_ref,
                  bon_ref, con_ref, wb_ref, wo_ref, y_ref):
    ob = _head_norm(ob_ref[...], bon_ref[...], B_DV) * _silu(bg_ref[...])
    oc = _head_norm(oc_ref[...], con_ref[...], C_DV) * _silu(cz_ref[...])
    merged = (_sigmoid(ga_ref[...]) * _dot(oa_ref[...], wb_ref[0])
              + _sigmoid(gb_ref[...]) * _dot(ob, wb_ref[1])
              + _sigmoid(gc_ref[...]) * _dot(oc, wb_ref[2]))
    y_ref[...] = x_ref[...] + _dot(merged, wo_ref[...])


def _merge(z, oa, ob, oc, x, bon, con, wb, wo, tm):
    m = x.shape[0]
    w = BRANCH_W
    row = lambda width, blk: pl.BlockSpec((tm, width), lambda i: (i, blk))
    const = lambda shape: pl.BlockSpec(shape, lambda i: (0,) * len(shape))
    return pl.pallas_call(
        _merge_kernel,
        grid=(m // tm,),
        in_specs=[row(w, 0), row(w, 0), row(w, Z_BG // w), row(w, 0), row(w, Z_CZ // w),
                  row(D_MODEL, Z_GATES // D_MODEL), row(D_MODEL, Z_GATES // D_MODEL + 1),
                  row(D_MODEL, Z_GATES // D_MODEL + 2), row(D_MODEL, 0),
                  const((1, w)), const((1, w)), const((3, w, D_MODEL)), const((D_MODEL, D_MODEL))],
        out_specs=row(D_MODEL, 0),
        out_shape=jax.ShapeDtypeStruct((m, D_MODEL), F32),
        compiler_params=_cparams(("parallel",)),
        name="merge",
    )(oa, ob, z, oc, z, z, z, z, x, bon, con, wb, wo)


def _ffn_kernel(x_ref, g_ref, wg_ref, wu_ref, wd_ref, y_ref, h_ref, acc_ref):
    f = pl.program_id(1)

    @pl.when(f == 0)
    def _():
        h_ref[...] = _rms(x_ref[...], g_ref[...]).astype(BF16)
        acc_ref[...] = jnp.zeros_like(acc_ref)

    h = h_ref[...]
    a = _silu(jnp.dot(h, wg_ref[...], preferred_element_type=F32)) * jnp.dot(h, wu_ref[...],
                                                                              preferred_element_type=F32)
    acc_ref[...] += jnp.dot(a.astype(BF16), wd_ref[...], preferred_element_type=F32)

    @pl.when(f == pl.num_programs(1) - 1)
    def _():
        y_ref[...] = x_ref[...] + acc_ref[...]


def _ffn(x, gain, wg, wu, wd, tm, tf):
    m, d = x.shape
    ff = wg.shape[1]
    return pl.pallas_call(
        _ffn_kernel,
        grid=(m // tm, ff // tf),
        in_specs=[pl.BlockSpec((tm, d), lambda i, f: (i, 0)),
                  pl.BlockSpec((1, d), lambda i, f: (0, 0)),
                  pl.BlockSpec((d, tf), lambda i, f: (0, f)),
                  pl.BlockSpec((d, tf), lambda i, f: (0, f)),
                  pl.BlockSpec((tf, d), lambda i, f: (f, 0))],
        out_specs=pl.BlockSpec((tm, d), lambda i, f: (i, 0)),
        out_shape=jax.ShapeDtypeStruct((m, d), F32),
        scratch_shapes=[pltpu.VMEM((tm, d), BF16), pltpu.VMEM((tm, d), F32)],
        compiler_params=_cparams(("parallel", "arbitrary")),
        name="ffn",
    )(x, gain.reshape(1, d), wg, wu, wd)


def _moe_kernel(x_ref, g_ref, rt_ref, wg_ref, wu_ref, wd_ref, y_ref, h_ref, acc_ref, gate_ref):
    e = pl.program_id(1)
    f = pl.program_id(2)
    lane = lax.broadcasted_iota(jnp.int32, (1, 128), 1).astype(F32)

    @pl.when(jnp.logical_and(e == 0, f == 0))
    def _():
        hf = _rms(x_ref[...], g_ref[...])
        h_ref[...] = hf.astype(BF16)
        acc_ref[...] = jnp.zeros_like(acc_ref)
        valid = lane < N_EXPERTS
        logits = jnp.where(valid, _dot(hf, rt_ref[...]), NEG)
        ex = jnp.exp(logits - jnp.max(logits, axis=-1, keepdims=True))
        probs = ex / jnp.sum(ex, axis=-1, keepdims=True)
        m1 = jnp.max(probs, axis=-1, keepdims=True)
        i1 = jnp.min(jnp.where(jnp.logical_and(probs == m1, valid), lane, 128.0), axis=-1, keepdims=True)
        hot1 = lane == i1
        rest = jnp.where(jnp.logical_or(hot1, jnp.logical_not(valid)), -1.0, probs)
        m2 = jnp.max(rest, axis=-1, keepdims=True)
        i2 = jnp.min(jnp.where(rest == m2, lane, 128.0), axis=-1, keepdims=True)
        hot2 = lane == i2
        den = m1 + m2
        gate_ref[...] = jnp.where(hot1, m1 / den, 0.0) + jnp.where(hot2, m2 / den, 0.0)

    h = h_ref[...]
    ge = jnp.sum(jnp.where(lane == e.astype(F32), gate_ref[...], 0.0), axis=-1, keepdims=True)
    a = (_silu(jnp.dot(h, wg_ref[...], preferred_element_type=F32))
         * jnp.dot(h, wu_ref[...], preferred_element_type=F32) * ge)
    acc_ref[...] += jnp.dot(a.astype(BF16), wd_ref[...], preferred_element_type=F32)

    @pl.when(jnp.logical_and(e == pl.num_programs(1) - 1, f == pl.num_programs(2) - 1))
    def _():
        y_ref[...] = x_ref[...] + acc_ref[...]


def _moe(x, gain, router_pad, wg, wu, wd, tm, tf):
    m, d = x.shape
    ne, _, ff = wg.shape
    return pl.pallas_call(
        _moe_kernel,
        grid=(m // tm, ne, ff // tf),
        in_specs=[pl.BlockSpec((tm, d), lambda i, e, f: (i, 0)),
                  pl.BlockSpec((1, d), lambda i, e, f: (0, 0)),
                  pl.BlockSpec((d, 128), lambda i, e, f: (0, 0)),
                  pl.BlockSpec((None, d, tf), lambda i, e, f: (e, 0, f)),
                  pl.BlockSpec((None, d, tf), lambda i, e, f: (e, 0, f)),
                  pl.BlockSpec((None, tf, d), lambda i, e, f: (e, f, 0))],
        out_specs=pl.BlockSpec((tm, d), lambda i, e, f: (i, 0)),
        out_shape=jax.ShapeDtypeStruct((m, d), F32),
        scratch_shapes=[pltpu.VMEM((tm, d), BF16), pltpu.VMEM((tm, d), F32), pltpu.VMEM((tm, 128), F32)],
        compiler_params=_cparams(("parallel", "arbitrary", "arbitrary")),
        name="moe",
    )(x, gain.reshape(1, d), router_pad, wg, wu, wd)


MOE_ROWS = 288


def _moe_routed_kernel(x_ref, g_ref, rt_ref, wg_ref, wu_ref, wd_ref, y_ref,
                       h_ref, gate_scr, mask_scr, rank_scr, tri_scr):
    e = pl.program_id(1)
    tm = x_ref.shape[0]
    sub = lax.broadcasted_iota(jnp.int32, (N_EXPERTS, 1), 0).astype(F32)

    @pl.when(e == 0)
    def _():
        x = x_ref[...]
        hf = _rms(x, g_ref[...])
        h_ref[...] = hf.astype(BF16)
        y_ref[...] = x
        logits = _dot_nt(rt_ref[...], hf)
        ex = jnp.exp(logits - jnp.max(logits, axis=0, keepdims=True))
        probs = ex / jnp.sum(ex, axis=0, keepdims=True)
        m1 = jnp.max(probs, axis=0, keepdims=True)
        hot1 = sub == jnp.min(jnp.where(probs == m1, sub, float(N_EXPERTS)), axis=0, keepdims=True)
        rest = jnp.where(hot1, -1.0, probs)
        m2 = jnp.max(rest, axis=0, keepdims=True)
        hot2 = sub == jnp.min(jnp.where(rest == m2, sub, float(N_EXPERTS)), axis=0, keepdims=True)
        den = m1 + m2
        gate_scr[...] = jnp.where(hot1, m1 / den, 0.0) + jnp.where(hot2, m2 / den, 0.0)
        mask = jnp.where(jnp.logical_or(hot1, hot2), 1.0, 0.0)
        mask_scr[...] = mask
        tri_scr[...] = jnp.where(lax.broadcasted_iota(jnp.int32, (tm, tm), 0)
                                 <= lax.broadcasted_iota(jnp.int32, (tm, tm), 1), 1.0, 0.0).astype(BF16)
        rank_scr[...] = jnp.dot(mask.astype(BF16), tri_scr[...], preferred_element_type=F32) - 1.0

    mine = sub == e.astype(F32)
    row = lambda ref: jnp.sum(jnp.where(mine, ref[...], 0.0), axis=0, keepdims=True)
    gate_row, mask_row, rank_row = row(gate_scr), row(mask_scr), row(rank_scr)
    count = jnp.sum(mask_row).astype(jnp.int32)
    slot = lax.broadcasted_iota(jnp.int32, (MOE_ROWS, 1), 0).astype(F32)

    def chunk(j, carry):
        base = (j * MOE_ROWS).astype(F32)
        pick = jnp.where(jnp.logical_and(rank_row == slot + base, mask_row > 0.0), 1.0, 0.0)
        pick16 = pick.astype(BF16)
        xg = jnp.dot(pick16, h_ref[...], preferred_element_type=F32).astype(BF16)
        gate_col = jnp.sum(pick * gate_row, axis=1, keepdims=True)
        a = (_silu(jnp.dot(xg, wg_ref[...], preferred_element_type=F32))
             * jnp.dot(xg, wu_ref[...], preferred_element_type=F32) * gate_col)
        out = jnp.dot(a.astype(BF16), wd_ref[...], preferred_element_type=F32)
        y_ref[...] += _dot_tn(pick16, out)
        return carry

    lax.fori_loop(0, (count + MOE_ROWS - 1) // MOE_ROWS, chunk, 0)


def _moe_routed(x, gain, router_t, wg, wu, wd, tm):
    m, d = x.shape
    ne, _, ff = wg.shape
    return pl.pallas_call(
        _moe_routed_kernel,
        grid=(m // tm, ne),
        in_specs=[pl.BlockSpec((tm, d), lambda i, e: (i, 0)),
                  pl.BlockSpec((1, d), lambda i, e: (0, 0)),
                  pl.BlockSpec((ne, d), lambda i, e: (0, 0)),
                  pl.BlockSpec((None, d, ff), lambda i, e: (e, 0, 0)),
                  pl.BlockSpec((None, d, ff), lambda i, e: (e, 0, 0)),
                  pl.BlockSpec((None, ff, d), lambda i, e: (e, 0, 0))],
        out_specs=pl.BlockSpec((tm, d), lambda i, e: (i, 0)),
        out_shape=jax.ShapeDtypeStruct((m, d), F32),
        scratch_shapes=[pltpu.VMEM((tm, d), BF16), pltpu.VMEM((ne, tm), F32), pltpu.VMEM((ne, tm), F32),
                        pltpu.VMEM((ne, tm), F32), pltpu.VMEM((tm, tm), BF16)],
        compiler_params=_cparams(("parallel", "arbitrary")),
        name="moe_routed",
    )(x, gain.reshape(1, d), router_t, wg, wu, wd)


def _attn_sample_kernel(*refs, layer, full_roll):
    zq_ref, qg_ref, kg_ref, pavg_ref, b0_ref, b1_ref, b2_ref, bias0_ref, c0_ref, c1_ref, c2_ref = refs[:11]
    if full_roll:
        tails = (None,) * N_GROUPS
        rest = refs[11:]
    else:
        tails = refs[11:14]
        rest = refs[14:]
    o_ref, n0_ref, n1_ref, n2_ref, p0_scr, p1_scr, p2_scr, st_scr, oacc_scr = rest
    half = pl.program_id(1)
    active = (pl.program_id(2) == layer) if full_roll else True
    aw = A_WIDTH
    sel = (lax.broadcasted_iota(jnp.int32, (A_HEADS, aw), 1) // A_DH
           == lax.broadcasted_iota(jnp.int32, (A_HEADS, aw), 0))
    last_lane = lax.broadcasted_iota(jnp.int32, (1, 128), 1) == 127
    groups = ((c0_ref, n0_ref, b0_ref, p0_scr, tails[0]), (c1_ref, n1_ref, b1_ref, p1_scr, tails[1]),
              (c2_ref, n2_ref, b2_ref, p2_scr, tails[2]))
    for g, (c_ref, n_ref, b_ref, p_scr, t_ref) in enumerate(groups):
        wlen = c_ref.shape[1]
        if full_roll:
            n_ref[...] = pltpu.roll(c_ref[...], wlen - 1, axis=1)

        def newest(col, n_ref=n_ref, t_ref=t_ref, wlen=wlen):
            old = n_ref[:, wlen - 128:] if full_roll else t_ref[...]
            dst = n_ref.at[:, wlen - 128:] if full_roll else n_ref
            dst[...] = jnp.where(last_lane, col, old)

        lo = g * aw

        @pl.when(jnp.logical_and(active, half == 0))
        def _(g=g, c_ref=c_ref, b_ref=b_ref, p_scr=p_scr, lo=lo, newest=newest):
            q = zq_ref[:, Z_AQ + lo:Z_AQ + lo + aw]
            k = zq_ref[:, Z_AK + lo:Z_AK + lo + aw]
            qn = q * lax.rsqrt(_dot_hi(q * q, pavg_ref[...]) + EPS) * qg_ref[...] * (A_DH ** -0.5)
            kn = k * lax.rsqrt(_dot_hi(k * k, pavg_ref[...]) + EPS) * kg_ref[...]
            qbd = jnp.where(sel, jnp.broadcast_to(qn, (A_HEADS, aw)), 0.0)
            lc = jnp.sum((c_ref[...] * _to_col(qn, aw)).reshape(A_HEADS, A_DH, c_ref.shape[1]), axis=1) + b_ref[...]
            l0 = jnp.sum(qbd * kn, axis=1, keepdims=True) + bias0_ref[g]
            m = jnp.maximum(jnp.max(lc, axis=1, keepdims=True), l0)
            pc = jnp.exp(lc - m)
            p0 = jnp.exp(l0 - m)
            p_scr[...] = pc
            st_scr[g, 0] = jnp.broadcast_to(p0, (A_HEADS, 128))
            st_scr[g, 1] = jnp.broadcast_to(jnp.sum(pc, axis=1, keepdims=True) + p0, (A_HEADS, 128))
            st_scr[g, 2] = jnp.broadcast_to(m, (A_HEADS, 128))
            newest(_to_col(kn, aw))

        @pl.when(jnp.logical_and(active, half == 1))
        def _(g=g, c_ref=c_ref, p_scr=p_scr, lo=lo, newest=newest):
            v = zq_ref[:, Z_AV + lo:Z_AV + lo + aw]
            p0 = st_scr[g, 0][:, 0:1]
            s = st_scr[g, 1][:, 0:1]
            m = st_scr[g, 2][:, 0:1]
            o8 = (_dot_nt(p_scr[...], c_ref[...]) + _r16(p0) * _r16(v)) / s
            oacc_scr[g, 0:1, :] = jnp.sum(jnp.where(sel, o8, 0.0), axis=0, keepdims=True)
            oacc_scr[g, 1:2, :] = jnp.sum(jnp.where(sel, m + jnp.log(s), 0.0), axis=0, keepdims=True)
            newest(_to_col(v, aw))

    @pl.when(jnp.logical_and(active, half == 1))
    def _():
        l_1, l_2, l_3 = oacc_scr[0, 1:2, :], oacc_scr[1, 1:2, :], oacc_scr[2, 1:2, :]
        mx = jnp.maximum(jnp.maximum(l_1, l_2), l_3)
        e_1, e_2, e_3 = jnp.exp(l_1 - mx), jnp.exp(l_2 - mx), jnp.exp(l_3 - mx)
        o_ref[...] = ((e_1 * oacc_scr[0, 0:1, :] + e_2 * oacc_scr[1, 0:1, :] + e_3 * oacc_scr[2, 0:1, :])
                      / (e_1 + e_2 + e_3))


def _attn_sample(zs3, caches_t, rolled, layer, qgain, kgain, pavg, biases, bias0):
    bd = zs3.shape[0]
    full_roll = rolled is None
    aw = A_WIDTH
    if full_roll:
        grid = (bd, 2, DEPTH)
        const = lambda shape: pl.BlockSpec(shape, lambda i, h, l: (0,) * len(shape))
        zspec = pl.BlockSpec((None, 1, Z_AV + N_GROUPS * aw), lambda i, h, l: (i, 0, 0))
        ospec = pl.BlockSpec((None, 1, aw), lambda i, h, l: (i, 0, 0))
        cache_specs = [pl.BlockSpec((None, None, None, aw, c.shape[-1]), lambda i, h, l: (l, i, h, 0, 0))
                       for c in caches_t]
        new_specs = cache_specs
        extra_specs, extra_args, aliases = [], [], {}
        sem = ("parallel", "arbitrary", "arbitrary")
    else:
        grid = (bd, 2)
        const = lambda shape: pl.BlockSpec(shape, lambda i, h: (0,) * len(shape))
        zspec = pl.BlockSpec((None, 1, Z_AV + N_GROUPS * aw), lambda i, h: (i, 0, 0))
        ospec = pl.BlockSpec((None, 1, aw), lambda i, h: (i, 0, 0))
        cache_specs = [pl.BlockSpec((None, None, None, aw, c.shape[-1]), lambda i, h: (layer, i, h, 0, 0))
                       for c in caches_t]
        new_specs = [pl.BlockSpec((None, None, None, aw, 128), functools.partial(
            lambda last, i, h: (layer, i, h, 0, last), c.shape[-1] // 128 - 1)) for c in caches_t]
        extra_specs, extra_args = new_specs, list(rolled)
        aliases = {11 + g: 1 + g for g in range(N_GROUPS)}
        sem = ("parallel", "arbitrary")
    return pl.pallas_call(
        functools.partial(_attn_sample_kernel, layer=layer, full_roll=full_roll),
        grid=grid,
        in_specs=[zspec, const((1, aw)), const((1, aw)), const((aw, aw))]
                 + [const((A_HEADS, c.shape[-1])) for c in caches_t] + [const((N_GROUPS, A_HEADS, 1))]
                 + cache_specs + extra_specs,
        out_specs=[ospec] + new_specs,
        out_shape=[jax.ShapeDtypeStruct((bd, 1, aw), F32)]
                  + [jax.ShapeDtypeStruct(c.shape, c.dtype) for c in caches_t],
        scratch_shapes=[pltpu.VMEM((A_HEADS, c.shape[-1]), F32) for c in caches_t]
                       + [pltpu.VMEM((N_GROUPS, 3, A_HEADS, 128), F32), pltpu.VMEM((N_GROUPS, 8, aw), F32)],
        input_output_aliases=aliases,
        compiler_params=_cparams(sem),
        name="attn_sample",
    )(zs3, qgain, kgain, pavg, *biases, bias0, *caches_t, *extra_args)


def _to_col(row, n):
    eye = lax.broadcasted_iota(jnp.int32, (n, n), 0) == lax.broadcasted_iota(jnp.int32, (n, n), 1)
    return jnp.sum(jnp.where(eye, jnp.broadcast_to(row, (n, n)), 0.0), axis=1, keepdims=True)


def _sample_mix_kernel(bq_ref, bk_ref, bv_ref, cqkv_ref, sm_ref, cbuf_ref, sgla_ref, sgdn_ref,
                       wlr_ref, lrb_ref, cw_ref, dtb_ref, alog_ref,
                       ob_ref, oc_ref, nconv_ref, ngla_ref, ngdn_ref):
    sm = sm_ref[...]
    la =_log_sigmoid(_dot(sm, wlr_ref[...]) + lrb_ref[...]) * (1.0 / B_TAU)
    q = bq_ref[...] * (B_DK ** -0.5)
    k = bk_ref[...]
    v = bv_ref[...]
    for h in range(B_HEADS):
        lk = slice(B_DK * h, B_DK * (h + 1))
        lv = slice(B_DV * h, B_DV * (h + 1))
        s_old = sgla_ref[h]
        decay = jnp.exp(la[:, lk])
        ngla_ref[h] = _to_col(decay, B_DK) * s_old + _to_col(k[:, lk], B_DK) * v[:, lv]
        att = jnp.sum(q[:, lk] * k[:, lk], axis=-1, keepdims=True)
        ob_ref[:, lv] = (jnp.sum(_to_col(q[:, lk] * decay, B_DK) * s_old, axis=0, keepdims=True)
                         + att * v[:, lv])

    cat = jnp.concatenate([cbuf_ref[...], cqkv_ref[...]], axis=0)
    acc = None
    for j in range(CONV_W):
        term = cat[j:j + 1, :] * cw_ref[j:j + 1, :]
        acc = term if acc is None else acc + term
    y = _silu(acc)
    nconv_ref[...] = cat[1:CONV_W, :]
    hd = C_HEADS * C_DK
    for h in range(C_HEADS):
        ld = slice(C_DK * h, C_DK * (h + 1))
        qh = y[:, C_DK * h:C_DK * (h + 1)]
        qh = qh * lax.rsqrt(jnp.sum(qh * qh, axis=-1, keepdims=True) + EPS) * (C_DK ** -0.5)
        kh = y[:, hd + C_DK * h:hd + C_DK * (h + 1)]
        kh = kh * lax.rsqrt(jnp.sum(kh * kh, axis=-1, keepdims=True) + EPS)
        vh = y[:, 2 * hd + C_DV * h:2 * hd + C_DV * (h + 1)]
        beta = _sigmoid(sm[:, SM_CB + h:SM_CB + h + 1])
        g = -jnp.exp(alog_ref[:, ld]) * _softplus(sm[:, SM_CA + h:SM_CA + h + 1] + dtb_ref[:, ld])
        eg = jnp.exp(g)
        s = sgdn_ref[h]
        kcol = _to_col(kh, C_DK)
        u = beta * (vh - eg * jnp.sum(kcol * s, axis=0, keepdims=True))
        qs = jnp.sum(_to_col(qh, C_DK) * s, axis=0, keepdims=True)
        oc_ref[:, ld] = eg * qs + jnp.sum(qh * kh, axis=-1, keepdims=True) * u
        ngdn_ref[h] = eg * s + kcol * u


def _sample_mix(zs3, cbuf, sgla, sgdn, wlr_pad, lr_bias, conv_w, dtb_bc, alog_bc):
    bd = zs3.shape[0]
    hk, hv, hd = B_HEADS * B_DK, B_HEADS * B_DV, C_HEADS * C_DV
    const = lambda shape: pl.BlockSpec(shape, lambda i: (0,) * len(shape))
    zrow = lambda width, off: pl.BlockSpec((None, 1, width), lambda i: (i, 0, off // width))
    vec = lambda width: pl.BlockSpec((None, 1, width), lambda i: (i, 0, 0))
    st = lambda shape: pl.BlockSpec((None,) + shape, lambda i: (i,) + (0,) * len(shape))
    return pl.pallas_call(
        _sample_mix_kernel,
        grid=(bd,),
        in_specs=[zrow(hk, Z_BQ), zrow(hk, Z_BK), zrow(hv, Z_BV), zrow(C_CONV_CH, Z_CQKV), zrow(SMALL_W, Z_SMALL),
                  st((CONV_W - 1, C_CONV_CH)), st((B_HEADS, B_DK, B_DV)), st((C_HEADS, C_DK, C_DV))]
                 + [const((SMALL_W, hk)), const((1, hk)), const((CONV_W, C_CONV_CH)), const((1, hd)), const((1, hd))],
        out_specs=[vec(hv), vec(hd), st((CONV_W - 1, C_CONV_CH)),
                   st((B_HEADS, B_DK, B_DV)), st((C_HEADS, C_DK, C_DV))],
        out_shape=[jax.ShapeDtypeStruct((bd, 1, hv), F32),
                   jax.ShapeDtypeStruct((bd, 1, hd), F32), jax.ShapeDtypeStruct((bd, CONV_W - 1, C_CONV_CH), F32),
                   jax.ShapeDtypeStruct((bd, B_HEADS, B_DK, B_DV), F32),
                   jax.ShapeDtypeStruct((bd, C_HEADS, C_DK, C_DV), F32)],
        compiler_params=_cparams(("parallel",)),
        name="sample_mix",
    )(zs3, zs3, zs3, zs3, zs3, cbuf, sgla, sgdn, wlr_pad, lr_bias, conv_w, dtb_bc, alog_bc)


def _t5_bucket(dist):
    max_exact = N_BUCKETS // 2
    d = jnp.maximum(dist.astype(F32), 1.0)
    large = max_exact + (jnp.log(d / max_exact) / math.log(MAX_DIST / max_exact)
                         * (N_BUCKETS - max_exact)).astype(jnp.int32)
    large = jnp.minimum(large, N_BUCKETS - 1)
    return jnp.where(dist < max_exact, dist, large).astype(jnp.int32)


def _bias_tables(rel_bias):
    i = jnp.arange(SPAN)[:, None]
    c = jnp.arange(2 * SPAN)[None, :]
    dist = i + SPAN - c
    valid = (dist >= 0) & (dist <= SPAN)

    def lookup(bias_g, idx):
        hot = (idx[..., None] == jnp.arange(N_BUCKETS)).astype(F32)
        return jnp.einsum("...k,kh->...h", hot, bias_g.astype(F32), precision=HI)

    prompt, cached, new = [], [], []
    for g, (win, dil) in enumerate(A_GROUPS):
        bias_g = rel_bias[:, g * A_HEADS:(g + 1) * A_HEADS]
        tbl = lookup(bias_g, _t5_bucket(jnp.maximum(dist, 0) * dil)).transpose(2, 0, 1)
        prompt.append(jnp.where(valid[None], tbl, NEG))
        w = jnp.arange(win)
        tbl_s = lookup(bias_g, _t5_bucket(win - w)).T
        cached.append(jnp.where((w % dil == 0)[None], tbl_s, NEG))
        new.append(lookup(bias_g, _t5_bucket(jnp.zeros((1,), jnp.int32))).T)
    return jnp.stack(prompt), cached, jnp.stack(new)


def _prep_w_in(w):
    offs = np.concatenate([[0], np.cumsum(IN_SIZES)])
    seg = lambda i: w[:, offs[i]:offs[i + 1]]
    order = (0, 1, 2, 3, 4, 5, 6, 8, 9, 12, 7, 10, 11)
    used = sum(IN_SIZES)
    parts = [seg(i) for i in order] + [jnp.zeros((w.shape[0], NZ - used), w.dtype)]
    return jnp.concatenate(parts, axis=1).astype(BF16)


def _selector(offset):
    sel = np.zeros((SMALL_W, C_HEADS * C_DV), np.float32)
    for h in range(C_HEADS):
        sel[offset + h, C_DV * h:C_DV * (h + 1)] = 1.0
    return jnp.asarray(sel)


def _block_avg(width, group):
    idx = np.arange(width) // group
    return jnp.asarray((idx[:, None] == idx[None, :]).astype(np.float32) / group)


def kernel(x_prompt, x_sample, cache_win128_kv, cache_win512_kv, cache_win2048_kv, state_gla, state_gdn, state_conv, norm_mix, w_in, a_q_norm, a_k_norm, rel_bias, b_w_lr, b_lr_bias, b_out_norm, c_conv, c_a_log, c_dt_bias, c_out_norm, w_branch, w_out, norm_ffn, ffn_w_gate, ffn_w_up, ffn_w_down, moe_router, moe_w_gate, moe_w_up, moe_w_down):
    bp, t, d = x_prompt.shape
    bs = x_sample.shape[0]
    mp = bp * t
    caches_t = [jnp.transpose(c, (0, 1, 3, 4, 5, 2)).reshape(c.shape[:2] + (2, A_WIDTH, c.shape[2]))
                for c in (cache_win128_kv, cache_win512_kv, cache_win2048_kv)]
    bias_p, bias_c, bias_n = _bias_tables(rel_bias)
    pavg512 = _block_avg(A_WIDTH, A_DH)
    selb, sela = _selector(SM_CB), _selector(SM_CA)
    rolled = windows = None

    xp = x_prompt.reshape(mp, d)
    xs = x_sample.reshape(bs, d)
    outs_p = {k: [] for k in ("w0", "w1", "w2", "gla", "gdn", "conv")}
    outs_s = {k: [] for k in ("w0", "w1", "w2", "gla", "gdn", "conv")}
    for l in range(DEPTH):
        w_in_l = _prep_w_in(w_in[l])
        wb = w_branch[l].astype(BF16)
        wo = w_out[l].astype(BF16)
        qg128 = jnp.tile(a_q_norm[l], 2).reshape(1, 2 * A_DH)
        kg128 = jnp.tile(a_k_norm[l], 2).reshape(1, 2 * A_DH)
        qg512 = jnp.tile(a_q_norm[l], A_HEADS).reshape(1, A_WIDTH)
        kg512 = jnp.tile(a_k_norm[l], A_HEADS).reshape(1, A_WIDTH)
        wlr_pad = jnp.zeros((SMALL_W, B_HEADS * B_DK), F32).at[:B_RANK].set(b_w_lr[l])
        lr_bias = b_lr_bias[l].reshape(1, -1)
        bon = jnp.tile(b_out_norm[l], B_HEADS).reshape(1, -1)
        con = jnp.tile(c_out_norm[l], C_HEADS).reshape(1, -1)
        dtb_bc = jnp.repeat(c_dt_bias[l], C_DV).reshape(1, -1)
        alog_bc = jnp.repeat(c_a_log[l], C_DV).reshape(1, -1)

        z = _norm_matmul(xp, norm_mix[l], w_in_l, tm=1024, tn=1280)
        z3 = z.reshape(bp, t, NZ)
        o_a, *windows = _attn_prompt(z3, qg128, kg128, bias_p, l, windows)
        o_b, gla_t = _gla_prompt(z3, wlr_pad, lr_bias, tb=512)
        o_c, gdn_s = _gdn_prompt(z3, c_conv[l], selb, sela, dtb_bc, alog_bc, tb=512)
        xp = _merge(z, o_a.reshape(mp, -1), o_b.reshape(mp, -1), o_c.reshape(mp, -1), xp, bon, con, wb, wo, tm=256)
        gla = gla_t.reshape(bp, B_HEADS, B_DV, B_HEADS, B_DK)
        gla = jnp.stack([gla[:, h, :, h, :] for h in range(B_HEADS)], axis=1)
        outs_p["gla"].append(jnp.swapaxes(gla, 2, 3))
        outs_p["gdn"].append(gdn_s)
        outs_p["conv"].append(z3[:, t - (CONV_W - 1):, Z_CQKV:Z_CQKV + C_CONV_CH])

        zs = _norm_matmul(xs, norm_mix[l], w_in_l, tm=bs, tn=1280)
        zs3 = zs.reshape(bs, 1, NZ)
        oa_s, *rolled = _attn_sample(zs3, caches_t, rolled, l, qg512, kg512, pavg512, bias_c, bias_n)
        ob_s, oc_s, nconv, ngla, ngdn = _sample_mix(
            zs3, state_conv[l], state_gla[l], state_gdn[l], wlr_pad, lr_bias, c_conv[l], dtb_bc, alog_bc)
        xs = _merge(zs, oa_s.reshape(bs, -1), ob_s.reshape(bs, -1), oc_s.reshape(bs, -1), xs, bon, con, wb, wo, tm=bs)
        outs_s["gla"].append(ngla)
        outs_s["gdn"].append(ngdn)
        outs_s["conv"].append(nconv)

        i = l // 2
        if l % 2 == 0:
            wg, wu, wd = ffn_w_gate[i].astype(BF16), ffn_w_up[i].astype(BF16), ffn_w_down[i].astype(BF16)
            xp = _ffn(xp, norm_ffn[l], wg, wu, wd, tm=512, tf=1408)
            xs = _ffn(xs, norm_ffn[l], wg, wu, wd, tm=bs, tf=1408)
        else:
            wg, wu, wd = moe_w_gate[i].astype(BF16), moe_w_up[i].astype(BF16), moe_w_down[i].astype(BF16)
            router_pad = jnp.zeros((d, 128), F32).at[:, :N_EXPERTS].set(moe_router[i])
            xp = _moe_routed(xp, norm_ffn[l], moe_router[i].T, wg, wu, wd, tm=1024)
            xs = _moe(xs, norm_ffn[l], router_pad, wg, wu, wd, tm=bs, tf=896)

    st = lambda name, d_: jnp.stack(d_[name])

    def window_out(w):
        w = w.reshape(w.shape[:3] + (A_HEADS, A_DH, w.shape[-1]))
        return jnp.transpose(w, (0, 1, 5, 2, 3, 4))

    return (xp.reshape(bp, t, d), xs.reshape(bs, 1, d),
            window_out(windows[0]), window_out(rolled[0]), window_out(windows[1]), window_out(rolled[1]),
            window_out(windows[2]), window_out(rolled[2]),
            st("gla", outs_p), st("gla", outs_s), st("gdn", outs_p), st("gdn", outs_s),
            st("conv", outs_p), st("conv", outs_s))
```

```python
import functools
import math

import jax
import jax.numpy as jnp
import numpy as np
from jax import lax
from jax.experimental import pallas as pl
from jax.experimental.pallas import tpu as pltpu

F32 = jnp.float32
BF16 = jnp.bfloat16
HI = lax.Precision.HIGHEST
NEG = -1e30

D_MODEL = 1024
DEPTH = 2
EPS = 1e-6
A_GROUPS = ((128, 1), (512, 4), (2048, 16))
N_GROUPS = 3
A_HEADS = 8
A_DH = 64
A_WIDTH = A_HEADS * A_DH
SPAN = 128
ATT_PAIR = 4
GLA_PAIR = 4
GDN_PAIR = 4
N_BUCKETS = 32
MAX_DIST = 2048
B_HEADS = 4
B_DK = 64
B_DV = 128
B_RANK = 16
B_TAU = 16.0
C_HEADS = 4
C_DK = 128
C_DV = 128
CONV_W = 4
C_CONV_CH = 2 * C_HEADS * C_DK + C_HEADS * C_DV
CHUNK = 64
SUB = 16
BRANCH_W = 512
D_FF = 2816
N_EXPERTS = 8
D_FF_EXPERT = 1792
IN_SIZES = (N_GROUPS * A_WIDTH, N_GROUPS * A_WIDTH, N_GROUPS * A_WIDTH,
            B_HEADS * B_DK, B_HEADS * B_DK, B_HEADS * B_DV, B_HEADS * B_DV, B_RANK,
            C_CONV_CH, C_HEADS * C_DV, C_HEADS, C_HEADS, 3 * D_MODEL)

Z_AQ, Z_AK, Z_AV = 0, 1536, 3072
Z_BQ, Z_BK, Z_BV, Z_BG = 4608, 4864, 5120, 5632
Z_CQKV, Z_CZ, Z_GATES, Z_SMALL = 6144, 7680, 8192, 11264
SMALL_W = 256
NZ = Z_SMALL + SMALL_W
SM_CB, SM_CA = B_RANK, B_RANK + C_HEADS

VMEM_LIMIT = 56 * 1024 * 1024


def _cparams(sem):
    return pltpu.CompilerParams(dimension_semantics=sem, vmem_limit_bytes=VMEM_LIMIT)


def _dot(a, b):
    return jnp.dot(a.astype(BF16), b.astype(BF16), preferred_element_type=F32)


def _dot_nt(a, b):
    return lax.dot_general(a.astype(BF16), b.astype(BF16), (((1,), (1,)), ((), ())), preferred_element_type=F32)


def _dot_tn(a, b):
    return lax.dot_general(a.astype(BF16), b.astype(BF16), (((0,), (0,)), ((), ())), preferred_element_type=F32)


def _dot_hi(a, b):
    return jnp.dot(a, b, precision=HI, preferred_element_type=F32)


def _r16(x):
    return x.astype(BF16).astype(F32)


def _sigmoid(x):
    return jax.nn.sigmoid(x)


def _silu(x):
    return x * jax.nn.sigmoid(x)


def _softplus(x):
    return jnp.maximum(x, 0.0) + jnp.log1p(jnp.exp(-jnp.abs(x)))


def _log_sigmoid(x):
    return jnp.minimum(x, 0.0) - jnp.log1p(jnp.exp(-jnp.abs(x)))


def _rms(x, gain):
    return x * lax.rsqrt(jnp.mean(x * x, axis=-1, keepdims=True) + EPS) * gain


def _norm_matmul_kernel(x_ref, g_ref, w_ref, o_ref, h_ref):
    @pl.when(pl.program_id(1) == 0)
    def _():
        h_ref[...] = _rms(x_ref[...], g_ref[...]).astype(BF16)

    o_ref[...] = jnp.dot(h_ref[...], w_ref[...], preferred_element_type=F32)


def _norm_matmul(x, gain, w, tm, tn):
    m, d = x.shape
    n = w.shape[1]
    return pl.pallas_call(
        _norm_matmul_kernel,
        grid=(m // tm, n // tn),
        in_specs=[pl.BlockSpec((tm, d), lambda i, j: (i, 0)),
                  pl.BlockSpec((1, d), lambda i, j: (0, 0)),
                  pl.BlockSpec((d, tn), lambda i, j: (0, j))],
        out_specs=pl.BlockSpec((tm, tn), lambda i, j: (i, j)),
        out_shape=jax.ShapeDtypeStruct((m, n), F32),
        scratch_shapes=[pltpu.VMEM((tm, d), BF16)],
        compiler_params=_cparams(("parallel", "arbitrary")),
        name="norm_matmul",
    )(x, gain.reshape(1, d), w)


def _pair_rms(x, gain, head0):
    sq = x * x
    s0 = jnp.sum(jnp.where(head0, sq, 0.0), axis=-1, keepdims=True)
    s1 = jnp.sum(sq, axis=-1, keepdims=True) - s0
    return x * lax.rsqrt(jnp.where(head0, s0, s1) * (1.0 / A_DH) + EPS) * gain


def _attn_prompt_kernel(q1, k1, v1, q2, k2, v2, q3, k3, v3, qg_ref, kg_ref, bias_ref,
                        o_ref, w1, w2, w3, qt, kt, qs, ks, vs, og, lg, to, tl, *, slots):
    t_len = o_ref.shape[0]
    head0 = lax.broadcasted_iota(jnp.int32, (1, 2 * A_DH), 1) < A_DH
    first_half = lax.broadcasted_iota(jnp.int32, (1, 2 * SPAN), 1) < SPAN
    groups = ((q1, k1, v1, w1), (q2, k2, v2, w2), (q3, k3, v3, w3))
    for g, (qr, kr, vr, wr) in enumerate(groups):
        dil = A_GROUPS[g][1]
        length = t_len // dil
        nb = length // SPAN
        qt[...] = _pair_rms(qr[...], qg_ref[...], head0) * (A_DH ** -0.5)
        kt[...] = _pair_rms(kr[...], kg_ref[...], head0)
        wlen = wr.shape[-1]
        k_win = kt[t_len - wlen:, :].T
        v_win = vr[t_len - wlen:, :].T
        for s in range(slots):
            wr[s, 0] = k_win
            wr[s, 1] = v_win
        knr = kt
        for r in range(dil):
            rows = pl.ds(r, length, stride=dil) if dil > 1 else pl.ds(0, length)
            dst = pl.ds(r * length, length)
            qs[dst, :] = qt[rows, :].astype(BF16)
            ks[dst, :] = knr[rows, :].astype(BF16)
            vs[dst, :] = vr[rows, :].astype(BF16)

        if nb > 1:
            bias2 = jnp.concatenate([bias_ref[g, 0], bias_ref[g, 1]], axis=0)
        else:
            bias2 = jnp.concatenate([bias_ref[g, 0, :, SPAN:], bias_ref[g, 1, :, SPAN:]], axis=0)

        def block(jj, carry, nb=nb, bias2=bias2):
            row0s, k2s, v2s, logits = [], [], [], []
            for i in range(ATT_PAIR):
                j = jj * ATT_PAIR + i
                row0 = pl.multiple_of(j * SPAN, SPAN)
                qb = qs[pl.ds(row0, SPAN), :]
                qst = jnp.concatenate([jnp.where(head0, qb, jnp.zeros_like(qb)),
                                       jnp.where(head0, jnp.zeros_like(qb), qb)], axis=0)
                if nb > 1:
                    prow = pl.multiple_of(jnp.maximum(row0 - SPAN, 0), SPAN)
                    k2 = jnp.concatenate([ks[pl.ds(prow, SPAN), :], ks[pl.ds(row0, SPAN), :]], axis=0)
                    v2 = jnp.concatenate([vs[pl.ds(prow, SPAN), :], vs[pl.ds(row0, SPAN), :]], axis=0)
                else:
                    k2 = ks[pl.ds(row0, SPAN), :]
                    v2 = vs[pl.ds(row0, SPAN), :]
                lg_ = lax.dot_general(qst, k2, (((1,), (1,)), ((), ())), preferred_element_type=F32) + bias2
                if nb > 1:
                    lg_ = lg_ + jnp.where(first_half, jnp.where(j % nb == 0, NEG, 0.0), 0.0)
                row0s.append(row0)
                k2s.append(k2)
                v2s.append(v2)
                logits.append(lg_)
            lg_all = jnp.concatenate(logits, axis=0)
            m = jnp.max(lg_all, axis=-1, keepdims=True)
            p = jnp.exp(lg_all - m)
            s = jnp.sum(p, axis=-1, keepdims=True)
            p16 = p.astype(BF16)
            lse = m + jnp.log(s)
            for i in range(ATT_PAIR):
                lo = 2 * SPAN * i
                acc = jnp.dot(p16[lo:lo + 2 * SPAN], v2s[i], preferred_element_type=F32) / s[lo:lo + 2 * SPAN]
                to[pl.ds(row0s[i], SPAN), :] = jnp.where(head0, acc[:SPAN], acc[SPAN:])
                tl[pl.ds(row0s[i], SPAN), :] = jnp.where(head0, jnp.broadcast_to(lse[lo:lo + SPAN], (SPAN, 2 * A_DH)),
                                                         jnp.broadcast_to(lse[lo + SPAN:lo + 2 * SPAN], (SPAN, 2 * A_DH)))
            return carry

        lax.fori_loop(0, t_len // (SPAN * ATT_PAIR), block, 0)
        for r in range(dil):
            rows = pl.ds(r, length, stride=dil) if dil > 1 else pl.ds(0, length)
            src = pl.ds(r * length, length)
            og[g, rows, :] = to[src, :]
            lg[g, rows, :] = tl[src, :]
    l_1, l_2, l_3 = lg[0], lg[1], lg[2]
    mx = jnp.maximum(jnp.maximum(l_1, l_2), l_3)
    e_1, e_2, e_3 = jnp.exp(l_1 - mx), jnp.exp(l_2 - mx), jnp.exp(l_3 - mx)
    o_ref[...] = (e_1 * og[0] + e_2 * og[1] + e_3 * og[2]) / (e_1 + e_2 + e_3)


def _attn_prompt(z3, qgain, kgain, bias, layer, windows):
    b, t, _ = z3.shape
    hp_blocks = A_WIDTH // 128

    def col(base, g):
        return lambda i, hp: (i, 0, base // 128 + g * hp_blocks + hp)

    in_specs = []
    for g in range(N_GROUPS):
        for base in (Z_AQ, Z_AK, Z_AV):
            in_specs.append(pl.BlockSpec((None, t, 128), col(base, g)))
    in_specs += [pl.BlockSpec((1, 128), lambda i, hp: (0, 0)),
                 pl.BlockSpec((1, 128), lambda i, hp: (0, 0)),
                 pl.BlockSpec((N_GROUPS, 2, SPAN, 2 * SPAN), lambda i, hp: (0, hp, 0, 0))]
    args = [z3] * 9 + [qgain, kgain, bias]
    slots = DEPTH if windows is None else 1
    win_specs, win_shapes = [], []
    for win, _ in A_GROUPS:
        wp = min(win, t)
        win_specs.append(pl.BlockSpec((slots, None, 2, 128, wp), lambda i, hp: (layer // slots, i, 0, hp, 0)))
        win_shapes.append(jax.ShapeDtypeStruct((DEPTH, b, 2, A_WIDTH, wp), F32))
    aliases = {}
    if windows is not None:
        aliases = {len(args) + g: 1 + g for g in range(N_GROUPS)}
        in_specs += [pl.BlockSpec(memory_space=pl.ANY)] * N_GROUPS
        args += list(windows)
    kern = functools.partial(_attn_prompt_kernel, slots=slots)
    if windows is not None:
        kern = functools.partial(_drop_refs, kern, len(args) - N_GROUPS, N_GROUPS)
    return pl.pallas_call(
        kern,
        grid=(b, hp_blocks),
        in_specs=in_specs,
        out_specs=[pl.BlockSpec((None, t, 128), lambda i, hp: (i, 0, hp))] + win_specs,
        out_shape=[jax.ShapeDtypeStruct((b, t, A_WIDTH), F32)] + win_shapes,
        scratch_shapes=[pltpu.VMEM((t, 128), F32), pltpu.VMEM((t, 128), F32),
                        pltpu.VMEM((t, 128), BF16), pltpu.VMEM((t, 128), BF16), pltpu.VMEM((t, 128), BF16),
                        pltpu.VMEM((N_GROUPS, t, 128), F32), pltpu.VMEM((N_GROUPS, t, 128), F32),
                        pltpu.VMEM((t, 128), F32), pltpu.VMEM((t, 128), F32)],
        input_output_aliases=aliases,
        compiler_params=_cparams(("parallel", "parallel")),
        name="attn_prompt",
    )(*args)


def _drop_refs(kern, start, count, *refs):
    return kern(*refs[:start], *refs[start + count:])


def _gla_prompt_kernel(q_ref, k_ref, v_ref, sm_ref, wlr_ref, lrb_ref, o_ref, st_ref, s_scr, la_scr):
    tb = q_ref.shape[0]
    hk = B_HEADS * B_DK
    hv = B_HEADS * B_DV

    @pl.when(pl.program_id(1) == 0)
    def _():
        s_scr[...] = jnp.zeros_like(s_scr)

    la_scr[...] = _log_sigmoid(_dot(sm_ref[...], wlr_ref[...]) + lrb_ref[...]) * (1.0 / B_TAU)

    rr = lax.broadcasted_iota(jnp.int32, (CHUNK, CHUNK), 0)
    cc = lax.broadcasted_iota(jnp.int32, (CHUNK, CHUNK), 1)
    causal = rr >= cc
    ltri = causal.astype(F32)
    lane_head = lax.broadcasted_iota(jnp.int32, (1, hk), 1) // B_DK
    row = lax.broadcasted_iota(jnp.int32, (CHUNK, 1), 0)
    same_head = (lax.broadcasted_iota(jnp.int32, (hv, hk), 0) // B_DV
                 == lax.broadcasted_iota(jnp.int32, (hv, hk), 1) // B_DK)

    def intra(r0):
        b = _dot_hi(ltri, la_scr[pl.ds(r0, CHUNK), :])
        q = q_ref[pl.ds(r0, CHUNK), :] * (B_DK ** -0.5)
        k = k_ref[pl.ds(r0, CHUNK), :]
        vb = v_ref[pl.ds(r0, CHUNK), :].astype(BF16)
        b_last = b[CHUNK - 1:CHUNK, :]
        atts = []
        for i in range(CHUNK // SUB):
            b_ref0 = b[SUB * i:SUB * i + 1, :]
            qi = q[SUB * i:SUB * (i + 1), :] * jnp.exp(b[SUB * i:SUB * (i + 1), :] - b_ref0)
            qst = jnp.concatenate([jnp.where(lane_head == h, qi, 0.0) for h in range(B_HEADS)], axis=0)
            kt = k * jnp.exp(jnp.where(row < SUB * (i + 1), b_ref0 - b, 0.0))
            atts.append(_dot_nt(qst, kt))
        o_parts = []
        for h in range(B_HEADS):
            att = jnp.concatenate([a[SUB * h:SUB * (h + 1), :] for a in atts], axis=0)
            att = jnp.where(causal, att, 0.0)
            o_parts.append(_dot(att, vb[:, B_DV * h:B_DV * (h + 1)]))
        upd = jnp.where(same_head, _dot_tn(vb, k * jnp.exp(b_last - b)), 0.0)
        return (q * jnp.exp(b)).astype(BF16), jnp.concatenate(o_parts, axis=1), jnp.exp(b_last), upd

    def chunks(cc, carry):
        r0s = [pl.multiple_of((cc * GLA_PAIR + i) * CHUNK, CHUNK) for i in range(GLA_PAIR)]
        parts = [intra(r0) for r0 in r0s]
        for r0, (qe, o_intra, decay, upd) in zip(r0s, parts):
            s = s_scr[...]
            o_ref[pl.ds(r0, CHUNK), :] = _dot_nt(qe, s) + o_intra
            s_scr[...] = s * decay + upd
        return carry

    lax.fori_loop(0, tb // (CHUNK * GLA_PAIR), chunks, 0)

    @pl.when(pl.program_id(1) == pl.num_programs(1) - 1)
    def _():
        st_ref[...] = s_scr[...]


def _gla_prompt(z3, wlr_pad, lr_bias, tb):
    b, t, _ = z3.shape
    hk, hv = B_HEADS * B_DK, B_HEADS * B_DV
    return pl.pallas_call(
        _gla_prompt_kernel,
        grid=(b, t // tb),
        in_specs=[pl.BlockSpec((None, tb, hk), lambda i, j: (i, j, Z_BQ // hk)),
                  pl.BlockSpec((None, tb, hk), lambda i, j: (i, j, Z_BK // hk)),
                  pl.BlockSpec((None, tb, hv), lambda i, j: (i, j, Z_BV // hv)),
                  pl.BlockSpec((None, tb, SMALL_W), lambda i, j: (i, j, Z_SMALL // SMALL_W)),
                  pl.BlockSpec((SMALL_W, hk), lambda i, j: (0, 0)),
                  pl.BlockSpec((1, hk), lambda i, j: (0, 0))],
        out_specs=[pl.BlockSpec((None, tb, hv), lambda i, j: (i, j, 0)),
                   pl.BlockSpec((None, hv, hk), lambda i, j: (i, 0, 0))],
        out_shape=[jax.ShapeDtypeStruct((b, t, hv), F32), jax.ShapeDtypeStruct((b, hv, hk), F32)],
        scratch_shapes=[pltpu.VMEM((hv, hk), F32), pltpu.VMEM((tb, hk), F32)],
        compiler_params=_cparams(("parallel", "arbitrary")),
        name="gla_prompt",
    )(z3, z3, z3, z3, wlr_pad, lr_bias)


def _gdn_prompt_kernel(x_ref, sm_ref, cw_ref, selb_ref, sela_ref, dtb_ref, alog_ref, o_ref, s_ref,
                       xpad, cv, gsc, bsc, s_scr):
    tb = x_ref.shape[0]
    hd = C_HEADS * C_DK
    pad = 8

    @pl.when(pl.program_id(1) == 0)
    def _():
        s_scr[...] = jnp.zeros_like(s_scr)
        xpad[0:pad, :] = jnp.zeros((pad, C_CONV_CH), F32)

    xpad[pad:pad + tb, :] = x_ref[...]
    rb = 128
    for i in range(tb // rb):
        acc = None
        for j in range(CONV_W):
            lo = pad - (CONV_W - 1) + j + i * rb
            term = xpad[lo:lo + rb, :] * cw_ref[j:j + 1, :]
            acc = term if acc is None else acc + term
        y = _silu(acc)
        rows = slice(i * rb, (i + 1) * rb)
        for h in range(C_HEADS):
            qh = y[:, C_DK * h:C_DK * (h + 1)]
            cv[rows, C_DK * h:C_DK * (h + 1)] = (
                qh * lax.rsqrt(jnp.sum(qh * qh, axis=-1, keepdims=True) + EPS) * (C_DK ** -0.5))
            kh = y[:, hd + C_DK * h:hd + C_DK * (h + 1)]
            cv[rows, hd + C_DK * h:hd + C_DK * (h + 1)] = (
                kh * lax.rsqrt(jnp.sum(kh * kh, axis=-1, keepdims=True) + EPS))
        cv[rows, 2 * hd:] = y[:, 2 * hd:]
    xpad[pad - (CONV_W - 1):pad, :] = xpad[pad + tb - (CONV_W - 1):pad + tb, :]

    sm = sm_ref[...]
    gsc[...] = -jnp.exp(alog_ref[...]) * _softplus(_dot_hi(sm, sela_ref[...]) + dtb_ref[...])
    bsc[...] = _sigmoid(_dot_hi(sm, selb_ref[...]))

    n = C_HEADS * CHUNK
    rr = lax.broadcasted_iota(jnp.int32, (CHUNK, CHUNK), 0)
    cc = lax.broadcasted_iota(jnp.int32, (CHUNK, CHUNK), 1)
    ltri = (rr >= cc).astype(F32)
    ri = lax.broadcasted_iota(jnp.int32, (n, n), 0)
    ci = lax.broadcasted_iota(jnp.int32, (n, n), 1)
    same = (ri // CHUNK) == (ci // CHUNK)
    incl = jnp.logical_and(same, ci <= ri)
    strict = jnp.logical_and(same, ci < ri)
    eye = (ri == ci).astype(F32)

    def stack(x):
        return jnp.concatenate([x[:, C_DK * h:C_DK * (h + 1)] for h in range(C_HEADS)], axis=0)

    def chunks(cc, carry):
        pair = range(GDN_PAIR)
        r0s = [pl.multiple_of((cc * GDN_PAIR + i) * CHUNK, CHUNK) for i in pair]
        bst = [stack(_dot_hi(ltri, gsc[pl.ds(r0, CHUNK), :])) for r0 in r0s]
        beta = [stack(bsc[pl.ds(r0, CHUNK), :]) for r0 in r0s]
        qst = [stack(cv[pl.ds(r0, CHUNK), 0:hd]) for r0 in r0s]
        kst = [stack(cv[pl.ds(r0, CHUNK), hd:2 * hd]) for r0 in r0s]
        vst = [stack(cv[pl.ds(r0, CHUNK), 2 * hd:3 * hd]) for r0 in r0s]
        kb = [k.astype(BF16) for k in kst]
        nt = (((1,), (1,)), ((), ()))
        kk = [lax.dot_general(kb[i], kb[i], nt, preferred_element_type=F32) for i in pair]
        qk = [lax.dot_general(qst[i].astype(BF16), kb[i], nt, preferred_element_type=F32) for i in pair]
        dec, a, aqk = [], [], []
        for i in pair:
            bst_t = bst[i].T
            d_ = jnp.exp(jnp.where(incl, jnp.concatenate([bst[i], bst[i]], axis=1)
                                   - jnp.concatenate([bst_t, bst_t], axis=0), NEG))
            dec.append(d_)
            a.append(jnp.where(strict, jnp.concatenate([beta[i], beta[i]], axis=1) * d_ * kk[i], 0.0))
            aqk.append(d_ * qk[i])
        inv = [eye - a[i] for i in pair]
        pw = a
        for _ in range(int(math.log2(CHUNK)) - 1):
            pw = [_dot(pw[i], pw[i]) for i in pair]
            inv = [inv[i] + _dot(inv[i], pw[i]) for i in pair]
        eb = [jnp.exp(bst[i]) for i in pair]
        wu = [_dot(inv[i], jnp.concatenate([beta[i] * eb[i] * kst[i], beta[i] * vst[i]], axis=1)) for i in pair]
        for i in pair:
            w, uv = wu[i][:, :C_DK], wu[i][:, C_DK:]
            us, qss = [], []
            for h in range(C_HEADS):
                sl = slice(CHUNK * h, CHUNK * (h + 1))
                ws = _dot(jnp.concatenate([w[sl], qst[i][sl]], axis=0), s_scr[h])
                us.append(uv[sl] - ws[:CHUNK])
                qss.append(ws[CHUNK:])
            o = eb[i] * jnp.concatenate(qss, axis=0) + _dot(aqk[i], jnp.concatenate(us, axis=0))
            for h in range(C_HEADS):
                sl = slice(CHUNK * h, CHUNK * (h + 1))
                o_ref[pl.ds(r0s[i], CHUNK), C_DV * h:C_DV * (h + 1)] = o[sl]
                b_last = bst[i][CHUNK * (h + 1) - 1:CHUNK * (h + 1), :]
                s_scr[h] = (jnp.exp(b_last) * s_scr[h]
                            + _dot_tn(kst[i][sl] * jnp.exp(b_last - bst[i][sl]), us[h]))
        return carry

    lax.fori_loop(0, tb // (CHUNK * GDN_PAIR), chunks, 0)

    @pl.when(pl.program_id(1) == pl.num_programs(1) - 1)
    def _():
        s_ref[...] = s_scr[...]


def _gdn_prompt(z3, conv_w, selb, sela, dtb_bc, alog_bc, tb):
    b, t, _ = z3.shape
    hd = C_HEADS * C_DV
    const = lambda shape: pl.BlockSpec(shape, lambda i, j: (0,) * len(shape))
    return pl.pallas_call(
        _gdn_prompt_kernel,
        grid=(b, t // tb),
        in_specs=[pl.BlockSpec((None, tb, C_CONV_CH), lambda i, j: (i, j, Z_CQKV // C_CONV_CH)),
                  pl.BlockSpec((None, tb, SMALL_W), lambda i, j: (i, j, Z_SMALL // SMALL_W)),
                  const((CONV_W, C_CONV_CH)), const((SMALL_W, hd)), const((SMALL_W, hd)),
                  const((1, hd)), const((1, hd))],
        out_specs=[pl.BlockSpec((None, tb, hd), lambda i, j: (i, j, 0)),
                   pl.BlockSpec((None, C_HEADS, C_DK, C_DV), lambda i, j: (i, 0, 0, 0))],
        out_shape=[jax.ShapeDtypeStruct((b, t, hd), F32),
                   jax.ShapeDtypeStruct((b, C_HEADS, C_DK, C_DV), F32)],
        scratch_shapes=[pltpu.VMEM((tb + 8, C_CONV_CH), F32), pltpu.VMEM((tb, C_CONV_CH), F32),
                        pltpu.VMEM((tb, hd), F32), pltpu.VMEM((tb, hd), F32),
                        pltpu.VMEM((C_HEADS, C_DK, C_DV), F32)],
        compiler_params=_cparams(("parallel", "arbitrary")),
        name="gdn_prompt",
    )(z3, z3, conv_w, selb, sela, dtb_bc, alog_bc)


def _head_norm(o, gain, width):
    parts = []
    for h in range(o.shape[1] // width):
        oh = o[:, width * h:width * (h + 1)]
        parts.append(oh * lax.rsqrt(jnp.mean(oh * oh, axis=-1, keepdims=True) + EPS))
    return jnp.concatenate(parts, axis=1) * gain


def _merge_kernel(oa_ref, ob_ref, bg_ref, oc_ref, cz_ref, ga_ref, gb_ref, gc_ref, x_ref,
                  bon_ref, con_ref, wb_ref, wo_ref, y_ref):
    ob = _head_norm(ob_ref[...], bon_ref[...], B_DV) * _silu(bg_ref[...])
    oc = _head_norm(oc_ref[...], con_ref[...], C_DV) * _silu(cz_ref[...])
    merged = (_sigmoid(ga_ref[...]) * _dot(oa_ref[...], wb_ref[0])
              + _sigmoid(gb_ref[...]) * _dot(ob, wb_ref[1])
              + _sigmoid(gc_ref[...]) * _dot(oc, wb_ref[2]))
    y_ref[...] = x_ref[...] + _dot(merged, wo_ref[...])


def _merge(z, oa, ob, oc, x, bon, con, wb, wo, tm):
    m = x.shape[0]
    w = BRANCH_W
    row = lambda width, blk: pl.BlockSpec((tm, width), lambda i: (i, blk))
    const = lambda shape: pl.BlockSpec(shape, lambda i: (0,) * len(shape))
    return pl.pallas_call(
        _merge_kernel,
        grid=(m // tm,),
        in_specs=[row(w, 0), row(w, 0), row(w, Z_BG // w), row(w, 0), row(w, Z_CZ // w),
                  row(D_MODEL, Z_GATES // D_MODEL), row(D_MODEL, Z_GATES // D_MODEL + 1),
                  row(D_MODEL, Z_GATES // D_MODEL + 2), row(D_MODEL, 0),
                  const((1, w)), const((1, w)), const((3, w, D_MODEL)), const((D_MODEL, D_MODEL))],
        out_specs=row(D_MODEL, 0),
        out_shape=jax.ShapeDtypeStruct((m, D_MODEL), F32),
        compiler_params=_cparams(("parallel",)),
        name="merge",
    )(oa, ob, z, oc, z, z, z, z, x, bon, con, wb, wo)


def _ffn_kernel(x_ref, g_ref, wg_ref, wu_ref, wd_ref, y_ref, h_ref, acc_ref):
    f = pl.program_id(1)

    @pl.when(f == 0)
    def _():
        h_ref[...] = _rms(x_ref[...], g_ref[...]).astype(BF16)
        acc_ref[...] = jnp.zeros_like(acc_ref)

    h = h_ref[...]
    a = _silu(jnp.dot(h, wg_ref[...], preferred_element_type=F32)) * jnp.dot(h, wu_ref[...],
                                                                              preferred_element_type=F32)
    acc_ref[...] += jnp.dot(a.astype(BF16), wd_ref[...], preferred_element_type=F32)

    @pl.when(f == pl.num_programs(1) - 1)
    def _():
        y_ref[...] = x_ref[...] + acc_ref[...]


def _ffn(x, gain, wg, wu, wd, tm, tf):
    m, d = x.shape
    ff = wg.shape[1]
    return pl.pallas_call(
        _ffn_kernel,
        grid=(m // tm, ff // tf),
        in_specs=[pl.BlockSpec((tm, d), lambda i, f: (i, 0)),
                  pl.BlockSpec((1, d), lambda i, f: (0, 0)),
                  pl.BlockSpec((d, tf), lambda i, f: (0, f)),
                  pl.BlockSpec((d, tf), lambda i, f: (0, f)),
                  pl.BlockSpec((tf, d), lambda i, f: (f, 0))],
        out_specs=pl.BlockSpec((tm, d), lambda i, f: (i, 0)),
        out_shape=jax.ShapeDtypeStruct((m, d), F32),
        scratch_shapes=[pltpu.VMEM((tm, d), BF16), pltpu.VMEM((tm, d), F32)],
        compiler_params=_cparams(("parallel", "arbitrary")),
        name="ffn",
    )(x, gain.reshape(1, d), wg, wu, wd)


def _moe_kernel(x_ref, g_ref, rt_ref, wg_ref, wu_ref, wd_ref, y_ref, h_ref, acc_ref, gate_ref):
    e = pl.program_id(1)
    f = pl.program_id(2)
    lane = lax.broadcasted_iota(jnp.int32, (1, 128), 1).astype(F32)

    @pl.when(jnp.logical_and(e == 0, f == 0))
    def _():
        hf = _rms(x_ref[...], g_ref[...])
        h_ref[...] = hf.astype(BF16)
        acc_ref[...] = jnp.zeros_like(acc_ref)
        valid = lane < N_EXPERTS
        logits = jnp.where(valid, _dot(hf, rt_ref[...]), NEG)
        ex = jnp.exp(logits - jnp.max(logits, axis=-1, keepdims=True))
        probs = ex / jnp.sum(ex, axis=-1, keepdims=True)
        m1 = jnp.max(probs, axis=-1, keepdims=True)
        i1 = jnp.min(jnp.where(jnp.logical_and(probs == m1, valid), lane, 128.0), axis=-1, keepdims=True)
        hot1 = lane == i1
        rest = jnp.where(jnp.logical_or(hot1, jnp.logical_not(valid)), -1.0, probs)
        m2 = jnp.max(rest, axis=-1, keepdims=True)
        i2 = jnp.min(jnp.where(rest == m2, lane, 128.0), axis=-1, keepdims=True)
        hot2 = lane == i2
        den = m1 + m2
        gate_ref[...] = jnp.where(hot1, m1 / den, 0.0) + jnp.where(hot2, m2 / den, 0.0)

    h = h_ref[...]
    ge = jnp.sum(jnp.where(lane == e.astype(F32), gate_ref[...], 0.0), axis=-1, keepdims=True)
    a = (_silu(jnp.dot(h, wg_ref[...], preferred_element_type=F32))
         * jnp.dot(h, wu_ref[...], preferred_element_type=F32) * ge)
    acc_ref[...] += jnp.dot(a.astype(BF16), wd_ref[...], preferred_element_type=F32)

    @pl.when(jnp.logical_and(e == pl.num_programs(1) - 1, f == pl.num_programs(2) - 1))
    def _():
        y_ref[...] = x_ref[...] + acc_ref[...]


def _moe(x, gain, router_pad, wg, wu, wd, tm, tf):
    m, d = x.shape
    ne, _, ff = wg.shape
    return pl.pallas_call(
        _moe_kernel,
        grid=(m // tm, ne, ff // tf),
        in_specs=[pl.BlockSpec((tm, d), lambda i, e, f: (i, 0)),
                  pl.BlockSpec((1, d), lambda i, e, f: (0, 0)),
                  pl.BlockSpec((d, 128), lambda i, e, f: (0, 0)),
                  pl.BlockSpec((None, d, tf), lambda i, e, f: (e, 0, f)),
                  pl.BlockSpec((None, d, tf), lambda i, e, f: (e, 0, f)),
                  pl.BlockSpec((None, tf, d), lambda i, e, f: (e, f, 0))],
        out_specs=pl.BlockSpec((tm, d), lambda i, e, f: (i, 0)),
        out_shape=jax.ShapeDtypeStruct((m, d), F32),
        scratch_shapes=[pltpu.VMEM((tm, d), BF16), pltpu.VMEM((tm, d), F32), pltpu.VMEM((tm, 128), F32)],
        compiler_params=_cparams(("parallel", "arbitrary", "arbitrary")),
        name="moe",
    )(x, gain.reshape(1, d), router_pad, wg, wu, wd)


MOE_ROWS = 288


def _moe_routed_kernel(x_ref, g_ref, rt_ref, wg_ref, wu_ref, wd_ref, y_ref,
                       h_ref, gate_scr, mask_scr, rank_scr, tri_scr):
    e = pl.program_id(1)
    tm = x_ref.shape[0]
    sub = lax.broadcasted_iota(jnp.int32, (N_EXPERTS, 1), 0).astype(F32)

    @pl.when(e == 0)
    def _():
        x = x_ref[...]
        hf = _rms(x, g_ref[...])
        h_ref[...] = hf.astype(BF16)
        y_ref[...] = x
        logits = _dot_nt(rt_ref[...], hf)
        ex = jnp.exp(logits - jnp.max(logits, axis=0, keepdims=True))
        probs = ex / jnp.sum(ex, axis=0, keepdims=True)
        m1 = jnp.max(probs, axis=0, keepdims=True)
        hot1 = sub == jnp.min(jnp.where(probs == m1, sub, float(N_EXPERTS)), axis=0, keepdims=True)
        rest = jnp.where(hot1, -1.0, probs)
        m2 = jnp.max(rest, axis=0, keepdims=True)
        hot2 = sub == jnp.min(jnp.where(rest == m2, sub, float(N_EXPERTS)), axis=0, keepdims=True)
        den = m1 + m2
        gate_scr[...] = jnp.where(hot1, m1 / den, 0.0) + jnp.where(hot2, m2 / den, 0.0)
        mask = jnp.where(jnp.logical_or(hot1, hot2), 1.0, 0.0)
        mask_scr[...] = mask
        tri_scr[...] = jnp.where(lax.broadcasted_iota(jnp.int32, (tm, tm), 0)
                                 <= lax.broadcasted_iota(jnp.int32, (tm, tm), 1), 1.0, 0.0).astype(BF16)
        rank_scr[...] = jnp.dot(mask.astype(BF16), tri_scr[...], preferred_element_type=F32) - 1.0

    mine = sub == e.astype(F32)
    row = lambda ref: jnp.sum(jnp.where(mine, ref[...], 0.0), axis=0, keepdims=True)
    gate_row, mask_row, rank_row = row(gate_scr), row(mask_scr), row(rank_scr)
    count = jnp.sum(mask_row).astype(jnp.int32)
    slot = lax.broadcasted_iota(jnp.int32, (MOE_ROWS, 1), 0).astype(F32)

    def chunk(j, carry):
        base = (j * MOE_ROWS).astype(F32)
        pick = jnp.where(jnp.logical_and(rank_row == slot + base, mask_row > 0.0), 1.0, 0.0)
        pick16 = pick.astype(BF16)
        xg = jnp.dot(pick16, h_ref[...], preferred_element_type=F32).astype(BF16)
        gate_col = jnp.sum(pick * gate_row, axis=1, keepdims=True)
        a = (_silu(jnp.dot(xg, wg_ref[...], preferred_element_type=F32))
             * jnp.dot(xg, wu_ref[...], preferred_element_type=F32) * gate_col)
        out = jnp.dot(a.astype(BF16), wd_ref[...], preferred_element_type=F32)
        y_ref[...] += _dot_tn(pick16, out)
        return carry

    lax.fori_loop(0, (count + MOE_ROWS - 1) // MOE_ROWS, chunk, 0)


def _moe_routed(x, gain, router_t, wg, wu, wd, tm):
    m, d = x.shape
    ne, _, ff = wg.shape
    return pl.pallas_call(
        _moe_routed_kernel,
        grid=(m // tm, ne),
        in_specs=[pl.BlockSpec((tm, d), lambda i, e: (i, 0)),
                  pl.BlockSpec((1, d), lambda i, e: (0, 0)),
                  pl.BlockSpec((ne, d), lambda i, e: (0, 0)),
                  pl.BlockSpec((None, d, ff), lambda i, e: (e, 0, 0)),
                  pl.BlockSpec((None, d, ff), lambda i, e: (e, 0, 0)),
                  pl.BlockSpec((None, ff, d), lambda i, e: (e, 0, 0))],
        out_specs=pl.BlockSpec((tm, d), lambda i, e: (i, 0)),
        out_shape=jax.ShapeDtypeStruct((m, d), F32),
        scratch_shapes=[pltpu.VMEM((tm, d), BF16), pltpu.VMEM((ne, tm), F32), pltpu.VMEM((ne, tm), F32),
                        pltpu.VMEM((ne, tm), F32), pltpu.VMEM((tm, tm), BF16)],
        compiler_params=_cparams(("parallel", "arbitrary")),
        name="moe_routed",
    )(x, gain.reshape(1, d), router_t, wg, wu, wd)


def _attn_sample_kernel(*refs, layer, full_roll):
    zq_ref, qg_ref, kg_ref, pavg_ref, b0_ref, b1_ref, b2_ref, bias0_ref, c0_ref, c1_ref, c2_ref = refs[:11]
    if full_roll:
        tails = (None,) * N_GROUPS
        rest = refs[11:]
    else:
        tails = refs[11:14]
        rest = refs[14:]
    o_ref, n0_ref, n1_ref, n2_ref, p0_scr, p1_scr, p2_scr, st_scr, oacc_scr = rest
    half = pl.program_id(1)
    active = (pl.program_id(2) == layer) if full_roll else True
    aw = A_WIDTH
    sel = (lax.broadcasted_iota(jnp.int32, (A_HEADS, aw), 1) // A_DH
           == lax.broadcasted_iota(jnp.int32, (A_HEADS, aw), 0))
    last_lane = lax.broadcasted_iota(jnp.int32, (1, 128), 1) == 127
    groups = ((c0_ref, n0_ref, b0_ref, p0_scr, tails[0]), (c1_ref, n1_ref, b1_ref, p1_scr, tails[1]),
              (c2_ref, n2_ref, b2_ref, p2_scr, tails[2]))
    for g, (c_ref, n_ref, b_ref, p_scr, t_ref) in enumerate(groups):
        wlen = c_ref.shape[1]
        if full_roll:
            n_ref[...] = pltpu.roll(c_ref[...], wlen - 1, axis=1)

        def newest(col, n_ref=n_ref, t_ref=t_ref, wlen=wlen):
            old = n_ref[:, wlen - 128:] if full_roll else t_ref[...]
            dst = n_ref.at[:, wlen - 128:] if full_roll else n_ref
            dst[...] = jnp.where(last_lane, col, old)

        lo = g * aw

        @pl.when(jnp.logical_and(active, half == 0))
        def _(g=g, c_ref=c_ref, b_ref=b_ref, p_scr=p_scr, lo=lo, newest=newest):
            q = zq_ref[:, Z_AQ + lo:Z_AQ + lo + aw]
            k = zq_ref[:, Z_AK + lo:Z_AK + lo + aw]
            qn = q * lax.rsqrt(_dot_hi(q * q, pavg_ref[...]) + EPS) * qg_ref[...] * (A_DH ** -0.5)
            kn = k * lax.rsqrt(_dot_hi(k * k, pavg_ref[...]) + EPS) * kg_ref[...]
            qbd = jnp.where(sel, jnp.broadcast_to(qn, (A_HEADS, aw)), 0.0)
            lc = jnp.sum((c_ref[...] * _to_col(qn, aw)).reshape(A_HEADS, A_DH, c_ref.shape[1]), axis=1) + b_ref[...]
            l0 = jnp.sum(qbd * kn, axis=1, keepdims=True) + bias0_ref[g]
            m = jnp.maximum(jnp.max(lc, axis=1, keepdims=True), l0)
            pc = jnp.exp(lc - m)
            p0 = jnp.exp(l0 - m)
            p_scr[...] = pc
            st_scr[g, 0] = jnp.broadcast_to(p0, (A_HEADS, 128))
            st_scr[g, 1] = jnp.broadcast_to(jnp.sum(pc, axis=1, keepdims=True) + p0, (A_HEADS, 128))
            st_scr[g, 2] = jnp.broadcast_to(m, (A_HEADS, 128))
            newest(_to_col(kn, aw))

        @pl.when(jnp.logical_and(active, half == 1))
        def _(g=g, c_ref=c_ref, p_scr=p_scr, lo=lo, newest=newest):
            v = zq_ref[:, Z_AV + lo:Z_AV + lo + aw]
            p0 = st_scr[g, 0][:, 0:1]
            s = st_scr[g, 1][:, 0:1]
            m = st_scr[g, 2][:, 0:1]
            o8 = (_dot_nt(p_scr[...], c_ref[...]) + _r16(p0) * _r16(v)) / s
            oacc_scr[g, 0:1, :] = jnp.sum(jnp.where(sel, o8, 0.0), axis=0, keepdims=True)
            oacc_scr[g, 1:2, :] = jnp.sum(jnp.where(sel, m + jnp.log(s), 0.0), axis=0, keepdims=True)
            newest(_to_col(v, aw))

    @pl.when(jnp.logical_and(active, half == 1))
    def _():
        l_1, l_2, l_3 = oacc_scr[0, 1:2, :], oacc_scr[1, 1:2, :], oacc_scr[2, 1:2, :]
        mx = jnp.maximum(jnp.maximum(l_1, l_2), l_3)
        e_1, e_2, e_3 = jnp.exp(l_1 - mx), jnp.exp(l_2 - mx), jnp.exp(l_3 - mx)
        o_ref[...] = ((e_1 * oacc_scr[0, 0:1, :] + e_2 * oacc_scr[1, 0:1, :] + e_3 * oacc_scr[2, 0:1, :])
                      / (e_1 + e_2 + e_3))


def _attn_sample(zs3, caches_t, rolled, layer, qgain, kgain, pavg, biases, bias0):
    bd = zs3.shape[0]
    full_roll = rolled is None
    aw = A_WIDTH
    if full_roll:
        grid = (bd, 2, DEPTH)
        const = lambda shape: pl.BlockSpec(shape, lambda i, h, l: (0,) * len(shape))
        zspec = pl.BlockSpec((None, 1, Z_AV + N_GROUPS * aw), lambda i, h, l: (i, 0, 0))
        ospec = pl.BlockSpec((None, 1, aw), lambda i, h, l: (i, 0, 0))
        cache_specs = [pl.BlockSpec((None, None, None, aw, c.shape[-1]), lambda i, h, l: (l, i, h, 0, 0))
                       for c in caches_t]
        new_specs = cache_specs
        extra_specs, extra_args, aliases = [], [], {}
        sem = ("parallel", "arbitrary", "arbitrary")
    else:
        grid = (bd, 2)
        const = lambda shape: pl.BlockSpec(shape, lambda i, h: (0,) * len(shape))
        zspec = pl.BlockSpec((None, 1, Z_AV + N_GROUPS * aw), lambda i, h: (i, 0, 0))
        ospec = pl.BlockSpec((None, 1, aw), lambda i, h: (i, 0, 0))
        cache_specs = [pl.BlockSpec((None, None, None, aw, c.shape[-1]), lambda i, h: (layer, i, h, 0, 0))
                       for c in caches_t]
        new_specs = [pl.BlockSpec((None, None, None, aw, 128), functools.partial(
            lambda last, i, h: (layer, i, h, 0, last), c.shape[-1] // 128 - 1)) for c in caches_t]
        extra_specs, extra_args = new_specs, list(rolled)
        aliases = {11 + g: 1 + g for g in range(N_GROUPS)}
        sem = ("parallel", "arbitrary")
    return pl.pallas_call(
        functools.partial(_attn_sample_kernel, layer=layer, full_roll=full_roll),
        grid=grid,
        in_specs=[zspec, const((1, aw)), const((1, aw)), const((aw, aw))]
                 + [const((A_HEADS, c.shape[-1])) for c in caches_t] + [const((N_GROUPS, A_HEADS, 1))]
                 + cache_specs + extra_specs,
        out_specs=[ospec] + new_specs,
        out_shape=[jax.ShapeDtypeStruct((bd, 1, aw), F32)]
                  + [jax.ShapeDtypeStruct(c.shape, c.dtype) for c in caches_t],
        scratch_shapes=[pltpu.VMEM((A_HEADS, c.shape[-1]), F32) for c in caches_t]
                       + [pltpu.VMEM((N_GROUPS, 3, A_HEADS, 128), F32), pltpu.VMEM((N_GROUPS, 8, aw), F32)],
        input_output_aliases=aliases,
        compiler_params=_cparams(sem),
        name="attn_sample",
    )(zs3, qgain, kgain, pavg, *biases, bias0, *caches_t, *extra_args)


def _to_col(row, n):
    eye = lax.broadcasted_iota(jnp.int32, (n, n), 0) == lax.broadcasted_iota(jnp.int32, (n, n), 1)
    return jnp.sum(jnp.where(eye, jnp.broadcast_to(row, (n, n)), 0.0), axis=1, keepdims=True)


def _sample_mix_kernel(bq_ref, bk_ref, bv_ref, cqkv_ref, sm_ref, cbuf_ref, sgla_ref, sgdn_ref,
                       wlr_ref, lrb_ref, cw_ref, dtb_ref, alog_ref,
                       ob_ref, oc_ref, nconv_ref, ngla_ref, ngdn_ref):
    sm = sm_ref[...]
    la =_log_sigmoid(_dot(sm, wlr_ref[...]) + lrb_ref[...]) * (1.0 / B_TAU)
    q = bq_ref[...] * (B_DK ** -0.5)
    k = bk_ref[...]
    v = bv_ref[...]
    for h in range(B_HEADS):
        lk = slice(B_DK * h, B_DK * (h + 1))
        lv = slice(B_DV * h, B_DV * (h + 1))
        s_old = sgla_ref[h]
        decay = jnp.exp(la[:, lk])
        ngla_ref[h] = _to_col(decay, B_DK) * s_old + _to_col(k[:, lk], B_DK) * v[:, lv]
        att = jnp.sum(q[:, lk] * k[:, lk], axis=-1, keepdims=True)
        ob_ref[:, lv] = (jnp.sum(_to_col(q[:, lk] * decay, B_DK) * s_old, axis=0, keepdims=True)
                         + att * v[:, lv])

    cat = jnp.concatenate([cbuf_ref[...], cqkv_ref[...]], axis=0)
    acc = None
    for j in range(CONV_W):
        term = cat[j:j + 1, :] * cw_ref[j:j + 1, :]
        acc = term if acc is None else acc + term
    y = _silu(acc)
    nconv_ref[...] = cat[1:CONV_W, :]
    hd = C_HEADS * C_DK
    for h in range(C_HEADS):
        ld = slice(C_DK * h, C_DK * (h + 1))
        qh = y[:, C_DK * h:C_DK * (h + 1)]
        qh = qh * lax.rsqrt(jnp.sum(qh * qh, axis=-1, keepdims=True) + EPS) * (C_DK ** -0.5)
        kh = y[:, hd + C_DK * h:hd + C_DK * (h + 1)]
        kh = kh * lax.rsqrt(jnp.sum(kh * kh, axis=-1, keepdims=True) + EPS)
        vh = y[:, 2 * hd + C_DV * h:2 * hd + C_DV * (h + 1)]
        beta = _sigmoid(sm[:, SM_CB + h:SM_CB + h + 1])
        g = -jnp.exp(alog_ref[:, ld]) * _softplus(sm[:, SM_CA + h:SM_CA + h + 1] + dtb_ref[:, ld])
        eg = jnp.exp(g)
        s = sgdn_ref[h]
        kcol = _to_col(kh, C_DK)
        u = beta * (vh - eg * jnp.sum(kcol * s, axis=0, keepdims=True))
        qs = jnp.sum(_to_col(qh, C_DK) * s, axis=0, keepdims=True)
        oc_ref[:, ld] = eg * qs + jnp.sum(qh * kh, axis=-1, keepdims=True) * u
        ngdn_ref[h] = eg * s + kcol * u


def _sample_mix(zs3, cbuf, sgla, sgdn, wlr_pad, lr_bias, conv_w, dtb_bc, alog_bc):
    bd = zs3.shape[0]
    hk, hv, hd = B_HEADS * B_DK, B_HEADS * B_DV, C_HEADS * C_DV
    const = lambda shape: pl.BlockSpec(shape, lambda i: (0,) * len(shape))
    zrow = lambda width, off: pl.BlockSpec((None, 1, width), lambda i: (i, 0, off // width))
    vec = lambda width: pl.BlockSpec((None, 1, width), lambda i: (i, 0, 0))
    st = lambda shape: pl.BlockSpec((None,) + shape, lambda i: (i,) + (0,) * len(shape))
    return pl.pallas_call(
        _sample_mix_kernel,
        grid=(bd,),
        in_specs=[zrow(hk, Z_BQ), zrow(hk, Z_BK), zrow(hv, Z_BV), zrow(C_CONV_CH, Z_CQKV), zrow(SMALL_W, Z_SMALL),
                  st((CONV_W - 1, C_CONV_CH)), st((B_HEADS, B_DK, B_DV)), st((C_HEADS, C_DK, C_DV))]
                 + [const((SMALL_W, hk)), const((1, hk)), const((CONV_W, C_CONV_CH)), const((1, hd)), const((1, hd))],
        out_specs=[vec(hv), vec(hd), st((CONV_W - 1, C_CONV_CH)),
                   st((B_HEADS, B_DK, B_DV)), st((C_HEADS, C_DK, C_DV))],
        out_shape=[jax.ShapeDtypeStruct((bd, 1, hv), F32),
                   jax.ShapeDtypeStruct((bd, 1, hd), F32), jax.ShapeDtypeStruct((bd, CONV_W - 1, C_CONV_CH), F32),
                   jax.ShapeDtypeStruct((bd, B_HEADS, B_DK, B_DV), F32),
                   jax.ShapeDtypeStruct((bd, C_HEADS, C_DK, C_DV), F32)],
        compiler_params=_cparams(("parallel",)),
        name="sample_mix",
    )(zs3, zs3, zs3, zs3, zs3, cbuf, sgla, sgdn, wlr_pad, lr_bias, conv_w, dtb_bc, alog_bc)


def _t5_bucket(dist):
    max_exact = N_BUCKETS // 2
    d = jnp.maximum(dist.astype(F32), 1.0)
    large = max_exact + (jnp.log(d / max_exact) / math.log(MAX_DIST / max_exact)
                         * (N_BUCKETS - max_exact)).astype(jnp.int32)
    large = jnp.minimum(large, N_BUCKETS - 1)
    return jnp.where(dist < max_exact, dist, large).astype(jnp.int32)


def _bias_tables(rel_bias):
    i = jnp.arange(SPAN)[:, None]
    c = jnp.arange(2 * SPAN)[None, :]
    dist = i + SPAN - c
    valid = (dist >= 0) & (dist <= SPAN)

    def lookup(bias_g, idx):
        hot = (idx[..., None] == jnp.arange(N_BUCKETS)).astype(F32)
        return jnp.einsum("...k,kh->...h", hot, bias_g.astype(F32), precision=HI)

    prompt, cached, new = [], [], []
    for g, (win, dil) in enumerate(A_GROUPS):
        bias_g = rel_bias[:, g * A_HEADS:(g + 1) * A_HEADS]
        tbl = lookup(bias_g, _t5_bucket(jnp.maximum(dist, 0) * dil)).transpose(2, 0, 1)
        prompt.append(jnp.where(valid[None], tbl, NEG))
        w = jnp.arange(win)
        tbl_s = lookup(bias_g, _t5_bucket(win - w)).T
        cached.append(jnp.where((w % dil == 0)[None], tbl_s, NEG))
        new.append(lookup(bias_g, _t5_bucket(jnp.zeros((1,), jnp.int32))).T)
    return jnp.stack(prompt), cached, jnp.stack(new)


def _prep_w_in(w):
    offs = np.concatenate([[0], np.cumsum(IN_SIZES)])
    seg = lambda i: w[:, offs[i]:offs[i + 1]]
    order = (0, 1, 2, 3, 4, 5, 6, 8, 9, 12, 7, 10, 11)
    used = sum(IN_SIZES)
    parts = [seg(i) for i in order] + [jnp.zeros((w.shape[0], NZ - used), w.dtype)]
    return jnp.concatenate(parts, axis=1).astype(BF16)


def _selector(offset):
    sel = np.zeros((SMALL_W, C_HEADS * C_DV), np.float32)
    for h in range(C_HEADS):
        sel[offset + h, C_DV * h:C_DV * (h + 1)] = 1.0
    return jnp.asarray(sel)


def _block_avg(width, group):
    idx = np.arange(width) // group
    return jnp.asarray((idx[:, None] == idx[None, :]).astype(np.float32) / group)


def kernel(x_prompt, x_sample, cache_win128_kv, cache_win512_kv, cache_win2048_kv, state_gla, state_gdn, state_conv, norm_mix, w_in, a_q_norm, a_k_norm, rel_bias, b_w_lr, b_lr_bias, b_out_norm, c_conv, c_a_log, c_dt_bias, c_out_norm, w_branch, w_out, norm_ffn, ffn_w_gate, ffn_w_up, ffn_w_down, moe_router, moe_w_gate, moe_w_up, moe_w_down):
    bp, t, d = x_prompt.shape
    bs = x_sample.shape[0]
    mp = bp * t
    caches_t = [jnp.transpose(c, (0, 1, 3, 4, 5, 2)).reshape(c.shape[:2] + (2, A_WIDTH, c.shape[2]))
                for c in (cache_win128_kv, cache_win512_kv, cache_win2048_kv)]
    bias_p, bias_c, bias_n = _bias_tables(rel_bias)
    pavg512 = _block_avg(A_WIDTH, A_DH)
    selb, sela = _selector(SM_CB), _selector(SM_CA)
    rolled = windows = None

    xp = x_prompt.reshape(mp, d)
    xs = x_sample.reshape(bs, d)
    outs_p = {k: [] for k in ("w0", "w1", "w2", "gla", "gdn", "conv")}
    outs_s = {k: [] for k in ("w0", "w1", "w2", "gla", "gdn", "conv")}
    for l in range(DEPTH):
        w_in_l = _prep_w_in(w_in[l])
        wb = w_branch[l].astype(BF16)
        wo = w_out[l].astype(BF16)
        qg128 = jnp.tile(a_q_norm[l], 2).reshape(1, 2 * A_DH)
        kg128 = jnp.tile(a_k_norm[l], 2).reshape(1, 2 * A_DH)
        qg512 = jnp.tile(a_q_norm[l], A_HEADS).reshape(1, A_WIDTH)
        kg512 = jnp.tile(a_k_norm[l], A_HEADS).reshape(1, A_WIDTH)
        wlr_pad = jnp.zeros((SMALL_W, B_HEADS * B_DK), F32).at[:B_RANK].set(b_w_lr[l])
        lr_bias = b_lr_bias[l].reshape(1, -1)
        bon = jnp.tile(b_out_norm[l], B_HEADS).reshape(1, -1)
        con = jnp.tile(c_out_norm[l], C_HEADS).reshape(1, -1)
        dtb_bc = jnp.repeat(c_dt_bias[l], C_DV).reshape(1, -1)
        alog_bc = jnp.repeat(c_a_log[l], C_DV).reshape(1, -1)

        z = _norm_matmul(xp, norm_mix[l], w_in_l, tm=1024, tn=1280)
        z3 = z.reshape(bp, t, NZ)
        o_a, *windows = _attn_prompt(z3, qg128, kg128, bias_p, l, windows)
        o_b, gla_t = _gla_prompt(z3, wlr_pad, lr_bias, tb=512)
        o_c, gdn_s = _gdn_prompt(z3, c_conv[l], selb, sela, dtb_bc, alog_bc, tb=512)
        xp = _merge(z, o_a.reshape(mp, -1), o_b.reshape(mp, -1), o_c.reshape(mp, -1), xp, bon, con, wb, wo, tm=256)
        gla = gla_t.reshape(bp, B_HEADS, B_DV, B_HEADS, B_DK)
        gla = jnp.stack([gla[:, h, :, h, :] for h in range(B_HEADS)], axis=1)
        outs_p["gla"].append(jnp.swapaxes(gla, 2, 3))
        outs_p["gdn"].append(gdn_s)
        outs_p["conv"].append(z3[:, t - (CONV_W - 1):, Z_CQKV:Z_CQKV + C_CONV_CH])

        zs = _norm_matmul(xs, norm_mix[l], w_in_l, tm=bs, tn=1280)
        zs3 = zs.reshape(bs, 1, NZ)
        oa_s, *rolled = _attn_sample(zs3, caches_t, rolled, l, qg512, kg512, pavg512, bias_c, bias_n)
        ob_s, oc_s, nconv, ngla, ngdn = _sample_mix(
            zs3, state_conv[l], state_gla[l], state_gdn[l], wlr_pad, lr_bias, c_conv[l], dtb_bc, alog_bc)
        xs = _merge(zs, oa_s.reshape(bs, -1), ob_s.reshape(bs, -1), oc_s.reshape(bs, -1), xs, bon, con, wb, wo, tm=bs)
        outs_s["gla"].append(ngla)
        outs_s["gdn"].append(ngdn)
        outs_s["conv"].append(nconv)

        i = l // 2
        if l % 2 == 0:
            wg, wu, wd = ffn_w_gate[i].astype(BF16), ffn_w_up[i].astype(BF16), ffn_w_down[i].astype(BF16)
            xp = _ffn(xp, norm_ffn[l], wg, wu, wd, tm=512, tf=1408)
            xs = _ffn(xs, norm_ffn[l], wg, wu, wd, tm=bs, tf=1408)
        else:
            wg, wu, wd = moe_w_gate[i].astype(BF16), moe_w_up[i].astype(BF16), moe_w_down[i].astype(BF16)
            router_pad = jnp.zeros((d, 128), F32).at[:, :N_EXPERTS].set(moe_router[i])
            xp = _moe_routed(xp, norm_ffn[l], moe_router[i].T, wg, wu, wd, tm=1024)
            xs = _moe(xs, norm_ffn[l], router_pad, wg, wu, wd, tm=bs, tf=896)

    st = lambda name, d_: jnp.stack(d_[name])

    def window_out(w):
        w = w.reshape(w.shape[:3] + (A_HEADS, A_DH, w.shape[-1]))
        return jnp.transpose(w, (0, 1, 5, 2, 3, 4))

    return (xp.reshape(bp, t, d), xs.reshape(bs, 1, d),
            window_out(windows[0]), window_out(rolled[0]), window_out(windows[1]), window_out(rolled[1]),
            window_out(windows[2]), window_out(rolled[2]),
            st("gla", outs_p), st("gla", outs_s), st("gdn", outs_p), st("gdn", outs_s),
            st("conv", outs_p), st("conv", outs_s))
```

```python
import functools
import math

import jax
import jax.numpy as jnp
import numpy as np
from jax import lax
from jax.experimental import pallas as pl
from jax.experimental.pallas import tpu as pltpu

F32 = jnp.float32
BF16 = jnp.bfloat16
HI = lax.Precision.HIGHEST
NEG = -1e30

D_MODEL = 1024
DEPTH = 2
EPS = 1e-6
A_GROUPS = ((128, 1), (512, 4), (2048, 16))
N_GROUPS = 3
A_HEADS = 8
A_DH = 64
A_WIDTH = A_HEADS * A_DH
SPAN = 128
ATT_PAIR = 4
GLA_PAIR = 4
GDN_PAIR = 4
N_BUCKETS = 32
MAX_DIST = 2048
B_HEADS = 4
B_DK = 64
B_DV = 128
B_RANK = 16
B_TAU = 16.0
C_HEADS = 4
C_DK = 128
C_DV = 128
CONV_W = 4
C_CONV_CH = 2 * C_HEADS * C_DK + C_HEADS * C_DV
CHUNK = 64
SUB = 16
BRANCH_W = 512
D_FF = 2816
N_EXPERTS = 8
D_FF_EXPERT = 1792
IN_SIZES = (N_GROUPS * A_WIDTH, N_GROUPS * A_WIDTH, N_GROUPS * A_WIDTH,
            B_HEADS * B_DK, B_HEADS * B_DK, B_HEADS * B_DV, B_HEADS * B_DV, B_RANK,
            C_CONV_CH, C_HEADS * C_DV, C_HEADS, C_HEADS, 3 * D_MODEL)

Z_AQ, Z_AK, Z_AV = 0, 1536, 3072
Z_BQ, Z_BK, Z_BV, Z_BG = 4608, 4864, 5120, 5632
Z_CQKV, Z_CZ, Z_GATES, Z_SMALL = 6144, 7680, 8192, 11264
SMALL_W = 256
NZ = Z_SMALL + SMALL_W
SM_CB, SM_CA = B_RANK, B_RANK + C_HEADS

VMEM_LIMIT = 56 * 1024 * 1024


def _cparams(sem):
    return pltpu.CompilerParams(dimension_semantics=sem, vmem_limit_bytes=VMEM_LIMIT)


def _dot(a, b):
    return jnp.dot(a.astype(BF16), b.astype(BF16), preferred_element_type=F32)


def _dot_nt(a, b):
    return lax.dot_general(a.astype(BF16), b.astype(BF16), (((1,), (1,)), ((), ())), preferred_element_type=F32)


def _dot_tn(a, b):
    return lax.dot_general(a.astype(BF16), b.astype(BF16), (((0,), (0,)), ((), ())), preferred_element_type=F32)


def _dot_hi(a, b):
    return jnp.dot(a, b, precision=HI, preferred_element_type=F32)


def _r16(x):
    return x.astype(BF16).astype(F32)


def _sigmoid(x):
    return jax.nn.sigmoid(x)


def _silu(x):
    return x * jax.nn.sigmoid(x)


def _softplus(x):
    return jnp.maximum(x, 0.0) + jnp.log1p(jnp.exp(-jnp.abs(x)))


def _log_sigmoid(x):
    return jnp.minimum(x, 0.0) - jnp.log1p(jnp.exp(-jnp.abs(x)))


def _rms(x, gain):
    return x * lax.rsqrt(jnp.mean(x * x, axis=-1, keepdims=True) + EPS) * gain


def _norm_matmul_kernel(x_ref, g_ref, w_ref, o_ref, h_ref):
    @pl.when(pl.program_id(1) == 0)
    def _():
        h_ref[...] = _rms(x_ref[...], g_ref[...]).astype(BF16)

    o_ref[...] = jnp.dot(h_ref[...], w_ref[...], preferred_element_type=F32)


def _norm_matmul(x, gain, w, tm, tn):
    m, d = x.shape
    n = w.shape[1]
    return pl.pallas_call(
        _norm_matmul_kernel,
        grid=(m // tm, n // tn),
        in_specs=[pl.BlockSpec((tm, d), lambda i, j: (i, 0)),
                  pl.BlockSpec((1, d), lambda i, j: (0, 0)),
                  pl.BlockSpec((d, tn), lambda i, j: (0, j))],
        out_specs=pl.BlockSpec((tm, tn), lambda i, j: (i, j)),
        out_shape=jax.ShapeDtypeStruct((m, n), F32),
        scratch_shapes=[pltpu.VMEM((tm, d), BF16)],
        compiler_params=_cparams(("parallel", "arbitrary")),
        name="norm_matmul",
    )(x, gain.reshape(1, d), w)


def _pair_rms(x, gain, head0):
    sq = x * x
    s0 = jnp.sum(jnp.where(head0, sq, 0.0), axis=-1, keepdims=True)
    s1 = jnp.sum(sq, axis=-1, keepdims=True) - s0
    return x * lax.rsqrt(jnp.where(head0, s0, s1) * (1.0 / A_DH) + EPS) * gain


def _attn_prompt_kernel(q1, k1, v1, q2, k2, v2, q3, k3, v3, qg_ref, kg_ref, bias_ref,
                        o_ref, w1, w2, w3, qt, kt, qs, ks, vs, og, lg, to, tl, *, slots):
    t_len = o_ref.shape[0]
    head0 = lax.broadcasted_iota(jnp.int32, (1, 2 * A_DH), 1) < A_DH
    first_half = lax.broadcasted_iota(jnp.int32, (1, 2 * SPAN), 1) < SPAN
    groups = ((q1, k1, v1, w1), (q2, k2, v2, w2), (q3, k3, v3, w3))
    for g, (qr, kr, vr, wr) in enumerate(groups):
        dil = A_GROUPS[g][1]
        length = t_len // dil
        nb = length // SPAN
        qt[...] = _pair_rms(qr[...], qg_ref[...], head0) * (A_DH ** -0.5)
        kt[...] = _pair_rms(kr[...], kg_ref[...], head0)
        wlen = wr.shape[-1]
        k_win = kt[t_len - wlen:, :].T
        v_win = vr[t_len - wlen:, :].T
        for s in range(slots):
            wr[s, 0] = k_win
            wr[s, 1] = v_win
        knr = kt
        for r in range(dil):
            rows = pl.ds(r, length, stride=dil) if dil > 1 else pl.ds(0, length)
            dst = pl.ds(r * length, length)
            qs[dst, :] = qt[rows, :].astype(BF16)
            ks[dst, :] = knr[rows, :].astype(BF16)
            vs[dst, :] = vr[rows, :].astype(BF16)

        if nb > 1:
            bias2 = jnp.concatenate([bias_ref[g, 0], bias_ref[g, 1]], axis=0)
        else:
            bias2 = jnp.concatenate([bias_ref[g, 0, :, SPAN:], bias_ref[g, 1, :, SPAN:]], axis=0)

        def block(jj, carry, nb=nb, bias2=bias2):
            row0s, k2s, v2s, logits = [], [], [], []
            for i in range(ATT_PAIR):
                j = jj * ATT_PAIR + i
                row0 = pl.multiple_of(j * SPAN, SPAN)
                qb = qs[pl.ds(row0, SPAN), :]
                qst = jnp.concatenate([jnp.where(head0, qb, jnp.zeros_like(qb)),
                                       jnp.where(head0, jnp.zeros_like(qb), qb)], axis=0)
                if nb > 1:
                    prow = pl.multiple_of(jnp.maximum(row0 - SPAN, 0), SPAN)
                    k2 = jnp.concatenate([ks[pl.ds(prow, SPAN), :], ks[pl.ds(row0, SPAN), :]], axis=0)
                    v2 = jnp.concatenate([vs[pl.ds(prow, SPAN), :], vs[pl.ds(row0, SPAN), :]], axis=0)
                else:
                    k2 = ks[pl.ds(row0, SPAN), :]
                    v2 = vs[pl.ds(row0, SPAN), :]
                lg_ = lax.dot_general(qst, k2, (((1,), (1,)), ((), ())), preferred_element_type=F32) + bias2
                if nb > 1:
                    lg_ = lg_ + jnp.where(first_half, jnp.where(j % nb == 0, NEG, 0.0), 0.0)
                row0s.append(row0)
                k2s.append(k2)
                v2s.append(v2)
                logits.append(lg_)
            lg_all = jnp.concatenate(logits, axis=0)
            m = jnp.max(lg_all, axis=-1, keepdims=True)
            p = jnp.exp(lg_all - m)
            s = jnp.sum(p, axis=-1, keepdims=True)
            p16 = p.astype(BF16)
            lse = m + jnp.log(s)
            for i in range(ATT_PAIR):
                lo = 2 * SPAN * i
                acc = jnp.dot(p16[lo:lo + 2 * SPAN], v2s[i], preferred_element_type=F32) / s[lo:lo + 2 * SPAN]
                to[pl.ds(row0s[i], SPAN), :] = jnp.where(head0, acc[:SPAN], acc[SPAN:])
                tl[pl.ds(row0s[i], SPAN), :] = jnp.where(head0, jnp.broadcast_to(lse[lo:lo + SPAN], (SPAN, 2 * A_DH)),
                                                         jnp.broadcast_to(lse[lo + SPAN:lo + 2 * SPAN], (SPAN, 2 * A_DH)))
            return carry

        lax.fori_loop(0, t_len // (SPAN * ATT_PAIR), block, 0)
        for r in range(dil):
            rows = pl.ds(r, length, stride=dil) if dil > 1 else pl.ds(0, length)
            src = pl.ds(r * length, length)
            og[g, rows, :] = to[src, :]
            lg[g, rows, :] = tl[src, :]
    l_1, l_2, l_3 = lg[0], lg[1], lg[2]
    mx = jnp.maximum(jnp.maximum(l_1, l_2), l_3)
    e_1, e_2, e_3 = jnp.exp(l_1 - mx), jnp.exp(l_2 - mx), jnp.exp(l_3 - mx)
    o_ref[...] = (e_1 * og[0] + e_2 * og[1] + e_3 * og[2]) / (e_1 + e_2 + e_3)


def _attn_prompt(z3, qgain, kgain, bias, layer, windows):
    b, t, _ = z3.shape
    hp_blocks = A_WIDTH // 128

    def col(base, g):
        return lambda i, hp: (i, 0, base // 128 + g * hp_blocks + hp)

    in_specs = []
    for g in range(N_GROUPS):
        for base in (Z_AQ, Z_AK, Z_AV):
            in_specs.append(pl.BlockSpec((None, t, 128), col(base, g)))
    in_specs += [pl.BlockSpec((1, 128), lambda i, hp: (0, 0)),
                 pl.BlockSpec((1, 128), lambda i, hp: (0, 0)),
                 pl.BlockSpec((N_GROUPS, 2, SPAN, 2 * SPAN), lambda i, hp: (0, hp, 0, 0))]
    args = [z3] * 9 + [qgain, kgain, bias]
    slots = DEPTH if windows is None else 1
    win_specs, win_shapes = [], []
    for win, _ in A_GROUPS:
        wp = min(win, t)
        win_specs.append(pl.BlockSpec((slots, None, 2, 128, wp), lambda i, hp: (layer // slots, i, 0, hp, 0)))
        win_shapes.append(jax.ShapeDtypeStruct((DEPTH, b, 2, A_WIDTH, wp), F32))
    aliases = {}
    if windows is not None:
        aliases = {len(args) + g: 1 + g for g in range(N_GROUPS)}
        in_specs += [pl.BlockSpec(memory_space=pl.ANY)] * N_GROUPS
        args += list(windows)
    kern = functools.partial(_attn_prompt_kernel, slots=slots)
    if windows is not None:
        kern = functools.partial(_drop_refs, kern, len(args) - N_GROUPS, N_GROUPS)
    return pl.pallas_call(
        kern,
        grid=(b, hp_blocks),
        in_specs=in_specs,
        out_specs=[pl.BlockSpec((None, t, 128), lambda i, hp: (i, 0, hp))] + win_specs,
        out_shape=[jax.ShapeDtypeStruct((b, t, A_WIDTH), F32)] + win_shapes,
        scratch_shapes=[pltpu.VMEM((t, 128), F32), pltpu.VMEM((t, 128), F32),
                        pltpu.VMEM((t, 128), BF16), pltpu.VMEM((t, 128), BF16), pltpu.VMEM((t, 128), BF16),
                        pltpu.VMEM((N_GROUPS, t, 128), F32), pltpu.VMEM((N_GROUPS, t, 128), F32),
                        pltpu.VMEM((t, 128), F32), pltpu.VMEM((t, 128), F32)],
        input_output_aliases=aliases,
        compiler_params=_cparams(("parallel", "parallel")),
        name="attn_prompt",
    )(*args)


def _drop_refs(kern, start, count, *refs):
    return kern(*refs[:start], *refs[start + count:])


def _gla_prompt_kernel(q_ref, k_ref, v_ref, sm_ref, wlr_ref, lrb_ref, o_ref, st_ref, s_scr, la_scr):
    tb = q_ref.shape[0]
    hk = B_HEADS * B_DK
    hv = B_HEADS * B_DV

    @pl.when(pl.program_id(1) == 0)
    def _():
        s_scr[...] = jnp.zeros_like(s_scr)

    la_scr[...] = _log_sigmoid(_dot(sm_ref[...], wlr_ref[...]) + lrb_ref[...]) * (1.0 / B_TAU)

    rr = lax.broadcasted_iota(jnp.int32, (CHUNK, CHUNK), 0)
    cc = lax.broadcasted_iota(jnp.int32, (CHUNK, CHUNK), 1)
    causal = rr >= cc
    ltri = causal.astype(F32)
    lane_head = lax.broadcasted_iota(jnp.int32, (1, hk), 1) // B_DK
    row = lax.broadcasted_iota(jnp.int32, (CHUNK, 1), 0)
    same_head = (lax.broadcasted_iota(jnp.int32, (hv, hk), 0) // B_DV
                 == lax.broadcasted_iota(jnp.int32, (hv, hk), 1) // B_DK)

    def intra(r0):
        b = _dot_hi(ltri, la_scr[pl.ds(r0, CHUNK), :])
        q = q_ref[pl.ds(r0, CHUNK), :] * (B_DK ** -0.5)
        k = k_ref[pl.ds(r0, CHUNK), :]
        vb = v_ref[pl.ds(r0, CHUNK), :].astype(BF16)
        b_last = b[CHUNK - 1:CHUNK, :]
        atts = []
        for i in range(CHUNK // SUB):
            b_ref0 = b[SUB * i:SUB * i + 1, :]
            qi = q[SUB * i:SUB * (i + 1), :] * jnp.exp(b[SUB * i:SUB * (i + 1), :] - b_ref0)
            qst = jnp.concatenate([jnp.where(lane_head == h, qi, 0.0) for h in range(B_HEADS)], axis=0)
            kt = k * jnp.exp(jnp.where(row < SUB * (i + 1), b_ref0 - b, 0.0))
            atts.append(_dot_nt(qst, kt))
        o_parts = []
        for h in range(B_HEADS):
            att = jnp.concatenate([a[SUB * h:SUB * (h + 1), :] for a in atts], axis=0)
            att = jnp.where(causal, att, 0.0)
            o_parts.append(_dot(att, vb[:, B_DV * h:B_DV * (h + 1)]))
        upd = jnp.where(same_head, _dot_tn(vb, k * jnp.exp(b_last - b)), 0.0)
        return (q * jnp.exp(b)).astype(BF16), jnp.concatenate(o_parts, axis=1), jnp.exp(b_last), upd

    def chunks(cc, carry):
        r0s = [pl.multiple_of((cc * GLA_PAIR + i) * CHUNK, CHUNK) for i in range(GLA_PAIR)]
        parts = [intra(r0) for r0 in r0s]
        for r0, (qe, o_intra, decay, upd) in zip(r0s, parts):
            s = s_scr[...]
            o_ref[pl.ds(r0, CHUNK), :] = _dot_nt(qe, s) + o_intra
            s_scr[...] = s * decay + upd
        return carry

    lax.fori_loop(0, tb // (CHUNK * GLA_PAIR), chunks, 0)

    @pl.when(pl.program_id(1) == pl.num_programs(1) - 1)
    def _():
        st_ref[...] = s_scr[...]


def _gla_prompt(z3, wlr_pad, lr_bias, tb):
    b, t, _ = z3.shape
    hk, hv = B_HEADS * B_DK, B_HEADS * B_DV
    return pl.pallas_call(
        _gla_prompt_kernel,
        grid=(b, t // tb),
        in_specs=[pl.BlockSpec((None, tb, hk), lambda i, j: (i, j, Z_BQ // hk)),
                  pl.BlockSpec((None, tb, hk), lambda i, j: (i, j, Z_BK // hk)),
                  pl.BlockSpec((None, tb, hv), lambda i, j: (i, j, Z_BV // hv)),
                  pl.BlockSpec((None, tb, SMALL_W), lambda i, j: (i, j, Z_SMALL // SMALL_W)),
                  pl.BlockSpec((SMALL_W, hk), lambda i, j: (0, 0)),
                  pl.BlockSpec((1, hk), lambda i, j: (0, 0))],
        out_specs=[pl.BlockSpec((None, tb, hv), lambda i, j: (i, j, 0)),
                   pl.BlockSpec((None, hv, hk), lambda i, j: (i, 0, 0))],
        out_shape=[jax.ShapeDtypeStruct((b, t, hv), F32), jax.ShapeDtypeStruct((b, hv, hk), F32)],
        scratch_shapes=[pltpu.VMEM((hv, hk), F32), pltpu.VMEM((tb, hk), F32)],
        compiler_params=_cparams(("parallel", "arbitrary")),
        name="gla_prompt",
    )(z3, z3, z3, z3, wlr_pad, lr_bias)


def _gdn_prompt_kernel(x_ref, sm_ref, cw_ref, selb_ref, sela_ref, dtb_ref, alog_ref, o_ref, s_ref,
                       xpad, cv, gsc, bsc, s_scr):
    tb = x_ref.shape[0]
    hd = C_HEADS * C_DK
    pad = 8

    @pl.when(pl.program_id(1) == 0)
    def _():
        s_scr[...] = jnp.zeros_like(s_scr)
        xpad[0:pad, :] = jnp.zeros((pad, C_CONV_CH), F32)

    xpad[pad:pad + tb, :] = x_ref[...]
    rb = 128
    for i in range(tb // rb):
        acc = None
        for j in range(CONV_W):
            lo = pad - (CONV_W - 1) + j + i * rb
            term = xpad[lo:lo + rb, :] * cw_ref[j:j + 1, :]
            acc = term if acc is None else acc + term
        y = _silu(acc)
        rows = slice(i * rb, (i + 1) * rb)
        for h in range(C_HEADS):
            qh = y[:, C_DK * h:C_DK * (h + 1)]
            cv[rows, C_DK * h:C_DK * (h + 1)] = (
                qh * lax.rsqrt(jnp.sum(qh * qh, axis=-1, keepdims=True) + EPS) * (C_DK ** -0.5))
            kh = y[:, hd + C_DK * h:hd + C_DK * (h + 1)]
            cv[rows, hd + C_DK * h:hd + C_DK * (h + 1)] = (
                kh * lax.rsqrt(jnp.sum(kh * kh, axis=-1, keepdims=True) + EPS))
        cv[rows, 2 * hd:] = y[:, 2 * hd:]
    xpad[pad - (CONV_W - 1):pad, :] = xpad[pad + tb - (CONV_W - 1):pad + tb, :]

    sm = sm_ref[...]
    gsc[...] = _dot_hi(-jnp.exp(alog_ref[...]) * _softplus(sm + dtb_ref[...]), sela_ref[...])
    bsc[...] = _dot_hi(_sigmoid(sm), selb_ref[...])

    n = C_HEADS * CHUNK
    rr = lax.broadcasted_iota(jnp.int32, (CHUNK, CHUNK), 0)
    cc = lax.broadcasted_iota(jnp.int32, (CHUNK, CHUNK), 1)
    ltri = (rr >= cc).astype(F32)
    ri = lax.broadcasted_iota(jnp.int32, (n, n), 0)
    ci = lax.broadcasted_iota(jnp.int32, (n, n), 1)
    same = (ri // CHUNK) == (ci // CHUNK)
    incl = jnp.logical_and(same, ci <= ri)
    strict = jnp.logical_and(same, ci < ri)
    eye = (ri == ci).astype(F32)

    def stack(x):
        return jnp.concatenate([x[:, C_DK * h:C_DK * (h + 1)] for h in range(C_HEADS)], axis=0)

    def chunks(cc, carry):
        pair = range(GDN_PAIR)
        r0s = [pl.multiple_of((cc * GDN_PAIR + i) * CHUNK, CHUNK) for i in pair]
        bst = [stack(_dot_hi(ltri, gsc[pl.ds(r0, CHUNK), :])) for r0 in r0s]
        beta = [stack(bsc[pl.ds(r0, CHUNK), :]) for r0 in r0s]
        qst = [stack(cv[pl.ds(r0, CHUNK), 0:hd]) for r0 in r0s]
        kst = [stack(cv[pl.ds(r0, CHUNK), hd:2 * hd]) for r0 in r0s]
        vst = [stack(cv[pl.ds(r0, CHUNK), 2 * hd:3 * hd]) for r0 in r0s]
        kb = [k.astype(BF16) for k in kst]
        nt = (((1,), (1,)), ((), ()))
        kk = [lax.dot_general(kb[i], kb[i], nt, preferred_element_type=F32) for i in pair]
        qk = [lax.dot_general(qst[i].astype(BF16), kb[i], nt, preferred_element_type=F32) for i in pair]
        dec, a, aqk = [], [], []
        for i in pair:
            bst_t = bst[i].T
            d_ = jnp.exp(jnp.where(incl, jnp.concatenate([bst[i], bst[i]], axis=1)
                                   - jnp.concatenate([bst_t, bst_t], axis=0), NEG))
            dec.append(d_)
            a.append(jnp.where(strict, jnp.concatenate([beta[i], beta[i]], axis=1) * d_ * kk[i], 0.0))
            aqk.append(d_ * qk[i])
        inv = [eye - a[i] for i in pair]
        pw = a
        for _ in range(int(math.log2(CHUNK)) - 1):
            pw = [_dot(pw[i], pw[i]) for i in pair]
            inv = [inv[i] + _dot(inv[i], pw[i]) for i in pair]
        eb = [jnp.exp(bst[i]) for i in pair]
        wu = [_dot(inv[i], jnp.concatenate([beta[i] * eb[i] * kst[i], beta[i] * vst[i]], axis=1)) for i in pair]
        for i in pair:
            w, uv = wu[i][:, :C_DK], wu[i][:, C_DK:]
            us, qss = [], []
            for h in range(C_HEADS):
                sl = slice(CHUNK * h, CHUNK * (h + 1))
                ws = _dot(jnp.concatenate([w[sl], qst[i][sl]], axis=0), s_scr[h])
                us.append(uv[sl] - ws[:CHUNK])
                qss.append(ws[CHUNK:])
            o = eb[i] * jnp.concatenate(qss, axis=0) + _dot(aqk[i], jnp.concatenate(us, axis=0))
            for h in range(C_HEADS):
                sl = slice(CHUNK * h, CHUNK * (h + 1))
                o_ref[pl.ds(r0s[i], CHUNK), C_DV * h:C_DV * (h + 1)] = o[sl]
                b_last = bst[i][CHUNK * (h + 1) - 1:CHUNK * (h + 1), :]
                s_scr[h] = (jnp.exp(b_last) * s_scr[h]
                            + _dot_tn(kst[i][sl] * jnp.exp(b_last - bst[i][sl]), us[h]))
        return carry

    lax.fori_loop(0, tb // (CHUNK * GDN_PAIR), chunks, 0)

    @pl.when(pl.program_id(1) == pl.num_programs(1) - 1)
    def _():
        s_ref[...] = s_scr[...]


def _gdn_prompt(z3, conv_w, selb, sela, dtb_bc, alog_bc, tb):
    b, t, _ = z3.shape
    hd = C_HEADS * C_DV
    const = lambda shape: pl.BlockSpec(shape, lambda i, j: (0,) * len(shape))
    return pl.pallas_call(
        _gdn_prompt_kernel,
        grid=(b, t // tb),
        in_specs=[pl.BlockSpec((None, tb, C_CONV_CH), lambda i, j: (i, j, Z_CQKV // C_CONV_CH)),
                  pl.BlockSpec((None, tb, SMALL_W), lambda i, j: (i, j, Z_SMALL // SMALL_W)),
                  const((CONV_W, C_CONV_CH)), const((SMALL_W, hd)), const((SMALL_W, hd)),
                  const((1, SMALL_W)), const((1, SMALL_W))],
        out_specs=[pl.BlockSpec((None, tb, hd), lambda i, j: (i, j, 0)),
                   pl.BlockSpec((None, C_HEADS, C_DK, C_DV), lambda i, j: (i, 0, 0, 0))],
        out_shape=[jax.ShapeDtypeStruct((b, t, hd), F32),
                   jax.ShapeDtypeStruct((b, C_HEADS, C_DK, C_DV), F32)],
        scratch_shapes=[pltpu.VMEM((tb + 8, C_CONV_CH), F32), pltpu.VMEM((tb, C_CONV_CH), F32),
                        pltpu.VMEM((tb, hd), F32), pltpu.VMEM((tb, hd), F32),
                        pltpu.VMEM((C_HEADS, C_DK, C_DV), F32)],
        compiler_params=_cparams(("parallel", "arbitrary")),
        name="gdn_prompt",
    )(z3, z3, conv_w, selb, sela, dtb_bc, alog_bc)


def _head_norm(o, gain, width):
    parts = []
    for h in range(o.shape[1] // width):
        oh = o[:, width * h:width * (h + 1)]
        parts.append(oh * lax.rsqrt(jnp.mean(oh * oh, axis=-1, keepdims=True) + EPS))
    return jnp.concatenate(parts, axis=1) * gain


def _merge_kernel(oa_ref, ob_ref, bg_ref, oc_ref, cz_ref, ga_ref, gb_ref, gc_ref, x_ref,
                  bon_ref, con_ref, wb_ref, wo_ref, y_ref):
    ob = _head_norm(ob_ref[...], bon_ref[...], B_DV) * _silu(bg_ref[...])
    oc = _head_norm(oc_ref[...], con_ref[...], C_DV) * _silu(cz_ref[...])
    merged = (_sigmoid(ga_ref[...]) * _dot(oa_ref[...], wb_ref[0])
              + _sigmoid(gb_ref[...]) * _dot(ob, wb_ref[1])
              + _sigmoid(gc_ref[...]) * _dot(oc, wb_ref[2]))
    y_ref[...] = x_ref[...] + _dot(merged, wo_ref[...])


def _merge(z, oa, ob, oc, x, bon, con, wb, wo, tm):
    m = x.shape[0]
    w = BRANCH_W
    row = lambda width, blk: pl.BlockSpec((tm, width), lambda i: (i, blk))
    const = lambda shape: pl.BlockSpec(shape, lambda i: (0,) * len(shape))
    return pl.pallas_call(
        _merge_kernel,
        grid=(m // tm,),
        in_specs=[row(w, 0), row(w, 0), row(w, Z_BG // w), row(w, 0), row(w, Z_CZ // w),
                  row(D_MODEL, Z_GATES // D_MODEL), row(D_MODEL, Z_GATES // D_MODEL + 1),
                  row(D_MODEL, Z_GATES // D_MODEL + 2), row(D_MODEL, 0),
                  const((1, w)), const((1, w)), const((3, w, D_MODEL)), const((D_MODEL, D_MODEL))],
        out_specs=row(D_MODEL, 0),
        out_shape=jax.ShapeDtypeStruct((m, D_MODEL), F32),
        compiler_params=_cparams(("parallel",)),
        name="merge",
    )(oa, ob, z, oc, z, z, z, z, x, bon, con, wb, wo)


def _ffn_kernel(x_ref, g_ref, wg_ref, wu_ref, wd_ref, y_ref, h_ref, acc_ref):
    f = pl.program_id(1)

    @pl.when(f == 0)
    def _():
        h_ref[...] = _rms(x_ref[...], g_ref[...]).astype(BF16)
        acc_ref[...] = jnp.zeros_like(acc_ref)

    h = h_ref[...]
    a = _silu(jnp.dot(h, wg_ref[...], preferred_element_type=F32)) * jnp.dot(h, wu_ref[...],
                                                                              preferred_element_type=F32)
    acc_ref[...] += jnp.dot(a.astype(BF16), wd_ref[...], preferred_element_type=F32)

    @pl.when(f == pl.num_programs(1) - 1)
    def _():
        y_ref[...] = x_ref[...] + acc_ref[...]


def _ffn(x, gain, wg, wu, wd, tm, tf):
    m, d = x.shape
    ff = wg.shape[1]
    return pl.pallas_call(
        _ffn_kernel,
        grid=(m // tm, ff // tf),
        in_specs=[pl.BlockSpec((tm, d), lambda i, f: (i, 0)),
                  pl.BlockSpec((1, d), lambda i, f: (0, 0)),
                  pl.BlockSpec((d, tf), lambda i, f: (0, f)),
                  pl.BlockSpec((d, tf), lambda i, f: (0, f)),
                  pl.BlockSpec((tf, d), lambda i, f: (f, 0))],
        out_specs=pl.BlockSpec((tm, d), lambda i, f: (i, 0)),
        out_shape=jax.ShapeDtypeStruct((m, d), F32),
        scratch_shapes=[pltpu.VMEM((tm, d), BF16), pltpu.VMEM((tm, d), F32)],
        compiler_params=_cparams(("parallel", "arbitrary")),
        name="ffn",
    )(x, gain.reshape(1, d), wg, wu, wd)


def _moe_kernel(x_ref, g_ref, rt_ref, wg_ref, wu_ref, wd_ref, y_ref, h_ref, acc_ref, gate_ref):
    e = pl.program_id(1)
    f = pl.program_id(2)
    lane = lax.broadcasted_iota(jnp.int32, (1, 128), 1).astype(F32)

    @pl.when(jnp.logical_and(e == 0, f == 0))
    def _():
        hf = _rms(x_ref[...], g_ref[...])
        h_ref[...] = hf.astype(BF16)
        acc_ref[...] = jnp.zeros_like(acc_ref)
        valid = lane < N_EXPERTS
        logits = jnp.where(valid, _dot(hf, rt_ref[...]), NEG)
        ex = jnp.exp(logits - jnp.max(logits, axis=-1, keepdims=True))
        probs = ex / jnp.sum(ex, axis=-1, keepdims=True)
        m1 = jnp.max(probs, axis=-1, keepdims=True)
        i1 = jnp.min(jnp.where(jnp.logical_and(probs == m1, valid), lane, 128.0), axis=-1, keepdims=True)
        hot1 = lane == i1
        rest = jnp.where(jnp.logical_or(hot1, jnp.logical_not(valid)), -1.0, probs)
        m2 = jnp.max(rest, axis=-1, keepdims=True)
        i2 = jnp.min(jnp.where(rest == m2, lane, 128.0), axis=-1, keepdims=True)
        hot2 = lane == i2
        den = m1 + m2
        gate_ref[...] = jnp.where(hot1, m1 / den, 0.0) + jnp.where(hot2, m2 / den, 0.0)

    h = h_ref[...]
    ge = jnp.sum(jnp.where(lane == e.astype(F32), gate_ref[...], 0.0), axis=-1, keepdims=True)
    a = (_silu(jnp.dot(h, wg_ref[...], preferred_element_type=F32))
         * jnp.dot(h, wu_ref[...], preferred_element_type=F32) * ge)
    acc_ref[...] += jnp.dot(a.astype(BF16), wd_ref[...], preferred_element_type=F32)

    @pl.when(jnp.logical_and(e == pl.num_programs(1) - 1, f == pl.num_programs(2) - 1))
    def _():
        y_ref[...] = x_ref[...] + acc_ref[...]


def _moe(x, gain, router_pad, wg, wu, wd, tm, tf):
    m, d = x.shape
    ne, _, ff = wg.shape
    return pl.pallas_call(
        _moe_kernel,
        grid=(m // tm, ne, ff // tf),
        in_specs=[pl.BlockSpec((tm, d), lambda i, e, f: (i, 0)),
                  pl.BlockSpec((1, d), lambda i, e, f: (0, 0)),
                  pl.BlockSpec((d, 128), lambda i, e, f: (0, 0)),
                  pl.BlockSpec((None, d, tf), lambda i, e, f: (e, 0, f)),
                  pl.BlockSpec((None, d, tf), lambda i, e, f: (e, 0, f)),
                  pl.BlockSpec((None, tf, d), lambda i, e, f: (e, f, 0))],
        out_specs=pl.BlockSpec((tm, d), lambda i, e, f: (i, 0)),
        out_shape=jax.ShapeDtypeStruct((m, d), F32),
        scratch_shapes=[pltpu.VMEM((tm, d), BF16), pltpu.VMEM((tm, d), F32), pltpu.VMEM((tm, 128), F32)],
        compiler_params=_cparams(("parallel", "arbitrary", "arbitrary")),
        name="moe",
    )(x, gain.reshape(1, d), router_pad, wg, wu, wd)


MOE_ROWS = 288


def _moe_routed_kernel(x_ref, g_ref, rt_ref, wg_ref, wu_ref, wd_ref, y_ref,
                       h_ref, gate_scr, mask_scr, rank_scr, tri_scr):
    e = pl.program_id(1)
    tm = x_ref.shape[0]
    sub = lax.broadcasted_iota(jnp.int32, (N_EXPERTS, 1), 0).astype(F32)

    @pl.when(e == 0)
    def _():
        x = x_ref[...]
        hf = _rms(x, g_ref[...])
        h_ref[...] = hf.astype(BF16)
        y_ref[...] = x
        logits = _dot_nt(rt_ref[...], hf)
        ex = jnp.exp(logits - jnp.max(logits, axis=0, keepdims=True))
        probs = ex / jnp.sum(ex, axis=0, keepdims=True)
        m1 = jnp.max(probs, axis=0, keepdims=True)
        hot1 = sub == jnp.min(jnp.where(probs == m1, sub, float(N_EXPERTS)), axis=0, keepdims=True)
        rest = jnp.where(hot1, -1.0, probs)
        m2 = jnp.max(rest, axis=0, keepdims=True)
        hot2 = sub == jnp.min(jnp.where(rest == m2, sub, float(N_EXPERTS)), axis=0, keepdims=True)
        den = m1 + m2
        gate_scr[...] = jnp.where(hot1, m1 / den, 0.0) + jnp.where(hot2, m2 / den, 0.0)
        mask = jnp.where(jnp.logical_or(hot1, hot2), 1.0, 0.0)
        mask_scr[...] = mask
        tri_scr[...] = jnp.where(lax.broadcasted_iota(jnp.int32, (tm, tm), 0)
                                 <= lax.broadcasted_iota(jnp.int32, (tm, tm), 1), 1.0, 0.0).astype(BF16)
        rank_scr[...] = jnp.dot(mask.astype(BF16), tri_scr[...], preferred_element_type=F32) - 1.0

    mine = sub == e.astype(F32)
    row = lambda ref: jnp.sum(jnp.where(mine, ref[...], 0.0), axis=0, keepdims=True)
    gate_row, mask_row, rank_row = row(gate_scr), row(mask_scr), row(rank_scr)
    count = jnp.sum(mask_row).astype(jnp.int32)
    slot = lax.broadcasted_iota(jnp.int32, (MOE_ROWS, 1), 0).astype(F32)

    def chunk(j, carry):
        base = (j * MOE_ROWS).astype(F32)
        pick = jnp.where(jnp.logical_and(rank_row == slot + base, mask_row > 0.0), 1.0, 0.0)
        pick16 = pick.astype(BF16)
        xg = jnp.dot(pick16, h_ref[...], preferred_element_type=F32).astype(BF16)
        gate_col = jnp.sum(pick * gate_row, axis=1, keepdims=True)
        a = (_silu(jnp.dot(xg, wg_ref[...], preferred_element_type=F32))
             * jnp.dot(xg, wu_ref[...], preferred_element_type=F32) * gate_col)
        out = jnp.dot(a.astype(BF16), wd_ref[...], preferred_element_type=F32)
        y_ref[...] += _dot_tn(pick16, out)
        return carry

    lax.fori_loop(0, (count + MOE_ROWS - 1) // MOE_ROWS, chunk, 0)


def _moe_routed(x, gain, router_t, wg, wu, wd, tm):
    m, d = x.shape
    ne, _, ff = wg.shape
    return pl.pallas_call(
        _moe_routed_kernel,
        grid=(m // tm, ne),
        in_specs=[pl.BlockSpec((tm, d), lambda i, e: (i, 0)),
                  pl.BlockSpec((1, d), lambda i, e: (0, 0)),
                  pl.BlockSpec((ne, d), lambda i, e: (0, 0)),
                  pl.BlockSpec((None, d, ff), lambda i, e: (e, 0, 0)),
                  pl.BlockSpec((None, d, ff), lambda i, e: (e, 0, 0)),
                  pl.BlockSpec((None, ff, d), lambda i, e: (e, 0, 0))],
        out_specs=pl.BlockSpec((tm, d), lambda i, e: (i, 0)),
        out_shape=jax.ShapeDtypeStruct((m, d), F32),
        scratch_shapes=[pltpu.VMEM((tm, d), BF16), pltpu.VMEM((ne, tm), F32), pltpu.VMEM((ne, tm), F32),
                        pltpu.VMEM((ne, tm), F32), pltpu.VMEM((tm, tm), BF16)],
        compiler_params=_cparams(("parallel", "arbitrary")),
        name="moe_routed",
    )(x, gain.reshape(1, d), router_t, wg, wu, wd)


def _attn_sample_kernel(*refs, layer, full_roll):
    zq_ref, qg_ref, kg_ref, b0_ref, b1_ref, b2_ref, bias0_ref, c0_ref, c1_ref, c2_ref = refs[:10]
    if full_roll:
        tails = (None,) * N_GROUPS
        rest = refs[10:]
    else:
        tails = refs[10:13]
        rest = refs[13:]
    o_ref, n0_ref, n1_ref, n2_ref, p0_scr, p1_scr, p2_scr, st_scr, oacc_scr = rest
    half = pl.program_id(1)
    active = (pl.program_id(2) == layer) if full_roll else True
    aw = A_WIDTH
    sel = (lax.broadcasted_iota(jnp.int32, (A_HEADS, aw), 1) // A_DH
           == lax.broadcasted_iota(jnp.int32, (A_HEADS, aw), 0))
    last_lane = lax.broadcasted_iota(jnp.int32, (1, 128), 1) == 127
    groups = ((c0_ref, n0_ref, b0_ref, p0_scr, tails[0]), (c1_ref, n1_ref, b1_ref, p1_scr, tails[1]),
              (c2_ref, n2_ref, b2_ref, p2_scr, tails[2]))
    for g, (c_ref, n_ref, b_ref, p_scr, t_ref) in enumerate(groups):
        wlen = c_ref.shape[1]
        if full_roll:
            n_ref[...] = pltpu.roll(c_ref[...], wlen - 1, axis=1)

        def newest(col, n_ref=n_ref, t_ref=t_ref, wlen=wlen):
            old = n_ref[:, wlen - 128:] if full_roll else t_ref[...]
            dst = n_ref.at[:, wlen - 128:] if full_roll else n_ref
            dst[...] = jnp.where(last_lane, col, old)

        lo = g * aw

        @pl.when(jnp.logical_and(active, half == 0))
        def _(g=g, c_ref=c_ref, b_ref=b_ref, p_scr=p_scr, lo=lo, newest=newest):
            q = zq_ref[:, Z_AQ + lo:Z_AQ + lo + aw]
            k = zq_ref[:, Z_AK + lo:Z_AK + lo + aw]
            def head_rms(x, gain):
                s8 = jnp.sum(jnp.where(sel, jnp.broadcast_to(x * x, (A_HEADS, aw)), 0.0), axis=1, keepdims=True)
                ms = jnp.sum(jnp.where(sel, s8, 0.0), axis=0, keepdims=True) * (1.0 / A_DH)
                return x * lax.rsqrt(ms + EPS) * gain

            qn = head_rms(q, qg_ref[...]) * (A_DH ** -0.5)
            kn = head_rms(k, kg_ref[...])
            qbd = jnp.where(sel, jnp.broadcast_to(qn, (A_HEADS, aw)), 0.0)
            lc = jnp.sum((c_ref[...] * _to_col(qn, aw)).reshape(A_HEADS, A_DH, c_ref.shape[1]), axis=1) + b_ref[...]
            l0 = jnp.sum(qbd * kn, axis=1, keepdims=True) + bias0_ref[g]
            m = jnp.maximum(jnp.max(lc, axis=1, keepdims=True), l0)
            pc = jnp.exp(lc - m)
            p0 = jnp.exp(l0 - m)
            p_scr[...] = pc
            st_scr[g, 0] = jnp.broadcast_to(p0, (A_HEADS, 128))
            st_scr[g, 1] = jnp.broadcast_to(jnp.sum(pc, axis=1, keepdims=True) + p0, (A_HEADS, 128))
            st_scr[g, 2] = jnp.broadcast_to(m, (A_HEADS, 128))
            newest(_to_col(kn, aw))

        @pl.when(jnp.logical_and(active, half == 1))
        def _(g=g, c_ref=c_ref, p_scr=p_scr, lo=lo, newest=newest):
            v = zq_ref[:, Z_AV + lo:Z_AV + lo + aw]
            p0 = st_scr[g, 0][:, 0:1]
            s = st_scr[g, 1][:, 0:1]
            m = st_scr[g, 2][:, 0:1]
            o8 = (_dot_nt(p_scr[...], c_ref[...]) + _r16(p0) * _r16(v)) / s
            oacc_scr[g, 0:1, :] = jnp.sum(jnp.where(sel, o8, 0.0), axis=0, keepdims=True)
            oacc_scr[g, 1:2, :] = jnp.sum(jnp.where(sel, m + jnp.log(s), 0.0), axis=0, keepdims=True)
            newest(_to_col(v, aw))

    @pl.when(jnp.logical_and(active, half == 1))
    def _():
        l_1, l_2, l_3 = oacc_scr[0, 1:2, :], oacc_scr[1, 1:2, :], oacc_scr[2, 1:2, :]
        mx = jnp.maximum(jnp.maximum(l_1, l_2), l_3)
        e_1, e_2, e_3 = jnp.exp(l_1 - mx), jnp.exp(l_2 - mx), jnp.exp(l_3 - mx)
        o_ref[...] = ((e_1 * oacc_scr[0, 0:1, :] + e_2 * oacc_scr[1, 0:1, :] + e_3 * oacc_scr[2, 0:1, :])
                      / (e_1 + e_2 + e_3))


def _attn_sample(zs3, caches_t, rolled, layer, qgain, kgain, biases, bias0):
    bd = zs3.shape[0]
    full_roll = rolled is None
    aw = A_WIDTH
    if full_roll:
        grid = (bd, 2, DEPTH)
        const = lambda shape: pl.BlockSpec(shape, lambda i, h, l: (0,) * len(shape))
        zspec = pl.BlockSpec((None, 1, Z_AV + N_GROUPS * aw), lambda i, h, l: (i, 0, 0))
        ospec = pl.BlockSpec((None, 1, aw), lambda i, h, l: (i, 0, 0))
        cache_specs = [pl.BlockSpec((None, None, None, aw, c.shape[-1]), lambda i, h, l: (l, i, h, 0, 0))
                       for c in caches_t]
        new_specs = cache_specs
        extra_specs, extra_args, aliases = [], [], {}
        sem = ("parallel", "arbitrary", "arbitrary")
    else:
        grid = (bd, 2)
        const = lambda shape: pl.BlockSpec(shape, lambda i, h: (0,) * len(shape))
        zspec = pl.BlockSpec((None, 1, Z_AV + N_GROUPS * aw), lambda i, h: (i, 0, 0))
        ospec = pl.BlockSpec((None, 1, aw), lambda i, h: (i, 0, 0))
        cache_specs = [pl.BlockSpec((None, None, None, aw, c.shape[-1]), lambda i, h: (layer, i, h, 0, 0))
                       for c in caches_t]
        new_specs = [pl.BlockSpec((None, None, None, aw, 128), functools.partial(
            lambda last, i, h: (layer, i, h, 0, last), c.shape[-1] // 128 - 1)) for c in caches_t]
        extra_specs, extra_args = new_specs, list(rolled)
        aliases = {10 + g: 1 + g for g in range(N_GROUPS)}
        sem = ("parallel", "arbitrary")
    return pl.pallas_call(
        functools.partial(_attn_sample_kernel, layer=layer, full_roll=full_roll),
        grid=grid,
        in_specs=[zspec, const((1, aw)), const((1, aw))]
                 + [const((A_HEADS, c.shape[-1])) for c in caches_t] + [const((N_GROUPS, A_HEADS, 1))]
                 + cache_specs + extra_specs,
        out_specs=[ospec] + new_specs,
        out_shape=[jax.ShapeDtypeStruct((bd, 1, aw), F32)]
                  + [jax.ShapeDtypeStruct(c.shape, c.dtype) for c in caches_t],
        scratch_shapes=[pltpu.VMEM((A_HEADS, c.shape[-1]), F32) for c in caches_t]
                       + [pltpu.VMEM((N_GROUPS, 3, A_HEADS, 128), F32), pltpu.VMEM((N_GROUPS, 8, aw), F32)],
        input_output_aliases=aliases,
        compiler_params=_cparams(sem),
        name="attn_sample",
    )(zs3, qgain, kgain, *biases, bias0, *caches_t, *extra_args)


def _to_col(row, n):
    eye = lax.broadcasted_iota(jnp.int32, (n, n), 0) == lax.broadcasted_iota(jnp.int32, (n, n), 1)
    return jnp.sum(jnp.where(eye, jnp.broadcast_to(row, (n, n)), 0.0), axis=1, keepdims=True)


def _sample_mix_kernel(bq_ref, bk_ref, bv_ref, cqkv_ref, sm_ref, cbuf_ref, sgla_ref, sgdn_ref,
                       wlr_ref, lrb_ref, cw_ref, dtb_ref, alog_ref,
                       ob_ref, oc_ref, nconv_ref, ngla_ref, ngdn_ref):
    sm = sm_ref[...]
    la =_log_sigmoid(_dot(sm, wlr_ref[...]) + lrb_ref[...]) * (1.0 / B_TAU)
    q = bq_ref[...] * (B_DK ** -0.5)
    k = bk_ref[...]
    v = bv_ref[...]
    for h in range(B_HEADS):
        lk = slice(B_DK * h, B_DK * (h + 1))
        lv = slice(B_DV * h, B_DV * (h + 1))
        s_old = sgla_ref[h]
        decay = jnp.exp(la[:, lk])
        ngla_ref[h] = _to_col(decay, B_DK) * s_old + _to_col(k[:, lk], B_DK) * v[:, lv]
        att = jnp.sum(q[:, lk] * k[:, lk], axis=-1, keepdims=True)
        ob_ref[:, lv] = (jnp.sum(_to_col(q[:, lk] * decay, B_DK) * s_old, axis=0, keepdims=True)
                         + att * v[:, lv])

    cat = jnp.concatenate([cbuf_ref[...], cqkv_ref[...]], axis=0)
    acc = None
    for j in range(CONV_W):
        term = cat[j:j + 1, :] * cw_ref[j:j + 1, :]
        acc = term if acc is None else acc + term
    y = _silu(acc)
    nconv_ref[...] = cat[1:CONV_W, :]
    hd = C_HEADS * C_DK
    for h in range(C_HEADS):
        ld = slice(C_DK * h, C_DK * (h + 1))
        qh = y[:, C_DK * h:C_DK * (h + 1)]
        qh = qh * lax.rsqrt(jnp.sum(qh * qh, axis=-1, keepdims=True) + EPS) * (C_DK ** -0.5)
        kh = y[:, hd + C_DK * h:hd + C_DK * (h + 1)]
        kh = kh * lax.rsqrt(jnp.sum(kh * kh, axis=-1, keepdims=True) + EPS)
        vh = y[:, 2 * hd + C_DV * h:2 * hd + C_DV * (h + 1)]
        beta = _sigmoid(sm[:, SM_CB + h:SM_CB + h + 1])
        g = -jnp.exp(alog_ref[:, ld]) * _softplus(sm[:, SM_CA + h:SM_CA + h + 1] + dtb_ref[:, ld])
        eg = jnp.exp(g)
        s = sgdn_ref[h]
        kcol = _to_col(kh, C_DK)
        u = beta * (vh - eg * jnp.sum(kcol * s, axis=0, keepdims=True))
        qs = jnp.sum(_to_col(qh, C_DK) * s, axis=0, keepdims=True)
        oc_ref[:, ld] = eg * qs + jnp.sum(qh * kh, axis=-1, keepdims=True) * u
        ngdn_ref[h] = eg * s + kcol * u


def _sample_mix(zs3, cbuf, sgla, sgdn, wlr_pad, lr_bias, conv_w, dtb_bc, alog_bc):
    bd = zs3.shape[0]
    hk, hv, hd = B_HEADS * B_DK, B_HEADS * B_DV, C_HEADS * C_DV
    const = lambda shape: pl.BlockSpec(shape, lambda i: (0,) * len(shape))
    zrow = lambda width, off: pl.BlockSpec((None, 1, width), lambda i: (i, 0, off // width))
    vec = lambda width: pl.BlockSpec((None, 1, width), lambda i: (i, 0, 0))
    st = lambda shape: pl.BlockSpec((None,) + shape, lambda i: (i,) + (0,) * len(shape))
    return pl.pallas_call(
        _sample_mix_kernel,
        grid=(bd,),
        in_specs=[zrow(hk, Z_BQ), zrow(hk, Z_BK), zrow(hv, Z_BV), zrow(C_CONV_CH, Z_CQKV), zrow(SMALL_W, Z_SMALL),
                  st((CONV_W - 1, C_CONV_CH)), st((B_HEADS, B_DK, B_DV)), st((C_HEADS, C_DK, C_DV))]
                 + [const((SMALL_W, hk)), const((1, hk)), const((CONV_W, C_CONV_CH)), const((1, hd)), const((1, hd))],
        out_specs=[vec(hv), vec(hd), st((CONV_W - 1, C_CONV_CH)),
                   st((B_HEADS, B_DK, B_DV)), st((C_HEADS, C_DK, C_DV))],
        out_shape=[jax.ShapeDtypeStruct((bd, 1, hv), F32),
                   jax.ShapeDtypeStruct((bd, 1, hd), F32), jax.ShapeDtypeStruct((bd, CONV_W - 1, C_CONV_CH), F32),
                   jax.ShapeDtypeStruct((bd, B_HEADS, B_DK, B_DV), F32),
                   jax.ShapeDtypeStruct((bd, C_HEADS, C_DK, C_DV), F32)],
        compiler_params=_cparams(("parallel",)),
        name="sample_mix",
    )(zs3, zs3, zs3, zs3, zs3, cbuf, sgla, sgdn, wlr_pad, lr_bias, conv_w, dtb_bc, alog_bc)


def _t5_bucket(dist):
    max_exact = N_BUCKETS // 2
    d = jnp.maximum(dist.astype(F32), 1.0)
    large = max_exact + (jnp.log(d / max_exact) / math.log(MAX_DIST / max_exact)
                         * (N_BUCKETS - max_exact)).astype(jnp.int32)
    large = jnp.minimum(large, N_BUCKETS - 1)
    return jnp.where(dist < max_exact, dist, large).astype(jnp.int32)


def _bias_tables(rel_bias):
    i = jnp.arange(SPAN)[:, None]
    c = jnp.arange(2 * SPAN)[None, :]
    dist = i + SPAN - c
    valid = (dist >= 0) & (dist <= SPAN)

    def lookup(bias_g, idx):
        hot = (idx[..., None] == jnp.arange(N_BUCKETS)).astype(F32)
        return jnp.einsum("...k,kh->...h", hot, bias_g.astype(F32), precision=HI)

    prompt, cached, new = [], [], []
    for g, (win, dil) in enumerate(A_GROUPS):
        bias_g = rel_bias[:, g * A_HEADS:(g + 1) * A_HEADS]
        tbl = lookup(bias_g, _t5_bucket(jnp.maximum(dist, 0) * dil)).transpose(2, 0, 1)
        prompt.append(jnp.where(valid[None], tbl, NEG))
        w = jnp.arange(win)
        tbl_s = lookup(bias_g, _t5_bucket(win - w)).T
        cached.append(jnp.where((w % dil == 0)[None], tbl_s, NEG))
        new.append(lookup(bias_g, _t5_bucket(jnp.zeros((1,), jnp.int32))).T)
    return jnp.stack(prompt), cached, jnp.stack(new)


def _prep_w_in(w):
    offs = np.concatenate([[0], np.cumsum(IN_SIZES)])
    seg = lambda i: w[:, offs[i]:offs[i + 1]]
    order = (0, 1, 2, 3, 4, 5, 6, 8, 9, 12, 7, 10, 11)
    used = sum(IN_SIZES)
    parts = [seg(i) for i in order] + [jnp.zeros((w.shape[0], NZ - used), w.dtype)]
    return jnp.concatenate(parts, axis=1).astype(BF16)


def _selector(offset):
    sel = np.zeros((SMALL_W, C_HEADS * C_DV), np.float32)
    for h in range(C_HEADS):
        sel[offset + h, C_DV * h:C_DV * (h + 1)] = 1.0
    return jnp.asarray(sel)


def kernel(x_prompt, x_sample, cache_win128_kv, cache_win512_kv, cache_win2048_kv, state_gla, state_gdn, state_conv, norm_mix, w_in, a_q_norm, a_k_norm, rel_bias, b_w_lr, b_lr_bias, b_out_norm, c_conv, c_a_log, c_dt_bias, c_out_norm, w_branch, w_out, norm_ffn, ffn_w_gate, ffn_w_up, ffn_w_down, moe_router, moe_w_gate, moe_w_up, moe_w_down):
    bp, t, d = x_prompt.shape
    bs = x_sample.shape[0]
    mp = bp * t
    caches_t = [jnp.transpose(c, (0, 1, 3, 4, 5, 2)).reshape(c.shape[:2] + (2, A_WIDTH, c.shape[2]))
                for c in (cache_win128_kv, cache_win512_kv, cache_win2048_kv)]
    bias_p, bias_c, bias_n = _bias_tables(rel_bias)
    selb, sela = _selector(SM_CB), _selector(SM_CA)
    rolled = windows = None

    xp = x_prompt.reshape(mp, d)
    xs = x_sample.reshape(bs, d)
    outs_p = {k: [] for k in ("w0", "w1", "w2", "gla", "gdn", "conv")}
    outs_s = {k: [] for k in ("w0", "w1", "w2", "gla", "gdn", "conv")}
    for l in range(DEPTH):
        w_in_l = _prep_w_in(w_in[l])
        wb = w_branch[l].astype(BF16)
        wo = w_out[l].astype(BF16)
        qg128 = jnp.tile(a_q_norm[l], 2).reshape(1, 2 * A_DH)
        kg128 = jnp.tile(a_k_norm[l], 2).reshape(1, 2 * A_DH)
        qg512 = jnp.tile(a_q_norm[l], A_HEADS).reshape(1, A_WIDTH)
        kg512 = jnp.tile(a_k_norm[l], A_HEADS).reshape(1, A_WIDTH)
        wlr_pad = jnp.zeros((SMALL_W, B_HEADS * B_DK), F32).at[:B_RANK].set(b_w_lr[l])
        lr_bias = b_lr_bias[l].reshape(1, -1)
        bon = jnp.tile(b_out_norm[l], B_HEADS).reshape(1, -1)
        con = jnp.tile(c_out_norm[l], C_HEADS).reshape(1, -1)
        dtb_bc = jnp.repeat(c_dt_bias[l], C_DV).reshape(1, -1)
        alog_bc = jnp.repeat(c_a_log[l], C_DV).reshape(1, -1)

        z = _norm_matmul(xp, norm_mix[l], w_in_l, tm=1024, tn=2304)
        z3 = z.reshape(bp, t, NZ)
        o_a, *windows = _attn_prompt(z3, qg128, kg128, bias_p, l, windows)
        o_b, gla_t = _gla_prompt(z3, wlr_pad, lr_bias, tb=512)
        dtb_sm = jnp.zeros((1, SMALL_W), F32).at[0, SM_CA:SM_CA + C_HEADS].set(c_dt_bias[l])
        alog_sm = jnp.zeros((1, SMALL_W), F32).at[0, SM_CA:SM_CA + C_HEADS].set(c_a_log[l])
        o_c, gdn_s = _gdn_prompt(z3, c_conv[l], selb, sela, dtb_sm, alog_sm, tb=512)
        xp = _merge(z, o_a.reshape(mp, -1), o_b.reshape(mp, -1), o_c.reshape(mp, -1), xp, bon, con, wb, wo, tm=256)
        gla = gla_t.reshape(bp, B_HEADS, B_DV, B_HEADS, B_DK)
        gla = jnp.stack([gla[:, h, :, h, :] for h in range(B_HEADS)], axis=1)
        outs_p["gla"].append(jnp.swapaxes(gla, 2, 3))
        outs_p["gdn"].append(gdn_s)
        outs_p["conv"].append(z3[:, t - (CONV_W - 1):, Z_CQKV:Z_CQKV + C_CONV_CH])

        zs = _norm_matmul(xs, norm_mix[l], w_in_l, tm=bs, tn=1280)
        zs3 = zs.reshape(bs, 1, NZ)
        oa_s, *rolled = _attn_sample(zs3, caches_t, rolled, l, qg512, kg512, bias_c, bias_n)
        ob_s, oc_s, nconv, ngla, ngdn = _sample_mix(
            zs3, state_conv[l], state_gla[l], state_gdn[l], wlr_pad, lr_bias, c_conv[l], dtb_bc, alog_bc)
        xs = _merge(zs, oa_s.reshape(bs, -1), ob_s.reshape(bs, -1), oc_s.reshape(bs, -1), xs, bon, con, wb, wo, tm=bs)
        outs_s["gla"].append(ngla)
        outs_s["gdn"].append(ngdn)
        outs_s["conv"].append(nconv)

        i = l // 2
        if l % 2 == 0:
            wg, wu, wd = ffn_w_gate[i].astype(BF16), ffn_w_up[i].astype(BF16), ffn_w_down[i].astype(BF16)
            xp = _ffn(xp, norm_ffn[l], wg, wu, wd, tm=512, tf=1408)
            xs = _ffn(xs, norm_ffn[l], wg, wu, wd, tm=bs, tf=1408)
        else:
            wg, wu, wd = moe_w_gate[i].astype(BF16), moe_w_up[i].astype(BF16), moe_w_down[i].astype(BF16)
            router_pad = jnp.zeros((d, 128), F32).at[:, :N_EXPERTS].set(moe_router[i])
            xp = _moe_routed(xp, norm_ffn[l], moe_router[i].T, wg, wu, wd, tm=1024)
            xs = _moe(xs, norm_ffn[l], router_pad, wg, wu, wd, tm=bs, tf=896)

    st = lambda name, d_: jnp.stack(d_[name])

    def window_out(w):
        w = w.reshape(w.shape[:3] + (A_HEADS, A_DH, w.shape[-1]))
        return jnp.transpose(w, (0, 1, 5, 2, 3, 4))

    return (xp.reshape(bp, t, d), xs.reshape(bs, 1, d),
            window_out(windows[0]), window_out(rolled[0]), window_out(windows[1]), window_out(rolled[1]),
            window_out(windows[2]), window_out(rolled[2]),
            st("gla", outs_p), st("gla", outs_s), st("gdn", outs_p), st("gdn", outs_s),
            st("conv", outs_p), st("conv", outs_s))
```

```python
import functools
import math

import jax
import jax.numpy as jnp
import numpy as np
from jax import lax
from jax.experimental import pallas as pl
from jax.experimental.pallas import tpu as pltpu

F32 = jnp.float32
BF16 = jnp.bfloat16
HI = lax.Precision.HIGHEST
NEG = -1e30

D_MODEL = 1024
DEPTH = 2
EPS = 1e-6
A_GROUPS = ((128, 1), (512, 4), (2048, 16))
N_GROUPS = 3
A_HEADS = 8
A_DH = 64
A_WIDTH = A_HEADS * A_DH
SPAN = 128
ATT_PAIR = 8
GLA_PAIR = 4
GDN_PAIR = 4
N_BUCKETS = 32
MAX_DIST = 2048
B_HEADS = 4
B_DK = 64
B_DV = 128
B_RANK = 16
B_TAU = 16.0
C_HEADS = 4
C_DK = 128
C_DV = 128
CONV_W = 4
C_CONV_CH = 2 * C_HEADS * C_DK + C_HEADS * C_DV
CHUNK = 64
SUB = 16
BRANCH_W = 512
D_FF = 2816
N_EXPERTS = 8
D_FF_EXPERT = 1792
IN_SIZES = (N_GROUPS * A_WIDTH, N_GROUPS * A_WIDTH, N_GROUPS * A_WIDTH,
            B_HEADS * B_DK, B_HEADS * B_DK, B_HEADS * B_DV, B_HEADS * B_DV, B_RANK,
            C_CONV_CH, C_HEADS * C_DV, C_HEADS, C_HEADS, 3 * D_MODEL)

Z_AQ, Z_AK, Z_AV = 0, 1536, 3072
Z_BQ, Z_BK, Z_BV, Z_BG = 4608, 4864, 5120, 5632
Z_CQKV, Z_CZ, Z_GATES, Z_SMALL = 6144, 7680, 8192, 11264
SMALL_W = 256
NZ = Z_SMALL + SMALL_W
SM_CB, SM_CA = B_RANK, B_RANK + C_HEADS

VMEM_LIMIT = 56 * 1024 * 1024


def _cparams(sem):
    return pltpu.CompilerParams(dimension_semantics=sem, vmem_limit_bytes=VMEM_LIMIT)


def _dot(a, b):
    return jnp.dot(a.astype(BF16), b.astype(BF16), preferred_element_type=F32)


def _dot_nt(a, b):
    return lax.dot_general(a.astype(BF16), b.astype(BF16), (((1,), (1,)), ((), ())), preferred_element_type=F32)


def _dot_tn(a, b):
    return lax.dot_general(a.astype(BF16), b.astype(BF16), (((0,), (0,)), ((), ())), preferred_element_type=F32)


def _split3(x):
    x1 = x.astype(BF16)
    r1 = x - x1.astype(F32)
    x2 = r1.astype(BF16)
    return x1, x2, (r1 - x2.astype(F32)).astype(BF16)


def _mask_dot(mask, x):
    m = mask.astype(BF16)
    x1, x2, x3 = _split3(x)
    return (jnp.dot(m, x1, preferred_element_type=F32) + jnp.dot(m, x2, preferred_element_type=F32)
            + jnp.dot(m, x3, preferred_element_type=F32))


def _dot_mask(x, mask):
    m = mask.astype(BF16)
    x1, x2, x3 = _split3(x)
    return (jnp.dot(x1, m, preferred_element_type=F32) + jnp.dot(x2, m, preferred_element_type=F32)
            + jnp.dot(x3, m, preferred_element_type=F32))


def _r16(x):
    return x.astype(BF16).astype(F32)


def _sigmoid(x):
    return jax.nn.sigmoid(x)


def _silu(x):
    return x * jax.nn.sigmoid(x)


def _softplus(x):
    return jnp.maximum(x, 0.0) + jnp.log1p(jnp.exp(-jnp.abs(x)))


def _log_sigmoid(x):
    return jnp.minimum(x, 0.0) - jnp.log1p(jnp.exp(-jnp.abs(x)))


def _rms(x, gain):
    return x * lax.rsqrt(jnp.mean(x * x, axis=-1, keepdims=True) + EPS) * gain


def _norm_matmul_kernel(x_ref, g_ref, w_ref, o_ref, h_ref):
    @pl.when(pl.program_id(1) == 0)
    def _():
        h_ref[...] = _rms(x_ref[...], g_ref[...]).astype(BF16)

    o_ref[...] = jnp.dot(h_ref[...], w_ref[...], preferred_element_type=F32)


def _norm_matmul(x, gain, w, tm, tn):
    m, d = x.shape
    n = w.shape[1]
    return pl.pallas_call(
        _norm_matmul_kernel,
        grid=(m // tm, n // tn),
        in_specs=[pl.BlockSpec((tm, d), lambda i, j: (i, 0)),
                  pl.BlockSpec((1, d), lambda i, j: (0, 0)),
                  pl.BlockSpec((d, tn), lambda i, j: (0, j))],
        out_specs=pl.BlockSpec((tm, tn), lambda i, j: (i, j)),
        out_shape=jax.ShapeDtypeStruct((m, n), F32),
        scratch_shapes=[pltpu.VMEM((tm, d), BF16)],
        compiler_params=_cparams(("parallel", "arbitrary")),
        name="norm_matmul",
    )(x, gain.reshape(1, d), w)


def _pair_rms(x, gain, head0):
    sq = x * x
    s0 = jnp.sum(jnp.where(head0, sq, 0.0), axis=-1, keepdims=True)
    s1 = jnp.sum(sq, axis=-1, keepdims=True) - s0
    return x * lax.rsqrt(jnp.where(head0, s0, s1) * (1.0 / A_DH) + EPS) * gain


def _attn_prompt_kernel(q1, k1, v1, q2, k2, v2, q3, k3, v3, qg_ref, kg_ref, bias_ref,
                        o_ref, w1, w2, w3, qt, kt, qs, ks, vs, og, lg, to, tl, *, slots):
    t_len = o_ref.shape[0]
    head0 = lax.broadcasted_iota(jnp.int32, (1, 2 * A_DH), 1) < A_DH
    first_half = lax.broadcasted_iota(jnp.int32, (1, 2 * SPAN), 1) < SPAN
    groups = ((q1, k1, v1, w1), (q2, k2, v2, w2), (q3, k3, v3, w3))
    for g, (qr, kr, vr, wr) in enumerate(groups):
        dil = A_GROUPS[g][1]
        length = t_len // dil
        nb = length // SPAN
        qt[...] = _pair_rms(qr[...], qg_ref[...], head0) * (A_DH ** -0.5)
        kt[...] = _pair_rms(kr[...], kg_ref[...], head0)
        wlen = wr.shape[-1]
        k_win = kt[t_len - wlen:, :].T
        v_win = vr[t_len - wlen:, :].T
        for s in range(slots):
            wr[s, 0] = k_win
            wr[s, 1] = v_win
        knr = kt
        for r in range(dil):
            rows = pl.ds(r, length, stride=dil) if dil > 1 else pl.ds(0, length)
            dst = pl.ds(r * length, length)
            qs[dst, :] = qt[rows, :].astype(BF16)
            ks[dst, :] = knr[rows, :].astype(BF16)
            vs[dst, :] = vr[rows, :].astype(BF16)

        if nb > 1:
            bias2 = jnp.concatenate([bias_ref[g, 0], bias_ref[g, 1]], axis=0)
        else:
            bias2 = jnp.concatenate([bias_ref[g, 0, :, SPAN:], bias_ref[g, 1, :, SPAN:]], axis=0)

        def block(jj, carry, nb=nb, bias2=bias2):
            row0s, k2s, v2s, logits = [], [], [], []
            for i in range(ATT_PAIR):
                j = jj * ATT_PAIR + i
                row0 = pl.multiple_of(j * SPAN, SPAN)
                qb = qs[pl.ds(row0, SPAN), :]
                qst = jnp.concatenate([jnp.where(head0, qb, jnp.zeros_like(qb)),
                                       jnp.where(head0, jnp.zeros_like(qb), qb)], axis=0)
                if nb > 1:
                    prow = pl.multiple_of(jnp.maximum(row0 - SPAN, 0), SPAN)
                    k2 = jnp.concatenate([ks[pl.ds(prow, SPAN), :], ks[pl.ds(row0, SPAN), :]], axis=0)
                    v2 = jnp.concatenate([vs[pl.ds(prow, SPAN), :], vs[pl.ds(row0, SPAN), :]], axis=0)
                else:
                    k2 = ks[pl.ds(row0, SPAN), :]
                    v2 = vs[pl.ds(row0, SPAN), :]
                lg_ = lax.dot_general(qst, k2, (((1,), (1,)), ((), ())), preferred_element_type=F32) + bias2
                if nb > 1:
                    lg_ = lg_ + jnp.where(first_half, jnp.where(j % nb == 0, NEG, 0.0), 0.0)
                row0s.append(row0)
                k2s.append(k2)
                v2s.append(v2)
                logits.append(lg_)
            lg_all = jnp.concatenate(logits, axis=0)
            m = jnp.max(lg_all, axis=-1, keepdims=True)
            p = jnp.exp(lg_all - m)
            s = jnp.sum(p, axis=-1, keepdims=True)
            p16 = p.astype(BF16)
            lse = m + jnp.log(s)
            for i in range(ATT_PAIR):
                lo = 2 * SPAN * i
                acc = jnp.dot(p16[lo:lo + 2 * SPAN], v2s[i], preferred_element_type=F32) / s[lo:lo + 2 * SPAN]
                to[pl.ds(row0s[i], SPAN), :] = jnp.where(head0, acc[:SPAN], acc[SPAN:])
                tl[pl.ds(row0s[i], SPAN), :] = jnp.where(head0, jnp.broadcast_to(lse[lo:lo + SPAN], (SPAN, 2 * A_DH)),
                                                         jnp.broadcast_to(lse[lo + SPAN:lo + 2 * SPAN], (SPAN, 2 * A_DH)))
            return carry

        lax.fori_loop(0, t_len // (SPAN * ATT_PAIR), block, 0)
        for r in range(dil):
            rows = pl.ds(r, length, stride=dil) if dil > 1 else pl.ds(0, length)
            src = pl.ds(r * length, length)
            og[g, rows, :] = to[src, :]
            lg[g, rows, :] = tl[src, :]
    l_1, l_2, l_3 = lg[0], lg[1], lg[2]
    mx = jnp.maximum(jnp.maximum(l_1, l_2), l_3)
    e_1, e_2, e_3 = jnp.exp(l_1 - mx), jnp.exp(l_2 - mx), jnp.exp(l_3 - mx)
    o_ref[...] = (e_1 * og[0] + e_2 * og[1] + e_3 * og[2]) / (e_1 + e_2 + e_3)


def _attn_prompt(z3, qgain, kgain, bias, layer, windows):
    b, t, _ = z3.shape
    hp_blocks = A_WIDTH // 128

    def col(base, g):
        return lambda i, hp: (i, 0, base // 128 + g * hp_blocks + hp)

    in_specs = []
    for g in range(N_GROUPS):
        for base in (Z_AQ, Z_AK, Z_AV):
            in_specs.append(pl.BlockSpec((None, t, 128), col(base, g)))
    in_specs += [pl.BlockSpec((1, 128), lambda i, hp: (0, 0)),
                 pl.BlockSpec((1, 128), lambda i, hp: (0, 0)),
                 pl.BlockSpec((N_GROUPS, 2, SPAN, 2 * SPAN), lambda i, hp: (0, hp, 0, 0))]
    args = [z3] * 9 + [qgain, kgain, bias]
    slots = DEPTH if windows is None else 1
    win_specs, win_shapes = [], []
    for win, _ in A_GROUPS:
        wp = min(win, t)
        win_specs.append(pl.BlockSpec((slots, None, 2, 128, wp), lambda i, hp: (layer // slots, i, 0, hp, 0)))
        win_shapes.append(jax.ShapeDtypeStruct((DEPTH, b, 2, A_WIDTH, wp), F32))
    aliases = {}
    if windows is not None:
        aliases = {len(args) + g: 1 + g for g in range(N_GROUPS)}
        in_specs += [pl.BlockSpec(memory_space=pl.ANY)] * N_GROUPS
        args += list(windows)
    kern = functools.partial(_attn_prompt_kernel, slots=slots)
    if windows is not None:
        kern = functools.partial(_drop_refs, kern, len(args) - N_GROUPS, N_GROUPS)
    return pl.pallas_call(
        kern,
        grid=(b, hp_blocks),
        in_specs=in_specs,
        out_specs=[pl.BlockSpec((None, t, 128), lambda i, hp: (i, 0, hp))] + win_specs,
        out_shape=[jax.ShapeDtypeStruct((b, t, A_WIDTH), F32)] + win_shapes,
        scratch_shapes=[pltpu.VMEM((t, 128), F32), pltpu.VMEM((t, 128), F32),
                        pltpu.VMEM((t, 128), BF16), pltpu.VMEM((t, 128), BF16), pltpu.VMEM((t, 128), BF16),
                        pltpu.VMEM((N_GROUPS, t, 128), F32), pltpu.VMEM((N_GROUPS, t, 128), F32),
                        pltpu.VMEM((t, 128), F32), pltpu.VMEM((t, 128), F32)],
        input_output_aliases=aliases,
        compiler_params=_cparams(("parallel", "parallel")),
        name="attn_prompt",
    )(*args)


def _drop_refs(kern, start, count, *refs):
    return kern(*refs[:start], *refs[start + count:])


def _gla_prompt_kernel(q_ref, k_ref, v_ref, sm_ref, wlr_ref, lrb_ref, o_ref, st_ref, s_scr, la_scr):
    tb = q_ref.shape[0]
    hk = B_HEADS * B_DK
    hv = B_HEADS * B_DV

    @pl.when(pl.program_id(1) == 0)
    def _():
        s_scr[...] = jnp.zeros_like(s_scr)

    la_scr[...] = _log_sigmoid(_dot(sm_ref[...], wlr_ref[...]) + lrb_ref[...]) * (1.0 / B_TAU)

    rr = lax.broadcasted_iota(jnp.int32, (CHUNK, CHUNK), 0)
    cc = lax.broadcasted_iota(jnp.int32, (CHUNK, CHUNK), 1)
    causal = rr >= cc
    ltri = causal.astype(F32)
    lane_head = lax.broadcasted_iota(jnp.int32, (1, hk), 1) // B_DK
    row = lax.broadcasted_iota(jnp.int32, (CHUNK, 1), 0)
    same_head = (lax.broadcasted_iota(jnp.int32, (hv, hk), 0) // B_DV
                 == lax.broadcasted_iota(jnp.int32, (hv, hk), 1) // B_DK)

    def intra(r0):
        b = _mask_dot(ltri, la_scr[pl.ds(r0, CHUNK), :])
        q = q_ref[pl.ds(r0, CHUNK), :] * (B_DK ** -0.5)
        k = k_ref[pl.ds(r0, CHUNK), :]
        vb = v_ref[pl.ds(r0, CHUNK), :].astype(BF16)
        b_last = b[CHUNK - 1:CHUNK, :]
        atts = []
        for i in range(CHUNK // SUB):
            b_ref0 = b[SUB * i:SUB * i + 1, :]
            qi = q[SUB * i:SUB * (i + 1), :] * jnp.exp(b[SUB * i:SUB * (i + 1), :] - b_ref0)
            qst = jnp.concatenate([jnp.where(lane_head == h, qi, 0.0) for h in range(B_HEADS)], axis=0)
            kt = k * jnp.exp(jnp.where(row < SUB * (i + 1), b_ref0 - b, 0.0))
            atts.append(_dot_nt(qst, kt))
        o_parts = []
        for h in range(B_HEADS):
            att = jnp.concatenate([a[SUB * h:SUB * (h + 1), :] for a in atts], axis=0)
            att = jnp.where(causal, att, 0.0)
            o_parts.append(_dot(att, vb[:, B_DV * h:B_DV * (h + 1)]))
        upd = jnp.where(same_head, _dot_tn(vb, k * jnp.exp(b_last - b)), 0.0)
        return (q * jnp.exp(b)).astype(BF16), jnp.concatenate(o_parts, axis=1), jnp.exp(b_last), upd

    def chunks(cc, carry):
        r0s = [pl.multiple_of((cc * GLA_PAIR + i) * CHUNK, CHUNK) for i in range(GLA_PAIR)]
        parts = [intra(r0) for r0 in r0s]
        for r0, (qe, o_intra, decay, upd) in zip(r0s, parts):
            s = s_scr[...]
            o_ref[pl.ds(r0, CHUNK), :] = _dot_nt(qe, s) + o_intra
            s_scr[...] = s * decay + upd
        return carry

    lax.fori_loop(0, tb // (CHUNK * GLA_PAIR), chunks, 0)

    @pl.when(pl.program_id(1) == pl.num_programs(1) - 1)
    def _():
        st_ref[...] = s_scr[...]


def _gla_prompt(z3, wlr_pad, lr_bias, tb):
    b, t, _ = z3.shape
    hk, hv = B_HEADS * B_DK, B_HEADS * B_DV
    return pl.pallas_call(
        _gla_prompt_kernel,
        grid=(b, t // tb),
        in_specs=[pl.BlockSpec((None, tb, hk), lambda i, j: (i, j, Z_BQ // hk)),
                  pl.BlockSpec((None, tb, hk), lambda i, j: (i, j, Z_BK // hk)),
                  pl.BlockSpec((None, tb, hv), lambda i, j: (i, j, Z_BV // hv)),
                  pl.BlockSpec((None, tb, SMALL_W), lambda i, j: (i, j, Z_SMALL // SMALL_W)),
                  pl.BlockSpec((SMALL_W, hk), lambda i, j: (0, 0)),
                  pl.BlockSpec((1, hk), lambda i, j: (0, 0))],
        out_specs=[pl.BlockSpec((None, tb, hv), lambda i, j: (i, j, 0)),
                   pl.BlockSpec((None, hv, hk), lambda i, j: (i, 0, 0))],
        out_shape=[jax.ShapeDtypeStruct((b, t, hv), F32), jax.ShapeDtypeStruct((b, hv, hk), F32)],
        scratch_shapes=[pltpu.VMEM((hv, hk), F32), pltpu.VMEM((tb, hk), F32)],
        compiler_params=_cparams(("parallel", "arbitrary")),
        name="gla_prompt",
    )(z3, z3, z3, z3, wlr_pad, lr_bias)


def _gdn_prompt_kernel(x_ref, sm_ref, cw_ref, selb_ref, sela_ref, dtb_ref, alog_ref, o_ref, s_ref,
                       xpad, cv, gsc, bsc, s_scr):
    tb = x_ref.shape[0]
    hd = C_HEADS * C_DK
    pad = 8

    @pl.when(pl.program_id(1) == 0)
    def _():
        s_scr[...] = jnp.zeros_like(s_scr)
        xpad[0:pad, :] = jnp.zeros((pad, C_CONV_CH), F32)

    xpad[pad:pad + tb, :] = x_ref[...]
    rb = 128
    for i in range(tb // rb):
        acc = None
        for j in range(CONV_W):
            lo = pad - (CONV_W - 1) + j + i * rb
            term = xpad[lo:lo + rb, :] * cw_ref[j:j + 1, :]
            acc = term if acc is None else acc + term
        y = _silu(acc)
        rows = slice(i * rb, (i + 1) * rb)
        for h in range(C_HEADS):
            qh = y[:, C_DK * h:C_DK * (h + 1)]
            cv[rows, C_DK * h:C_DK * (h + 1)] = (
                qh * lax.rsqrt(jnp.sum(qh * qh, axis=-1, keepdims=True) + EPS) * (C_DK ** -0.5))
            kh = y[:, hd + C_DK * h:hd + C_DK * (h + 1)]
            cv[rows, hd + C_DK * h:hd + C_DK * (h + 1)] = (
                kh * lax.rsqrt(jnp.sum(kh * kh, axis=-1, keepdims=True) + EPS))
        cv[rows, 2 * hd:] = y[:, 2 * hd:]
    xpad[pad - (CONV_W - 1):pad, :] = xpad[pad + tb - (CONV_W - 1):pad + tb, :]

    sm = sm_ref[...]
    gsc[...] = _dot_mask(-jnp.exp(alog_ref[...]) * _softplus(sm + dtb_ref[...]), sela_ref[...])
    bsc[...] = _dot_mask(_sigmoid(sm), selb_ref[...])

    n = C_HEADS * CHUNK
    rr = lax.broadcasted_iota(jnp.int32, (CHUNK, CHUNK), 0)
    cc = lax.broadcasted_iota(jnp.int32, (CHUNK, CHUNK), 1)
    ltri = (rr >= cc).astype(F32)
    ri = lax.broadcasted_iota(jnp.int32, (n, n), 0)
    ci = lax.broadcasted_iota(jnp.int32, (n, n), 1)
    same = (ri // CHUNK) == (ci // CHUNK)
    incl = jnp.logical_and(same, ci <= ri)
    strict = jnp.logical_and(same, ci < ri)
    eye = (ri == ci).astype(F32)

    def stack(x):
        return jnp.concatenate([x[:, C_DK * h:C_DK * (h + 1)] for h in range(C_HEADS)], axis=0)

    def chunks(cc, carry):
        pair = range(GDN_PAIR)
        r0s = [pl.multiple_of((cc * GDN_PAIR + i) * CHUNK, CHUNK) for i in pair]
        bst = [stack(_mask_dot(ltri, gsc[pl.ds(r0, CHUNK), :])) for r0 in r0s]
        beta = [stack(bsc[pl.ds(r0, CHUNK), :]) for r0 in r0s]
        qst = [stack(cv[pl.ds(r0, CHUNK), 0:hd]) for r0 in r0s]
        kst = [stack(cv[pl.ds(r0, CHUNK), hd:2 * hd]) for r0 in r0s]
        vst = [stack(cv[pl.ds(r0, CHUNK), 2 * hd:3 * hd]) for r0 in r0s]
        kb = [k.astype(BF16) for k in kst]
        nt = (((1,), (1,)), ((), ()))
        kk = [lax.dot_general(kb[i], kb[i], nt, preferred_element_type=F32) for i in pair]
        qk = [lax.dot_general(qst[i].astype(BF16), kb[i], nt, preferred_element_type=F32) for i in pair]
        dec, a, aqk = [], [], []
        for i in pair:
            bst_t = bst[i].T
            d_ = jnp.exp(jnp.where(incl, jnp.concatenate([bst[i], bst[i]], axis=1)
                                   - jnp.concatenate([bst_t, bst_t], axis=0), NEG))
            dec.append(d_)
            a.append(jnp.where(strict, jnp.concatenate([beta[i], beta[i]], axis=1) * d_ * kk[i], 0.0))
            aqk.append(d_ * qk[i])
        inv = [eye - a[i] for i in pair]
        pw = a
        for _ in range(int(math.log2(CHUNK)) - 1):
            pw = [_dot(pw[i], pw[i]) for i in pair]
            inv = [inv[i] + _dot(inv[i], pw[i]) for i in pair]
        eb = [jnp.exp(bst[i]) for i in pair]
        wu = [_dot(inv[i], jnp.concatenate([beta[i] * eb[i] * kst[i], beta[i] * vst[i]], axis=1)) for i in pair]
        for i in pair:
            w, uv = wu[i][:, :C_DK], wu[i][:, C_DK:]
            us, qss = [], []
            for h in range(C_HEADS):
                sl = slice(CHUNK * h, CHUNK * (h + 1))
                ws = _dot(jnp.concatenate([w[sl], qst[i][sl]], axis=0), s_scr[h])
                us.append(uv[sl] - ws[:CHUNK])
                qss.append(ws[CHUNK:])
            o = eb[i] * jnp.concatenate(qss, axis=0) + _dot(aqk[i], jnp.concatenate(us, axis=0))
            for h in range(C_HEADS):
                sl = slice(CHUNK * h, CHUNK * (h + 1))
                o_ref[pl.ds(r0s[i], CHUNK), C_DV * h:C_DV * (h + 1)] = o[sl]
                b_last = bst[i][CHUNK * (h + 1) - 1:CHUNK * (h + 1), :]
                s_scr[h] = (jnp.exp(b_last) * s_scr[h]
                            + _dot_tn(kst[i][sl] * jnp.exp(b_last - bst[i][sl]), us[h]))
        return carry

    lax.fori_loop(0, tb // (CHUNK * GDN_PAIR), chunks, 0)

    @pl.when(pl.program_id(1) == pl.num_programs(1) - 1)
    def _():
        s_ref[...] = s_scr[...]


def _gdn_prompt(z3, conv_w, selb, sela, dtb_bc, alog_bc, tb):
    b, t, _ = z3.shape
    hd = C_HEADS * C_DV
    const = lambda shape: pl.BlockSpec(shape, lambda i, j: (0,) * len(shape))
    return pl.pallas_call(
        _gdn_prompt_kernel,
        grid=(b, t // tb),
        in_specs=[pl.BlockSpec((None, tb, C_CONV_CH), lambda i, j: (i, j, Z_CQKV // C_CONV_CH)),
                  pl.BlockSpec((None, tb, SMALL_W), lambda i, j: (i, j, Z_SMALL // SMALL_W)),
                  const((CONV_W, C_CONV_CH)), const((SMALL_W, hd)), const((SMALL_W, hd)),
                  const((1, SMALL_W)), const((1, SMALL_W))],
        out_specs=[pl.BlockSpec((None, tb, hd), lambda i, j: (i, j, 0)),
                   pl.BlockSpec((None, C_HEADS, C_DK, C_DV), lambda i, j: (i, 0, 0, 0))],
        out_shape=[jax.ShapeDtypeStruct((b, t, hd), F32),
                   jax.ShapeDtypeStruct((b, C_HEADS, C_DK, C_DV), F32)],
        scratch_shapes=[pltpu.VMEM((tb + 8, C_CONV_CH), F32), pltpu.VMEM((tb, C_CONV_CH), F32),
                        pltpu.VMEM((tb, hd), F32), pltpu.VMEM((tb, hd), F32),
                        pltpu.VMEM((C_HEADS, C_DK, C_DV), F32)],
        compiler_params=_cparams(("parallel", "arbitrary")),
        name="gdn_prompt",
    )(z3, z3, conv_w, selb, sela, dtb_bc, alog_bc)


def _head_norm(o, gain, width):
    parts = []
    for h in range(o.shape[1] // width):
        oh = o[:, width * h:width * (h + 1)]
        parts.append(oh * lax.rsqrt(jnp.mean(oh * oh, axis=-1, keepdims=True) + EPS))
    return jnp.concatenate(parts, axis=1) * gain


def _merge_kernel(oa_ref, ob_ref, bg_ref, oc_ref, cz_ref, ga_ref, gb_ref, gc_ref, x_ref,
                  bon_ref, con_ref, wb_ref, wo_ref, y_ref):
    ob = _head_norm(ob_ref[...], bon_ref[...], B_DV) * _silu(bg_ref[...])
    oc = _head_norm(oc_ref[...], con_ref[...], C_DV) * _silu(cz_ref[...])
    merged = (_sigmoid(ga_ref[...]) * _dot(oa_ref[...], wb_ref[0])
              + _sigmoid(gb_ref[...]) * _dot(ob, wb_ref[1])
              + _sigmoid(gc_ref[...]) * _dot(oc, wb_ref[2]))
    y_ref[...] = x_ref[...] + _dot(merged, wo_ref[...])


def _merge(z, oa, ob, oc, x, bon, con, wb, wo, tm):
    m = x.shape[0]
    w = BRANCH_W
    row = lambda width, blk: pl.BlockSpec((tm, width), lambda i: (i, blk))
    const = lambda shape: pl.BlockSpec(shape, lambda i: (0,) * len(shape))
    return pl.pallas_call(
        _merge_kernel,
        grid=(m // tm,),
        in_specs=[row(w, 0), row(w, 0), row(w, Z_BG // w), row(w, 0), row(w, Z_CZ // w),
                  row(D_MODEL, Z_GATES // D_MODEL), row(D_MODEL, Z_GATES // D_MODEL + 1),
                  row(D_MODEL, Z_GATES // D_MODEL + 2), row(D_MODEL, 0),
                  const((1, w)), const((1, w)), const((3, w, D_MODEL)), const((D_MODEL, D_MODEL))],
        out_specs=row(D_MODEL, 0),
        out_shape=jax.ShapeDtypeStruct((m, D_MODEL), F32),
        compiler_params=_cparams(("parallel",)),
        name="merge",
    )(oa, ob, z, oc, z, z, z, z, x, bon, con, wb, wo)


def _ffn_kernel(x_ref, g_ref, wg_ref, wu_ref, wd_ref, y_ref, h_ref, acc_ref):
    f = pl.program_id(1)

    @pl.when(f == 0)
    def _():
        h_ref[...] = _rms(x_ref[...], g_ref[...]).astype(BF16)
        acc_ref[...] = jnp.zeros_like(acc_ref)

    h = h_ref[...]
    a = _silu(jnp.dot(h, wg_ref[...], preferred_element_type=F32)) * jnp.dot(h, wu_ref[...],
                                                                              preferred_element_type=F32)
    acc_ref[...] += jnp.dot(a.astype(BF16), wd_ref[...], preferred_element_type=F32)

    @pl.when(f == pl.num_programs(1) - 1)
    def _():
        y_ref[...] = x_ref[...] + acc_ref[...]


def _ffn(x, gain, wg, wu, wd, tm, tf):
    m, d = x.shape
    ff = wg.shape[1]
    return pl.pallas_call(
        _ffn_kernel,
        grid=(m // tm, ff // tf),
        in_specs=[pl.BlockSpec((tm, d), lambda i, f: (i, 0)),
                  pl.BlockSpec((1, d), lambda i, f: (0, 0)),
                  pl.BlockSpec((d, tf), lambda i, f: (0, f)),
                  pl.BlockSpec((d, tf), lambda i, f: (0, f)),
                  pl.BlockSpec((tf, d), lambda i, f: (f, 0))],
        out_specs=pl.BlockSpec((tm, d), lambda i, f: (i, 0)),
        out_shape=jax.ShapeDtypeStruct((m, d), F32),
        scratch_shapes=[pltpu.VMEM((tm, d), BF16), pltpu.VMEM((tm, d), F32)],
        compiler_params=_cparams(("parallel", "arbitrary")),
        name="ffn",
    )(x, gain.reshape(1, d), wg, wu, wd)


def _moe_kernel(x_ref, g_ref, rt_ref, wg_ref, wu_ref, wd_ref, y_ref, h_ref, acc_ref, gate_ref):
    e = pl.program_id(1)
    f = pl.program_id(2)
    lane = lax.broadcasted_iota(jnp.int32, (1, 128), 1).astype(F32)

    @pl.when(jnp.logical_and(e == 0, f == 0))
    def _():
        hf = _rms(x_ref[...], g_ref[...])
        h_ref[...] = hf.astype(BF16)
        acc_ref[...] = jnp.zeros_like(acc_ref)
        valid = lane < N_EXPERTS
        logits = jnp.where(valid, _dot(hf, rt_ref[...]), NEG)
        ex = jnp.exp(logits - jnp.max(logits, axis=-1, keepdims=True))
        probs = ex / jnp.sum(ex, axis=-1, keepdims=True)
        m1 = jnp.max(probs, axis=-1, keepdims=True)
        i1 = jnp.min(jnp.where(jnp.logical_and(probs == m1, valid), lane, 128.0), axis=-1, keepdims=True)
        hot1 = lane == i1
        rest = jnp.where(jnp.logical_or(hot1, jnp.logical_not(valid)), -1.0, probs)
        m2 = jnp.max(rest, axis=-1, keepdims=True)
        i2 = jnp.min(jnp.where(rest == m2, lane, 128.0), axis=-1, keepdims=True)
        hot2 = lane == i2
        den = m1 + m2
        gate_ref[...] = jnp.where(hot1, m1 / den, 0.0) + jnp.where(hot2, m2 / den, 0.0)

    h = h_ref[...]
    ge = jnp.sum(jnp.where(lane == e.astype(F32), gate_ref[...], 0.0), axis=-1, keepdims=True)
    a = (_silu(jnp.dot(h, wg_ref[...], preferred_element_type=F32))
         * jnp.dot(h, wu_ref[...], preferred_element_type=F32) * ge)
    acc_ref[...] += jnp.dot(a.astype(BF16), wd_ref[...], preferred_element_type=F32)

    @pl.when(jnp.logical_and(e == pl.num_programs(1) - 1, f == pl.num_programs(2) - 1))
    def _():
        y_ref[...] = x_ref[...] + acc_ref[...]


def _moe(x, gain, router_pad, wg, wu, wd, tm, tf):
    m, d = x.shape
    ne, _, ff = wg.shape
    return pl.pallas_call(
        _moe_kernel,
        grid=(m // tm, ne, ff // tf),
        in_specs=[pl.BlockSpec((tm, d), lambda i, e, f: (i, 0)),
                  pl.BlockSpec((1, d), lambda i, e, f: (0, 0)),
                  pl.BlockSpec((d, 128), lambda i, e, f: (0, 0)),
                  pl.BlockSpec((None, d, tf), lambda i, e, f: (e, 0, f)),
                  pl.BlockSpec((None, d, tf), lambda i, e, f: (e, 0, f)),
                  pl.BlockSpec((None, tf, d), lambda i, e, f: (e, f, 0))],
        out_specs=pl.BlockSpec((tm, d), lambda i, e, f: (i, 0)),
        out_shape=jax.ShapeDtypeStruct((m, d), F32),
        scratch_shapes=[pltpu.VMEM((tm, d), BF16), pltpu.VMEM((tm, d), F32), pltpu.VMEM((tm, 128), F32)],
        compiler_params=_cparams(("parallel", "arbitrary", "arbitrary")),
        name="moe",
    )(x, gain.reshape(1, d), router_pad, wg, wu, wd)


MOE_ROWS = 288


def _moe_routed_kernel(x_ref, g_ref, rt_ref, wg_ref, wu_ref, wd_ref, y_ref,
                       h_ref, gate_scr, mask_scr, rank_scr, tri_scr):
    e = pl.program_id(1)
    tm = x_ref.shape[0]
    sub = lax.broadcasted_iota(jnp.int32, (N_EXPERTS, 1), 0).astype(F32)

    @pl.when(e == 0)
    def _():
        x = x_ref[...]
        hf = _rms(x, g_ref[...])
        h_ref[...] = hf.astype(BF16)
        y_ref[...] = x
        logits = _dot_nt(rt_ref[...], hf)
        ex = jnp.exp(logits - jnp.max(logits, axis=0, keepdims=True))
        probs = ex / jnp.sum(ex, axis=0, keepdims=True)
        m1 = jnp.max(probs, axis=0, keepdims=True)
        hot1 = sub == jnp.min(jnp.where(probs == m1, sub, float(N_EXPERTS)), axis=0, keepdims=True)
        rest = jnp.where(hot1, -1.0, probs)
        m2 = jnp.max(rest, axis=0, keepdims=True)
        hot2 = sub == jnp.min(jnp.where(rest == m2, sub, float(N_EXPERTS)), axis=0, keepdims=True)
        den = m1 + m2
        gate_scr[...] = jnp.where(hot1, m1 / den, 0.0) + jnp.where(hot2, m2 / den, 0.0)
        mask = jnp.where(jnp.logical_or(hot1, hot2), 1.0, 0.0)
        mask_scr[...] = mask
        tri_scr[...] = jnp.where(lax.broadcasted_iota(jnp.int32, (tm, tm), 0)
                                 <= lax.broadcasted_iota(jnp.int32, (tm, tm), 1), 1.0, 0.0).astype(BF16)
        rank_scr[...] = jnp.dot(mask.astype(BF16), tri_scr[...], preferred_element_type=F32) - 1.0

    mine = sub == e.astype(F32)
    row = lambda ref: jnp.sum(jnp.where(mine, ref[...], 0.0), axis=0, keepdims=True)
    gate_row, mask_row, rank_row = row(gate_scr), row(mask_scr), row(rank_scr)
    count = jnp.sum(mask_row).astype(jnp.int32)
    slot = lax.broadcasted_iota(jnp.int32, (MOE_ROWS, 1), 0).astype(F32)

    def chunk(j, carry):
        base = (j * MOE_ROWS).astype(F32)
        pick = jnp.where(jnp.logical_and(rank_row == slot + base, mask_row > 0.0), 1.0, 0.0)
        pick16 = pick.astype(BF16)
        xg = jnp.dot(pick16, h_ref[...], preferred_element_type=F32).astype(BF16)
        gate_col = jnp.sum(pick * gate_row, axis=1, keepdims=True)
        a = (_silu(jnp.dot(xg, wg_ref[...], preferred_element_type=F32))
             * jnp.dot(xg, wu_ref[...], preferred_element_type=F32) * gate_col)
        out = jnp.dot(a.astype(BF16), wd_ref[...], preferred_element_type=F32)
        y_ref[...] += _dot_tn(pick16, out)
        return carry

    lax.fori_loop(0, (count + MOE_ROWS - 1) // MOE_ROWS, chunk, 0)


def _moe_routed(x, gain, router_t, wg, wu, wd, tm):
    m, d = x.shape
    ne, _, ff = wg.shape
    return pl.pallas_call(
        _moe_routed_kernel,
        grid=(m // tm, ne),
        in_specs=[pl.BlockSpec((tm, d), lambda i, e: (i, 0)),
                  pl.BlockSpec((1, d), lambda i, e: (0, 0)),
                  pl.BlockSpec((ne, d), lambda i, e: (0, 0)),
                  pl.BlockSpec((None, d, ff), lambda i, e: (e, 0, 0)),
                  pl.BlockSpec((None, d, ff), lambda i, e: (e, 0, 0)),
                  pl.BlockSpec((None, ff, d), lambda i, e: (e, 0, 0))],
        out_specs=pl.BlockSpec((tm, d), lambda i, e: (i, 0)),
        out_shape=jax.ShapeDtypeStruct((m, d), F32),
        scratch_shapes=[pltpu.VMEM((tm, d), BF16), pltpu.VMEM((ne, tm), F32), pltpu.VMEM((ne, tm), F32),
                        pltpu.VMEM((ne, tm), F32), pltpu.VMEM((tm, tm), BF16)],
        compiler_params=_cparams(("parallel", "arbitrary")),
        name="moe_routed",
    )(x, gain.reshape(1, d), router_t, wg, wu, wd)


def _attn_sample_kernel(*refs, layer, full_roll):
    zq_ref, qg_ref, kg_ref, b0_ref, b1_ref, b2_ref, bias0_ref, c0_ref, c1_ref, c2_ref = refs[:10]
    if full_roll:
        tails = (None,) * N_GROUPS
        rest = refs[10:]
    else:
        tails = refs[10:13]
        rest = refs[13:]
    o_ref, n0_ref, n1_ref, n2_ref, p0_scr, p1_scr, p2_scr, st_scr, oacc_scr = rest
    half = pl.program_id(1)
    active = (pl.program_id(2) == layer) if full_roll else True
    aw = A_WIDTH
    sel = (lax.broadcasted_iota(jnp.int32, (A_HEADS, aw), 1) // A_DH
           == lax.broadcasted_iota(jnp.int32, (A_HEADS, aw), 0))
    last_lane = lax.broadcasted_iota(jnp.int32, (1, 128), 1) == 127
    groups = ((c0_ref, n0_ref, b0_ref, p0_scr, tails[0]), (c1_ref, n1_ref, b1_ref, p1_scr, tails[1]),
              (c2_ref, n2_ref, b2_ref, p2_scr, tails[2]))
    for g, (c_ref, n_ref, b_ref, p_scr, t_ref) in enumerate(groups):
        wlen = c_ref.shape[1]
        if full_roll:
            n_ref[...] = pltpu.roll(c_ref[...], wlen - 1, axis=1)

        def newest(col, n_ref=n_ref, t_ref=t_ref, wlen=wlen):
            old = n_ref[:, wlen - 128:] if full_roll else t_ref[...]
            dst = n_ref.at[:, wlen - 128:] if full_roll else n_ref
            dst[...] = jnp.where(last_lane, col, old)

        lo = g * aw

        @pl.when(jnp.logical_and(active, half == 0))
        def _(g=g, c_ref=c_ref, b_ref=b_ref, p_scr=p_scr, lo=lo, newest=newest):
            q = zq_ref[:, Z_AQ + lo:Z_AQ + lo + aw]
            k = zq_ref[:, Z_AK + lo:Z_AK + lo + aw]
            def head_rms(x, gain):
                s8 = jnp.sum(jnp.where(sel, jnp.broadcast_to(x * x, (A_HEADS, aw)), 0.0), axis=1, keepdims=True)
                ms = jnp.sum(jnp.where(sel, s8, 0.0), axis=0, keepdims=True) * (1.0 / A_DH)
                return x * lax.rsqrt(ms + EPS) * gain

            qn = head_rms(q, qg_ref[...]) * (A_DH ** -0.5)
            kn = head_rms(k, kg_ref[...])
            qbd = jnp.where(sel, jnp.broadcast_to(qn, (A_HEADS, aw)), 0.0)
            lc = jnp.sum((c_ref[...] * _to_col(qn, aw)).reshape(A_HEADS, A_DH, c_ref.shape[1]), axis=1) + b_ref[...]
            l0 = jnp.sum(qbd * kn, axis=1, keepdims=True) + bias0_ref[g]
            m = jnp.maximum(jnp.max(lc, axis=1, keepdims=True), l0)
            pc = jnp.exp(lc - m)
            p0 = jnp.exp(l0 - m)
            p_scr[...] = pc
            st_scr[g, 0] = jnp.broadcast_to(p0, (A_HEADS, 128))
            st_scr[g, 1] = jnp.broadcast_to(jnp.sum(pc, axis=1, keepdims=True) + p0, (A_HEADS, 128))
            st_scr[g, 2] = jnp.broadcast_to(m, (A_HEADS, 128))
            newest(_to_col(kn, aw))

        @pl.when(jnp.logical_and(active, half == 1))
        def _(g=g, c_ref=c_ref, p_scr=p_scr, lo=lo, newest=newest):
            v = zq_ref[:, Z_AV + lo:Z_AV + lo + aw]
            p0 = st_scr[g, 0][:, 0:1]
            s = st_scr[g, 1][:, 0:1]
            m = st_scr[g, 2][:, 0:1]
            o8 = (_dot_nt(p_scr[...], c_ref[...]) + _r16(p0) * _r16(v)) / s
            oacc_scr[g, 0:1, :] = jnp.sum(jnp.where(sel, o8, 0.0), axis=0, keepdims=True)
            oacc_scr[g, 1:2, :] = jnp.sum(jnp.where(sel, m + jnp.log(s), 0.0), axis=0, keepdims=True)
            newest(_to_col(v, aw))

    @pl.when(jnp.logical_and(active, half == 1))
    def _():
        l_1, l_2, l_3 = oacc_scr[0, 1:2, :], oacc_scr[1, 1:2, :], oacc_scr[2, 1:2, :]
        mx = jnp.maximum(jnp.maximum(l_1, l_2), l_3)
        e_1, e_2, e_3 = jnp.exp(l_1 - mx), jnp.exp(l_2 - mx), jnp.exp(l_3 - mx)
        o_ref[...] = ((e_1 * oacc_scr[0, 0:1, :] + e_2 * oacc_scr[1, 0:1, :] + e_3 * oacc_scr[2, 0:1, :])
                      / (e_1 + e_2 + e_3))


def _attn_sample(zs3, caches_t, rolled, layer, qgain, kgain, biases, bias0):
    bd = zs3.shape[0]
    full_roll = rolled is None
    aw = A_WIDTH
    if full_roll:
        grid = (bd, 2, DEPTH)
        const = lambda shape: pl.BlockSpec(shape, lambda i, h, l: (0,) * len(shape))
        zspec = pl.BlockSpec((None, 1, Z_AV + N_GROUPS * aw), lambda i, h, l: (i, 0, 0))
        ospec = pl.BlockSpec((None, 1, aw), lambda i, h, l: (i, 0, 0))
        cache_specs = [pl.BlockSpec((None, None, None, aw, c.shape[-1]), lambda i, h, l: (l, i, h, 0, 0))
                       for c in caches_t]
        new_specs = cache_specs
        extra_specs, extra_args, aliases = [], [], {}
        sem = ("parallel", "arbitrary", "arbitrary")
    else:
        grid = (bd, 2)
        const = lambda shape: pl.BlockSpec(shape, lambda i, h: (0,) * len(shape))
        zspec = pl.BlockSpec((None, 1, Z_AV + N_GROUPS * aw), lambda i, h: (i, 0, 0))
        ospec = pl.BlockSpec((None, 1, aw), lambda i, h: (i, 0, 0))
        cache_specs = [pl.BlockSpec((None, None, None, aw, c.shape[-1]), lambda i, h: (layer, i, h, 0, 0))
                       for c in caches_t]
        new_specs = [pl.BlockSpec((None, None, None, aw, 128), functools.partial(
            lambda last, i, h: (layer, i, h, 0, last), c.shape[-1] // 128 - 1)) for c in caches_t]
        extra_specs, extra_args = new_specs, list(rolled)
        aliases = {10 + g: 1 + g for g in range(N_GROUPS)}
        sem = ("parallel", "arbitrary")
    return pl.pallas_call(
        functools.partial(_attn_sample_kernel, layer=layer, full_roll=full_roll),
        grid=grid,
        in_specs=[zspec, const((1, aw)), const((1, aw))]
                 + [const((A_HEADS, c.shape[-1])) for c in caches_t] + [const((N_GROUPS, A_HEADS, 1))]
                 + cache_specs + extra_specs,
        out_specs=[ospec] + new_specs,
        out_shape=[jax.ShapeDtypeStruct((bd, 1, aw), F32)]
                  + [jax.ShapeDtypeStruct(c.shape, c.dtype) for c in caches_t],
        scratch_shapes=[pltpu.VMEM((A_HEADS, c.shape[-1]), F32) for c in caches_t]
                       + [pltpu.VMEM((N_GROUPS, 3, A_HEADS, 128), F32), pltpu.VMEM((N_GROUPS, 8, aw), F32)],
        input_output_aliases=aliases,
        compiler_params=_cparams(sem),
        name="attn_sample",
    )(zs3, qgain, kgain, *biases, bias0, *caches_t, *extra_args)


def _to_col(row, n):
    eye = lax.broadcasted_iota(jnp.int32, (n, n), 0) == lax.broadcasted_iota(jnp.int32, (n, n), 1)
    return jnp.sum(jnp.where(eye, jnp.broadcast_to(row, (n, n)), 0.0), axis=1, keepdims=True)


def _sample_mix_kernel(bq_ref, bk_ref, bv_ref, cqkv_ref, sm_ref, cbuf_ref, sgla_ref, sgdn_ref,
                       wlr_ref, lrb_ref, cw_ref, dtb_ref, alog_ref,
                       ob_ref, oc_ref, nconv_ref, ngla_ref, ngdn_ref):
    sm = sm_ref[...]
    la =_log_sigmoid(_dot(sm, wlr_ref[...]) + lrb_ref[...]) * (1.0 / B_TAU)
    q = bq_ref[...] * (B_DK ** -0.5)
    k = bk_ref[...]
    v = bv_ref[...]
    for h in range(B_HEADS):
        lk = slice(B_DK * h, B_DK * (h + 1))
        lv = slice(B_DV * h, B_DV * (h + 1))
        s_old = sgla_ref[h]
        decay = jnp.exp(la[:, lk])
        ngla_ref[h] = _to_col(decay, B_DK) * s_old + _to_col(k[:, lk], B_DK) * v[:, lv]
        att = jnp.sum(q[:, lk] * k[:, lk], axis=-1, keepdims=True)
        ob_ref[:, lv] = (jnp.sum(_to_col(q[:, lk] * decay, B_DK) * s_old, axis=0, keepdims=True)
                         + att * v[:, lv])

    cat = jnp.concatenate([cbuf_ref[...], cqkv_ref[...]], axis=0)
    acc = None
    for j in range(CONV_W):
        term = cat[j:j + 1, :] * cw_ref[j:j + 1, :]
        acc = term if acc is None else acc + term
    y = _silu(acc)
    nconv_ref[...] = cat[1:CONV_W, :]
    hd = C_HEADS * C_DK
    for h in range(C_HEADS):
        ld = slice(C_DK * h, C_DK * (h + 1))
        qh = y[:, C_DK * h:C_DK * (h + 1)]
        qh = qh * lax.rsqrt(jnp.sum(qh * qh, axis=-1, keepdims=True) + EPS) * (C_DK ** -0.5)
        kh = y[:, hd + C_DK * h:hd + C_DK * (h + 1)]
        kh = kh * lax.rsqrt(jnp.sum(kh * kh, axis=-1, keepdims=True) + EPS)
        vh = y[:, 2 * hd + C_DV * h:2 * hd + C_DV * (h + 1)]
        beta = _sigmoid(sm[:, SM_CB + h:SM_CB + h + 1])
        g = -jnp.exp(alog_ref[:, ld]) * _softplus(sm[:, SM_CA + h:SM_CA + h + 1] + dtb_ref[:, ld])
        eg = jnp.exp(g)
        s = sgdn_ref[h]
        kcol = _to_col(kh, C_DK)
        u = beta * (vh - eg * jnp.sum(kcol * s, axis=0, keepdims=True))
        qs = jnp.sum(_to_col(qh, C_DK) * s, axis=0, keepdims=True)
        oc_ref[:, ld] = eg * qs + jnp.sum(qh * kh, axis=-1, keepdims=True) * u
        ngdn_ref[h] = eg * s + kcol * u


def _sample_mix(zs3, cbuf, sgla, sgdn, wlr_pad, lr_bias, conv_w, dtb_bc, alog_bc):
    bd = zs3.shape[0]
    hk, hv, hd = B_HEADS * B_DK, B_HEADS * B_DV, C_HEADS * C_DV
    const = lambda shape: pl.BlockSpec(shape, lambda i: (0,) * len(shape))
    zrow = lambda width, off: pl.BlockSpec((None, 1, width), lambda i: (i, 0, off // width))
    vec = lambda width: pl.BlockSpec((None, 1, width), lambda i: (i, 0, 0))
    st = lambda shape: pl.BlockSpec((None,) + shape, lambda i: (i,) + (0,) * len(shape))
    return pl.pallas_call(
        _sample_mix_kernel,
        grid=(bd,),
        in_specs=[zrow(hk, Z_BQ), zrow(hk, Z_BK), zrow(hv, Z_BV), zrow(C_CONV_CH, Z_CQKV), zrow(SMALL_W, Z_SMALL),
                  st((CONV_W - 1, C_CONV_CH)), st((B_HEADS, B_DK, B_DV)), st((C_HEADS, C_DK, C_DV))]
                 + [const((SMALL_W, hk)), const((1, hk)), const((CONV_W, C_CONV_CH)), const((1, hd)), const((1, hd))],
        out_specs=[vec(hv), vec(hd), st((CONV_W - 1, C_CONV_CH)),
                   st((B_HEADS, B_DK, B_DV)), st((C_HEADS, C_DK, C_DV))],
        out_shape=[jax.ShapeDtypeStruct((bd, 1, hv), F32),
                   jax.ShapeDtypeStruct((bd, 1, hd), F32), jax.ShapeDtypeStruct((bd, CONV_W - 1, C_CONV_CH), F32),
                   jax.ShapeDtypeStruct((bd, B_HEADS, B_DK, B_DV), F32),
                   jax.ShapeDtypeStruct((bd, C_HEADS, C_DK, C_DV), F32)],
        compiler_params=_cparams(("parallel",)),
        name="sample_mix",
    )(zs3, zs3, zs3, zs3, zs3, cbuf, sgla, sgdn, wlr_pad, lr_bias, conv_w, dtb_bc, alog_bc)


def _t5_bucket(dist):
    max_exact = N_BUCKETS // 2
    d = jnp.maximum(dist.astype(F32), 1.0)
    large = max_exact + (jnp.log(d / max_exact) / math.log(MAX_DIST / max_exact)
                         * (N_BUCKETS - max_exact)).astype(jnp.int32)
    large = jnp.minimum(large, N_BUCKETS - 1)
    return jnp.where(dist < max_exact, dist, large).astype(jnp.int32)


def _bias_tables(rel_bias):
    i = jnp.arange(SPAN)[:, None]
    c = jnp.arange(2 * SPAN)[None, :]
    dist = i + SPAN - c
    valid = (dist >= 0) & (dist <= SPAN)

    def lookup(bias_g, idx):
        hot = (idx[..., None] == jnp.arange(N_BUCKETS)).astype(F32)
        return jnp.einsum("...k,kh->...h", hot, bias_g.astype(F32), precision=HI)

    prompt, cached, new = [], [], []
    for g, (win, dil) in enumerate(A_GROUPS):
        bias_g = rel_bias[:, g * A_HEADS:(g + 1) * A_HEADS]
        tbl = lookup(bias_g, _t5_bucket(jnp.maximum(dist, 0) * dil)).transpose(2, 0, 1)
        prompt.append(jnp.where(valid[None], tbl, NEG))
        w = jnp.arange(win)
        tbl_s = lookup(bias_g, _t5_bucket(win - w)).T
        cached.append(jnp.where((w % dil == 0)[None], tbl_s, NEG))
        new.append(lookup(bias_g, _t5_bucket(jnp.zeros((1,), jnp.int32))).T)
    return jnp.stack(prompt), cached, jnp.stack(new)


def _prep_w_in(w):
    offs = np.concatenate([[0], np.cumsum(IN_SIZES)])
    seg = lambda i: w[:, offs[i]:offs[i + 1]]
    order = (0, 1, 2, 3, 4, 5, 6, 8, 9, 12, 7, 10, 11)
    used = sum(IN_SIZES)
    parts = [seg(i) for i in order] + [jnp.zeros((w.shape[0], NZ - used), w.dtype)]
    return jnp.concatenate(parts, axis=1).astype(BF16)


def _selector(offset):
    sel = np.zeros((SMALL_W, C_HEADS * C_DV), np.float32)
    for h in range(C_HEADS):
        sel[offset + h, C_DV * h:C_DV * (h + 1)] = 1.0
    return jnp.asarray(sel)


def kernel(x_prompt, x_sample, cache_win128_kv, cache_win512_kv, cache_win2048_kv, state_gla, state_gdn, state_conv, norm_mix, w_in, a_q_norm, a_k_norm, rel_bias, b_w_lr, b_lr_bias, b_out_norm, c_conv, c_a_log, c_dt_bias, c_out_norm, w_branch, w_out, norm_ffn, ffn_w_gate, ffn_w_up, ffn_w_down, moe_router, moe_w_gate, moe_w_up, moe_w_down):
    bp, t, d = x_prompt.shape
    bs = x_sample.shape[0]
    mp = bp * t
    caches_t = [jnp.transpose(c, (0, 1, 3, 4, 5, 2)).reshape(c.shape[:2] + (2, A_WIDTH, c.shape[2]))
                for c in (cache_win128_kv, cache_win512_kv, cache_win2048_kv)]
    bias_p, bias_c, bias_n = _bias_tables(rel_bias)
    selb, sela = _selector(SM_CB), _selector(SM_CA)
    rolled = windows = None

    xp = x_prompt.reshape(mp, d)
    xs = x_sample.reshape(bs, d)
    outs_p = {k: [] for k in ("w0", "w1", "w2", "gla", "gdn", "conv")}
    outs_s = {k: [] for k in ("w0", "w1", "w2", "gla", "gdn", "conv")}
    for l in range(DEPTH):
        w_in_l = _prep_w_in(w_in[l])
        wb = w_branch[l].astype(BF16)
        wo = w_out[l].astype(BF16)
        qg128 = jnp.tile(a_q_norm[l], 2).reshape(1, 2 * A_DH)
        kg128 = jnp.tile(a_k_norm[l], 2).reshape(1, 2 * A_DH)
        qg512 = jnp.tile(a_q_norm[l], A_HEADS).reshape(1, A_WIDTH)
        kg512 = jnp.tile(a_k_norm[l], A_HEADS).reshape(1, A_WIDTH)
        wlr_pad = jnp.zeros((SMALL_W, B_HEADS * B_DK), F32).at[:B_RANK].set(b_w_lr[l])
        lr_bias = b_lr_bias[l].reshape(1, -1)
        bon = jnp.tile(b_out_norm[l], B_HEADS).reshape(1, -1)
        con = jnp.tile(c_out_norm[l], C_HEADS).reshape(1, -1)
        dtb_bc = jnp.repeat(c_dt_bias[l], C_DV).reshape(1, -1)
        alog_bc = jnp.repeat(c_a_log[l], C_DV).reshape(1, -1)

        z = _norm_matmul(xp, norm_mix[l], w_in_l, tm=1024, tn=2304)
        z3 = z.reshape(bp, t, NZ)
        o_a, *windows = _attn_prompt(z3, qg128, kg128, bias_p, l, windows)
        o_b, gla_t = _gla_prompt(z3, wlr_pad, lr_bias, tb=512)
        dtb_sm = jnp.zeros((1, SMALL_W), F32).at[0, SM_CA:SM_CA + C_HEADS].set(c_dt_bias[l])
        alog_sm = jnp.zeros((1, SMALL_W), F32).at[0, SM_CA:SM_CA + C_HEADS].set(c_a_log[l])
        o_c, gdn_s = _gdn_prompt(z3, c_conv[l], selb, sela, dtb_sm, alog_sm, tb=512)
        xp = _merge(z, o_a.reshape(mp, -1), o_b.reshape(mp, -1), o_c.reshape(mp, -1), xp, bon, con, wb, wo, tm=256)
        gla = gla_t.reshape(bp, B_HEADS, B_DV, B_HEADS, B_DK)
        gla = jnp.stack([gla[:, h, :, h, :] for h in range(B_HEADS)], axis=1)
        outs_p["gla"].append(jnp.swapaxes(gla, 2, 3))
        outs_p["gdn"].append(gdn_s)
        outs_p["conv"].append(z3[:, t - (CONV_W - 1):, Z_CQKV:Z_CQKV + C_CONV_CH])

        zs = _norm_matmul(xs, norm_mix[l], w_in_l, tm=bs, tn=1280)
        zs3 = zs.reshape(bs, 1, NZ)
        oa_s, *rolled = _attn_sample(zs3, caches_t, rolled, l, qg512, kg512, bias_c, bias_n)
        ob_s, oc_s, nconv, ngla, ngdn = _sample_mix(
            zs3, state_conv[l], state_gla[l], state_gdn[l], wlr_pad, lr_bias, c_conv[l], dtb_bc, alog_bc)
        xs = _merge(zs, oa_s.reshape(bs, -1), ob_s.reshape(bs, -1), oc_s.reshape(bs, -1), xs, bon, con, wb, wo, tm=bs)
        outs_s["gla"].append(ngla)
        outs_s["gdn"].append(ngdn)
        outs_s["conv"].append(nconv)

        i = l // 2
        if l % 2 == 0:
            wg, wu, wd = ffn_w_gate[i].astype(BF16), ffn_w_up[i].astype(BF16), ffn_w_down[i].astype(BF16)
            xp = _ffn(xp, norm_ffn[l], wg, wu, wd, tm=512, tf=1408)
            xs = _ffn(xs, norm_ffn[l], wg, wu, wd, tm=bs, tf=1408)
        else:
            wg, wu, wd = moe_w_gate[i].astype(BF16), moe_w_up[i].astype(BF16), moe_w_down[i].astype(BF16)
            router_pad = jnp.zeros((d, 128), F32).at[:, :N_EXPERTS].set(moe_router[i])
            xp = _moe_routed(xp, norm_ffn[l], moe_router[i].T, wg, wu, wd, tm=1024)
            xs = _moe(xs, norm_ffn[l], router_pad, wg, wu, wd, tm=bs, tf=896)

    st = lambda name, d_: jnp.stack(d_[name])

    def window_out(w):
        w = w.reshape(w.shape[:3] + (A_HEADS, A_DH, w.shape[-1]))
        return jnp.transpose(w, (0, 1, 5, 2, 3, 4))

    return (xp.reshape(bp, t, d), xs.reshape(bs, 1, d),
            window_out(windows[0]), window_out(rolled[0]), window_out(windows[1]), window_out(rolled[1]),
            window_out(windows[2]), window_out(rolled[2]),
            st("gla", outs_p), st("gla", outs_s), st("gdn", outs_p), st("gdn", outs_s),
            st("conv", outs_p), st("conv", outs_s))
```

```python
import functools
import math

import jax
import jax.numpy as jnp
import numpy as np
from jax import lax
from jax.experimental import pallas as pl
from jax.experimental.pallas import tpu as pltpu

F32 = jnp.float32
BF16 = jnp.bfloat16
HI = lax.Precision.HIGHEST
NEG = -1e30

D_MODEL = 1024
DEPTH = 2
EPS = 1e-6
A_GROUPS = ((128, 1), (512, 4), (2048, 16))
N_GROUPS = 3
A_HEADS = 8
A_DH = 64
A_WIDTH = A_HEADS * A_DH
SPAN = 128
ATT_PAIR = 16
GLA_PAIR = 8
GDN_PAIR = 8
N_BUCKETS = 32
MAX_DIST = 2048
B_HEADS = 4
B_DK = 64
B_DV = 128
B_RANK = 16
B_TAU = 16.0
C_HEADS = 4
C_DK = 128
C_DV = 128
CONV_W = 4
C_CONV_CH = 2 * C_HEADS * C_DK + C_HEADS * C_DV
CHUNK = 64
SUB = 16
BRANCH_W = 512
D_FF = 2816
N_EXPERTS = 8
D_FF_EXPERT = 1792
IN_SIZES = (N_GROUPS * A_WIDTH, N_GROUPS * A_WIDTH, N_GROUPS * A_WIDTH,
            B_HEADS * B_DK, B_HEADS * B_DK, B_HEADS * B_DV, B_HEADS * B_DV, B_RANK,
            C_CONV_CH, C_HEADS * C_DV, C_HEADS, C_HEADS, 3 * D_MODEL)

Z_AQ, Z_AK, Z_AV = 0, 1536, 3072
Z_BQ, Z_BK, Z_BV, Z_BG = 4608, 4864, 5120, 5632
Z_CQKV, Z_CZ, Z_GATES, Z_SMALL = 6144, 7680, 8192, 11264
SMALL_W = 256
NZ = Z_SMALL + SMALL_W
SM_CB, SM_CA = B_RANK, B_RANK + C_HEADS

VMEM_LIMIT = 56 * 1024 * 1024


def _cparams(sem):
    return pltpu.CompilerParams(dimension_semantics=sem, vmem_limit_bytes=VMEM_LIMIT)


def _dot(a, b):
    return jnp.dot(a.astype(BF16), b.astype(BF16), preferred_element_type=F32)


def _dot_nt(a, b):
    return lax.dot_general(a.astype(BF16), b.astype(BF16), (((1,), (1,)), ((), ())), preferred_element_type=F32)


def _dot_tn(a, b):
    return lax.dot_general(a.astype(BF16), b.astype(BF16), (((0,), (0,)), ((), ())), preferred_element_type=F32)


def _split3(x):
    x1 = x.astype(BF16)
    r1 = x - x1.astype(F32)
    x2 = r1.astype(BF16)
    return x1, x2, (r1 - x2.astype(F32)).astype(BF16)


def _mask_dot(mask, x):
    m = mask.astype(BF16)
    x1, x2, x3 = _split3(x)
    return (jnp.dot(m, x1, preferred_element_type=F32) + jnp.dot(m, x2, preferred_element_type=F32)
            + jnp.dot(m, x3, preferred_element_type=F32))


def _dot_mask(x, mask):
    m = mask.astype(BF16)
    x1, x2, x3 = _split3(x)
    return (jnp.dot(x1, m, preferred_element_type=F32) + jnp.dot(x2, m, preferred_element_type=F32)
            + jnp.dot(x3, m, preferred_element_type=F32))


def _r16(x):
    return x.astype(BF16).astype(F32)


def _sigmoid(x):
    return jax.nn.sigmoid(x)


def _silu(x):
    return x * jax.nn.sigmoid(x)


def _softplus(x):
    return jnp.maximum(x, 0.0) + jnp.log1p(jnp.exp(-jnp.abs(x)))


def _log_sigmoid(x):
    return jnp.minimum(x, 0.0) - jnp.log1p(jnp.exp(-jnp.abs(x)))


def _rms(x, gain):
    return x * lax.rsqrt(jnp.mean(x * x, axis=-1, keepdims=True) + EPS) * gain


def _norm_matmul_kernel(x_ref, g_ref, w_ref, o_ref, h_ref):
    @pl.when(pl.program_id(1) == 0)
    def _():
        h_ref[...] = _rms(x_ref[...], g_ref[...]).astype(BF16)

    o_ref[...] = jnp.dot(h_ref[...], w_ref[...], preferred_element_type=F32)


def _norm_matmul(x, gain, w, tm, tn):
    m, d = x.shape
    n = w.shape[1]
    return pl.pallas_call(
        _norm_matmul_kernel,
        grid=(m // tm, n // tn),
        in_specs=[pl.BlockSpec((tm, d), lambda i, j: (i, 0)),
                  pl.BlockSpec((1, d), lambda i, j: (0, 0)),
                  pl.BlockSpec((d, tn), lambda i, j: (0, j))],
        out_specs=pl.BlockSpec((tm, tn), lambda i, j: (i, j)),
        out_shape=jax.ShapeDtypeStruct((m, n), F32),
        scratch_shapes=[pltpu.VMEM((tm, d), BF16)],
        compiler_params=_cparams(("parallel", "arbitrary")),
        name="norm_matmul",
    )(x, gain.reshape(1, d), w)


def _pair_rms(x, gain, head0):
    sq = x * x
    s0 = jnp.sum(jnp.where(head0, sq, 0.0), axis=-1, keepdims=True)
    s1 = jnp.sum(sq, axis=-1, keepdims=True) - s0
    return x * lax.rsqrt(jnp.where(head0, s0, s1) * (1.0 / A_DH) + EPS) * gain


def _attn_prompt_kernel(q1, k1, v1, q2, k2, v2, q3, k3, v3, qg_ref, kg_ref, bias_ref,
                        o_ref, w1, w2, w3, qt, kt, qs, ks, vs, og, lg, to, tl, *, slots):
    t_len = o_ref.shape[0]
    head0 = lax.broadcasted_iota(jnp.int32, (1, 2 * A_DH), 1) < A_DH
    first_half = lax.broadcasted_iota(jnp.int32, (1, 2 * SPAN), 1) < SPAN
    groups = ((q1, k1, v1, w1), (q2, k2, v2, w2), (q3, k3, v3, w3))
    for g, (qr, kr, vr, wr) in enumerate(groups):
        dil = A_GROUPS[g][1]
        length = t_len // dil
        nb = length // SPAN
        qt[...] = _pair_rms(qr[...], qg_ref[...], head0) * (A_DH ** -0.5)
        kt[...] = _pair_rms(kr[...], kg_ref[...], head0)
        wlen = wr.shape[-1]
        k_win = kt[t_len - wlen:, :].T
        v_win = vr[t_len - wlen:, :].T
        for s in range(slots):
            wr[s, 0] = k_win
            wr[s, 1] = v_win
        knr = kt
        for r in range(dil):
            rows = pl.ds(r, length, stride=dil) if dil > 1 else pl.ds(0, length)
            dst = pl.ds(r * length, length)
            qs[dst, :] = qt[rows, :].astype(BF16)
            ks[dst, :] = knr[rows, :].astype(BF16)
            vs[dst, :] = vr[rows, :].astype(BF16)

        if nb > 1:
            bias2 = jnp.concatenate([bias_ref[g, 0], bias_ref[g, 1]], axis=0)
        else:
            bias2 = jnp.concatenate([bias_ref[g, 0, :, SPAN:], bias_ref[g, 1, :, SPAN:]], axis=0)

        def block(jj, carry, nb=nb, bias2=bias2):
            row0s, k2s, v2s, logits = [], [], [], []
            for i in range(ATT_PAIR):
                j = jj * ATT_PAIR + i
                row0 = pl.multiple_of(j * SPAN, SPAN)
                qb = qs[pl.ds(row0, SPAN), :]
                qst = jnp.concatenate([jnp.where(head0, qb, jnp.zeros_like(qb)),
                                       jnp.where(head0, jnp.zeros_like(qb), qb)], axis=0)
                if nb > 1:
                    prow = pl.multiple_of(jnp.maximum(row0 - SPAN, 0), SPAN)
                    k2 = jnp.concatenate([ks[pl.ds(prow, SPAN), :], ks[pl.ds(row0, SPAN), :]], axis=0)
                    v2 = jnp.concatenate([vs[pl.ds(prow, SPAN), :], vs[pl.ds(row0, SPAN), :]], axis=0)
                else:
                    k2 = ks[pl.ds(row0, SPAN), :]
                    v2 = vs[pl.ds(row0, SPAN), :]
                lg_ = lax.dot_general(qst, k2, (((1,), (1,)), ((), ())), preferred_element_type=F32) + bias2
                if nb > 1:
                    lg_ = lg_ + jnp.where(first_half, jnp.where(j % nb == 0, NEG, 0.0), 0.0)
                row0s.append(row0)
                k2s.append(k2)
                v2s.append(v2)
                logits.append(lg_)
            lg_all = jnp.concatenate(logits, axis=0)
            m = jnp.max(lg_all, axis=-1, keepdims=True)
            p = jnp.exp(lg_all - m)
            s = jnp.sum(p, axis=-1, keepdims=True)
            p16 = p.astype(BF16)
            lse = m + jnp.log(s)
            for i in range(ATT_PAIR):
                lo = 2 * SPAN * i
                acc = jnp.dot(p16[lo:lo + 2 * SPAN], v2s[i], preferred_element_type=F32) / s[lo:lo + 2 * SPAN]
                to[pl.ds(row0s[i], SPAN), :] = jnp.where(head0, acc[:SPAN], acc[SPAN:])
                tl[pl.ds(row0s[i], SPAN), :] = jnp.where(head0, jnp.broadcast_to(lse[lo:lo + SPAN], (SPAN, 2 * A_DH)),
                                                         jnp.broadcast_to(lse[lo + SPAN:lo + 2 * SPAN], (SPAN, 2 * A_DH)))
            return carry

        lax.fori_loop(0, t_len // (SPAN * ATT_PAIR), block, 0)
        for r in range(dil):
            rows = pl.ds(r, length, stride=dil) if dil > 1 else pl.ds(0, length)
            src = pl.ds(r * length, length)
            og[g, rows, :] = to[src, :]
            lg[g, rows, :] = tl[src, :]
    l_1, l_2, l_3 = lg[0], lg[1], lg[2]
    mx = jnp.maximum(jnp.maximum(l_1, l_2), l_3)
    e_1, e_2, e_3 = jnp.exp(l_1 - mx), jnp.exp(l_2 - mx), jnp.exp(l_3 - mx)
    o_ref[...] = (e_1 * og[0] + e_2 * og[1] + e_3 * og[2]) / (e_1 + e_2 + e_3)


def _attn_prompt(z3, qgain, kgain, bias, layer, windows):
    b, t, _ = z3.shape
    hp_blocks = A_WIDTH // 128

    def col(base, g):
        return lambda i, hp: (i, 0, base // 128 + g * hp_blocks + hp)

    in_specs = []
    for g in range(N_GROUPS):
        for base in (Z_AQ, Z_AK, Z_AV):
            in_specs.append(pl.BlockSpec((None, t, 128), col(base, g)))
    in_specs += [pl.BlockSpec((1, 128), lambda i, hp: (0, 0)),
                 pl.BlockSpec((1, 128), lambda i, hp: (0, 0)),
                 pl.BlockSpec((N_GROUPS, 2, SPAN, 2 * SPAN), lambda i, hp: (0, hp, 0, 0))]
    args = [z3] * 9 + [qgain, kgain, bias]
    slots = DEPTH if windows is None else 1
    win_specs, win_shapes = [], []
    for win, _ in A_GROUPS:
        wp = min(win, t)
        win_specs.append(pl.BlockSpec((slots, None, 2, 128, wp), lambda i, hp: (layer // slots, i, 0, hp, 0)))
        win_shapes.append(jax.ShapeDtypeStruct((DEPTH, b, 2, A_WIDTH, wp), F32))
    aliases = {}
    if windows is not None:
        aliases = {len(args) + g: 1 + g for g in range(N_GROUPS)}
        in_specs += [pl.BlockSpec(memory_space=pl.ANY)] * N_GROUPS
        args += list(windows)
    kern = functools.partial(_attn_prompt_kernel, slots=slots)
    if windows is not None:
        kern = functools.partial(_drop_refs, kern, len(args) - N_GROUPS, N_GROUPS)
    return pl.pallas_call(
        kern,
        grid=(b, hp_blocks),
        in_specs=in_specs,
        out_specs=[pl.BlockSpec((None, t, 128), lambda i, hp: (i, 0, hp))] + win_specs,
        out_shape=[jax.ShapeDtypeStruct((b, t, A_WIDTH), F32)] + win_shapes,
        scratch_shapes=[pltpu.VMEM((t, 128), F32), pltpu.VMEM((t, 128), F32),
                        pltpu.VMEM((t, 128), BF16), pltpu.VMEM((t, 128), BF16), pltpu.VMEM((t, 128), BF16),
                        pltpu.VMEM((N_GROUPS, t, 128), F32), pltpu.VMEM((N_GROUPS, t, 128), F32),
                        pltpu.VMEM((t, 128), F32), pltpu.VMEM((t, 128), F32)],
        input_output_aliases=aliases,
        compiler_params=_cparams(("parallel", "parallel")),
        name="attn_prompt",
    )(*args)


def _drop_refs(kern, start, count, *refs):
    return kern(*refs[:start], *refs[start + count:])


def _gla_prompt_kernel(q_ref, k_ref, v_ref, sm_ref, wlr_ref, lrb_ref, o_ref, st_ref, s_scr, la_scr):
    tb = q_ref.shape[0]
    hk = B_HEADS * B_DK
    hv = B_HEADS * B_DV

    @pl.when(pl.program_id(1) == 0)
    def _():
        s_scr[...] = jnp.zeros_like(s_scr)

    la_scr[...] = _log_sigmoid(_dot(sm_ref[...], wlr_ref[...]) + lrb_ref[...]) * (1.0 / B_TAU)

    rr = lax.broadcasted_iota(jnp.int32, (CHUNK, CHUNK), 0)
    cc = lax.broadcasted_iota(jnp.int32, (CHUNK, CHUNK), 1)
    causal = rr >= cc
    ltri = causal.astype(F32)
    lane_head = lax.broadcasted_iota(jnp.int32, (1, hk), 1) // B_DK
    row = lax.broadcasted_iota(jnp.int32, (CHUNK, 1), 0)
    same_head = (lax.broadcasted_iota(jnp.int32, (hv, hk), 0) // B_DV
                 == lax.broadcasted_iota(jnp.int32, (hv, hk), 1) // B_DK)

    def intra(r0):
        b = _mask_dot(ltri, la_scr[pl.ds(r0, CHUNK), :])
        q = q_ref[pl.ds(r0, CHUNK), :] * (B_DK ** -0.5)
        k = k_ref[pl.ds(r0, CHUNK), :]
        vb = v_ref[pl.ds(r0, CHUNK), :].astype(BF16)
        b_last = b[CHUNK - 1:CHUNK, :]
        atts = []
        for i in range(CHUNK // SUB):
            b_ref0 = b[SUB * i:SUB * i + 1, :]
            qi = q[SUB * i:SUB * (i + 1), :] * jnp.exp(b[SUB * i:SUB * (i + 1), :] - b_ref0)
            qst = jnp.concatenate([jnp.where(lane_head == h, qi, 0.0) for h in range(B_HEADS)], axis=0)
            kt = k * jnp.exp(jnp.where(row < SUB * (i + 1), b_ref0 - b, 0.0))
            atts.append(_dot_nt(qst, kt))
        o_parts = []
        for h in range(B_HEADS):
            att = jnp.concatenate([a[SUB * h:SUB * (h + 1), :] for a in atts], axis=0)
            att = jnp.where(causal, att, 0.0)
            o_parts.append(_dot(att, vb[:, B_DV * h:B_DV * (h + 1)]))
        upd = jnp.where(same_head, _dot_tn(vb, k * jnp.exp(b_last - b)), 0.0)
        return (q * jnp.exp(b)).astype(BF16), jnp.concatenate(o_parts, axis=1), jnp.exp(b_last), upd

    def chunks(cc, carry):
        r0s = [pl.multiple_of((cc * GLA_PAIR + i) * CHUNK, CHUNK) for i in range(GLA_PAIR)]
        parts = [intra(r0) for r0 in r0s]
        for r0, (qe, o_intra, decay, upd) in zip(r0s, parts):
            s = s_scr[...]
            o_ref[pl.ds(r0, CHUNK), :] = _dot_nt(qe, s) + o_intra
            s_scr[...] = s * decay + upd
        return carry

    lax.fori_loop(0, tb // (CHUNK * GLA_PAIR), chunks, 0)

    @pl.when(pl.program_id(1) == pl.num_programs(1) - 1)
    def _():
        st_ref[...] = s_scr[...]


def _gla_prompt(z3, wlr_pad, lr_bias, tb):
    b, t, _ = z3.shape
    hk, hv = B_HEADS * B_DK, B_HEADS * B_DV
    return pl.pallas_call(
        _gla_prompt_kernel,
        grid=(b, t // tb),
        in_specs=[pl.BlockSpec((None, tb, hk), lambda i, j: (i, j, Z_BQ // hk)),
                  pl.BlockSpec((None, tb, hk), lambda i, j: (i, j, Z_BK // hk)),
                  pl.BlockSpec((None, tb, hv), lambda i, j: (i, j, Z_BV // hv)),
                  pl.BlockSpec((None, tb, SMALL_W), lambda i, j: (i, j, Z_SMALL // SMALL_W)),
                  pl.BlockSpec((SMALL_W, hk), lambda i, j: (0, 0)),
                  pl.BlockSpec((1, hk), lambda i, j: (0, 0))],
        out_specs=[pl.BlockSpec((None, tb, hv), lambda i, j: (i, j, 0)),
                   pl.BlockSpec((None, hv, hk), lambda i, j: (i, 0, 0))],
        out_shape=[jax.ShapeDtypeStruct((b, t, hv), F32), jax.ShapeDtypeStruct((b, hv, hk), F32)],
        scratch_shapes=[pltpu.VMEM((hv, hk), F32), pltpu.VMEM((tb, hk), F32)],
        compiler_params=_cparams(("parallel", "arbitrary")),
        name="gla_prompt",
    )(z3, z3, z3, z3, wlr_pad, lr_bias)


def _gdn_prompt_kernel(x_ref, sm_ref, cw_ref, selb_ref, sela_ref, dtb_ref, alog_ref, o_ref, s_ref,
                       xpad, cv, gsc, bsc, s_scr):
    tb = x_ref.shape[0]
    hd = C_HEADS * C_DK
    pad = 8

    @pl.when(pl.program_id(1) == 0)
    def _():
        s_scr[...] = jnp.zeros_like(s_scr)
        xpad[0:pad, :] = jnp.zeros((pad, C_CONV_CH), F32)

    xpad[pad:pad + tb, :] = x_ref[...]
    rb = 128
    for i in range(tb // rb):
        acc = None
        for j in range(CONV_W):
            lo = pad - (CONV_W - 1) + j + i * rb
            term = xpad[lo:lo + rb, :] * cw_ref[j:j + 1, :]
            acc = term if acc is None else acc + term
        y = _silu(acc)
        rows = slice(i * rb, (i + 1) * rb)
        for h in range(C_HEADS):
            qh = y[:, C_DK * h:C_DK * (h + 1)]
            cv[rows, C_DK * h:C_DK * (h + 1)] = (
                qh * lax.rsqrt(jnp.sum(qh * qh, axis=-1, keepdims=True) + EPS) * (C_DK ** -0.5))
            kh = y[:, hd + C_DK * h:hd + C_DK * (h + 1)]
            cv[rows, hd + C_DK * h:hd + C_DK * (h + 1)] = (
                kh * lax.rsqrt(jnp.sum(kh * kh, axis=-1, keepdims=True) + EPS))
        cv[rows, 2 * hd:] = y[:, 2 * hd:]
    xpad[pad - (CONV_W - 1):pad, :] = xpad[pad + tb - (CONV_W - 1):pad + tb, :]

    sm = sm_ref[...]
    gsc[...] = _dot_mask(-jnp.exp(alog_ref[...]) * _softplus(sm + dtb_ref[...]), sela_ref[...])
    bsc[...] = _dot_mask(_sigmoid(sm), selb_ref[...])

    n = C_HEADS * CHUNK
    rr = lax.broadcasted_iota(jnp.int32, (CHUNK, CHUNK), 0)
    cc = lax.broadcasted_iota(jnp.int32, (CHUNK, CHUNK), 1)
    ltri = (rr >= cc).astype(F32)
    ri = lax.broadcasted_iota(jnp.int32, (n, n), 0)
    ci = lax.broadcasted_iota(jnp.int32, (n, n), 1)
    same = (ri // CHUNK) == (ci // CHUNK)
    incl = jnp.logical_and(same, ci <= ri)
    strict = jnp.logical_and(same, ci < ri)
    eye = (ri == ci).astype(F32)

    def stack(x):
        return jnp.concatenate([x[:, C_DK * h:C_DK * (h + 1)] for h in range(C_HEADS)], axis=0)

    def chunks(cc, carry):
        pair = range(GDN_PAIR)
        r0s = [pl.multiple_of((cc * GDN_PAIR + i) * CHUNK, CHUNK) for i in pair]
        bst = [stack(_mask_dot(ltri, gsc[pl.ds(r0, CHUNK), :])) for r0 in r0s]
        beta = [stack(bsc[pl.ds(r0, CHUNK), :]) for r0 in r0s]
        qst = [stack(cv[pl.ds(r0, CHUNK), 0:hd]) for r0 in r0s]
        kst = [stack(cv[pl.ds(r0, CHUNK), hd:2 * hd]) for r0 in r0s]
        vst = [stack(cv[pl.ds(r0, CHUNK), 2 * hd:3 * hd]) for r0 in r0s]
        kb = [k.astype(BF16) for k in kst]
        nt = (((1,), (1,)), ((), ()))
        kk = [lax.dot_general(kb[i], kb[i], nt, preferred_element_type=F32) for i in pair]
        qk = [lax.dot_general(qst[i].astype(BF16), kb[i], nt, preferred_element_type=F32) for i in pair]
        dec, a, aqk = [], [], []
        for i in pair:
            bst_t = bst[i].T
            d_ = jnp.exp(jnp.where(incl, jnp.concatenate([bst[i], bst[i]], axis=1)
                                   - jnp.concatenate([bst_t, bst_t], axis=0), NEG))
            dec.append(d_)
            a.append(jnp.where(strict, jnp.concatenate([beta[i], beta[i]], axis=1) * d_ * kk[i], 0.0))
            aqk.append(d_ * qk[i])
        inv = [eye - a[i] for i in pair]
        pw = a
        for _ in range(int(math.log2(CHUNK)) - 1):
            pw = [_dot(pw[i], pw[i]) for i in pair]
            inv = [inv[i] + _dot(inv[i], pw[i]) for i in pair]
        eb = [jnp.exp(bst[i]) for i in pair]
        wu = [_dot(inv[i], jnp.concatenate([beta[i] * eb[i] * kst[i], beta[i] * vst[i]], axis=1)) for i in pair]
        for i in pair:
            w, uv = wu[i][:, :C_DK], wu[i][:, C_DK:]
            us, qss = [], []
            for h in range(C_HEADS):
                sl = slice(CHUNK * h, CHUNK * (h + 1))
                ws = _dot(jnp.concatenate([w[sl], qst[i][sl]], axis=0), s_scr[h])
                us.append(uv[sl] - ws[:CHUNK])
                qss.append(ws[CHUNK:])
            o = eb[i] * jnp.concatenate(qss, axis=0) + _dot(aqk[i], jnp.concatenate(us, axis=0))
            for h in range(C_HEADS):
                sl = slice(CHUNK * h, CHUNK * (h + 1))
                o_ref[pl.ds(r0s[i], CHUNK), C_DV * h:C_DV * (h + 1)] = o[sl]
                b_last = bst[i][CHUNK * (h + 1) - 1:CHUNK * (h + 1), :]
                s_scr[h] = (jnp.exp(b_last) * s_scr[h]
                            + _dot_tn(kst[i][sl] * jnp.exp(b_last - bst[i][sl]), us[h]))
        return carry

    lax.fori_loop(0, tb // (CHUNK * GDN_PAIR), chunks, 0)

    @pl.when(pl.program_id(1) == pl.num_programs(1) - 1)
    def _():
        s_ref[...] = s_scr[...]


def _gdn_prompt(z3, conv_w, selb, sela, dtb_bc, alog_bc, tb):
    b, t, _ = z3.shape
    hd = C_HEADS * C_DV
    const = lambda shape: pl.BlockSpec(shape, lambda i, j: (0,) * len(shape))
    return pl.pallas_call(
        _gdn_prompt_kernel,
        grid=(b, t // tb),
        in_specs=[pl.BlockSpec((None, tb, C_CONV_CH), lambda i, j: (i, j, Z_CQKV // C_CONV_CH)),
                  pl.BlockSpec((None, tb, SMALL_W), lambda i, j: (i, j, Z_SMALL // SMALL_W)),
                  const((CONV_W, C_CONV_CH)), const((SMALL_W, hd)), const((SMALL_W, hd)),
                  const((1, SMALL_W)), const((1, SMALL_W))],
        out_specs=[pl.BlockSpec((None, tb, hd), lambda i, j: (i, j, 0)),
                   pl.BlockSpec((None, C_HEADS, C_DK, C_DV), lambda i, j: (i, 0, 0, 0))],
        out_shape=[jax.ShapeDtypeStruct((b, t, hd), F32),
                   jax.ShapeDtypeStruct((b, C_HEADS, C_DK, C_DV), F32)],
        scratch_shapes=[pltpu.VMEM((tb + 8, C_CONV_CH), F32), pltpu.VMEM((tb, C_CONV_CH), F32),
                        pltpu.VMEM((tb, hd), F32), pltpu.VMEM((tb, hd), F32),
                        pltpu.VMEM((C_HEADS, C_DK, C_DV), F32)],
        compiler_params=_cparams(("parallel", "arbitrary")),
        name="gdn_prompt",
    )(z3, z3, conv_w, selb, sela, dtb_bc, alog_bc)


def _head_norm(o, gain, width):
    parts = []
    for h in range(o.shape[1] // width):
        oh = o[:, width * h:width * (h + 1)]
        parts.append(oh * lax.rsqrt(jnp.mean(oh * oh, axis=-1, keepdims=True) + EPS))
    return jnp.concatenate(parts, axis=1) * gain


def _merge_kernel(oa_ref, ob_ref, bg_ref, oc_ref, cz_ref, ga_ref, gb_ref, gc_ref, x_ref,
                  bon_ref, con_ref, wb_ref, wo_ref, y_ref):
    ob = _head_norm(ob_ref[...], bon_ref[...], B_DV) * _silu(bg_ref[...])
    oc = _head_norm(oc_ref[...], con_ref[...], C_DV) * _silu(cz_ref[...])
    merged = (_sigmoid(ga_ref[...]) * _dot(oa_ref[...], wb_ref[0])
              + _sigmoid(gb_ref[...]) * _dot(ob, wb_ref[1])
              + _sigmoid(gc_ref[...]) * _dot(oc, wb_ref[2]))
    y_ref[...] = x_ref[...] + _dot(merged, wo_ref[...])


def _merge(z, oa, ob, oc, x, bon, con, wb, wo, tm):
    m = x.shape[0]
    w = BRANCH_W
    row = lambda width, blk: pl.BlockSpec((tm, width), lambda i: (i, blk))
    const = lambda shape: pl.BlockSpec(shape, lambda i: (0,) * len(shape))
    return pl.pallas_call(
        _merge_kernel,
        grid=(m // tm,),
        in_specs=[row(w, 0), row(w, 0), row(w, Z_BG // w), row(w, 0), row(w, Z_CZ // w),
                  row(D_MODEL, Z_GATES // D_MODEL), row(D_MODEL, Z_GATES // D_MODEL + 1),
                  row(D_MODEL, Z_GATES // D_MODEL + 2), row(D_MODEL, 0),
                  const((1, w)), const((1, w)), const((3, w, D_MODEL)), const((D_MODEL, D_MODEL))],
        out_specs=row(D_MODEL, 0),
        out_shape=jax.ShapeDtypeStruct((m, D_MODEL), F32),
        compiler_params=_cparams(("parallel",)),
        name="merge",
    )(oa, ob, z, oc, z, z, z, z, x, bon, con, wb, wo)


def _ffn_kernel(x_ref, g_ref, wg_ref, wu_ref, wd_ref, y_ref, h_ref, acc_ref):
    f = pl.program_id(1)

    @pl.when(f == 0)
    def _():
        h_ref[...] = _rms(x_ref[...], g_ref[...]).astype(BF16)
        acc_ref[...] = jnp.zeros_like(acc_ref)

    h = h_ref[...]
    a = _silu(jnp.dot(h, wg_ref[...], preferred_element_type=F32)) * jnp.dot(h, wu_ref[...],
                                                                              preferred_element_type=F32)
    acc_ref[...] += jnp.dot(a.astype(BF16), wd_ref[...], preferred_element_type=F32)

    @pl.when(f == pl.num_programs(1) - 1)
    def _():
        y_ref[...] = x_ref[...] + acc_ref[...]


def _ffn(x, gain, wg, wu, wd, tm, tf):
    m, d = x.shape
    ff = wg.shape[1]
    return pl.pallas_call(
        _ffn_kernel,
        grid=(m // tm, ff // tf),
        in_specs=[pl.BlockSpec((tm, d), lambda i, f: (i, 0)),
                  pl.BlockSpec((1, d), lambda i, f: (0, 0)),
                  pl.BlockSpec((d, tf), lambda i, f: (0, f)),
                  pl.BlockSpec((d, tf), lambda i, f: (0, f)),
                  pl.BlockSpec((tf, d), lambda i, f: (f, 0))],
        out_specs=pl.BlockSpec((tm, d), lambda i, f: (i, 0)),
        out_shape=jax.ShapeDtypeStruct((m, d), F32),
        scratch_shapes=[pltpu.VMEM((tm, d), BF16), pltpu.VMEM((tm, d), F32)],
        compiler_params=_cparams(("parallel", "arbitrary")),
        name="ffn",
    )(x, gain.reshape(1, d), wg, wu, wd)


def _moe_kernel(x_ref, g_ref, rt_ref, wg_ref, wu_ref, wd_ref, y_ref, h_ref, acc_ref, gate_ref):
    e = pl.program_id(1)
    f = pl.program_id(2)
    lane = lax.broadcasted_iota(jnp.int32, (1, 128), 1).astype(F32)

    @pl.when(jnp.logical_and(e == 0, f == 0))
    def _():
        hf = _rms(x_ref[...], g_ref[...])
        h_ref[...] = hf.astype(BF16)
        acc_ref[...] = jnp.zeros_like(acc_ref)
        valid = lane < N_EXPERTS
        logits = jnp.where(valid, _dot(hf, rt_ref[...]), NEG)
        ex = jnp.exp(logits - jnp.max(logits, axis=-1, keepdims=True))
        probs = ex / jnp.sum(ex, axis=-1, keepdims=True)
        m1 = jnp.max(probs, axis=-1, keepdims=True)
        i1 = jnp.min(jnp.where(jnp.logical_and(probs == m1, valid), lane, 128.0), axis=-1, keepdims=True)
        hot1 = lane == i1
        rest = jnp.where(jnp.logical_or(hot1, jnp.logical_not(valid)), -1.0, probs)
        m2 = jnp.max(rest, axis=-1, keepdims=True)
        i2 = jnp.min(jnp.where(rest == m2, lane, 128.0), axis=-1, keepdims=True)
        hot2 = lane == i2
        den = m1 + m2
        gate_ref[...] = jnp.where(hot1, m1 / den, 0.0) + jnp.where(hot2, m2 / den, 0.0)

    h = h_ref[...]
    ge = jnp.sum(jnp.where(lane == e.astype(F32), gate_ref[...], 0.0), axis=-1, keepdims=True)
    a = (_silu(jnp.dot(h, wg_ref[...], preferred_element_type=F32))
         * jnp.dot(h, wu_ref[...], preferred_element_type=F32) * ge)
    acc_ref[...] += jnp.dot(a.astype(BF16), wd_ref[...], preferred_element_type=F32)

    @pl.when(jnp.logical_and(e == pl.num_programs(1) - 1, f == pl.num_programs(2) - 1))
    def _():
        y_ref[...] = x_ref[...] + acc_ref[...]


def _moe(x, gain, router_pad, wg, wu, wd, tm, tf):
    m, d = x.shape
    ne, _, ff = wg.shape
    return pl.pallas_call(
        _moe_kernel,
        grid=(m // tm, ne, ff // tf),
        in_specs=[pl.BlockSpec((tm, d), lambda i, e, f: (i, 0)),
                  pl.BlockSpec((1, d), lambda i, e, f: (0, 0)),
                  pl.BlockSpec((d, 128), lambda i, e, f: (0, 0)),
                  pl.BlockSpec((None, d, tf), lambda i, e, f: (e, 0, f)),
                  pl.BlockSpec((None, d, tf), lambda i, e, f: (e, 0, f)),
                  pl.BlockSpec((None, tf, d), lambda i, e, f: (e, f, 0))],
        out_specs=pl.BlockSpec((tm, d), lambda i, e, f: (i, 0)),
        out_shape=jax.ShapeDtypeStruct((m, d), F32),
        scratch_shapes=[pltpu.VMEM((tm, d), BF16), pltpu.VMEM((tm, d), F32), pltpu.VMEM((tm, 128), F32)],
        compiler_params=_cparams(("parallel", "arbitrary", "arbitrary")),
        name="moe",
    )(x, gain.reshape(1, d), router_pad, wg, wu, wd)


MOE_ROWS = 288


def _moe_routed_kernel(x_ref, g_ref, rt_ref, wg_ref, wu_ref, wd_ref, y_ref,
                       h_ref, gate_scr, mask_scr, rank_scr, tri_scr):
    e = pl.program_id(1)
    tm = x_ref.shape[0]
    sub = lax.broadcasted_iota(jnp.int32, (N_EXPERTS, 1), 0).astype(F32)

    @pl.when(e == 0)
    def _():
        x = x_ref[...]
        hf = _rms(x, g_ref[...])
        h_ref[...] = hf.astype(BF16)
        y_ref[...] = x
        logits = _dot_nt(rt_ref[...], hf)
        ex = jnp.exp(logits - jnp.max(logits, axis=0, keepdims=True))
        probs = ex / jnp.sum(ex, axis=0, keepdims=True)
        m1 = jnp.max(probs, axis=0, keepdims=True)
        hot1 = sub == jnp.min(jnp.where(probs == m1, sub, float(N_EXPERTS)), axis=0, keepdims=True)
        rest = jnp.where(hot1, -1.0, probs)
        m2 = jnp.max(rest, axis=0, keepdims=True)
        hot2 = sub == jnp.min(jnp.where(rest == m2, sub, float(N_EXPERTS)), axis=0, keepdims=True)
        den = m1 + m2
        gate_scr[...] = jnp.where(hot1, m1 / den, 0.0) + jnp.where(hot2, m2 / den, 0.0)
        mask = jnp.where(jnp.logical_or(hot1, hot2), 1.0, 0.0)
        mask_scr[...] = mask
        tri_scr[...] = jnp.where(lax.broadcasted_iota(jnp.int32, (tm, tm), 0)
                                 <= lax.broadcasted_iota(jnp.int32, (tm, tm), 1), 1.0, 0.0).astype(BF16)
        rank_scr[...] = jnp.dot(mask.astype(BF16), tri_scr[...], preferred_element_type=F32) - 1.0

    mine = sub == e.astype(F32)
    row = lambda ref: jnp.sum(jnp.where(mine, ref[...], 0.0), axis=0, keepdims=True)
    gate_row, mask_row, rank_row = row(gate_scr), row(mask_scr), row(rank_scr)
    count = jnp.sum(mask_row).astype(jnp.int32)
    slot = lax.broadcasted_iota(jnp.int32, (MOE_ROWS, 1), 0).astype(F32)

    def chunk(j, carry):
        base = (j * MOE_ROWS).astype(F32)
        pick = jnp.where(jnp.logical_and(rank_row == slot + base, mask_row > 0.0), 1.0, 0.0)
        pick16 = pick.astype(BF16)
        xg = jnp.dot(pick16, h_ref[...], preferred_element_type=F32).astype(BF16)
        gate_col = jnp.sum(pick * gate_row, axis=1, keepdims=True)
        a = (_silu(jnp.dot(xg, wg_ref[...], preferred_element_type=F32))
             * jnp.dot(xg, wu_ref[...], preferred_element_type=F32) * gate_col)
        out = jnp.dot(a.astype(BF16), wd_ref[...], preferred_element_type=F32)
        y_ref[...] += _dot_tn(pick16, out)
        return carry

    lax.fori_loop(0, (count + MOE_ROWS - 1) // MOE_ROWS, chunk, 0)


def _moe_routed(x, gain, router_t, wg, wu, wd, tm):
    m, d = x.shape
    ne, _, ff = wg.shape
    return pl.pallas_call(
        _moe_routed_kernel,
        grid=(m // tm, ne),
        in_specs=[pl.BlockSpec((tm, d), lambda i, e: (i, 0)),
                  pl.BlockSpec((1, d), lambda i, e: (0, 0)),
                  pl.BlockSpec((ne, d), lambda i, e: (0, 0)),
                  pl.BlockSpec((None, d, ff), lambda i, e: (e, 0, 0)),
                  pl.BlockSpec((None, d, ff), lambda i, e: (e, 0, 0)),
                  pl.BlockSpec((None, ff, d), lambda i, e: (e, 0, 0))],
        out_specs=pl.BlockSpec((tm, d), lambda i, e: (i, 0)),
        out_shape=jax.ShapeDtypeStruct((m, d), F32),
        scratch_shapes=[pltpu.VMEM((tm, d), BF16), pltpu.VMEM((ne, tm), F32), pltpu.VMEM((ne, tm), F32),
                        pltpu.VMEM((ne, tm), F32), pltpu.VMEM((tm, tm), BF16)],
        compiler_params=_cparams(("parallel", "arbitrary")),
        name="moe_routed",
    )(x, gain.reshape(1, d), router_t, wg, wu, wd)


def _attn_sample_kernel(*refs, layer, full_roll):
    zq_ref, qg_ref, kg_ref, b0_ref, b1_ref, b2_ref, bias0_ref, c0_ref, c1_ref, c2_ref = refs[:10]
    if full_roll:
        tails = (None,) * N_GROUPS
        rest = refs[10:]
    else:
        tails = refs[10:13]
        rest = refs[13:]
    o_ref, n0_ref, n1_ref, n2_ref, p0_scr, p1_scr, p2_scr, st_scr, oacc_scr = rest
    half = pl.program_id(1)
    active = (pl.program_id(2) == layer) if full_roll else True
    aw = A_WIDTH
    sel = (lax.broadcasted_iota(jnp.int32, (A_HEADS, aw), 1) // A_DH
           == lax.broadcasted_iota(jnp.int32, (A_HEADS, aw), 0))
    last_lane = lax.broadcasted_iota(jnp.int32, (1, 128), 1) == 127
    groups = ((c0_ref, n0_ref, b0_ref, p0_scr, tails[0]), (c1_ref, n1_ref, b1_ref, p1_scr, tails[1]),
              (c2_ref, n2_ref, b2_ref, p2_scr, tails[2]))
    for g, (c_ref, n_ref, b_ref, p_scr, t_ref) in enumerate(groups):
        wlen = c_ref.shape[1]
        if full_roll:
            n_ref[...] = pltpu.roll(c_ref[...], wlen - 1, axis=1)

        def newest(col, n_ref=n_ref, t_ref=t_ref, wlen=wlen):
            old = n_ref[:, wlen - 128:] if full_roll else t_ref[...]
            dst = n_ref.at[:, wlen - 128:] if full_roll else n_ref
            dst[...] = jnp.where(last_lane, col, old)

        lo = g * aw

        @pl.when(jnp.logical_and(active, half == 0))
        def _(g=g, c_ref=c_ref, b_ref=b_ref, p_scr=p_scr, lo=lo, newest=newest):
            q = zq_ref[:, Z_AQ + lo:Z_AQ + lo + aw]
            k = zq_ref[:, Z_AK + lo:Z_AK + lo + aw]
            def head_rms(x, gain):
                s8 = jnp.sum(jnp.where(sel, jnp.broadcast_to(x * x, (A_HEADS, aw)), 0.0), axis=1, keepdims=True)
                ms = jnp.sum(jnp.where(sel, s8, 0.0), axis=0, keepdims=True) * (1.0 / A_DH)
                return x * lax.rsqrt(ms + EPS) * gain

            qn = head_rms(q, qg_ref[...]) * (A_DH ** -0.5)
            kn = head_rms(k, kg_ref[...])
            qbd = jnp.where(sel, jnp.broadcast_to(qn, (A_HEADS, aw)), 0.0)
            lc = jnp.sum((c_ref[...] * _to_col(qn, aw)).reshape(A_HEADS, A_DH, c_ref.shape[1]), axis=1) + b_ref[...]
            l0 = jnp.sum(qbd * kn, axis=1, keepdims=True) + bias0_ref[g]
            m = jnp.maximum(jnp.max(lc, axis=1, keepdims=True), l0)
            pc = jnp.exp(lc - m)
            p0 = jnp.exp(l0 - m)
            p_scr[...] = pc
            st_scr[g, 0] = jnp.broadcast_to(p0, (A_HEADS, 128))
            st_scr[g, 1] = jnp.broadcast_to(jnp.sum(pc, axis=1, keepdims=True) + p0, (A_HEADS, 128))
            st_scr[g, 2] = jnp.broadcast_to(m, (A_HEADS, 128))
            newest(_to_col(kn, aw))

        @pl.when(jnp.logical_and(active, half == 1))
        def _(g=g, c_ref=c_ref, p_scr=p_scr, lo=lo, newest=newest):
            v = zq_ref[:, Z_AV + lo:Z_AV + lo + aw]
            p0 = st_scr[g, 0][:, 0:1]
            s = st_scr[g, 1][:, 0:1]
            m = st_scr[g, 2][:, 0:1]
            o8 = (_dot_nt(p_scr[...], c_ref[...]) + _r16(p0) * _r16(v)) / s
            oacc_scr[g, 0:1, :] = jnp.sum(jnp.where(sel, o8, 0.0), axis=0, keepdims=True)
            oacc_scr[g, 1:2, :] = jnp.sum(jnp.where(sel, m + jnp.log(s), 0.0), axis=0, keepdims=True)
            newest(_to_col(v, aw))

    @pl.when(jnp.logical_and(active, half == 1))
    def _():
        l_1, l_2, l_3 = oacc_scr[0, 1:2, :], oacc_scr[1, 1:2, :], oacc_scr[2, 1:2, :]
        mx = jnp.maximum(jnp.maximum(l_1, l_2), l_3)
        e_1, e_2, e_3 = jnp.exp(l_1 - mx), jnp.exp(l_2 - mx), jnp.exp(l_3 - mx)
        o_ref[...] = ((e_1 * oacc_scr[0, 0:1, :] + e_2 * oacc_scr[1, 0:1, :] + e_3 * oacc_scr[2, 0:1, :])
                      / (e_1 + e_2 + e_3))


def _attn_sample(zs3, caches_t, rolled, layer, qgain, kgain, biases, bias0):
    bd = zs3.shape[0]
    full_roll = rolled is None
    aw = A_WIDTH
    if full_roll:
        grid = (bd, 2, DEPTH)
        const = lambda shape: pl.BlockSpec(shape, lambda i, h, l: (0,) * len(shape))
        zspec = pl.BlockSpec((None, 1, Z_AV + N_GROUPS * aw), lambda i, h, l: (i, 0, 0))
        ospec = pl.BlockSpec((None, 1, aw), lambda i, h, l: (i, 0, 0))
        cache_specs = [pl.BlockSpec((None, None, None, aw, c.shape[-1]), lambda i, h, l: (l, i, h, 0, 0))
                       for c in caches_t]
        new_specs = cache_specs
        extra_specs, extra_args, aliases = [], [], {}
        sem = ("parallel", "arbitrary", "arbitrary")
    else:
        grid = (bd, 2)
        const = lambda shape: pl.BlockSpec(shape, lambda i, h: (0,) * len(shape))
        zspec = pl.BlockSpec((None, 1, Z_AV + N_GROUPS * aw), lambda i, h: (i, 0, 0))
        ospec = pl.BlockSpec((None, 1, aw), lambda i, h: (i, 0, 0))
        cache_specs = [pl.BlockSpec((None, None, None, aw, c.shape[-1]), lambda i, h: (layer, i, h, 0, 0))
                       for c in caches_t]
        new_specs = [pl.BlockSpec((None, None, None, aw, 128), functools.partial(
            lambda last, i, h: (layer, i, h, 0, last), c.shape[-1] // 128 - 1)) for c in caches_t]
        extra_specs, extra_args = new_specs, list(rolled)
        aliases = {10 + g: 1 + g for g in range(N_GROUPS)}
        sem = ("parallel", "arbitrary")
    return pl.pallas_call(
        functools.partial(_attn_sample_kernel, layer=layer, full_roll=full_roll),
        grid=grid,
        in_specs=[zspec, const((1, aw)), const((1, aw))]
                 + [const((A_HEADS, c.shape[-1])) for c in caches_t] + [const((N_GROUPS, A_HEADS, 1))]
                 + cache_specs + extra_specs,
        out_specs=[ospec] + new_specs,
        out_shape=[jax.ShapeDtypeStruct((bd, 1, aw), F32)]
                  + [jax.ShapeDtypeStruct(c.shape, c.dtype) for c in caches_t],
        scratch_shapes=[pltpu.VMEM((A_HEADS, c.shape[-1]), F32) for c in caches_t]
                       + [pltpu.VMEM((N_GROUPS, 3, A_HEADS, 128), F32), pltpu.VMEM((N_GROUPS, 8, aw), F32)],
        input_output_aliases=aliases,
        compiler_params=_cparams(sem),
        name="attn_sample",
    )(zs3, qgain, kgain, *biases, bias0, *caches_t, *extra_args)


def _to_col(row, n):
    eye = lax.broadcasted_iota(jnp.int32, (n, n), 0) == lax.broadcasted_iota(jnp.int32, (n, n), 1)
    return jnp.sum(jnp.where(eye, jnp.broadcast_to(row, (n, n)), 0.0), axis=1, keepdims=True)


def _sample_mix_kernel(bq_ref, bk_ref, bv_ref, cqkv_ref, sm_ref, cbuf_ref, sgla_ref, sgdn_ref,
                       wlr_ref, lrb_ref, cw_ref, dtb_ref, alog_ref,
                       ob_ref, oc_ref, nconv_ref, ngla_ref, ngdn_ref):
    sm = sm_ref[...]
    la =_log_sigmoid(_dot(sm, wlr_ref[...]) + lrb_ref[...]) * (1.0 / B_TAU)
    q = bq_ref[...] * (B_DK ** -0.5)
    k = bk_ref[...]
    v = bv_ref[...]
    for h in range(B_HEADS):
        lk = slice(B_DK * h, B_DK * (h + 1))
        lv = slice(B_DV * h, B_DV * (h + 1))
        s_old = sgla_ref[h]
        decay = jnp.exp(la[:, lk])
        ngla_ref[h] = _to_col(decay, B_DK) * s_old + _to_col(k[:, lk], B_DK) * v[:, lv]
        att = jnp.sum(q[:, lk] * k[:, lk], axis=-1, keepdims=True)
        ob_ref[:, lv] = (jnp.sum(_to_col(q[:, lk] * decay, B_DK) * s_old, axis=0, keepdims=True)
                         + att * v[:, lv])

    cat = jnp.concatenate([cbuf_ref[...], cqkv_ref[...]], axis=0)
    acc = None
    for j in range(CONV_W):
        term = cat[j:j + 1, :] * cw_ref[j:j + 1, :]
        acc = term if acc is None else acc + term
    y = _silu(acc)
    nconv_ref[...] = cat[1:CONV_W, :]
    hd = C_HEADS * C_DK
    for h in range(C_HEADS):
        ld = slice(C_DK * h, C_DK * (h + 1))
        qh = y[:, C_DK * h:C_DK * (h + 1)]
        qh = qh * lax.rsqrt(jnp.sum(qh * qh, axis=-1, keepdims=True) + EPS) * (C_DK ** -0.5)
        kh = y[:, hd + C_DK * h:hd + C_DK * (h + 1)]
        kh = kh * lax.rsqrt(jnp.sum(kh * kh, axis=-1, keepdims=True) + EPS)
        vh = y[:, 2 * hd + C_DV * h:2 * hd + C_DV * (h + 1)]
        beta = _sigmoid(sm[:, SM_CB + h:SM_CB + h + 1])
        g = -jnp.exp(alog_ref[:, ld]) * _softplus(sm[:, SM_CA + h:SM_CA + h + 1] + dtb_ref[:, ld])
        eg = jnp.exp(g)
        s = sgdn_ref[h]
        kcol = _to_col(kh, C_DK)
        u = beta * (vh - eg * jnp.sum(kcol * s, axis=0, keepdims=True))
        qs = jnp.sum(_to_col(qh, C_DK) * s, axis=0, keepdims=True)
        oc_ref[:, ld] = eg * qs + jnp.sum(qh * kh, axis=-1, keepdims=True) * u
        ngdn_ref[h] = eg * s + kcol * u


def _sample_mix(zs3, cbuf, sgla, sgdn, wlr_pad, lr_bias, conv_w, dtb_bc, alog_bc):
    bd = zs3.shape[0]
    hk, hv, hd = B_HEADS * B_DK, B_HEADS * B_DV, C_HEADS * C_DV
    const = lambda shape: pl.BlockSpec(shape, lambda i: (0,) * len(shape))
    zrow = lambda width, off: pl.BlockSpec((None, 1, width), lambda i: (i, 0, off // width))
    vec = lambda width: pl.BlockSpec((None, 1, width), lambda i: (i, 0, 0))
    st = lambda shape: pl.BlockSpec((None,) + shape, lambda i: (i,) + (0,) * len(shape))
    return pl.pallas_call(
        _sample_mix_kernel,
        grid=(bd,),
        in_specs=[zrow(hk, Z_BQ), zrow(hk, Z_BK), zrow(hv, Z_BV), zrow(C_CONV_CH, Z_CQKV), zrow(SMALL_W, Z_SMALL),
                  st((CONV_W - 1, C_CONV_CH)), st((B_HEADS, B_DK, B_DV)), st((C_HEADS, C_DK, C_DV))]
                 + [const((SMALL_W, hk)), const((1, hk)), const((CONV_W, C_CONV_CH)), const((1, hd)), const((1, hd))],
        out_specs=[vec(hv), vec(hd), st((CONV_W - 1, C_CONV_CH)),
                   st((B_HEADS, B_DK, B_DV)), st((C_HEADS, C_DK, C_DV))],
        out_shape=[jax.ShapeDtypeStruct((bd, 1, hv), F32),
                   jax.ShapeDtypeStruct((bd, 1, hd), F32), jax.ShapeDtypeStruct((bd, CONV_W - 1, C_CONV_CH), F32),
                   jax.ShapeDtypeStruct((bd, B_HEADS, B_DK, B_DV), F32),
                   jax.ShapeDtypeStruct((bd, C_HEADS, C_DK, C_DV), F32)],
        compiler_params=_cparams(("parallel",)),
        name="sample_mix",
    )(zs3, zs3, zs3, zs3, zs3, cbuf, sgla, sgdn, wlr_pad, lr_bias, conv_w, dtb_bc, alog_bc)


def _t5_bucket(dist):
    max_exact = N_BUCKETS // 2
    d = jnp.maximum(dist.astype(F32), 1.0)
    large = max_exact + (jnp.log(d / max_exact) / math.log(MAX_DIST / max_exact)
                         * (N_BUCKETS - max_exact)).astype(jnp.int32)
    large = jnp.minimum(large, N_BUCKETS - 1)
    return jnp.where(dist < max_exact, dist, large).astype(jnp.int32)


def _bias_tables(rel_bias):
    i = jnp.arange(SPAN)[:, None]
    c = jnp.arange(2 * SPAN)[None, :]
    dist = i + SPAN - c
    valid = (dist >= 0) & (dist <= SPAN)

    def lookup(bias_g, idx):
        hot = (idx[..., None] == jnp.arange(N_BUCKETS)).astype(F32)
        return jnp.einsum("...k,kh->...h", hot, bias_g.astype(F32), precision=HI)

    prompt, cached, new = [], [], []
    for g, (win, dil) in enumerate(A_GROUPS):
        bias_g = rel_bias[:, g * A_HEADS:(g + 1) * A_HEADS]
        tbl = lookup(bias_g, _t5_bucket(jnp.maximum(dist, 0) * dil)).transpose(2, 0, 1)
        prompt.append(jnp.where(valid[None], tbl, NEG))
        w = jnp.arange(win)
        tbl_s = lookup(bias_g, _t5_bucket(win - w)).T
        cached.append(jnp.where((w % dil == 0)[None], tbl_s, NEG))
        new.append(lookup(bias_g, _t5_bucket(jnp.zeros((1,), jnp.int32))).T)
    return jnp.stack(prompt), cached, jnp.stack(new)


def _prep_w_in(w):
    offs = np.concatenate([[0], np.cumsum(IN_SIZES)])
    seg = lambda i: w[:, offs[i]:offs[i + 1]]
    order = (0, 1, 2, 3, 4, 5, 6, 8, 9, 12, 7, 10, 11)
    used = sum(IN_SIZES)
    parts = [seg(i) for i in order] + [jnp.zeros((w.shape[0], NZ - used), w.dtype)]
    return jnp.concatenate(parts, axis=1).astype(BF16)


def _selector(offset):
    sel = np.zeros((SMALL_W, C_HEADS * C_DV), np.float32)
    for h in range(C_HEADS):
        sel[offset + h, C_DV * h:C_DV * (h + 1)] = 1.0
    return jnp.asarray(sel)


def kernel(x_prompt, x_sample, cache_win128_kv, cache_win512_kv, cache_win2048_kv, state_gla, state_gdn, state_conv, norm_mix, w_in, a_q_norm, a_k_norm, rel_bias, b_w_lr, b_lr_bias, b_out_norm, c_conv, c_a_log, c_dt_bias, c_out_norm, w_branch, w_out, norm_ffn, ffn_w_gate, ffn_w_up, ffn_w_down, moe_router, moe_w_gate, moe_w_up, moe_w_down):
    bp, t, d = x_prompt.shape
    bs = x_sample.shape[0]
    mp = bp * t
    caches_t = [jnp.transpose(c, (0, 1, 3, 4, 5, 2)).reshape(c.shape[:2] + (2, A_WIDTH, c.shape[2]))
                for c in (cache_win128_kv, cache_win512_kv, cache_win2048_kv)]
    bias_p, bias_c, bias_n = _bias_tables(rel_bias)
    selb, sela = _selector(SM_CB), _selector(SM_CA)
    rolled = windows = None

    xp = x_prompt.reshape(mp, d)
    xs = x_sample.reshape(bs, d)
    outs_p = {k: [] for k in ("w0", "w1", "w2", "gla", "gdn", "conv")}
    outs_s = {k: [] for k in ("w0", "w1", "w2", "gla", "gdn", "conv")}
    for l in range(DEPTH):
        w_in_l = _prep_w_in(w_in[l])
        wb = w_branch[l].astype(BF16)
        wo = w_out[l].astype(BF16)
        qg128 = jnp.tile(a_q_norm[l], 2).reshape(1, 2 * A_DH)
        kg128 = jnp.tile(a_k_norm[l], 2).reshape(1, 2 * A_DH)
        qg512 = jnp.tile(a_q_norm[l], A_HEADS).reshape(1, A_WIDTH)
        kg512 = jnp.tile(a_k_norm[l], A_HEADS).reshape(1, A_WIDTH)
        wlr_pad = jnp.zeros((SMALL_W, B_HEADS * B_DK), F32).at[:B_RANK].set(b_w_lr[l])
        lr_bias = b_lr_bias[l].reshape(1, -1)
        bon = jnp.tile(b_out_norm[l], B_HEADS).reshape(1, -1)
        con = jnp.tile(c_out_norm[l], C_HEADS).reshape(1, -1)
        dtb_bc = jnp.repeat(c_dt_bias[l], C_DV).reshape(1, -1)
        alog_bc = jnp.repeat(c_a_log[l], C_DV).reshape(1, -1)

        z = _norm_matmul(xp, norm_mix[l], w_in_l, tm=1024, tn=2304)
        z3 = z.reshape(bp, t, NZ)
        o_a, *windows = _attn_prompt(z3, qg128, kg128, bias_p, l, windows)
        o_b, gla_t = _gla_prompt(z3, wlr_pad, lr_bias, tb=512)
        dtb_sm = jnp.zeros((1, SMALL_W), F32).at[0, SM_CA:SM_CA + C_HEADS].set(c_dt_bias[l])
        alog_sm = jnp.zeros((1, SMALL_W), F32).at[0, SM_CA:SM_CA + C_HEADS].set(c_a_log[l])
        o_c, gdn_s = _gdn_prompt(z3, c_conv[l], selb, sela, dtb_sm, alog_sm, tb=512)
        xp = _merge(z, o_a.reshape(mp, -1), o_b.reshape(mp, -1), o_c.reshape(mp, -1), xp, bon, con, wb, wo, tm=256)
        gla = gla_t.reshape(bp, B_HEADS, B_DV, B_HEADS, B_DK)
        gla = jnp.stack([gla[:, h, :, h, :] for h in range(B_HEADS)], axis=1)
        outs_p["gla"].append(jnp.swapaxes(gla, 2, 3))
        outs_p["gdn"].append(gdn_s)
        outs_p["conv"].append(z3[:, t - (CONV_W - 1):, Z_CQKV:Z_CQKV + C_CONV_CH])

        zs = _norm_matmul(xs, norm_mix[l], w_in_l, tm=bs, tn=1280)
        zs3 = zs.reshape(bs, 1, NZ)
        oa_s, *rolled = _attn_sample(zs3, caches_t, rolled, l, qg512, kg512, bias_c, bias_n)
        ob_s, oc_s, nconv, ngla, ngdn = _sample_mix(
            zs3, state_conv[l], state_gla[l], state_gdn[l], wlr_pad, lr_bias, c_conv[l], dtb_bc, alog_bc)
        xs = _merge(zs, oa_s.reshape(bs, -1), ob_s.reshape(bs, -1), oc_s.reshape(bs, -1), xs, bon, con, wb, wo, tm=bs)
        outs_s["gla"].append(ngla)
        outs_s["gdn"].append(ngdn)
        outs_s["conv"].append(nconv)

        i = l // 2
        if l % 2 == 0:
            wg, wu, wd = ffn_w_gate[i].astype(BF16), ffn_w_up[i].astype(BF16), ffn_w_down[i].astype(BF16)
            xp = _ffn(xp, norm_ffn[l], wg, wu, wd, tm=512, tf=1408)
            xs = _ffn(xs, norm_ffn[l], wg, wu, wd, tm=bs, tf=1408)
        else:
            wg, wu, wd = moe_w_gate[i].astype(BF16), moe_w_up[i].astype(BF16), moe_w_down[i].astype(BF16)
            router_pad = jnp.zeros((d, 128), F32).at[:, :N_EXPERTS].set(moe_router[i])
            xp = _moe_routed(xp, norm_ffn[l], moe_router[i].T, wg, wu, wd, tm=1024)
            xs = _moe(xs, norm_ffn[l], router_pad, wg, wu, wd, tm=bs, tf=896)

    st = lambda name, d_: jnp.stack(d_[name])

    def window_out(w):
        w = w.reshape(w.shape[:3] + (A_HEADS, A_DH, w.shape[-1]))
        return jnp.transpose(w, (0, 1, 5, 2, 3, 4))

    return (xp.reshape(bp, t, d), xs.reshape(bs, 1, d),
            window_out(windows[0]), window_out(rolled[0]), window_out(windows[1]), window_out(rolled[1]),
            window_out(windows[2]), window_out(rolled[2]),
            st("gla", outs_p), st("gla", outs_s), st("gdn", outs_p), st("gdn", outs_s),
            st("conv", outs_p), st("conv", outs_s))
```

```python
import functools
import math

import jax
import jax.numpy as jnp
import numpy as np
from jax import lax
from jax.experimental import pallas as pl
from jax.experimental.pallas import tpu as pltpu

F32 = jnp.float32
BF16 = jnp.bfloat16
HI = lax.Precision.HIGHEST
NEG = -1e30

D_MODEL = 1024
DEPTH = 2
EPS = 1e-6
A_GROUPS = ((128, 1), (512, 4), (2048, 16))
N_GROUPS = 3
A_HEADS = 8
A_DH = 64
A_WIDTH = A_HEADS * A_DH
SPAN = 128
ATT_PAIR = 16
GLA_PAIR = 8
GDN_PAIR = 8
N_BUCKETS = 32
MAX_DIST = 2048
B_HEADS = 4
B_DK = 64
B_DV = 128
B_RANK = 16
B_TAU = 16.0
C_HEADS = 4
C_DK = 128
C_DV = 128
CONV_W = 4
C_CONV_CH = 2 * C_HEADS * C_DK + C_HEADS * C_DV
CHUNK = 64
SUB = 16
BRANCH_W = 512
D_FF = 2816
N_EXPERTS = 8
D_FF_EXPERT = 1792
IN_SIZES = (N_GROUPS * A_WIDTH, N_GROUPS * A_WIDTH, N_GROUPS * A_WIDTH,
            B_HEADS * B_DK, B_HEADS * B_DK, B_HEADS * B_DV, B_HEADS * B_DV, B_RANK,
            C_CONV_CH, C_HEADS * C_DV, C_HEADS, C_HEADS, 3 * D_MODEL)

Z_AQ, Z_AK, Z_AV = 0, 1536, 3072
Z_CQKV = 4608
Z_BQ, Z_BK, Z_BV, Z_SMALL = 6144, 6400, 6656, 7168
SMALL_W = 256
NZ = 7680
ZG_GATES, ZG_BG, ZG_CZ = 0, 3072, 3584
NZG = 4096
SM_CB, SM_CA = B_RANK, B_RANK + C_HEADS

VMEM_LIMIT = 56 * 1024 * 1024


def _cparams(sem):
    return pltpu.CompilerParams(dimension_semantics=sem, vmem_limit_bytes=VMEM_LIMIT)


def _dot(a, b):
    return jnp.dot(a.astype(BF16), b.astype(BF16), preferred_element_type=F32)


def _dot_nt(a, b):
    return lax.dot_general(a.astype(BF16), b.astype(BF16), (((1,), (1,)), ((), ())), preferred_element_type=F32)


def _dot_tn(a, b):
    return lax.dot_general(a.astype(BF16), b.astype(BF16), (((0,), (0,)), ((), ())), preferred_element_type=F32)


def _split3(x):
    x1 = x.astype(BF16)
    r1 = x - x1.astype(F32)
    x2 = r1.astype(BF16)
    return x1, x2, (r1 - x2.astype(F32)).astype(BF16)


def _mask_dot(mask, x):
    m = mask.astype(BF16)
    x1, x2, x3 = _split3(x)
    return (jnp.dot(m, x1, preferred_element_type=F32) + jnp.dot(m, x2, preferred_element_type=F32)
            + jnp.dot(m, x3, preferred_element_type=F32))


def _dot_mask(x, mask):
    m = mask.astype(BF16)
    x1, x2, x3 = _split3(x)
    return (jnp.dot(x1, m, preferred_element_type=F32) + jnp.dot(x2, m, preferred_element_type=F32)
            + jnp.dot(x3, m, preferred_element_type=F32))


def _r16(x):
    return x.astype(BF16).astype(F32)


def _sigmoid(x):
    return jax.nn.sigmoid(x)


def _silu(x):
    return x * jax.nn.sigmoid(x)


def _softplus(x):
    return jnp.maximum(x, 0.0) + jnp.log1p(jnp.exp(-jnp.abs(x)))


def _log_sigmoid(x):
    return jnp.minimum(x, 0.0) - jnp.log1p(jnp.exp(-jnp.abs(x)))


def _rms(x, gain):
    return x * lax.rsqrt(jnp.mean(x * x, axis=-1, keepdims=True) + EPS) * gain


def _norm_matmul_kernel(x_ref, g_ref, w_ref, o_ref, h_ref):
    @pl.when(pl.program_id(1) == 0)
    def _():
        h_ref[...] = _rms(x_ref[...], g_ref[...]).astype(BF16)

    o_ref[...] = jnp.dot(h_ref[...], w_ref[...], preferred_element_type=F32).astype(o_ref.dtype)


def _norm_matmul(x, gain, w, tm, tn, out_dtype=F32):
    m, d = x.shape
    n = w.shape[1]
    return pl.pallas_call(
        _norm_matmul_kernel,
        grid=(m // tm, n // tn),
        in_specs=[pl.BlockSpec((tm, d), lambda i, j: (i, 0)),
                  pl.BlockSpec((1, d), lambda i, j: (0, 0)),
                  pl.BlockSpec((d, tn), lambda i, j: (0, j))],
        out_specs=pl.BlockSpec((tm, tn), lambda i, j: (i, j)),
        out_shape=jax.ShapeDtypeStruct((m, n), out_dtype),
        scratch_shapes=[pltpu.VMEM((tm, d), BF16)],
        compiler_params=_cparams(("parallel", "arbitrary")),
        name="norm_matmul",
    )(x, gain.reshape(1, d), w)


def _pair_rms(x, gain, head0):
    sq = x * x
    s0 = jnp.sum(jnp.where(head0, sq, 0.0), axis=-1, keepdims=True)
    s1 = jnp.sum(sq, axis=-1, keepdims=True) - s0
    return x * lax.rsqrt(jnp.where(head0, s0, s1) * (1.0 / A_DH) + EPS) * gain


def _attn_prompt_kernel(q1, k1, v1, q2, k2, v2, q3, k3, v3, qg_ref, kg_ref, bias_ref,
                        o_ref, w1, w2, w3, qt, kt, qs, ks, vs, og, lg, to, tl, *, slots):
    t_len = o_ref.shape[0]
    head0 = lax.broadcasted_iota(jnp.int32, (1, 2 * A_DH), 1) < A_DH
    first_half = lax.broadcasted_iota(jnp.int32, (1, 2 * SPAN), 1) < SPAN
    groups = ((q1, k1, v1, w1), (q2, k2, v2, w2), (q3, k3, v3, w3))
    for g, (qr, kr, vr, wr) in enumerate(groups):
        dil = A_GROUPS[g][1]
        length = t_len // dil
        nb = length // SPAN
        qt[...] = _pair_rms(qr[...], qg_ref[...], head0) * (A_DH ** -0.5)
        kt[...] = _pair_rms(kr[...], kg_ref[...], head0)
        wlen = wr.shape[-1]
        k_win = kt[t_len - wlen:, :].T
        v_win = vr[t_len - wlen:, :].T
        for s in range(slots):
            wr[s, 0] = k_win
            wr[s, 1] = v_win
        knr = kt
        for r in range(dil):
            rows = pl.ds(r, length, stride=dil) if dil > 1 else pl.ds(0, length)
            dst = pl.ds(r * length, length)
            qs[dst, :] = qt[rows, :].astype(BF16)
            ks[dst, :] = knr[rows, :].astype(BF16)
            vs[dst, :] = vr[rows, :].astype(BF16)

        if nb > 1:
            bias2 = jnp.concatenate([bias_ref[g, 0], bias_ref[g, 1]], axis=0)
        else:
            bias2 = jnp.concatenate([bias_ref[g, 0, :, SPAN:], bias_ref[g, 1, :, SPAN:]], axis=0)

        def block(jj, carry, nb=nb, bias2=bias2):
            row0s, k2s, v2s, logits = [], [], [], []
            for i in range(ATT_PAIR):
                j = jj * ATT_PAIR + i
                row0 = pl.multiple_of(j * SPAN, SPAN)
                qb = qs[pl.ds(row0, SPAN), :]
                qst = jnp.concatenate([jnp.where(head0, qb, jnp.zeros_like(qb)),
                                       jnp.where(head0, jnp.zeros_like(qb), qb)], axis=0)
                if nb > 1:
                    prow = pl.multiple_of(jnp.maximum(row0 - SPAN, 0), SPAN)
                    k2 = jnp.concatenate([ks[pl.ds(prow, SPAN), :], ks[pl.ds(row0, SPAN), :]], axis=0)
                    v2 = jnp.concatenate([vs[pl.ds(prow, SPAN), :], vs[pl.ds(row0, SPAN), :]], axis=0)
                else:
                    k2 = ks[pl.ds(row0, SPAN), :]
                    v2 = vs[pl.ds(row0, SPAN), :]
                lg_ = lax.dot_general(qst, k2, (((1,), (1,)), ((), ())), preferred_element_type=F32) + bias2
                if nb > 1:
                    lg_ = lg_ + jnp.where(first_half, jnp.where(j % nb == 0, NEG, 0.0), 0.0)
                row0s.append(row0)
                k2s.append(k2)
                v2s.append(v2)
                logits.append(lg_)
            lg_all = jnp.concatenate(logits, axis=0)
            m = jnp.max(lg_all, axis=-1, keepdims=True)
            p = jnp.exp(lg_all - m)
            s = jnp.sum(p, axis=-1, keepdims=True)
            p16 = p.astype(BF16)
            lse = m + jnp.log(s)
            for i in range(ATT_PAIR):
                lo = 2 * SPAN * i
                acc = jnp.dot(p16[lo:lo + 2 * SPAN], v2s[i], preferred_element_type=F32) / s[lo:lo + 2 * SPAN]
                to[pl.ds(row0s[i], SPAN), :] = jnp.where(head0, acc[:SPAN], acc[SPAN:])
                tl[pl.ds(row0s[i], SPAN), :] = jnp.where(head0, jnp.broadcast_to(lse[lo:lo + SPAN], (SPAN, 2 * A_DH)),
                                                         jnp.broadcast_to(lse[lo + SPAN:lo + 2 * SPAN], (SPAN, 2 * A_DH)))
            return carry

        lax.fori_loop(0, t_len // (SPAN * ATT_PAIR), block, 0)
        for r in range(dil):
            rows = pl.ds(r, length, stride=dil) if dil > 1 else pl.ds(0, length)
            src = pl.ds(r * length, length)
            og[g, rows, :] = to[src, :]
            lg[g, rows, :] = tl[src, :]
    l_1, l_2, l_3 = lg[0], lg[1], lg[2]
    mx = jnp.maximum(jnp.maximum(l_1, l_2), l_3)
    e_1, e_2, e_3 = jnp.exp(l_1 - mx), jnp.exp(l_2 - mx), jnp.exp(l_3 - mx)
    o_ref[...] = (e_1 * og[0] + e_2 * og[1] + e_3 * og[2]) / (e_1 + e_2 + e_3)


def _attn_prompt(z3, qgain, kgain, bias, layer, windows):
    b, t, _ = z3.shape
    hp_blocks = A_WIDTH // 128

    def col(base, g):
        return lambda i, hp: (i, 0, base // 128 + g * hp_blocks + hp)

    in_specs = []
    for g in range(N_GROUPS):
        for base in (Z_AQ, Z_AK, Z_AV):
            in_specs.append(pl.BlockSpec((None, t, 128), col(base, g)))
    in_specs += [pl.BlockSpec((1, 128), lambda i, hp: (0, 0)),
                 pl.BlockSpec((1, 128), lambda i, hp: (0, 0)),
                 pl.BlockSpec((N_GROUPS, 2, SPAN, 2 * SPAN), lambda i, hp: (0, hp, 0, 0))]
    args = [z3] * 9 + [qgain, kgain, bias]
    slots = DEPTH if windows is None else 1
    win_specs, win_shapes = [], []
    for win, _ in A_GROUPS:
        wp = min(win, t)
        win_specs.append(pl.BlockSpec((slots, None, 2, 128, wp), lambda i, hp: (layer // slots, i, 0, hp, 0)))
        win_shapes.append(jax.ShapeDtypeStruct((DEPTH, b, 2, A_WIDTH, wp), F32))
    aliases = {}
    if windows is not None:
        aliases = {len(args) + g: 1 + g for g in range(N_GROUPS)}
        in_specs += [pl.BlockSpec(memory_space=pl.ANY)] * N_GROUPS
        args += list(windows)
    kern = functools.partial(_attn_prompt_kernel, slots=slots)
    if windows is not None:
        kern = functools.partial(_drop_refs, kern, len(args) - N_GROUPS, N_GROUPS)
    return pl.pallas_call(
        kern,
        grid=(b, hp_blocks),
        in_specs=in_specs,
        out_specs=[pl.BlockSpec((None, t, 128), lambda i, hp: (i, 0, hp))] + win_specs,
        out_shape=[jax.ShapeDtypeStruct((b, t, A_WIDTH), F32)] + win_shapes,
        scratch_shapes=[pltpu.VMEM((t, 128), F32), pltpu.VMEM((t, 128), F32),
                        pltpu.VMEM((t, 128), BF16), pltpu.VMEM((t, 128), BF16), pltpu.VMEM((t, 128), BF16),
                        pltpu.VMEM((N_GROUPS, t, 128), F32), pltpu.VMEM((N_GROUPS, t, 128), F32),
                        pltpu.VMEM((t, 128), F32), pltpu.VMEM((t, 128), F32)],
        input_output_aliases=aliases,
        compiler_params=_cparams(("parallel", "parallel")),
        name="attn_prompt",
    )(*args)


def _drop_refs(kern, start, count, *refs):
    return kern(*refs[:start], *refs[start + count:])


def _gla_prompt_kernel(q_ref, k_ref, v_ref, sm_ref, wlr_ref, lrb_ref, o_ref, st_ref, s_scr, la_scr):
    tb = q_ref.shape[0]
    hk = B_HEADS * B_DK
    hv = B_HEADS * B_DV

    @pl.when(pl.program_id(1) == 0)
    def _():
        s_scr[...] = jnp.zeros_like(s_scr)

    la_scr[...] = _log_sigmoid(_dot(sm_ref[...], wlr_ref[...]) + lrb_ref[...]) * (1.0 / B_TAU)

    rr = lax.broadcasted_iota(jnp.int32, (CHUNK, CHUNK), 0)
    cc = lax.broadcasted_iota(jnp.int32, (CHUNK, CHUNK), 1)
    causal = rr >= cc
    ltri = causal.astype(F32)
    lane_head = lax.broadcasted_iota(jnp.int32, (1, hk), 1) // B_DK
    row = lax.broadcasted_iota(jnp.int32, (CHUNK, 1), 0)
    same_head = (lax.broadcasted_iota(jnp.int32, (hv, hk), 0) // B_DV
                 == lax.broadcasted_iota(jnp.int32, (hv, hk), 1) // B_DK)

    def intra(r0):
        b = _mask_dot(ltri, la_scr[pl.ds(r0, CHUNK), :])
        q = q_ref[pl.ds(r0, CHUNK), :] * (B_DK ** -0.5)
        k = k_ref[pl.ds(r0, CHUNK), :]
        vb = v_ref[pl.ds(r0, CHUNK), :].astype(BF16)
        b_last = b[CHUNK - 1:CHUNK, :]
        atts = []
        for i in range(CHUNK // SUB):
            b_ref0 = b[SUB * i:SUB * i + 1, :]
            qi = q[SUB * i:SUB * (i + 1), :] * jnp.exp(b[SUB * i:SUB * (i + 1), :] - b_ref0)
            qst = jnp.concatenate([jnp.where(lane_head == h, qi, 0.0) for h in range(B_HEADS)], axis=0)
            kt = k * jnp.exp(jnp.where(row < SUB * (i + 1), b_ref0 - b, 0.0))
            atts.append(_dot_nt(qst, kt))
        o_parts = []
        for h in range(B_HEADS):
            att = jnp.concatenate([a[SUB * h:SUB * (h + 1), :] for a in atts], axis=0)
            att = jnp.where(causal, att, 0.0)
            o_parts.append(_dot(att, vb[:, B_DV * h:B_DV * (h + 1)]))
        upd = jnp.where(same_head, _dot_tn(vb, k * jnp.exp(b_last - b)), 0.0)
        return (q * jnp.exp(b)).astype(BF16), jnp.concatenate(o_parts, axis=1), jnp.exp(b_last), upd

    def chunks(cc, carry):
        r0s = [pl.multiple_of((cc * GLA_PAIR + i) * CHUNK, CHUNK) for i in range(GLA_PAIR)]
        parts = [intra(r0) for r0 in r0s]
        for r0, (qe, o_intra, decay, upd) in zip(r0s, parts):
            s = s_scr[...]
            o_ref[pl.ds(r0, CHUNK), :] = _dot_nt(qe, s) + o_intra
            s_scr[...] = s * decay + upd
        return carry

    lax.fori_loop(0, tb // (CHUNK * GLA_PAIR), chunks, 0)

    @pl.when(pl.program_id(1) == pl.num_programs(1) - 1)
    def _():
        st_ref[...] = s_scr[...]


def _gla_prompt(z3, wlr_pad, lr_bias, tb):
    b, t, _ = z3.shape
    hk, hv = B_HEADS * B_DK, B_HEADS * B_DV
    return pl.pallas_call(
        _gla_prompt_kernel,
        grid=(b, t // tb),
        in_specs=[pl.BlockSpec((None, tb, hk), lambda i, j: (i, j, Z_BQ // hk)),
                  pl.BlockSpec((None, tb, hk), lambda i, j: (i, j, Z_BK // hk)),
                  pl.BlockSpec((None, tb, hv), lambda i, j: (i, j, Z_BV // hv)),
                  pl.BlockSpec((None, tb, SMALL_W), lambda i, j: (i, j, Z_SMALL // SMALL_W)),
                  pl.BlockSpec((SMALL_W, hk), lambda i, j: (0, 0)),
                  pl.BlockSpec((1, hk), lambda i, j: (0, 0))],
        out_specs=[pl.BlockSpec((None, tb, hv), lambda i, j: (i, j, 0)),
                   pl.BlockSpec((None, hv, hk), lambda i, j: (i, 0, 0))],
        out_shape=[jax.ShapeDtypeStruct((b, t, hv), F32), jax.ShapeDtypeStruct((b, hv, hk), F32)],
        scratch_shapes=[pltpu.VMEM((hv, hk), F32), pltpu.VMEM((tb, hk), F32)],
        compiler_params=_cparams(("parallel", "arbitrary")),
        name="gla_prompt",
    )(z3, z3, z3, z3, wlr_pad, lr_bias)


def _gdn_prompt_kernel(x_ref, sm_ref, cw_ref, selb_ref, sela_ref, dtb_ref, alog_ref, o_ref, s_ref,
                       xpad, cv, gsc, bsc, s_scr):
    tb = x_ref.shape[0]
    hd = C_HEADS * C_DK
    pad = 8

    @pl.when(pl.program_id(1) == 0)
    def _():
        s_scr[...] = jnp.zeros_like(s_scr)
        xpad[0:pad, :] = jnp.zeros((pad, C_CONV_CH), F32)

    xpad[pad:pad + tb, :] = x_ref[...]
    rb = 128
    for i in range(tb // rb):
        acc = None
        for j in range(CONV_W):
            lo = pad - (CONV_W - 1) + j + i * rb
            term = xpad[lo:lo + rb, :] * cw_ref[j:j + 1, :]
            acc = term if acc is None else acc + term
        y = _silu(acc)
        rows = slice(i * rb, (i + 1) * rb)
        for h in range(C_HEADS):
            qh = y[:, C_DK * h:C_DK * (h + 1)]
            cv[rows, C_DK * h:C_DK * (h + 1)] = (
                qh * lax.rsqrt(jnp.sum(qh * qh, axis=-1, keepdims=True) + EPS) * (C_DK ** -0.5))
            kh = y[:, hd + C_DK * h:hd + C_DK * (h + 1)]
            cv[rows, hd + C_DK * h:hd + C_DK * (h + 1)] = (
                kh * lax.rsqrt(jnp.sum(kh * kh, axis=-1, keepdims=True) + EPS))
        cv[rows, 2 * hd:] = y[:, 2 * hd:]
    xpad[pad - (CONV_W - 1):pad, :] = xpad[pad + tb - (CONV_W - 1):pad + tb, :]

    sm = sm_ref[...]
    gsc[...] = _dot_mask(-jnp.exp(alog_ref[...]) * _softplus(sm + dtb_ref[...]), sela_ref[...])
    bsc[...] = _dot_mask(_sigmoid(sm), selb_ref[...])

    n = C_HEADS * CHUNK
    rr = lax.broadcasted_iota(jnp.int32, (CHUNK, CHUNK), 0)
    cc = lax.broadcasted_iota(jnp.int32, (CHUNK, CHUNK), 1)
    ltri = (rr >= cc).astype(F32)
    ri = lax.broadcasted_iota(jnp.int32, (n, n), 0)
    ci = lax.broadcasted_iota(jnp.int32, (n, n), 1)
    same = (ri // CHUNK) == (ci // CHUNK)
    incl = jnp.logical_and(same, ci <= ri)
    strict = jnp.logical_and(same, ci < ri)
    eye = (ri == ci).astype(F32)

    def stack(x):
        return jnp.concatenate([x[:, C_DK * h:C_DK * (h + 1)] for h in range(C_HEADS)], axis=0)

    def chunks(cc, carry):
        pair = range(GDN_PAIR)
        r0s = [pl.multiple_of((cc * GDN_PAIR + i) * CHUNK, CHUNK) for i in pair]
        bst = [stack(_mask_dot(ltri, gsc[pl.ds(r0, CHUNK), :])) for r0 in r0s]
        beta = [stack(bsc[pl.ds(r0, CHUNK), :]) for r0 in r0s]
        qst = [stack(cv[pl.ds(r0, CHUNK), 0:hd]) for r0 in r0s]
        kst = [stack(cv[pl.ds(r0, CHUNK), hd:2 * hd]) for r0 in r0s]
        vst = [stack(cv[pl.ds(r0, CHUNK), 2 * hd:3 * hd]) for r0 in r0s]
        kb = [k.astype(BF16) for k in kst]
        nt = (((1,), (1,)), ((), ()))
        kk = [lax.dot_general(kb[i], kb[i], nt, preferred_element_type=F32) for i in pair]
        qk = [lax.dot_general(qst[i].astype(BF16), kb[i], nt, preferred_element_type=F32) for i in pair]
        dec, a, aqk = [], [], []
        for i in pair:
            bst_t = bst[i].T
            d_ = jnp.exp(jnp.where(incl, jnp.concatenate([bst[i], bst[i]], axis=1)
                                   - jnp.concatenate([bst_t, bst_t], axis=0), NEG))
            dec.append(d_)
            a.append(jnp.where(strict, jnp.concatenate([beta[i], beta[i]], axis=1) * d_ * kk[i], 0.0))
            aqk.append(d_ * qk[i])
        inv = [eye - a[i] for i in pair]
        pw = a
        for _ in range(int(math.log2(CHUNK)) - 1):
            pw = [_dot(pw[i], pw[i]) for i in pair]
            inv = [inv[i] + _dot(inv[i], pw[i]) for i in pair]
        eb = [jnp.exp(bst[i]) for i in pair]
        wu = [_dot(inv[i], jnp.concatenate([beta[i] * eb[i] * kst[i], beta[i] * vst[i]], axis=1)) for i in pair]
        for i in pair:
            w, uv = wu[i][:, :C_DK], wu[i][:, C_DK:]
            us, qss = [], []
            for h in range(C_HEADS):
                sl = slice(CHUNK * h, CHUNK * (h + 1))
                ws = _dot(jnp.concatenate([w[sl], qst[i][sl]], axis=0), s_scr[h])
                us.append(uv[sl] - ws[:CHUNK])
                qss.append(ws[CHUNK:])
            o = eb[i] * jnp.concatenate(qss, axis=0) + _dot(aqk[i], jnp.concatenate(us, axis=0))
            for h in range(C_HEADS):
                sl = slice(CHUNK * h, CHUNK * (h + 1))
                o_ref[pl.ds(r0s[i], CHUNK), C_DV * h:C_DV * (h + 1)] = o[sl]
                b_last = bst[i][CHUNK * (h + 1) - 1:CHUNK * (h + 1), :]
                s_scr[h] = (jnp.exp(b_last) * s_scr[h]
                            + _dot_tn(kst[i][sl] * jnp.exp(b_last - bst[i][sl]), us[h]))
        return carry

    lax.fori_loop(0, tb // (CHUNK * GDN_PAIR), chunks, 0)

    @pl.when(pl.program_id(1) == pl.num_programs(1) - 1)
    def _():
        s_ref[...] = s_scr[...]


def _gdn_prompt(z3, conv_w, selb, sela, dtb_bc, alog_bc, tb):
    b, t, _ = z3.shape
    hd = C_HEADS * C_DV
    const = lambda shape: pl.BlockSpec(shape, lambda i, j: (0,) * len(shape))
    return pl.pallas_call(
        _gdn_prompt_kernel,
        grid=(b, t // tb),
        in_specs=[pl.BlockSpec((None, tb, C_CONV_CH), lambda i, j: (i, j, Z_CQKV // C_CONV_CH)),
                  pl.BlockSpec((None, tb, SMALL_W), lambda i, j: (i, j, Z_SMALL // SMALL_W)),
                  const((CONV_W, C_CONV_CH)), const((SMALL_W, hd)), const((SMALL_W, hd)),
                  const((1, SMALL_W)), const((1, SMALL_W))],
        out_specs=[pl.BlockSpec((None, tb, hd), lambda i, j: (i, j, 0)),
                   pl.BlockSpec((None, C_HEADS, C_DK, C_DV), lambda i, j: (i, 0, 0, 0))],
        out_shape=[jax.ShapeDtypeStruct((b, t, hd), F32),
                   jax.ShapeDtypeStruct((b, C_HEADS, C_DK, C_DV), F32)],
        scratch_shapes=[pltpu.VMEM((tb + 8, C_CONV_CH), F32), pltpu.VMEM((tb, C_CONV_CH), F32),
                        pltpu.VMEM((tb, hd), F32), pltpu.VMEM((tb, hd), F32),
                        pltpu.VMEM((C_HEADS, C_DK, C_DV), F32)],
        compiler_params=_cparams(("parallel", "arbitrary")),
        name="gdn_prompt",
    )(z3, z3, conv_w, selb, sela, dtb_bc, alog_bc)


def _head_norm(o, gain, width):
    parts = []
    for h in range(o.shape[1] // width):
        oh = o[:, width * h:width * (h + 1)]
        parts.append(oh * lax.rsqrt(jnp.mean(oh * oh, axis=-1, keepdims=True) + EPS))
    return jnp.concatenate(parts, axis=1) * gain


def _merge_kernel(oa_ref, ob_ref, bg_ref, oc_ref, cz_ref, ga_ref, gb_ref, gc_ref, x_ref,
                  bon_ref, con_ref, wb_ref, wo_ref, y_ref):
    f32 = lambda ref: ref[...].astype(F32)
    ob = _head_norm(ob_ref[...], bon_ref[...], B_DV) * _silu(f32(bg_ref))
    oc = _head_norm(oc_ref[...], con_ref[...], C_DV) * _silu(f32(cz_ref))
    merged = (_sigmoid(f32(ga_ref)) * _dot(oa_ref[...], wb_ref[0])
              + _sigmoid(f32(gb_ref)) * _dot(ob, wb_ref[1])
              + _sigmoid(f32(gc_ref)) * _dot(oc, wb_ref[2]))
    y_ref[...] = x_ref[...] + _dot(merged, wo_ref[...])


def _merge(zg, oa, ob, oc, x, bon, con, wb, wo, tm):
    m = x.shape[0]
    w = BRANCH_W
    row = lambda width, blk: pl.BlockSpec((tm, width), lambda i: (i, blk))
    const = lambda shape: pl.BlockSpec(shape, lambda i: (0,) * len(shape))
    return pl.pallas_call(
        _merge_kernel,
        grid=(m // tm,),
        in_specs=[row(w, 0), row(w, 0), row(w, ZG_BG // w), row(w, 0), row(w, ZG_CZ // w),
                  row(D_MODEL, ZG_GATES // D_MODEL), row(D_MODEL, ZG_GATES // D_MODEL + 1),
                  row(D_MODEL, ZG_GATES // D_MODEL + 2), row(D_MODEL, 0),
                  const((1, w)), const((1, w)), const((3, w, D_MODEL)), const((D_MODEL, D_MODEL))],
        out_specs=row(D_MODEL, 0),
        out_shape=jax.ShapeDtypeStruct((m, D_MODEL), F32),
        compiler_params=_cparams(("parallel",)),
        name="merge",
    )(oa, ob, zg, oc, zg, zg, zg, zg, x, bon, con, wb, wo)


def _ffn_kernel(x_ref, g_ref, wg_ref, wu_ref, wd_ref, y_ref, h_ref, acc_ref):
    f = pl.program_id(1)

    @pl.when(f == 0)
    def _():
        h_ref[...] = _rms(x_ref[...], g_ref[...]).astype(BF16)
        acc_ref[...] = jnp.zeros_like(acc_ref)

    h = h_ref[...]
    a = _silu(jnp.dot(h, wg_ref[...], preferred_element_type=F32)) * jnp.dot(h, wu_ref[...],
                                                                              preferred_element_type=F32)
    acc_ref[...] += jnp.dot(a.astype(BF16), wd_ref[...], preferred_element_type=F32)

    @pl.when(f == pl.num_programs(1) - 1)
    def _():
        y_ref[...] = x_ref[...] + acc_ref[...]


def _ffn(x, gain, wg, wu, wd, tm, tf):
    m, d = x.shape
    ff = wg.shape[1]
    return pl.pallas_call(
        _ffn_kernel,
        grid=(m // tm, ff // tf),
        in_specs=[pl.BlockSpec((tm, d), lambda i, f: (i, 0)),
                  pl.BlockSpec((1, d), lambda i, f: (0, 0)),
                  pl.BlockSpec((d, tf), lambda i, f: (0, f)),
                  pl.BlockSpec((d, tf), lambda i, f: (0, f)),
                  pl.BlockSpec((tf, d), lambda i, f: (f, 0))],
        out_specs=pl.BlockSpec((tm, d), lambda i, f: (i, 0)),
        out_shape=jax.ShapeDtypeStruct((m, d), F32),
        scratch_shapes=[pltpu.VMEM((tm, d), BF16), pltpu.VMEM((tm, d), F32)],
        compiler_params=_cparams(("parallel", "arbitrary")),
        name="ffn",
    )(x, gain.reshape(1, d), wg, wu, wd)


def _moe_kernel(x_ref, g_ref, rt_ref, wg_ref, wu_ref, wd_ref, y_ref, h_ref, acc_ref, gate_ref):
    e = pl.program_id(1)
    f = pl.program_id(2)
    lane = lax.broadcasted_iota(jnp.int32, (1, 128), 1).astype(F32)

    @pl.when(jnp.logical_and(e == 0, f == 0))
    def _():
        hf = _rms(x_ref[...], g_ref[...])
        h_ref[...] = hf.astype(BF16)
        acc_ref[...] = jnp.zeros_like(acc_ref)
        valid = lane < N_EXPERTS
        logits = jnp.where(valid, _dot(hf, rt_ref[...]), NEG)
        ex = jnp.exp(logits - jnp.max(logits, axis=-1, keepdims=True))
        probs = ex / jnp.sum(ex, axis=-1, keepdims=True)
        m1 = jnp.max(probs, axis=-1, keepdims=True)
        i1 = jnp.min(jnp.where(jnp.logical_and(probs == m1, valid), lane, 128.0), axis=-1, keepdims=True)
        hot1 = lane == i1
        rest = jnp.where(jnp.logical_or(hot1, jnp.logical_not(valid)), -1.0, probs)
        m2 = jnp.max(rest, axis=-1, keepdims=True)
        i2 = jnp.min(jnp.where(rest == m2, lane, 128.0), axis=-1, keepdims=True)
        hot2 = lane == i2
        den = m1 + m2
        gate_ref[...] = jnp.where(hot1, m1 / den, 0.0) + jnp.where(hot2, m2 / den, 0.0)

    h = h_ref[...]
    ge = jnp.sum(jnp.where(lane == e.astype(F32), gate_ref[...], 0.0), axis=-1, keepdims=True)
    a = (_silu(jnp.dot(h, wg_ref[...], preferred_element_type=F32))
         * jnp.dot(h, wu_ref[...], preferred_element_type=F32) * ge)
    acc_ref[...] += jnp.dot(a.astype(BF16), wd_ref[...], preferred_element_type=F32)

    @pl.when(jnp.logical_and(e == pl.num_programs(1) - 1, f == pl.num_programs(2) - 1))
    def _():
        y_ref[...] = x_ref[...] + acc_ref[...]


def _moe(x, gain, router_pad, wg, wu, wd, tm, tf):
    m, d = x.shape
    ne, _, ff = wg.shape
    return pl.pallas_call(
        _moe_kernel,
        grid=(m // tm, ne, ff // tf),
        in_specs=[pl.BlockSpec((tm, d), lambda i, e, f: (i, 0)),
                  pl.BlockSpec((1, d), lambda i, e, f: (0, 0)),
                  pl.BlockSpec((d, 128), lambda i, e, f: (0, 0)),
                  pl.BlockSpec((None, d, tf), lambda i, e, f: (e, 0, f)),
                  pl.BlockSpec((None, d, tf), lambda i, e, f: (e, 0, f)),
                  pl.BlockSpec((None, tf, d), lambda i, e, f: (e, f, 0))],
        out_specs=pl.BlockSpec((tm, d), lambda i, e, f: (i, 0)),
        out_shape=jax.ShapeDtypeStruct((m, d), F32),
        scratch_shapes=[pltpu.VMEM((tm, d), BF16), pltpu.VMEM((tm, d), F32), pltpu.VMEM((tm, 128), F32)],
        compiler_params=_cparams(("parallel", "arbitrary", "arbitrary")),
        name="moe",
    )(x, gain.reshape(1, d), router_pad, wg, wu, wd)


MOE_ROWS = 288


def _moe_routed_kernel(x_ref, g_ref, rt_ref, wg_ref, wu_ref, wd_ref, y_ref,
                       h_ref, gate_scr, mask_scr, rank_scr, tri_scr):
    e = pl.program_id(1)
    tm = x_ref.shape[0]
    sub = lax.broadcasted_iota(jnp.int32, (N_EXPERTS, 1), 0).astype(F32)

    @pl.when(e == 0)
    def _():
        x = x_ref[...]
        hf = _rms(x, g_ref[...])
        h_ref[...] = hf.astype(BF16)
        y_ref[...] = x
        logits = _dot_nt(rt_ref[...], hf)
        ex = jnp.exp(logits - jnp.max(logits, axis=0, keepdims=True))
        probs = ex / jnp.sum(ex, axis=0, keepdims=True)
        m1 = jnp.max(probs, axis=0, keepdims=True)
        hot1 = sub == jnp.min(jnp.where(probs == m1, sub, float(N_EXPERTS)), axis=0, keepdims=True)
        rest = jnp.where(hot1, -1.0, probs)
        m2 = jnp.max(rest, axis=0, keepdims=True)
        hot2 = sub == jnp.min(jnp.where(rest == m2, sub, float(N_EXPERTS)), axis=0, keepdims=True)
        den = m1 + m2
        gate_scr[...] = jnp.where(hot1, m1 / den, 0.0) + jnp.where(hot2, m2 / den, 0.0)
        mask = jnp.where(jnp.logical_or(hot1, hot2), 1.0, 0.0)
        mask_scr[...] = mask
        tri_scr[...] = jnp.where(lax.broadcasted_iota(jnp.int32, (tm, tm), 0)
                                 <= lax.broadcasted_iota(jnp.int32, (tm, tm), 1), 1.0, 0.0).astype(BF16)
        rank_scr[...] = jnp.dot(mask.astype(BF16), tri_scr[...], preferred_element_type=F32) - 1.0

    mine = sub == e.astype(F32)
    row = lambda ref: jnp.sum(jnp.where(mine, ref[...], 0.0), axis=0, keepdims=True)
    gate_row, mask_row, rank_row = row(gate_scr), row(mask_scr), row(rank_scr)
    count = jnp.sum(mask_row).astype(jnp.int32)
    slot = lax.broadcasted_iota(jnp.int32, (MOE_ROWS, 1), 0).astype(F32)

    def chunk(j, carry):
        base = (j * MOE_ROWS).astype(F32)
        pick = jnp.where(jnp.logical_and(rank_row == slot + base, mask_row > 0.0), 1.0, 0.0)
        pick16 = pick.astype(BF16)
        xg = jnp.dot(pick16, h_ref[...], preferred_element_type=F32).astype(BF16)
        gate_col = jnp.sum(pick * gate_row, axis=1, keepdims=True)
        a = (_silu(jnp.dot(xg, wg_ref[...], preferred_element_type=F32))
             * jnp.dot(xg, wu_ref[...], preferred_element_type=F32) * gate_col)
        out = jnp.dot(a.astype(BF16), wd_ref[...], preferred_element_type=F32)
        y_ref[...] += _dot_tn(pick16, out)
        return carry

    lax.fori_loop(0, (count + MOE_ROWS - 1) // MOE_ROWS, chunk, 0)


def _moe_routed(x, gain, router_t, wg, wu, wd, tm):
    m, d = x.shape
    ne, _, ff = wg.shape
    return pl.pallas_call(
        _moe_routed_kernel,
        grid=(m // tm, ne),
        in_specs=[pl.BlockSpec((tm, d), lambda i, e: (i, 0)),
                  pl.BlockSpec((1, d), lambda i, e: (0, 0)),
                  pl.BlockSpec((ne, d), lambda i, e: (0, 0)),
                  pl.BlockSpec((None, d, ff), lambda i, e: (e, 0, 0)),
                  pl.BlockSpec((None, d, ff), lambda i, e: (e, 0, 0)),
                  pl.BlockSpec((None, ff, d), lambda i, e: (e, 0, 0))],
        out_specs=pl.BlockSpec((tm, d), lambda i, e: (i, 0)),
        out_shape=jax.ShapeDtypeStruct((m, d), F32),
        scratch_shapes=[pltpu.VMEM((tm, d), BF16), pltpu.VMEM((ne, tm), F32), pltpu.VMEM((ne, tm), F32),
                        pltpu.VMEM((ne, tm), F32), pltpu.VMEM((tm, tm), BF16)],
        compiler_params=_cparams(("parallel", "arbitrary")),
        name="moe_routed",
    )(x, gain.reshape(1, d), router_t, wg, wu, wd)


def _attn_sample_kernel(*refs, layer, full_roll):
    zq_ref, qg_ref, kg_ref, b0_ref, b1_ref, b2_ref, bias0_ref, c0_ref, c1_ref, c2_ref = refs[:10]
    if full_roll:
        tails = (None,) * N_GROUPS
        rest = refs[10:]
    else:
        tails = refs[10:13]
        rest = refs[13:]
    o_ref, n0_ref, n1_ref, n2_ref, p0_scr, p1_scr, p2_scr, st_scr, oacc_scr = rest
    half = pl.program_id(1)
    active = (pl.program_id(2) == layer) if full_roll else True
    aw = A_WIDTH
    sel = (lax.broadcasted_iota(jnp.int32, (A_HEADS, aw), 1) // A_DH
           == lax.broadcasted_iota(jnp.int32, (A_HEADS, aw), 0))
    last_lane = lax.broadcasted_iota(jnp.int32, (1, 128), 1) == 127
    groups = ((c0_ref, n0_ref, b0_ref, p0_scr, tails[0]), (c1_ref, n1_ref, b1_ref, p1_scr, tails[1]),
              (c2_ref, n2_ref, b2_ref, p2_scr, tails[2]))
    for g, (c_ref, n_ref, b_ref, p_scr, t_ref) in enumerate(groups):
        wlen = c_ref.shape[1]
        if full_roll:
            n_ref[...] = pltpu.roll(c_ref[...], wlen - 1, axis=1)

        def newest(col, n_ref=n_ref, t_ref=t_ref, wlen=wlen):
            old = n_ref[:, wlen - 128:] if full_roll else t_ref[...]
            dst = n_ref.at[:, wlen - 128:] if full_roll else n_ref
            dst[...] = jnp.where(last_lane, col, old)

        lo = g * aw

        @pl.when(jnp.logical_and(active, half == 0))
        def _(g=g, c_ref=c_ref, b_ref=b_ref, p_scr=p_scr, lo=lo, newest=newest):
            q = zq_ref[:, Z_AQ + lo:Z_AQ + lo + aw]
            k = zq_ref[:, Z_AK + lo:Z_AK + lo + aw]
            def head_rms(x, gain):
                s8 = jnp.sum(jnp.where(sel, jnp.broadcast_to(x * x, (A_HEADS, aw)), 0.0), axis=1, keepdims=True)
                ms = jnp.sum(jnp.where(sel, s8, 0.0), axis=0, keepdims=True) * (1.0 / A_DH)
                return x * lax.rsqrt(ms + EPS) * gain

            qn = head_rms(q, qg_ref[...]) * (A_DH ** -0.5)
            kn = head_rms(k, kg_ref[...])
            qbd = jnp.where(sel, jnp.broadcast_to(qn, (A_HEADS, aw)), 0.0)
            lc = jnp.sum((c_ref[...] * _to_col(qn, aw)).reshape(A_HEADS, A_DH, c_ref.shape[1]), axis=1) + b_ref[...]
            l0 = jnp.sum(qbd * kn, axis=1, keepdims=True) + bias0_ref[g]
            m = jnp.maximum(jnp.max(lc, axis=1, keepdims=True), l0)
            pc = jnp.exp(lc - m)
            p0 = jnp.exp(l0 - m)
            p_scr[...] = pc
            st_scr[g, 0] = jnp.broadcast_to(p0, (A_HEADS, 128))
            st_scr[g, 1] = jnp.broadcast_to(jnp.sum(pc, axis=1, keepdims=True) + p0, (A_HEADS, 128))
            st_scr[g, 2] = jnp.broadcast_to(m, (A_HEADS, 128))
            newest(_to_col(kn, aw))

        @pl.when(jnp.logical_and(active, half == 1))
        def _(g=g, c_ref=c_ref, p_scr=p_scr, lo=lo, newest=newest):
            v = zq_ref[:, Z_AV + lo:Z_AV + lo + aw]
            p0 = st_scr[g, 0][:, 0:1]
            s = st_scr[g, 1][:, 0:1]
            m = st_scr[g, 2][:, 0:1]
            o8 = (_dot_nt(p_scr[...], c_ref[...]) + _r16(p0) * _r16(v)) / s
            oacc_scr[g, 0:1, :] = jnp.sum(jnp.where(sel, o8, 0.0), axis=0, keepdims=True)
            oacc_scr[g, 1:2, :] = jnp.sum(jnp.where(sel, m + jnp.log(s), 0.0), axis=0, keepdims=True)
            newest(_to_col(v, aw))

    @pl.when(jnp.logical_and(active, half == 1))
    def _():
        l_1, l_2, l_3 = oacc_scr[0, 1:2, :], oacc_scr[1, 1:2, :], oacc_scr[2, 1:2, :]
        mx = jnp.maximum(jnp.maximum(l_1, l_2), l_3)
        e_1, e_2, e_3 = jnp.exp(l_1 - mx), jnp.exp(l_2 - mx), jnp.exp(l_3 - mx)
        o_ref[...] = ((e_1 * oacc_scr[0, 0:1, :] + e_2 * oacc_scr[1, 0:1, :] + e_3 * oacc_scr[2, 0:1, :])
                      / (e_1 + e_2 + e_3))


def _attn_sample(zs3, caches_t, rolled, layer, qgain, kgain, biases, bias0):
    bd = zs3.shape[0]
    full_roll = rolled is None
    aw = A_WIDTH
    if full_roll:
        grid = (bd, 2, DEPTH)
        const = lambda shape: pl.BlockSpec(shape, lambda i, h, l: (0,) * len(shape))
        zspec = pl.BlockSpec((None, 1, Z_AV + N_GROUPS * aw), lambda i, h, l: (i, 0, 0))
        ospec = pl.BlockSpec((None, 1, aw), lambda i, h, l: (i, 0, 0))
        cache_specs = [pl.BlockSpec((None, None, None, aw, c.shape[-1]), lambda i, h, l: (l, i, h, 0, 0))
                       for c in caches_t]
        new_specs = cache_specs
        extra_specs, extra_args, aliases = [], [], {}
        sem = ("parallel", "arbitrary", "arbitrary")
    else:
        grid = (bd, 2)
        const = lambda shape: pl.BlockSpec(shape, lambda i, h: (0,) * len(shape))
        zspec = pl.BlockSpec((None, 1, Z_AV + N_GROUPS * aw), lambda i, h: (i, 0, 0))
        ospec = pl.BlockSpec((None, 1, aw), lambda i, h: (i, 0, 0))
        cache_specs = [pl.BlockSpec((None, None, None, aw, c.shape[-1]), lambda i, h: (layer, i, h, 0, 0))
                       for c in caches_t]
        new_specs = [pl.BlockSpec((None, None, None, aw, 128), functools.partial(
            lambda last, i, h: (layer, i, h, 0, last), c.shape[-1] // 128 - 1)) for c in caches_t]
        extra_specs, extra_args = new_specs, list(rolled)
        aliases = {10 + g: 1 + g for g in range(N_GROUPS)}
        sem = ("parallel", "arbitrary")
    return pl.pallas_call(
        functools.partial(_attn_sample_kernel, layer=layer, full_roll=full_roll),
        grid=grid,
        in_specs=[zspec, const((1, aw)), const((1, aw))]
                 + [const((A_HEADS, c.shape[-1])) for c in caches_t] + [const((N_GROUPS, A_HEADS, 1))]
                 + cache_specs + extra_specs,
        out_specs=[ospec] + new_specs,
        out_shape=[jax.ShapeDtypeStruct((bd, 1, aw), F32)]
                  + [jax.ShapeDtypeStruct(c.shape, c.dtype) for c in caches_t],
        scratch_shapes=[pltpu.VMEM((A_HEADS, c.shape[-1]), F32) for c in caches_t]
                       + [pltpu.VMEM((N_GROUPS, 3, A_HEADS, 128), F32), pltpu.VMEM((N_GROUPS, 8, aw), F32)],
        input_output_aliases=aliases,
        compiler_params=_cparams(sem),
        name="attn_sample",
    )(zs3, qgain, kgain, *biases, bias0, *caches_t, *extra_args)


def _to_col(row, n):
    eye = lax.broadcasted_iota(jnp.int32, (n, n), 0) == lax.broadcasted_iota(jnp.int32, (n, n), 1)
    return jnp.sum(jnp.where(eye, jnp.broadcast_to(row, (n, n)), 0.0), axis=1, keepdims=True)


def _sample_mix_kernel(bq_ref, bk_ref, bv_ref, cqkv_ref, sm_ref, cbuf_ref, sgla_ref, sgdn_ref,
                       wlr_ref, lrb_ref, cw_ref, dtb_ref, alog_ref,
                       ob_ref, oc_ref, nconv_ref, ngla_ref, ngdn_ref):
    sm = sm_ref[...]
    la =_log_sigmoid(_dot(sm, wlr_ref[...]) + lrb_ref[...]) * (1.0 / B_TAU)
    q = bq_ref[...] * (B_DK ** -0.5)
    k = bk_ref[...]
    v = bv_ref[...]
    for h in range(B_HEADS):
        lk = slice(B_DK * h, B_DK * (h + 1))
        lv = slice(B_DV * h, B_DV * (h + 1))
        s_old = sgla_ref[h]
        decay = jnp.exp(la[:, lk])
        ngla_ref[h] = _to_col(decay, B_DK) * s_old + _to_col(k[:, lk], B_DK) * v[:, lv]
        att = jnp.sum(q[:, lk] * k[:, lk], axis=-1, keepdims=True)
        ob_ref[:, lv] = (jnp.sum(_to_col(q[:, lk] * decay, B_DK) * s_old, axis=0, keepdims=True)
                         + att * v[:, lv])

    cat = jnp.concatenate([cbuf_ref[...], cqkv_ref[...]], axis=0)
    acc = None
    for j in range(CONV_W):
        term = cat[j:j + 1, :] * cw_ref[j:j + 1, :]
        acc = term if acc is None else acc + term
    y = _silu(acc)
    nconv_ref[...] = cat[1:CONV_W, :]
    hd = C_HEADS * C_DK
    for h in range(C_HEADS):
        ld = slice(C_DK * h, C_DK * (h + 1))
        qh = y[:, C_DK * h:C_DK * (h + 1)]
        qh = qh * lax.rsqrt(jnp.sum(qh * qh, axis=-1, keepdims=True) + EPS) * (C_DK ** -0.5)
        kh = y[:, hd + C_DK * h:hd + C_DK * (h + 1)]
        kh = kh * lax.rsqrt(jnp.sum(kh * kh, axis=-1, keepdims=True) + EPS)
        vh = y[:, 2 * hd + C_DV * h:2 * hd + C_DV * (h + 1)]
        beta = _sigmoid(sm[:, SM_CB + h:SM_CB + h + 1])
        g = -jnp.exp(alog_ref[:, ld]) * _softplus(sm[:, SM_CA + h:SM_CA + h + 1] + dtb_ref[:, ld])
        eg = jnp.exp(g)
        s = sgdn_ref[h]
        kcol = _to_col(kh, C_DK)
        u = beta * (vh - eg * jnp.sum(kcol * s, axis=0, keepdims=True))
        qs = jnp.sum(_to_col(qh, C_DK) * s, axis=0, keepdims=True)
        oc_ref[:, ld] = eg * qs + jnp.sum(qh * kh, axis=-1, keepdims=True) * u
        ngdn_ref[h] = eg * s + kcol * u


def _sample_mix(zs3, cbuf, sgla, sgdn, wlr_pad, lr_bias, conv_w, dtb_bc, alog_bc):
    bd = zs3.shape[0]
    hk, hv, hd = B_HEADS * B_DK, B_HEADS * B_DV, C_HEADS * C_DV
    const = lambda shape: pl.BlockSpec(shape, lambda i: (0,) * len(shape))
    zrow = lambda width, off: pl.BlockSpec((None, 1, width), lambda i: (i, 0, off // width))
    vec = lambda width: pl.BlockSpec((None, 1, width), lambda i: (i, 0, 0))
    st = lambda shape: pl.BlockSpec((None,) + shape, lambda i: (i,) + (0,) * len(shape))
    return pl.pallas_call(
        _sample_mix_kernel,
        grid=(bd,),
        in_specs=[zrow(hk, Z_BQ), zrow(hk, Z_BK), zrow(hv, Z_BV), zrow(C_CONV_CH, Z_CQKV), zrow(SMALL_W, Z_SMALL),
                  st((CONV_W - 1, C_CONV_CH)), st((B_HEADS, B_DK, B_DV)), st((C_HEADS, C_DK, C_DV))]
                 + [const((SMALL_W, hk)), const((1, hk)), const((CONV_W, C_CONV_CH)), const((1, hd)), const((1, hd))],
        out_specs=[vec(hv), vec(hd), st((CONV_W - 1, C_CONV_CH)),
                   st((B_HEADS, B_DK, B_DV)), st((C_HEADS, C_DK, C_DV))],
        out_shape=[jax.ShapeDtypeStruct((bd, 1, hv), F32),
                   jax.ShapeDtypeStruct((bd, 1, hd), F32), jax.ShapeDtypeStruct((bd, CONV_W - 1, C_CONV_CH), F32),
                   jax.ShapeDtypeStruct((bd, B_HEADS, B_DK, B_DV), F32),
                   jax.ShapeDtypeStruct((bd, C_HEADS, C_DK, C_DV), F32)],
        compiler_params=_cparams(("parallel",)),
        name="sample_mix",
    )(zs3, zs3, zs3, zs3, zs3, cbuf, sgla, sgdn, wlr_pad, lr_bias, conv_w, dtb_bc, alog_bc)


def _t5_bucket(dist):
    max_exact = N_BUCKETS // 2
    d = jnp.maximum(dist.astype(F32), 1.0)
    large = max_exact + (jnp.log(d / max_exact) / math.log(MAX_DIST / max_exact)
                         * (N_BUCKETS - max_exact)).astype(jnp.int32)
    large = jnp.minimum(large, N_BUCKETS - 1)
    return jnp.where(dist < max_exact, dist, large).astype(jnp.int32)


def _bias_tables(rel_bias):
    i = jnp.arange(SPAN)[:, None]
    c = jnp.arange(2 * SPAN)[None, :]
    dist = i + SPAN - c
    valid = (dist >= 0) & (dist <= SPAN)

    def lookup(bias_g, idx):
        hot = (idx[..., None] == jnp.arange(N_BUCKETS)).astype(F32)
        return jnp.einsum("...k,kh->...h", hot, bias_g.astype(F32), precision=HI)

    prompt, cached, new = [], [], []
    for g, (win, dil) in enumerate(A_GROUPS):
        bias_g = rel_bias[:, g * A_HEADS:(g + 1) * A_HEADS]
        tbl = lookup(bias_g, _t5_bucket(jnp.maximum(dist, 0) * dil)).transpose(2, 0, 1)
        prompt.append(jnp.where(valid[None], tbl, NEG))
        w = jnp.arange(win)
        tbl_s = lookup(bias_g, _t5_bucket(win - w)).T
        cached.append(jnp.where((w % dil == 0)[None], tbl_s, NEG))
        new.append(lookup(bias_g, _t5_bucket(jnp.zeros((1,), jnp.int32))).T)
    return jnp.stack(prompt), cached, jnp.stack(new)


def _prep_w_in(w):
    offs = np.concatenate([[0], np.cumsum(IN_SIZES)])
    seg = lambda i: w[:, offs[i]:offs[i + 1]]
    z_order = (0, 1, 2, 8, 3, 4, 5, 7, 10, 11)
    used = sum(IN_SIZES[i] for i in z_order)
    w_z = jnp.concatenate([seg(i) for i in z_order] + [jnp.zeros((w.shape[0], NZ - used), w.dtype)], axis=1)
    w_zg = jnp.concatenate([seg(12), seg(6), seg(9)], axis=1)
    return w_z.astype(BF16), w_zg.astype(BF16)


def _selector(offset):
    sel = np.zeros((SMALL_W, C_HEADS * C_DV), np.float32)
    for h in range(C_HEADS):
        sel[offset + h, C_DV * h:C_DV * (h + 1)] = 1.0
    return jnp.asarray(sel)


def kernel(x_prompt, x_sample, cache_win128_kv, cache_win512_kv, cache_win2048_kv, state_gla, state_gdn, state_conv, norm_mix, w_in, a_q_norm, a_k_norm, rel_bias, b_w_lr, b_lr_bias, b_out_norm, c_conv, c_a_log, c_dt_bias, c_out_norm, w_branch, w_out, norm_ffn, ffn_w_gate, ffn_w_up, ffn_w_down, moe_router, moe_w_gate, moe_w_up, moe_w_down):
    bp, t, d = x_prompt.shape
    bs = x_sample.shape[0]
    mp = bp * t
    caches_t = [jnp.transpose(c, (0, 1, 3, 4, 5, 2)).reshape(c.shape[:2] + (2, A_WIDTH, c.shape[2]))
                for c in (cache_win128_kv, cache_win512_kv, cache_win2048_kv)]
    bias_p, bias_c, bias_n = _bias_tables(rel_bias)
    selb, sela = _selector(SM_CB), _selector(SM_CA)
    rolled = windows = None

    xp = x_prompt.reshape(mp, d)
    xs = x_sample.reshape(bs, d)
    outs_p = {k: [] for k in ("w0", "w1", "w2", "gla", "gdn", "conv")}
    outs_s = {k: [] for k in ("w0", "w1", "w2", "gla", "gdn", "conv")}
    for l in range(DEPTH):
        w_z, w_zg = _prep_w_in(w_in[l])
        wb = w_branch[l].astype(BF16)
        wo = w_out[l].astype(BF16)
        qg128 = jnp.tile(a_q_norm[l], 2).reshape(1, 2 * A_DH)
        kg128 = jnp.tile(a_k_norm[l], 2).reshape(1, 2 * A_DH)
        qg512 = jnp.tile(a_q_norm[l], A_HEADS).reshape(1, A_WIDTH)
        kg512 = jnp.tile(a_k_norm[l], A_HEADS).reshape(1, A_WIDTH)
        wlr_pad = jnp.zeros((SMALL_W, B_HEADS * B_DK), F32).at[:B_RANK].set(b_w_lr[l])
        lr_bias = b_lr_bias[l].reshape(1, -1)
        bon = jnp.tile(b_out_norm[l], B_HEADS).reshape(1, -1)
        con = jnp.tile(c_out_norm[l], C_HEADS).reshape(1, -1)
        dtb_bc = jnp.repeat(c_dt_bias[l], C_DV).reshape(1, -1)
        alog_bc = jnp.repeat(c_a_log[l], C_DV).reshape(1, -1)

        z = _norm_matmul(xp, norm_mix[l], w_z, tm=1024, tn=2560)
        zg = _norm_matmul(xp, norm_mix[l], w_zg, tm=1024, tn=2048, out_dtype=BF16)
        z3 = z.reshape(bp, t, NZ)
        o_a, *windows = _attn_prompt(z3, qg128, kg128, bias_p, l, windows)
        o_b, gla_t = _gla_prompt(z3, wlr_pad, lr_bias, tb=512)
        dtb_sm = jnp.zeros((1, SMALL_W), F32).at[0, SM_CA:SM_CA + C_HEADS].set(c_dt_bias[l])
        alog_sm = jnp.zeros((1, SMALL_W), F32).at[0, SM_CA:SM_CA + C_HEADS].set(c_a_log[l])
        o_c, gdn_s = _gdn_prompt(z3, c_conv[l], selb, sela, dtb_sm, alog_sm, tb=512)
        xp = _merge(zg, o_a.reshape(mp, -1), o_b.reshape(mp, -1), o_c.reshape(mp, -1), xp, bon, con, wb, wo, tm=256)
        gla = gla_t.reshape(bp, B_HEADS, B_DV, B_HEADS, B_DK)
        gla = jnp.stack([gla[:, h, :, h, :] for h in range(B_HEADS)], axis=1)
        outs_p["gla"].append(jnp.swapaxes(gla, 2, 3))
        outs_p["gdn"].append(gdn_s)
        outs_p["conv"].append(z3[:, t - (CONV_W - 1):, Z_CQKV:Z_CQKV + C_CONV_CH])

        zs = _norm_matmul(xs, norm_mix[l], w_z, tm=bs, tn=2560)
        zgs = _norm_matmul(xs, norm_mix[l], w_zg, tm=bs, tn=2048)
        zs3 = zs.reshape(bs, 1, NZ)
        oa_s, *rolled = _attn_sample(zs3, caches_t, rolled, l, qg512, kg512, bias_c, bias_n)
        ob_s, oc_s, nconv, ngla, ngdn = _sample_mix(
            zs3, state_conv[l], state_gla[l], state_gdn[l], wlr_pad, lr_bias, c_conv[l], dtb_bc, alog_bc)
        xs = _merge(zgs, oa_s.reshape(bs, -1), ob_s.reshape(bs, -1), oc_s.reshape(bs, -1), xs, bon, con, wb, wo, tm=bs)
        outs_s["gla"].append(ngla)
        outs_s["gdn"].append(ngdn)
        outs_s["conv"].append(nconv)

        i = l // 2
        if l % 2 == 0:
            wg, wu, wd = ffn_w_gate[i].astype(BF16), ffn_w_up[i].astype(BF16), ffn_w_down[i].astype(BF16)
            xp = _ffn(xp, norm_ffn[l], wg, wu, wd, tm=512, tf=1408)
            xs = _ffn(xs, norm_ffn[l], wg, wu, wd, tm=bs, tf=1408)
        else:
            wg, wu, wd = moe_w_gate[i].astype(BF16), moe_w_up[i].astype(BF16), moe_w_down[i].astype(BF16)
            router_pad = jnp.zeros((d, 128), F32).at[:, :N_EXPERTS].set(moe_router[i])
            xp = _moe_routed(xp, norm_ffn[l], moe_router[i].T, wg, wu, wd, tm=1024)
            xs = _moe(xs, norm_ffn[l], router_pad, wg, wu, wd, tm=bs, tf=896)

    st = lambda name, d_: jnp.stack(d_[name])

    def window_out(w):
        w = w.reshape(w.shape[:3] + (A_HEADS, A_DH, w.shape[-1]))
        return jnp.transpose(w, (0, 1, 5, 2, 3, 4))

    return (xp.reshape(bp, t, d), xs.reshape(bs, 1, d),
            window_out(windows[0]), window_out(rolled[0]), window_out(windows[1]), window_out(rolled[1]),
            window_out(windows[2]), window_out(rolled[2]),
            st("gla", outs_p), st("gla", outs_s), st("gdn", outs_p), st("gdn", outs_s),
            st("conv", outs_p), st("conv", outs_s))
```

```python
import functools
import math

import jax
import jax.numpy as jnp
import numpy as np
from jax import lax
from jax.experimental import pallas as pl
from jax.experimental.pallas import tpu as pltpu

F32 = jnp.float32
BF16 = jnp.bfloat16
HI = lax.Precision.HIGHEST
NEG = -1e30

D_MODEL = 1024
DEPTH = 2
EPS = 1e-6
A_GROUPS = ((128, 1), (512, 4), (2048, 16))
N_GROUPS = 3
A_HEADS = 8
A_DH = 64
A_WIDTH = A_HEADS * A_DH
SPAN = 128
ATT_PAIR = 16
GLA_PAIR = 8
GDN_PAIR = 8
N_BUCKETS = 32
MAX_DIST = 2048
B_HEADS = 4
B_DK = 64
B_DV = 128
B_RANK = 16
B_TAU = 16.0
C_HEADS = 4
C_DK = 128
C_DV = 128
CONV_W = 4
C_CONV_CH = 2 * C_HEADS * C_DK + C_HEADS * C_DV
CHUNK = 64
SUB = 16
BRANCH_W = 512
D_FF = 2816
N_EXPERTS = 8
D_FF_EXPERT = 1792
IN_SIZES = (N_GROUPS * A_WIDTH, N_GROUPS * A_WIDTH, N_GROUPS * A_WIDTH,
            B_HEADS * B_DK, B_HEADS * B_DK, B_HEADS * B_DV, B_HEADS * B_DV, B_RANK,
            C_CONV_CH, C_HEADS * C_DV, C_HEADS, C_HEADS, 3 * D_MODEL)

Z_AQ, Z_AK, Z_AV = 0, 1536, 3072
Z_CQKV = 4608
Z_BQ, Z_BK, Z_BV, Z_SMALL = 6144, 6400, 6656, 7168
SMALL_W = 256
NZ = 7680
ZG_GATES, ZG_BG, ZG_CZ = 0, 3072, 3584
NZG = 4096
SM_CB, SM_CA = B_RANK, B_RANK + C_HEADS

VMEM_LIMIT = 56 * 1024 * 1024


def _cparams(sem):
    return pltpu.CompilerParams(dimension_semantics=sem, vmem_limit_bytes=VMEM_LIMIT)


def _dot(a, b):
    return jnp.dot(a.astype(BF16), b.astype(BF16), preferred_element_type=F32)


def _dot_nt(a, b):
    return lax.dot_general(a.astype(BF16), b.astype(BF16), (((1,), (1,)), ((), ())), preferred_element_type=F32)


def _dot_tn(a, b):
    return lax.dot_general(a.astype(BF16), b.astype(BF16), (((0,), (0,)), ((), ())), preferred_element_type=F32)


def _split3(x):
    x1 = x.astype(BF16)
    r1 = x - x1.astype(F32)
    x2 = r1.astype(BF16)
    return x1, x2, (r1 - x2.astype(F32)).astype(BF16)


def _mask_dot(mask, x):
    m = mask.astype(BF16)
    x1, x2, x3 = _split3(x)
    return (jnp.dot(m, x1, preferred_element_type=F32) + jnp.dot(m, x2, preferred_element_type=F32)
            + jnp.dot(m, x3, preferred_element_type=F32))


def _dot_mask(x, mask):
    m = mask.astype(BF16)
    x1, x2, x3 = _split3(x)
    return (jnp.dot(x1, m, preferred_element_type=F32) + jnp.dot(x2, m, preferred_element_type=F32)
            + jnp.dot(x3, m, preferred_element_type=F32))


def _r16(x):
    return x.astype(BF16).astype(F32)


def _sigmoid(x):
    return jax.nn.sigmoid(x)


def _silu(x):
    return x * jax.nn.sigmoid(x)


def _softplus(x):
    return jnp.maximum(x, 0.0) + jnp.log1p(jnp.exp(-jnp.abs(x)))


def _log_sigmoid(x):
    return jnp.minimum(x, 0.0) - jnp.log1p(jnp.exp(-jnp.abs(x)))


def _rms(x, gain):
    return x * lax.rsqrt(jnp.mean(x * x, axis=-1, keepdims=True) + EPS) * gain


def _norm_matmul_kernel(x_ref, g_ref, w_ref, o_ref, h_ref):
    @pl.when(pl.program_id(1) == 0)
    def _():
        h_ref[...] = _rms(x_ref[...], g_ref[...]).astype(BF16)

    o_ref[...] = jnp.dot(h_ref[...], w_ref[...], preferred_element_type=F32).astype(o_ref.dtype)


def _norm_matmul(x, gain, w, tm, tn, out_dtype=F32):
    m, d = x.shape
    n = w.shape[1]
    return pl.pallas_call(
        _norm_matmul_kernel,
        grid=(m // tm, n // tn),
        in_specs=[pl.BlockSpec((tm, d), lambda i, j: (i, 0)),
                  pl.BlockSpec((1, d), lambda i, j: (0, 0)),
                  pl.BlockSpec((d, tn), lambda i, j: (0, j))],
        out_specs=pl.BlockSpec((tm, tn), lambda i, j: (i, j)),
        out_shape=jax.ShapeDtypeStruct((m, n), out_dtype),
        scratch_shapes=[pltpu.VMEM((tm, d), BF16)],
        compiler_params=_cparams(("parallel", "arbitrary")),
        name="norm_matmul",
    )(x, gain.reshape(1, d), w)


def _pair_rms(x, gain, head0):
    sq = x * x
    s0 = jnp.sum(jnp.where(head0, sq, 0.0), axis=-1, keepdims=True)
    s1 = jnp.sum(sq, axis=-1, keepdims=True) - s0
    return x * lax.rsqrt(jnp.where(head0, s0, s1) * (1.0 / A_DH) + EPS) * gain


def _attn_prompt_kernel(q1, k1, v1, q2, k2, v2, q3, k3, v3, qg_ref, kg_ref, bias_ref,
                        o_ref, w1, w2, w3, qt, kt, qs, ks, vs, og, lg, to, tl, *, slots):
    t_len = o_ref.shape[0]
    head0 = lax.broadcasted_iota(jnp.int32, (1, 2 * A_DH), 1) < A_DH
    first_half = lax.broadcasted_iota(jnp.int32, (1, 2 * SPAN), 1) < SPAN
    groups = ((q1, k1, v1, w1), (q2, k2, v2, w2), (q3, k3, v3, w3))
    for g, (qr, kr, vr, wr) in enumerate(groups):
        dil = A_GROUPS[g][1]
        length = t_len // dil
        nb = length // SPAN
        qt[...] = _pair_rms(qr[...], qg_ref[...], head0) * (A_DH ** -0.5)
        kt[...] = _pair_rms(kr[...], kg_ref[...], head0)
        wlen = wr.shape[-1]
        k_win = kt[t_len - wlen:, :].T
        v_win = vr[t_len - wlen:, :].T
        for s in range(slots):
            wr[s, 0] = k_win
            wr[s, 1] = v_win
        knr = kt
        for r in range(dil):
            rows = pl.ds(r, length, stride=dil) if dil > 1 else pl.ds(0, length)
            dst = pl.ds(r * length, length)
            qs[dst, :] = qt[rows, :].astype(BF16)
            ks[dst, :] = knr[rows, :].astype(BF16)
            vs[dst, :] = vr[rows, :].astype(BF16)

        if nb > 1:
            bias2 = jnp.concatenate([bias_ref[g, 0], bias_ref[g, 1]], axis=0)
        else:
            bias2 = jnp.concatenate([bias_ref[g, 0, :, SPAN:], bias_ref[g, 1, :, SPAN:]], axis=0)

        def block(jj, carry, nb=nb, bias2=bias2):
            row0s, k2s, v2s, logits = [], [], [], []
            for i in range(ATT_PAIR):
                j = jj * ATT_PAIR + i
                row0 = pl.multiple_of(j * SPAN, SPAN)
                qb = qs[pl.ds(row0, SPAN), :]
                qst = jnp.concatenate([jnp.where(head0, qb, jnp.zeros_like(qb)),
                                       jnp.where(head0, jnp.zeros_like(qb), qb)], axis=0)
                if nb > 1:
                    prow = pl.multiple_of(jnp.maximum(row0 - SPAN, 0), SPAN)
                    k2 = jnp.concatenate([ks[pl.ds(prow, SPAN), :], ks[pl.ds(row0, SPAN), :]], axis=0)
                    v2 = jnp.concatenate([vs[pl.ds(prow, SPAN), :], vs[pl.ds(row0, SPAN), :]], axis=0)
                else:
                    k2 = ks[pl.ds(row0, SPAN), :]
                    v2 = vs[pl.ds(row0, SPAN), :]
                lg_ = lax.dot_general(qst, k2, (((1,), (1,)), ((), ())), preferred_element_type=F32) + bias2
                if nb > 1:
                    lg_ = lg_ + jnp.where(first_half, jnp.where(j % nb == 0, NEG, 0.0), 0.0)
                row0s.append(row0)
                k2s.append(k2)
                v2s.append(v2)
                logits.append(lg_)
            lg_all = jnp.concatenate(logits, axis=0)
            m = jnp.max(lg_all, axis=-1, keepdims=True)
            p16 = jnp.exp(lg_all - m).astype(BF16)
            for i in range(ATT_PAIR):
                lo = 2 * SPAN * i
                v_ext = jnp.concatenate([v2s[i], jnp.ones_like(v2s[i])], axis=1)
                acc = jnp.dot(p16[lo:lo + 2 * SPAN], v_ext, preferred_element_type=F32)
                den = acc[:, 2 * A_DH:]
                out = acc[:, :2 * A_DH] / den
                lse = m[lo:lo + 2 * SPAN] + jnp.log(den)
                to[pl.ds(row0s[i], SPAN), :] = jnp.where(head0, out[:SPAN], out[SPAN:])
                tl[pl.ds(row0s[i], SPAN), :] = jnp.where(head0, lse[:SPAN], lse[SPAN:])
            return carry

        lax.fori_loop(0, t_len // (SPAN * ATT_PAIR), block, 0)
        for r in range(dil):
            rows = pl.ds(r, length, stride=dil) if dil > 1 else pl.ds(0, length)
            src = pl.ds(r * length, length)
            og[g, rows, :] = to[src, :]
            lg[g, rows, :] = tl[src, :]
    l_1, l_2, l_3 = lg[0], lg[1], lg[2]
    mx = jnp.maximum(jnp.maximum(l_1, l_2), l_3)
    e_1, e_2, e_3 = jnp.exp(l_1 - mx), jnp.exp(l_2 - mx), jnp.exp(l_3 - mx)
    o_ref[...] = (e_1 * og[0] + e_2 * og[1] + e_3 * og[2]) / (e_1 + e_2 + e_3)


def _attn_prompt(z3, qgain, kgain, bias, layer, windows):
    b, t, _ = z3.shape
    hp_blocks = A_WIDTH // 128

    def col(base, g):
        return lambda i, hp: (i, 0, base // 128 + g * hp_blocks + hp)

    in_specs = []
    for g in range(N_GROUPS):
        for base in (Z_AQ, Z_AK, Z_AV):
            in_specs.append(pl.BlockSpec((None, t, 128), col(base, g)))
    in_specs += [pl.BlockSpec((1, 128), lambda i, hp: (0, 0)),
                 pl.BlockSpec((1, 128), lambda i, hp: (0, 0)),
                 pl.BlockSpec((N_GROUPS, 2, SPAN, 2 * SPAN), lambda i, hp: (0, hp, 0, 0))]
    args = [z3] * 9 + [qgain, kgain, bias]
    slots = DEPTH if windows is None else 1
    win_specs, win_shapes = [], []
    for win, _ in A_GROUPS:
        wp = min(win, t)
        win_specs.append(pl.BlockSpec((slots, None, 2, 128, wp), lambda i, hp: (layer // slots, i, 0, hp, 0)))
        win_shapes.append(jax.ShapeDtypeStruct((DEPTH, b, 2, A_WIDTH, wp), F32))
    aliases = {}
    if windows is not None:
        aliases = {len(args) + g: 1 + g for g in range(N_GROUPS)}
        in_specs += [pl.BlockSpec(memory_space=pl.ANY)] * N_GROUPS
        args += list(windows)
    kern = functools.partial(_attn_prompt_kernel, slots=slots)
    if windows is not None:
        kern = functools.partial(_drop_refs, kern, len(args) - N_GROUPS, N_GROUPS)
    return pl.pallas_call(
        kern,
        grid=(b, hp_blocks),
        in_specs=in_specs,
        out_specs=[pl.BlockSpec((None, t, 128), lambda i, hp: (i, 0, hp))] + win_specs,
        out_shape=[jax.ShapeDtypeStruct((b, t, A_WIDTH), F32)] + win_shapes,
        scratch_shapes=[pltpu.VMEM((t, 128), F32), pltpu.VMEM((t, 128), F32),
                        pltpu.VMEM((t, 128), BF16), pltpu.VMEM((t, 128), BF16), pltpu.VMEM((t, 128), BF16),
                        pltpu.VMEM((N_GROUPS, t, 128), F32), pltpu.VMEM((N_GROUPS, t, 128), F32),
                        pltpu.VMEM((t, 128), F32), pltpu.VMEM((t, 128), F32)],
        input_output_aliases=aliases,
        compiler_params=_cparams(("parallel", "parallel")),
        name="attn_prompt",
    )(*args)


def _drop_refs(kern, start, count, *refs):
    return kern(*refs[:start], *refs[start + count:])


def _gla_prompt_kernel(q_ref, k_ref, v_ref, sm_ref, wlr_ref, lrb_ref, o_ref, st_ref, s_scr, la_scr):
    tb = q_ref.shape[0]
    hk = B_HEADS * B_DK
    hv = B_HEADS * B_DV

    @pl.when(pl.program_id(1) == 0)
    def _():
        s_scr[...] = jnp.zeros_like(s_scr)

    la_scr[...] = _log_sigmoid(_dot(sm_ref[...], wlr_ref[...]) + lrb_ref[...]) * (1.0 / B_TAU)

    rr = lax.broadcasted_iota(jnp.int32, (CHUNK, CHUNK), 0)
    cc = lax.broadcasted_iota(jnp.int32, (CHUNK, CHUNK), 1)
    causal = rr >= cc
    ltri = causal.astype(F32)
    lane_head = lax.broadcasted_iota(jnp.int32, (1, hk), 1) // B_DK
    row = lax.broadcasted_iota(jnp.int32, (CHUNK, 1), 0)
    same_head = (lax.broadcasted_iota(jnp.int32, (hv, hk), 0) // B_DV
                 == lax.broadcasted_iota(jnp.int32, (hv, hk), 1) // B_DK)

    def intra(r0):
        b = _mask_dot(ltri, la_scr[pl.ds(r0, CHUNK), :])
        q = q_ref[pl.ds(r0, CHUNK), :] * (B_DK ** -0.5)
        k = k_ref[pl.ds(r0, CHUNK), :]
        vb = v_ref[pl.ds(r0, CHUNK), :].astype(BF16)
        b_last = b[CHUNK - 1:CHUNK, :]
        atts = []
        for i in range(CHUNK // SUB):
            b_ref0 = b[SUB * i:SUB * i + 1, :]
            qi = q[SUB * i:SUB * (i + 1), :] * jnp.exp(b[SUB * i:SUB * (i + 1), :] - b_ref0)
            qst = jnp.concatenate([jnp.where(lane_head == h, qi, 0.0) for h in range(B_HEADS)], axis=0)
            kt = k * jnp.exp(jnp.where(row < SUB * (i + 1), b_ref0 - b, 0.0))
            atts.append(_dot_nt(qst, kt))
        o_parts = []
        for h in range(B_HEADS):
            att = jnp.concatenate([a[SUB * h:SUB * (h + 1), :] for a in atts], axis=0)
            att = jnp.where(causal, att, 0.0)
            o_parts.append(_dot(att, vb[:, B_DV * h:B_DV * (h + 1)]))
        upd = jnp.where(same_head, _dot_tn(vb, k * jnp.exp(b_last - b)), 0.0)
        return (q * jnp.exp(b)).astype(BF16), jnp.concatenate(o_parts, axis=1), jnp.exp(b_last), upd

    def chunks(cc, carry):
        r0s = [pl.multiple_of((cc * GLA_PAIR + i) * CHUNK, CHUNK) for i in range(GLA_PAIR)]
        parts = [intra(r0) for r0 in r0s]
        for r0, (qe, o_intra, decay, upd) in zip(r0s, parts):
            s = s_scr[...]
            o_ref[pl.ds(r0, CHUNK), :] = _dot_nt(qe, s) + o_intra
            s_scr[...] = s * decay + upd
        return carry

    lax.fori_loop(0, tb // (CHUNK * GLA_PAIR), chunks, 0)

    @pl.when(pl.program_id(1) == pl.num_programs(1) - 1)
    def _():
        st_ref[...] = s_scr[...]


def _gla_prompt(z3, wlr_pad, lr_bias, tb):
    b, t, _ = z3.shape
    hk, hv = B_HEADS * B_DK, B_HEADS * B_DV
    return pl.pallas_call(
        _gla_prompt_kernel,
        grid=(b, t // tb),
        in_specs=[pl.BlockSpec((None, tb, hk), lambda i, j: (i, j, Z_BQ // hk)),
                  pl.BlockSpec((None, tb, hk), lambda i, j: (i, j, Z_BK // hk)),
                  pl.BlockSpec((None, tb, hv), lambda i, j: (i, j, Z_BV // hv)),
                  pl.BlockSpec((None, tb, SMALL_W), lambda i, j: (i, j, Z_SMALL // SMALL_W)),
                  pl.BlockSpec((SMALL_W, hk), lambda i, j: (0, 0)),
                  pl.BlockSpec((1, hk), lambda i, j: (0, 0))],
        out_specs=[pl.BlockSpec((None, tb, hv), lambda i, j: (i, j, 0)),
                   pl.BlockSpec((None, hv, hk), lambda i, j: (i, 0, 0))],
        out_shape=[jax.ShapeDtypeStruct((b, t, hv), F32), jax.ShapeDtypeStruct((b, hv, hk), F32)],
        scratch_shapes=[pltpu.VMEM((hv, hk), F32), pltpu.VMEM((tb, hk), F32)],
        compiler_params=_cparams(("parallel", "arbitrary")),
        name="gla_prompt",
    )(z3, z3, z3, z3, wlr_pad, lr_bias)


def _gdn_prompt_kernel(x_ref, sm_ref, cw_ref, selb_ref, sela_ref, dtb_ref, alog_ref, o_ref, s_ref,
                       xpad, cv, gsc, bsc, s_scr):
    tb = x_ref.shape[0]
    hd = C_HEADS * C_DK
    pad = 8

    @pl.when(pl.program_id(1) == 0)
    def _():
        s_scr[...] = jnp.zeros_like(s_scr)
        xpad[0:pad, :] = jnp.zeros((pad, C_CONV_CH), F32)

    xpad[pad:pad + tb, :] = x_ref[...]
    rb = 128
    for i in range(tb // rb):
        acc = None
        for j in range(CONV_W):
            lo = pad - (CONV_W - 1) + j + i * rb
            term = xpad[lo:lo + rb, :] * cw_ref[j:j + 1, :]
            acc = term if acc is None else acc + term
        y = _silu(acc)
        rows = slice(i * rb, (i + 1) * rb)
        for h in range(C_HEADS):
            qh = y[:, C_DK * h:C_DK * (h + 1)]
            cv[rows, C_DK * h:C_DK * (h + 1)] = (
                qh * lax.rsqrt(jnp.sum(qh * qh, axis=-1, keepdims=True) + EPS) * (C_DK ** -0.5))
            kh = y[:, hd + C_DK * h:hd + C_DK * (h + 1)]
            cv[rows, hd + C_DK * h:hd + C_DK * (h + 1)] = (
                kh * lax.rsqrt(jnp.sum(kh * kh, axis=-1, keepdims=True) + EPS))
        cv[rows, 2 * hd:] = y[:, 2 * hd:]
    xpad[pad - (CONV_W - 1):pad, :] = xpad[pad + tb - (CONV_W - 1):pad + tb, :]

    sm = sm_ref[...]
    gsc[...] = _dot_mask(-jnp.exp(alog_ref[...]) * _softplus(sm + dtb_ref[...]), sela_ref[...])
    bsc[...] = _dot_mask(_sigmoid(sm), selb_ref[...])

    n = C_HEADS * CHUNK
    rr = lax.broadcasted_iota(jnp.int32, (CHUNK, CHUNK), 0)
    cc = lax.broadcasted_iota(jnp.int32, (CHUNK, CHUNK), 1)
    ltri = (rr >= cc).astype(F32)
    ri = lax.broadcasted_iota(jnp.int32, (n, n), 0)
    ci = lax.broadcasted_iota(jnp.int32, (n, n), 1)
    same = (ri // CHUNK) == (ci // CHUNK)
    incl = jnp.logical_and(same, ci <= ri)
    strict = jnp.logical_and(same, ci < ri)
    eye = (ri == ci).astype(F32)

    def stack(x):
        return jnp.concatenate([x[:, C_DK * h:C_DK * (h + 1)] for h in range(C_HEADS)], axis=0)

    def chunks(cc, carry):
        pair = range(GDN_PAIR)
        r0s = [pl.multiple_of((cc * GDN_PAIR + i) * CHUNK, CHUNK) for i in pair]
        bst = [stack(_mask_dot(ltri, gsc[pl.ds(r0, CHUNK), :])) for r0 in r0s]
        beta = [stack(bsc[pl.ds(r0, CHUNK), :]) for r0 in r0s]
        qst = [stack(cv[pl.ds(r0, CHUNK), 0:hd]) for r0 in r0s]
        kst = [stack(cv[pl.ds(r0, CHUNK), hd:2 * hd]) for r0 in r0s]
        vst = [stack(cv[pl.ds(r0, CHUNK), 2 * hd:3 * hd]) for r0 in r0s]
        kb = [k.astype(BF16) for k in kst]
        nt = (((1,), (1,)), ((), ()))
        kk = [lax.dot_general(kb[i], kb[i], nt, preferred_element_type=F32) for i in pair]
        qk = [lax.dot_general(qst[i].astype(BF16), kb[i], nt, preferred_element_type=F32) for i in pair]
        dec, a, aqk = [], [], []
        for i in pair:
            bst_t = bst[i].T
            d_ = jnp.exp(jnp.where(incl, jnp.concatenate([bst[i], bst[i]], axis=1)
                                   - jnp.concatenate([bst_t, bst_t], axis=0), NEG))
            dec.append(d_)
            a.append(jnp.where(strict, jnp.concatenate([beta[i], beta[i]], axis=1) * d_ * kk[i], 0.0))
            aqk.append(d_ * qk[i])
        inv = [eye - a[i] for i in pair]
        pw = a
        for _ in range(int(math.log2(CHUNK)) - 1):
            pw = [_dot(pw[i], pw[i]) for i in pair]
            inv = [inv[i] + _dot(inv[i], pw[i]) for i in pair]
        eb = [jnp.exp(bst[i]) for i in pair]
        wu = [_dot(inv[i], jnp.concatenate([beta[i] * eb[i] * kst[i], beta[i] * vst[i]], axis=1)) for i in pair]
        for i in pair:
            w, uv = wu[i][:, :C_DK], wu[i][:, C_DK:]
            us, qss = [], []
            for h in range(C_HEADS):
                sl = slice(CHUNK * h, CHUNK * (h + 1))
                ws = _dot(jnp.concatenate([w[sl], qst[i][sl]], axis=0), s_scr[h])
                us.append(uv[sl] - ws[:CHUNK])
                qss.append(ws[CHUNK:])
            o = eb[i] * jnp.concatenate(qss, axis=0) + _dot(aqk[i], jnp.concatenate(us, axis=0))
            for h in range(C_HEADS):
                sl = slice(CHUNK * h, CHUNK * (h + 1))
                o_ref[pl.ds(r0s[i], CHUNK), C_DV * h:C_DV * (h + 1)] = o[sl]
                b_last = bst[i][CHUNK * (h + 1) - 1:CHUNK * (h + 1), :]
                s_scr[h] = (jnp.exp(b_last) * s_scr[h]
                            + _dot_tn(kst[i][sl] * jnp.exp(b_last - bst[i][sl]), us[h]))
        return carry

    lax.fori_loop(0, tb // (CHUNK * GDN_PAIR), chunks, 0)

    @pl.when(pl.program_id(1) == pl.num_programs(1) - 1)
    def _():
        s_ref[...] = s_scr[...]


def _gdn_prompt(z3, conv_w, selb, sela, dtb_bc, alog_bc, tb):
    b, t, _ = z3.shape
    hd = C_HEADS * C_DV
    const = lambda shape: pl.BlockSpec(shape, lambda i, j: (0,) * len(shape))
    return pl.pallas_call(
        _gdn_prompt_kernel,
        grid=(b, t // tb),
        in_specs=[pl.BlockSpec((None, tb, C_CONV_CH), lambda i, j: (i, j, Z_CQKV // C_CONV_CH)),
                  pl.BlockSpec((None, tb, SMALL_W), lambda i, j: (i, j, Z_SMALL // SMALL_W)),
                  const((CONV_W, C_CONV_CH)), const((SMALL_W, hd)), const((SMALL_W, hd)),
                  const((1, SMALL_W)), const((1, SMALL_W))],
        out_specs=[pl.BlockSpec((None, tb, hd), lambda i, j: (i, j, 0)),
                   pl.BlockSpec((None, C_HEADS, C_DK, C_DV), lambda i, j: (i, 0, 0, 0))],
        out_shape=[jax.ShapeDtypeStruct((b, t, hd), F32),
                   jax.ShapeDtypeStruct((b, C_HEADS, C_DK, C_DV), F32)],
        scratch_shapes=[pltpu.VMEM((tb + 8, C_CONV_CH), F32), pltpu.VMEM((tb, C_CONV_CH), F32),
                        pltpu.VMEM((tb, hd), F32), pltpu.VMEM((tb, hd), F32),
                        pltpu.VMEM((C_HEADS, C_DK, C_DV), F32)],
        compiler_params=_cparams(("parallel", "arbitrary")),
        name="gdn_prompt",
    )(z3, z3, conv_w, selb, sela, dtb_bc, alog_bc)


def _head_norm(o, gain, width):
    parts = []
    for h in range(o.shape[1] // width):
        oh = o[:, width * h:width * (h + 1)]
        parts.append(oh * lax.rsqrt(jnp.mean(oh * oh, axis=-1, keepdims=True) + EPS))
    return jnp.concatenate(parts, axis=1) * gain


def _merge_kernel(oa_ref, ob_ref, bg_ref, oc_ref, cz_ref, ga_ref, gb_ref, gc_ref, x_ref,
                  bon_ref, con_ref, wb_ref, wo_ref, y_ref):
    f32 = lambda ref: ref[...].astype(F32)
    ob = _head_norm(ob_ref[...], bon_ref[...], B_DV) * _silu(f32(bg_ref))
    oc = _head_norm(oc_ref[...], con_ref[...], C_DV) * _silu(f32(cz_ref))
    merged = (_sigmoid(f32(ga_ref)) * _dot(oa_ref[...], wb_ref[0])
              + _sigmoid(f32(gb_ref)) * _dot(ob, wb_ref[1])
              + _sigmoid(f32(gc_ref)) * _dot(oc, wb_ref[2]))
    y_ref[...] = x_ref[...] + _dot(merged, wo_ref[...])


def _merge(zg, oa, ob, oc, x, bon, con, wb, wo, tm):
    m = x.shape[0]
    w = BRANCH_W
    row = lambda width, blk: pl.BlockSpec((tm, width), lambda i: (i, blk))
    const = lambda shape: pl.BlockSpec(shape, lambda i: (0,) * len(shape))
    return pl.pallas_call(
        _merge_kernel,
        grid=(m // tm,),
        in_specs=[row(w, 0), row(w, 0), row(w, ZG_BG // w), row(w, 0), row(w, ZG_CZ // w),
                  row(D_MODEL, ZG_GATES // D_MODEL), row(D_MODEL, ZG_GATES // D_MODEL + 1),
                  row(D_MODEL, ZG_GATES // D_MODEL + 2), row(D_MODEL, 0),
                  const((1, w)), const((1, w)), const((3, w, D_MODEL)), const((D_MODEL, D_MODEL))],
        out_specs=row(D_MODEL, 0),
        out_shape=jax.ShapeDtypeStruct((m, D_MODEL), F32),
        compiler_params=_cparams(("parallel",)),
        name="merge",
    )(oa, ob, zg, oc, zg, zg, zg, zg, x, bon, con, wb, wo)


def _ffn_kernel(x_ref, g_ref, wg_ref, wu_ref, wd_ref, y_ref, h_ref, acc_ref):
    f = pl.program_id(1)

    @pl.when(f == 0)
    def _():
        h_ref[...] = _rms(x_ref[...], g_ref[...]).astype(BF16)
        acc_ref[...] = jnp.zeros_like(acc_ref)

    h = h_ref[...]
    a = _silu(jnp.dot(h, wg_ref[...], preferred_element_type=F32)) * jnp.dot(h, wu_ref[...],
                                                                              preferred_element_type=F32)
    acc_ref[...] += jnp.dot(a.astype(BF16), wd_ref[...], preferred_element_type=F32)

    @pl.when(f == pl.num_programs(1) - 1)
    def _():
        y_ref[...] = x_ref[...] + acc_ref[...]


def _ffn(x, gain, wg, wu, wd, tm, tf):
    m, d = x.shape
    ff = wg.shape[1]
    return pl.pallas_call(
        _ffn_kernel,
        grid=(m // tm, ff // tf),
        in_specs=[pl.BlockSpec((tm, d), lambda i, f: (i, 0)),
                  pl.BlockSpec((1, d), lambda i, f: (0, 0)),
                  pl.BlockSpec((d, tf), lambda i, f: (0, f)),
                  pl.BlockSpec((d, tf), lambda i, f: (0, f)),
                  pl.BlockSpec((tf, d), lambda i, f: (f, 0))],
        out_specs=pl.BlockSpec((tm, d), lambda i, f: (i, 0)),
        out_shape=jax.ShapeDtypeStruct((m, d), F32),
        scratch_shapes=[pltpu.VMEM((tm, d), BF16), pltpu.VMEM((tm, d), F32)],
        compiler_params=_cparams(("parallel", "arbitrary")),
        name="ffn",
    )(x, gain.reshape(1, d), wg, wu, wd)


def _moe_kernel(x_ref, g_ref, rt_ref, wg_ref, wu_ref, wd_ref, y_ref, h_ref, acc_ref, gate_ref):
    e = pl.program_id(1)
    f = pl.program_id(2)
    lane = lax.broadcasted_iota(jnp.int32, (1, 128), 1).astype(F32)

    @pl.when(jnp.logical_and(e == 0, f == 0))
    def _():
        hf = _rms(x_ref[...], g_ref[...])
        h_ref[...] = hf.astype(BF16)
        acc_ref[...] = jnp.zeros_like(acc_ref)
        valid = lane < N_EXPERTS
        logits = jnp.where(valid, _dot(hf, rt_ref[...]), NEG)
        ex = jnp.exp(logits - jnp.max(logits, axis=-1, keepdims=True))
        probs = ex / jnp.sum(ex, axis=-1, keepdims=True)
        m1 = jnp.max(probs, axis=-1, keepdims=True)
        i1 = jnp.min(jnp.where(jnp.logical_and(probs == m1, valid), lane, 128.0), axis=-1, keepdims=True)
        hot1 = lane == i1
        rest = jnp.where(jnp.logical_or(hot1, jnp.logical_not(valid)), -1.0, probs)
        m2 = jnp.max(rest, axis=-1, keepdims=True)
        i2 = jnp.min(jnp.where(rest == m2, lane, 128.0), axis=-1, keepdims=True)
        hot2 = lane == i2
        den = m1 + m2
        gate_ref[...] = jnp.where(hot1, m1 / den, 0.0) + jnp.where(hot2, m2 / den, 0.0)

    h = h_ref[...]
    ge = jnp.sum(jnp.where(lane == e.astype(F32), gate_ref[...], 0.0), axis=-1, keepdims=True)
    a = (_silu(jnp.dot(h, wg_ref[...], preferred_element_type=F32))
         * jnp.dot(h, wu_ref[...], preferred_element_type=F32) * ge)
    acc_ref[...] += jnp.dot(a.astype(BF16), wd_ref[...], preferred_element_type=F32)

    @pl.when(jnp.logical_and(e == pl.num_programs(1) - 1, f == pl.num_programs(2) - 1))
    def _():
        y_ref[...] = x_ref[...] + acc_ref[...]


def _moe(x, gain, router_pad, wg, wu, wd, tm, tf):
    m, d = x.shape
    ne, _, ff = wg.shape
    return pl.pallas_call(
        _moe_kernel,
        grid=(m // tm, ne, ff // tf),
        in_specs=[pl.BlockSpec((tm, d), lambda i, e, f: (i, 0)),
                  pl.BlockSpec((1, d), lambda i, e, f: (0, 0)),
                  pl.BlockSpec((d, 128), lambda i, e, f: (0, 0)),
                  pl.BlockSpec((None, d, tf), lambda i, e, f: (e, 0, f)),
                  pl.BlockSpec((None, d, tf), lambda i, e, f: (e, 0, f)),
                  pl.BlockSpec((None, tf, d), lambda i, e, f: (e, f, 0))],
        out_specs=pl.BlockSpec((tm, d), lambda i, e, f: (i, 0)),
        out_shape=jax.ShapeDtypeStruct((m, d), F32),
        scratch_shapes=[pltpu.VMEM((tm, d), BF16), pltpu.VMEM((tm, d), F32), pltpu.VMEM((tm, 128), F32)],
        compiler_params=_cparams(("parallel", "arbitrary", "arbitrary")),
        name="moe",
    )(x, gain.reshape(1, d), router_pad, wg, wu, wd)


MOE_ROWS = 288


def _moe_routed_kernel(x_ref, g_ref, rt_ref, wg_ref, wu_ref, wd_ref, y_ref,
                       h_ref, gate_scr, mask_scr, rank_scr, tri_scr):
    e = pl.program_id(1)
    tm = x_ref.shape[0]
    sub = lax.broadcasted_iota(jnp.int32, (N_EXPERTS, 1), 0).astype(F32)

    @pl.when(e == 0)
    def _():
        x = x_ref[...]
        hf = _rms(x, g_ref[...])
        h_ref[...] = hf.astype(BF16)
        y_ref[...] = x
        logits = _dot_nt(rt_ref[...], hf)
        ex = jnp.exp(logits - jnp.max(logits, axis=0, keepdims=True))
        probs = ex / jnp.sum(ex, axis=0, keepdims=True)
        m1 = jnp.max(probs, axis=0, keepdims=True)
        hot1 = sub == jnp.min(jnp.where(probs == m1, sub, float(N_EXPERTS)), axis=0, keepdims=True)
        rest = jnp.where(hot1, -1.0, probs)
        m2 = jnp.max(rest, axis=0, keepdims=True)
        hot2 = sub == jnp.min(jnp.where(rest == m2, sub, float(N_EXPERTS)), axis=0, keepdims=True)
        den = m1 + m2
        gate_scr[...] = jnp.where(hot1, m1 / den, 0.0) + jnp.where(hot2, m2 / den, 0.0)
        mask = jnp.where(jnp.logical_or(hot1, hot2), 1.0, 0.0)
        mask_scr[...] = mask
        tri_scr[...] = jnp.where(lax.broadcasted_iota(jnp.int32, (tm, tm), 0)
                                 <= lax.broadcasted_iota(jnp.int32, (tm, tm), 1), 1.0, 0.0).astype(BF16)
        rank_scr[...] = jnp.dot(mask.astype(BF16), tri_scr[...], preferred_element_type=F32) - 1.0

    mine = sub == e.astype(F32)
    row = lambda ref: jnp.sum(jnp.where(mine, ref[...], 0.0), axis=0, keepdims=True)
    gate_row, mask_row, rank_row = row(gate_scr), row(mask_scr), row(rank_scr)
    count = jnp.sum(mask_row).astype(jnp.int32)
    slot = lax.broadcasted_iota(jnp.int32, (MOE_ROWS, 1), 0).astype(F32)

    def chunk(j, carry):
        base = (j * MOE_ROWS).astype(F32)
        pick = jnp.where(jnp.logical_and(rank_row == slot + base, mask_row > 0.0), 1.0, 0.0)
        pick16 = pick.astype(BF16)
        xg = jnp.dot(pick16, h_ref[...], preferred_element_type=F32).astype(BF16)
        gate_col = jnp.sum(pick * gate_row, axis=1, keepdims=True)
        a = (_silu(jnp.dot(xg, wg_ref[...], preferred_element_type=F32))
             * jnp.dot(xg, wu_ref[...], preferred_element_type=F32) * gate_col)
        out = jnp.dot(a.astype(BF16), wd_ref[...], preferred_element_type=F32)
        y_ref[...] += _dot_tn(pick16, out)
        return carry

    lax.fori_loop(0, (count + MOE_ROWS - 1) // MOE_ROWS, chunk, 0)


def _moe_routed(x, gain, router_t, wg, wu, wd, tm):
    m, d = x.shape
    ne, _, ff = wg.shape
    return pl.pallas_call(
        _moe_routed_kernel,
        grid=(m // tm, ne),
        in_specs=[pl.BlockSpec((tm, d), lambda i, e: (i, 0)),
                  pl.BlockSpec((1, d), lambda i, e: (0, 0)),
                  pl.BlockSpec((ne, d), lambda i, e: (0, 0)),
                  pl.BlockSpec((None, d, ff), lambda i, e: (e, 0, 0)),
                  pl.BlockSpec((None, d, ff), lambda i, e: (e, 0, 0)),
                  pl.BlockSpec((None, ff, d), lambda i, e: (e, 0, 0))],
        out_specs=pl.BlockSpec((tm, d), lambda i, e: (i, 0)),
        out_shape=jax.ShapeDtypeStruct((m, d), F32),
        scratch_shapes=[pltpu.VMEM((tm, d), BF16), pltpu.VMEM((ne, tm), F32), pltpu.VMEM((ne, tm), F32),
                        pltpu.VMEM((ne, tm), F32), pltpu.VMEM((tm, tm), BF16)],
        compiler_params=_cparams(("parallel", "arbitrary")),
        name="moe_routed",
    )(x, gain.reshape(1, d), router_t, wg, wu, wd)


def _attn_sample_kernel(*refs, layer, full_roll):
    zq_ref, qg_ref, kg_ref, b0_ref, b1_ref, b2_ref, bias0_ref, c0_ref, c1_ref, c2_ref = refs[:10]
    if full_roll:
        tails = (None,) * N_GROUPS
        rest = refs[10:]
    else:
        tails = refs[10:13]
        rest = refs[13:]
    o_ref, n0_ref, n1_ref, n2_ref, p0_scr, p1_scr, p2_scr, st_scr, oacc_scr = rest
    half = pl.program_id(1)
    active = (pl.program_id(2) == layer) if full_roll else True
    aw = A_WIDTH
    sel = (lax.broadcasted_iota(jnp.int32, (A_HEADS, aw), 1) // A_DH
           == lax.broadcasted_iota(jnp.int32, (A_HEADS, aw), 0))
    last_lane = lax.broadcasted_iota(jnp.int32, (1, 128), 1) == 127
    groups = ((c0_ref, n0_ref, b0_ref, p0_scr, tails[0]), (c1_ref, n1_ref, b1_ref, p1_scr, tails[1]),
              (c2_ref, n2_ref, b2_ref, p2_scr, tails[2]))
    for g, (c_ref, n_ref, b_ref, p_scr, t_ref) in enumerate(groups):
        wlen = c_ref.shape[1]
        if full_roll:
            n_ref[...] = pltpu.roll(c_ref[...], wlen - 1, axis=1)

        def newest(col, n_ref=n_ref, t_ref=t_ref, wlen=wlen):
            old = n_ref[:, wlen - 128:] if full_roll else t_ref[...]
            dst = n_ref.at[:, wlen - 128:] if full_roll else n_ref
            dst[...] = jnp.where(last_lane, col, old)

        lo = g * aw

        @pl.when(jnp.logical_and(active, half == 0))
        def _(g=g, c_ref=c_ref, b_ref=b_ref, p_scr=p_scr, lo=lo, newest=newest):
            q = zq_ref[:, Z_AQ + lo:Z_AQ + lo + aw]
            k = zq_ref[:, Z_AK + lo:Z_AK + lo + aw]
            def head_rms(x, gain):
                s8 = jnp.sum(jnp.where(sel, jnp.broadcast_to(x * x, (A_HEADS, aw)), 0.0), axis=1, keepdims=True)
                ms = jnp.sum(jnp.where(sel, s8, 0.0), axis=0, keepdims=True) * (1.0 / A_DH)
                return x * lax.rsqrt(ms + EPS) * gain

            qn = head_rms(q, qg_ref[...]) * (A_DH ** -0.5)
            kn = head_rms(k, kg_ref[...])
            qbd = jnp.where(sel, jnp.broadcast_to(qn, (A_HEADS, aw)), 0.0)
            lc = jnp.sum((c_ref[...] * _to_col(qn, aw)).reshape(A_HEADS, A_DH, c_ref.shape[1]), axis=1) + b_ref[...]
            l0 = jnp.sum(qbd * kn, axis=1, keepdims=True) + bias0_ref[g]
            m = jnp.maximum(jnp.max(lc, axis=1, keepdims=True), l0)
            pc = jnp.exp(lc - m)
            p0 = jnp.exp(l0 - m)
            p_scr[...] = pc
            st_scr[g, 0] = jnp.broadcast_to(p0, (A_HEADS, 128))
            st_scr[g, 1] = jnp.broadcast_to(jnp.sum(pc, axis=1, keepdims=True) + p0, (A_HEADS, 128))
            st_scr[g, 2] = jnp.broadcast_to(m, (A_HEADS, 128))
            newest(_to_col(kn, aw))

        @pl.when(jnp.logical_and(active, half == 1))
        def _(g=g, c_ref=c_ref, p_scr=p_scr, lo=lo, newest=newest):
            v = zq_ref[:, Z_AV + lo:Z_AV + lo + aw]
            p0 = st_scr[g, 0][:, 0:1]
            s = st_scr[g, 1][:, 0:1]
            m = st_scr[g, 2][:, 0:1]
            o8 = (_dot_nt(p_scr[...], c_ref[...]) + _r16(p0) * _r16(v)) / s
            oacc_scr[g, 0:1, :] = jnp.sum(jnp.where(sel, o8, 0.0), axis=0, keepdims=True)
            oacc_scr[g, 1:2, :] = jnp.sum(jnp.where(sel, m + jnp.log(s), 0.0), axis=0, keepdims=True)
            newest(_to_col(v, aw))

    @pl.when(jnp.logical_and(active, half == 1))
    def _():
        l_1, l_2, l_3 = oacc_scr[0, 1:2, :], oacc_scr[1, 1:2, :], oacc_scr[2, 1:2, :]
        mx = jnp.maximum(jnp.maximum(l_1, l_2), l_3)
        e_1, e_2, e_3 = jnp.exp(l_1 - mx), jnp.exp(l_2 - mx), jnp.exp(l_3 - mx)
        o_ref[...] = ((e_1 * oacc_scr[0, 0:1, :] + e_2 * oacc_scr[1, 0:1, :] + e_3 * oacc_scr[2, 0:1, :])
                      / (e_1 + e_2 + e_3))


def _attn_sample(zs3, caches_t, rolled, layer, qgain, kgain, biases, bias0):
    bd = zs3.shape[0]
    full_roll = rolled is None
    aw = A_WIDTH
    if full_roll:
        grid = (bd, 2, DEPTH)
        const = lambda shape: pl.BlockSpec(shape, lambda i, h, l: (0,) * len(shape))
        zspec = pl.BlockSpec((None, 1, Z_AV + N_GROUPS * aw), lambda i, h, l: (i, 0, 0))
        ospec = pl.BlockSpec((None, 1, aw), lambda i, h, l: (i, 0, 0))
        cache_specs = [pl.BlockSpec((None, None, None, aw, c.shape[-1]), lambda i, h, l: (l, i, h, 0, 0))
                       for c in caches_t]
        new_specs = cache_specs
        extra_specs, extra_args, aliases = [], [], {}
        sem = ("parallel", "arbitrary", "arbitrary")
    else:
        grid = (bd, 2)
        const = lambda shape: pl.BlockSpec(shape, lambda i, h: (0,) * len(shape))
        zspec = pl.BlockSpec((None, 1, Z_AV + N_GROUPS * aw), lambda i, h: (i, 0, 0))
        ospec = pl.BlockSpec((None, 1, aw), lambda i, h: (i, 0, 0))
        cache_specs = [pl.BlockSpec((None, None, None, aw, c.shape[-1]), lambda i, h: (layer, i, h, 0, 0))
                       for c in caches_t]
        new_specs = [pl.BlockSpec((None, None, None, aw, 128), functools.partial(
            lambda last, i, h: (layer, i, h, 0, last), c.shape[-1] // 128 - 1)) for c in caches_t]
        extra_specs, extra_args = new_specs, list(rolled)
        aliases = {10 + g: 1 + g for g in range(N_GROUPS)}
        sem = ("parallel", "arbitrary")
    return pl.pallas_call(
        functools.partial(_attn_sample_kernel, layer=layer, full_roll=full_roll),
        grid=grid,
        in_specs=[zspec, const((1, aw)), const((1, aw))]
                 + [const((A_HEADS, c.shape[-1])) for c in caches_t] + [const((N_GROUPS, A_HEADS, 1))]
                 + cache_specs + extra_specs,
        out_specs=[ospec] + new_specs,
        out_shape=[jax.ShapeDtypeStruct((bd, 1, aw), F32)]
                  + [jax.ShapeDtypeStruct(c.shape, c.dtype) for c in caches_t],
        scratch_shapes=[pltpu.VMEM((A_HEADS, c.shape[-1]), F32) for c in caches_t]
                       + [pltpu.VMEM((N_GROUPS, 3, A_HEADS, 128), F32), pltpu.VMEM((N_GROUPS, 8, aw), F32)],
        input_output_aliases=aliases,
        compiler_params=_cparams(sem),
        name="attn_sample",
    )(zs3, qgain, kgain, *biases, bias0, *caches_t, *extra_args)


def _to_col(row, n):
    eye = lax.broadcasted_iota(jnp.int32, (n, n), 0) == lax.broadcasted_iota(jnp.int32, (n, n), 1)
    return jnp.sum(jnp.where(eye, jnp.broadcast_to(row, (n, n)), 0.0), axis=1, keepdims=True)


def _sample_mix_kernel(bq_ref, bk_ref, bv_ref, cqkv_ref, sm_ref, cbuf_ref, sgla_ref, sgdn_ref,
                       wlr_ref, lrb_ref, cw_ref, dtb_ref, alog_ref,
                       ob_ref, oc_ref, nconv_ref, ngla_ref, ngdn_ref):
    sm = sm_ref[...]
    la =_log_sigmoid(_dot(sm, wlr_ref[...]) + lrb_ref[...]) * (1.0 / B_TAU)
    q = bq_ref[...] * (B_DK ** -0.5)
    k = bk_ref[...]
    v = bv_ref[...]
    for h in range(B_HEADS):
        lk = slice(B_DK * h, B_DK * (h + 1))
        lv = slice(B_DV * h, B_DV * (h + 1))
        s_old = sgla_ref[h]
        decay = jnp.exp(la[:, lk])
        ngla_ref[h] = _to_col(decay, B_DK) * s_old + _to_col(k[:, lk], B_DK) * v[:, lv]
        att = jnp.sum(q[:, lk] * k[:, lk], axis=-1, keepdims=True)
        ob_ref[:, lv] = (jnp.sum(_to_col(q[:, lk] * decay, B_DK) * s_old, axis=0, keepdims=True)
                         + att * v[:, lv])

    cat = jnp.concatenate([cbuf_ref[...], cqkv_ref[...]], axis=0)
    acc = None
    for j in range(CONV_W):
        term = cat[j:j + 1, :] * cw_ref[j:j + 1, :]
        acc = term if acc is None else acc + term
    y = _silu(acc)
    nconv_ref[...] = cat[1:CONV_W, :]
    hd = C_HEADS * C_DK
    for h in range(C_HEADS):
        ld = slice(C_DK * h, C_DK * (h + 1))
        qh = y[:, C_DK * h:C_DK * (h + 1)]
        qh = qh * lax.rsqrt(jnp.sum(qh * qh, axis=-1, keepdims=True) + EPS) * (C_DK ** -0.5)
        kh = y[:, hd + C_DK * h:hd + C_DK * (h + 1)]
        kh = kh * lax.rsqrt(jnp.sum(kh * kh, axis=-1, keepdims=True) + EPS)
        vh = y[:, 2 * hd + C_DV * h:2 * hd + C_DV * (h + 1)]
        beta = _sigmoid(sm[:, SM_CB + h:SM_CB + h + 1])
        g = -jnp.exp(alog_ref[:, ld]) * _softplus(sm[:, SM_CA + h:SM_CA + h + 1] + dtb_ref[:, ld])
        eg = jnp.exp(g)
        s = sgdn_ref[h]
        kcol = _to_col(kh, C_DK)
        u = beta * (vh - eg * jnp.sum(kcol * s, axis=0, keepdims=True))
        qs = jnp.sum(_to_col(qh, C_DK) * s, axis=0, keepdims=True)
        oc_ref[:, ld] = eg * qs + jnp.sum(qh * kh, axis=-1, keepdims=True) * u
        ngdn_ref[h] = eg * s + kcol * u


def _sample_mix(zs3, cbuf, sgla, sgdn, wlr_pad, lr_bias, conv_w, dtb_bc, alog_bc):
    bd = zs3.shape[0]
    hk, hv, hd = B_HEADS * B_DK, B_HEADS * B_DV, C_HEADS * C_DV
    const = lambda shape: pl.BlockSpec(shape, lambda i: (0,) * len(shape))
    zrow = lambda width, off: pl.BlockSpec((None, 1, width), lambda i: (i, 0, off // width))
    vec = lambda width: pl.BlockSpec((None, 1, width), lambda i: (i, 0, 0))
    st = lambda shape: pl.BlockSpec((None,) + shape, lambda i: (i,) + (0,) * len(shape))
    return pl.pallas_call(
        _sample_mix_kernel,
        grid=(bd,),
        in_specs=[zrow(hk, Z_BQ), zrow(hk, Z_BK), zrow(hv, Z_BV), zrow(C_CONV_CH, Z_CQKV), zrow(SMALL_W, Z_SMALL),
                  st((CONV_W - 1, C_CONV_CH)), st((B_HEADS, B_DK, B_DV)), st((C_HEADS, C_DK, C_DV))]
                 + [const((SMALL_W, hk)), const((1, hk)), const((CONV_W, C_CONV_CH)), const((1, hd)), const((1, hd))],
        out_specs=[vec(hv), vec(hd), st((CONV_W - 1, C_CONV_CH)),
                   st((B_HEADS, B_DK, B_DV)), st((C_HEADS, C_DK, C_DV))],
        out_shape=[jax.ShapeDtypeStruct((bd, 1, hv), F32),
                   jax.ShapeDtypeStruct((bd, 1, hd), F32), jax.ShapeDtypeStruct((bd, CONV_W - 1, C_CONV_CH), F32),
                   jax.ShapeDtypeStruct((bd, B_HEADS, B_DK, B_DV), F32),
                   jax.ShapeDtypeStruct((bd, C_HEADS, C_DK, C_DV), F32)],
        compiler_params=_cparams(("parallel",)),
        name="sample_mix",
    )(zs3, zs3, zs3, zs3, zs3, cbuf, sgla, sgdn, wlr_pad, lr_bias, conv_w, dtb_bc, alog_bc)


def _t5_bucket(dist):
    max_exact = N_BUCKETS // 2
    d = jnp.maximum(dist.astype(F32), 1.0)
    large = max_exact + (jnp.log(d / max_exact) / math.log(MAX_DIST / max_exact)
                         * (N_BUCKETS - max_exact)).astype(jnp.int32)
    large = jnp.minimum(large, N_BUCKETS - 1)
    return jnp.where(dist < max_exact, dist, large).astype(jnp.int32)


def _bias_tables(rel_bias):
    i = jnp.arange(SPAN)[:, None]
    c = jnp.arange(2 * SPAN)[None, :]
    dist = i + SPAN - c
    valid = (dist >= 0) & (dist <= SPAN)

    def lookup(bias_g, idx):
        hot = (idx[..., None] == jnp.arange(N_BUCKETS)).astype(F32)
        return jnp.einsum("...k,kh->...h", hot, bias_g.astype(F32), precision=HI)

    prompt, cached, new = [], [], []
    for g, (win, dil) in enumerate(A_GROUPS):
        bias_g = rel_bias[:, g * A_HEADS:(g + 1) * A_HEADS]
        tbl = lookup(bias_g, _t5_bucket(jnp.maximum(dist, 0) * dil)).transpose(2, 0, 1)
        prompt.append(jnp.where(valid[None], tbl, NEG))
        w = jnp.arange(win)
        tbl_s = lookup(bias_g, _t5_bucket(win - w)).T
        cached.append(jnp.where((w % dil == 0)[None], tbl_s, NEG))
        new.append(lookup(bias_g, _t5_bucket(jnp.zeros((1,), jnp.int32))).T)
    return jnp.stack(prompt), cached, jnp.stack(new)


def _prep_w_in(w):
    offs = np.concatenate([[0], np.cumsum(IN_SIZES)])
    seg = lambda i: w[:, offs[i]:offs[i + 1]]
    z_order = (0, 1, 2, 8, 3, 4, 5, 7, 10, 11)
    used = sum(IN_SIZES[i] for i in z_order)
    w_z = jnp.concatenate([seg(i) for i in z_order] + [jnp.zeros((w.shape[0], NZ - used), w.dtype)], axis=1)
    w_zg = jnp.concatenate([seg(12), seg(6), seg(9)], axis=1)
    return w_z.astype(BF16), w_zg.astype(BF16)


def _selector(offset):
    sel = np.zeros((SMALL_W, C_HEADS * C_DV), np.float32)
    for h in range(C_HEADS):
        sel[offset + h, C_DV * h:C_DV * (h + 1)] = 1.0
    return jnp.asarray(sel)


def kernel(x_prompt, x_sample, cache_win128_kv, cache_win512_kv, cache_win2048_kv, state_gla, state_gdn, state_conv, norm_mix, w_in, a_q_norm, a_k_norm, rel_bias, b_w_lr, b_lr_bias, b_out_norm, c_conv, c_a_log, c_dt_bias, c_out_norm, w_branch, w_out, norm_ffn, ffn_w_gate, ffn_w_up, ffn_w_down, moe_router, moe_w_gate, moe_w_up, moe_w_down):
    bp, t, d = x_prompt.shape
    bs = x_sample.shape[0]
    mp = bp * t
    caches_t = [jnp.transpose(c, (0, 1, 3, 4, 5, 2)).reshape(c.shape[:2] + (2, A_WIDTH, c.shape[2]))
                for c in (cache_win128_kv, cache_win512_kv, cache_win2048_kv)]
    bias_p, bias_c, bias_n = _bias_tables(rel_bias)
    selb, sela = _selector(SM_CB), _selector(SM_CA)
    rolled = windows = None

    xp = x_prompt.reshape(mp, d)
    xs = x_sample.reshape(bs, d)
    outs_p = {k: [] for k in ("w0", "w1", "w2", "gla", "gdn", "conv")}
    outs_s = {k: [] for k in ("w0", "w1", "w2", "gla", "gdn", "conv")}
    for l in range(DEPTH):
        w_z, w_zg = _prep_w_in(w_in[l])
        wb = w_branch[l].astype(BF16)
        wo = w_out[l].astype(BF16)
        qg128 = jnp.tile(a_q_norm[l], 2).reshape(1, 2 * A_DH)
        kg128 = jnp.tile(a_k_norm[l], 2).reshape(1, 2 * A_DH)
        qg512 = jnp.tile(a_q_norm[l], A_HEADS).reshape(1, A_WIDTH)
        kg512 = jnp.tile(a_k_norm[l], A_HEADS).reshape(1, A_WIDTH)
        wlr_pad = jnp.zeros((SMALL_W, B_HEADS * B_DK), F32).at[:B_RANK].set(b_w_lr[l])
        lr_bias = b_lr_bias[l].reshape(1, -1)
        bon = jnp.tile(b_out_norm[l], B_HEADS).reshape(1, -1)
        con = jnp.tile(c_out_norm[l], C_HEADS).reshape(1, -1)
        dtb_bc = jnp.repeat(c_dt_bias[l], C_DV).reshape(1, -1)
        alog_bc = jnp.repeat(c_a_log[l], C_DV).reshape(1, -1)

        z = _norm_matmul(xp, norm_mix[l], w_z, tm=1024, tn=2560)
        zg = _norm_matmul(xp, norm_mix[l], w_zg, tm=1024, tn=2048, out_dtype=BF16)
        z3 = z.reshape(bp, t, NZ)
        o_a, *windows = _attn_prompt(z3, qg128, kg128, bias_p, l, windows)
        o_b, gla_t = _gla_prompt(z3, wlr_pad, lr_bias, tb=512)
        dtb_sm = jnp.zeros((1, SMALL_W), F32).at[0, SM_CA:SM_CA + C_HEADS].set(c_dt_bias[l])
        alog_sm = jnp.zeros((1, SMALL_W), F32).at[0, SM_CA:SM_CA + C_HEADS].set(c_a_log[l])
        o_c, gdn_s = _gdn_prompt(z3, c_conv[l], selb, sela, dtb_sm, alog_sm, tb=512)
        xp = _merge(zg, o_a.reshape(mp, -1), o_b.reshape(mp, -1), o_c.reshape(mp, -1), xp, bon, con, wb, wo, tm=512)
        gla = gla_t.reshape(bp, B_HEADS, B_DV, B_HEADS, B_DK)
        gla = jnp.stack([gla[:, h, :, h, :] for h in range(B_HEADS)], axis=1)
        outs_p["gla"].append(jnp.swapaxes(gla, 2, 3))
        outs_p["gdn"].append(gdn_s)
        outs_p["conv"].append(z3[:, t - (CONV_W - 1):, Z_CQKV:Z_CQKV + C_CONV_CH])

        zs = _norm_matmul(xs, norm_mix[l], w_z, tm=bs, tn=2560)
        zgs = _norm_matmul(xs, norm_mix[l], w_zg, tm=bs, tn=2048)
        zs3 = zs.reshape(bs, 1, NZ)
        oa_s, *rolled = _attn_sample(zs3, caches_t, rolled, l, qg512, kg512, bias_c, bias_n)
        ob_s, oc_s, nconv, ngla, ngdn = _sample_mix(
            zs3, state_conv[l], state_gla[l], state_gdn[l], wlr_pad, lr_bias, c_conv[l], dtb_bc, alog_bc)
        xs = _merge(zgs, oa_s.reshape(bs, -1), ob_s.reshape(bs, -1), oc_s.reshape(bs, -1), xs, bon, con, wb, wo, tm=bs)
        outs_s["gla"].append(ngla)
        outs_s["gdn"].append(ngdn)
        outs_s["conv"].append(nconv)

        i = l // 2
        if l % 2 == 0:
            wg, wu, wd = ffn_w_gate[i].astype(BF16), ffn_w_up[i].astype(BF16), ffn_w_down[i].astype(BF16)
            xp = _ffn(xp, norm_ffn[l], wg, wu, wd, tm=512, tf=1408)
            xs = _ffn(xs, norm_ffn[l], wg, wu, wd, tm=bs, tf=1408)
        else:
            wg, wu, wd = moe_w_gate[i].astype(BF16), moe_w_up[i].astype(BF16), moe_w_down[i].astype(BF16)
            router_pad = jnp.zeros((d, 128), F32).at[:, :N_EXPERTS].set(moe_router[i])
            xp = _moe_routed(xp, norm_ffn[l], moe_router[i].T, wg, wu, wd, tm=1024)
            xs = _moe(xs, norm_ffn[l], router_pad, wg, wu, wd, tm=bs, tf=896)

    st = lambda name, d_: jnp.stack(d_[name])

    def window_out(w):
        w = w.reshape(w.shape[:3] + (A_HEADS, A_DH, w.shape[-1]))
        return jnp.transpose(w, (0, 1, 5, 2, 3, 4))

    return (xp.reshape(bp, t, d), xs.reshape(bs, 1, d),
            window_out(windows[0]), window_out(rolled[0]), window_out(windows[1]), window_out(rolled[1]),
            window_out(windows[2]), window_out(rolled[2]),
            st("gla", outs_p), st("gla", outs_s), st("gdn", outs_p), st("gdn", outs_s),
            st("conv", outs_p), st("conv", outs_s))
```

```python
import functools
import math

import jax
import jax.numpy as jnp
import numpy as np
from jax import lax
from jax.experimental import pallas as pl
from jax.experimental.pallas import tpu as pltpu

F32 = jnp.float32
BF16 = jnp.bfloat16
HI = lax.Precision.HIGHEST
NEG = -1e30

D_MODEL = 1024
DEPTH = 2
EPS = 1e-6
A_GROUPS = ((128, 1), (512, 4), (2048, 16))
N_GROUPS = 3
A_HEADS = 8
A_DH = 64
A_WIDTH = A_HEADS * A_DH
SPAN = 128
ATT_PAIR = 16
GLA_PAIR = 8
GDN_PAIR = 8
N_BUCKETS = 32
MAX_DIST = 2048
B_HEADS = 4
B_DK = 64
B_DV = 128
B_RANK = 16
B_TAU = 16.0
C_HEADS = 4
C_DK = 128
C_DV = 128
CONV_W = 4
C_CONV_CH = 2 * C_HEADS * C_DK + C_HEADS * C_DV
CHUNK = 64
SUB = 16
BRANCH_W = 512
D_FF = 2816
N_EXPERTS = 8
D_FF_EXPERT = 1792
IN_SIZES = (N_GROUPS * A_WIDTH, N_GROUPS * A_WIDTH, N_GROUPS * A_WIDTH,
            B_HEADS * B_DK, B_HEADS * B_DK, B_HEADS * B_DV, B_HEADS * B_DV, B_RANK,
            C_CONV_CH, C_HEADS * C_DV, C_HEADS, C_HEADS, 3 * D_MODEL)

Z_AQ, Z_AK, Z_AV = 0, 1536, 3072
Z_CQKV = 4608
Z_BQ, Z_BK, Z_BV, Z_SMALL = 6144, 6400, 6656, 7168
SMALL_W = 256
NZ = 7680
ZG_GATES, ZG_BG, ZG_CZ = 0, 3072, 3584
NZG = 4096
SM_CB, SM_CA = B_RANK, B_RANK + C_HEADS

VMEM_LIMIT = 56 * 1024 * 1024


def _cparams(sem):
    return pltpu.CompilerParams(dimension_semantics=sem, vmem_limit_bytes=VMEM_LIMIT)


def _dot(a, b):
    return jnp.dot(a.astype(BF16), b.astype(BF16), preferred_element_type=F32)


def _dot_nt(a, b):
    return lax.dot_general(a.astype(BF16), b.astype(BF16), (((1,), (1,)), ((), ())), preferred_element_type=F32)


def _dot_tn(a, b):
    return lax.dot_general(a.astype(BF16), b.astype(BF16), (((0,), (0,)), ((), ())), preferred_element_type=F32)


def _split3(x):
    x1 = x.astype(BF16)
    r1 = x - x1.astype(F32)
    x2 = r1.astype(BF16)
    return x1, x2, (r1 - x2.astype(F32)).astype(BF16)


def _mask_dot(mask, x):
    m = mask.astype(BF16)
    x1, x2, x3 = _split3(x)
    return (jnp.dot(m, x1, preferred_element_type=F32) + jnp.dot(m, x2, preferred_element_type=F32)
            + jnp.dot(m, x3, preferred_element_type=F32))


def _dot_mask(x, mask):
    m = mask.astype(BF16)
    x1, x2, x3 = _split3(x)
    return (jnp.dot(x1, m, preferred_element_type=F32) + jnp.dot(x2, m, preferred_element_type=F32)
            + jnp.dot(x3, m, preferred_element_type=F32))


def _r16(x):
    return x.astype(BF16).astype(F32)


def _sigmoid(x):
    return jax.nn.sigmoid(x)


def _silu(x):
    return x * jax.nn.sigmoid(x)


def _softplus(x):
    return jnp.maximum(x, 0.0) + jnp.log1p(jnp.exp(-jnp.abs(x)))


def _log_sigmoid(x):
    return jnp.minimum(x, 0.0) - jnp.log1p(jnp.exp(-jnp.abs(x)))


def _rms(x, gain):
    return x * lax.rsqrt(jnp.mean(x * x, axis=-1, keepdims=True) + EPS) * gain


def _norm_matmul_kernel(x_ref, g_ref, w_ref, o_ref, h_ref):
    @pl.when(pl.program_id(1) == 0)
    def _():
        h_ref[...] = _rms(x_ref[...], g_ref[...]).astype(BF16)

    o_ref[...] = jnp.dot(h_ref[...], w_ref[...], preferred_element_type=F32).astype(o_ref.dtype)


def _norm_matmul(x, gain, w, tm, tn, out_dtype=F32):
    m, d = x.shape
    n = w.shape[1]
    return pl.pallas_call(
        _norm_matmul_kernel,
        grid=(m // tm, n // tn),
        in_specs=[pl.BlockSpec((tm, d), lambda i, j: (i, 0)),
                  pl.BlockSpec((1, d), lambda i, j: (0, 0)),
                  pl.BlockSpec((d, tn), lambda i, j: (0, j))],
        out_specs=pl.BlockSpec((tm, tn), lambda i, j: (i, j)),
        out_shape=jax.ShapeDtypeStruct((m, n), out_dtype),
        scratch_shapes=[pltpu.VMEM((tm, d), BF16)],
        compiler_params=_cparams(("parallel", "arbitrary")),
        name="norm_matmul",
    )(x, gain.reshape(1, d), w)


def _pair_rms(x, gain, head0):
    sq = x * x
    s0 = jnp.sum(jnp.where(head0, sq, 0.0), axis=-1, keepdims=True)
    s1 = jnp.sum(sq, axis=-1, keepdims=True) - s0
    return x * lax.rsqrt(jnp.where(head0, s0, s1) * (1.0 / A_DH) + EPS) * gain


def _attn_prompt_kernel(q1, k1, v1, q2, k2, v2, q3, k3, v3, qg_ref, kg_ref, bias_ref,
                        o_ref, w1, w2, w3, qt, kt, qs, ks, vs, og, lg, to, tl, *, slots):
    t_len = o_ref.shape[0]
    head0 = lax.broadcasted_iota(jnp.int32, (1, 2 * A_DH), 1) < A_DH
    first_half = lax.broadcasted_iota(jnp.int32, (1, 2 * SPAN), 1) < SPAN
    groups = ((q1, k1, v1, w1), (q2, k2, v2, w2), (q3, k3, v3, w3))
    for g, (qr, kr, vr, wr) in enumerate(groups):
        dil = A_GROUPS[g][1]
        length = t_len // dil
        nb = length // SPAN
        qt[...] = _pair_rms(qr[...], qg_ref[...], head0) * (A_DH ** -0.5)
        kt[...] = _pair_rms(kr[...], kg_ref[...], head0)
        wlen = wr.shape[-1]
        k_win = kt[t_len - wlen:, :].T
        v_win = vr[t_len - wlen:, :].T
        for s in range(slots):
            wr[s, 0] = k_win
            wr[s, 1] = v_win
        knr = kt
        for r in range(dil):
            rows = pl.ds(r, length, stride=dil) if dil > 1 else pl.ds(0, length)
            dst = pl.ds(r * length, length)
            qs[dst, :] = qt[rows, :].astype(BF16)
            ks[dst, :] = knr[rows, :].astype(BF16)
            vs[dst, :] = vr[rows, :].astype(BF16)

        if nb > 1:
            bias2 = jnp.concatenate([bias_ref[g, 0], bias_ref[g, 1]], axis=0)
        else:
            bias2 = jnp.concatenate([bias_ref[g, 0, :, SPAN:], bias_ref[g, 1, :, SPAN:]], axis=0)

        def block(jj, carry, nb=nb, bias2=bias2):
            row0s, k2s, v2s, logits = [], [], [], []
            for i in range(ATT_PAIR):
                j = jj * ATT_PAIR + i
                row0 = pl.multiple_of(j * SPAN, SPAN)
                qb = qs[pl.ds(row0, SPAN), :]
                qst = jnp.concatenate([jnp.where(head0, qb, jnp.zeros_like(qb)),
                                       jnp.where(head0, jnp.zeros_like(qb), qb)], axis=0)
                if nb > 1:
                    prow = pl.multiple_of(jnp.maximum(row0 - SPAN, 0), SPAN)
                    k2 = jnp.concatenate([ks[pl.ds(prow, SPAN), :], ks[pl.ds(row0, SPAN), :]], axis=0)
                    v2 = jnp.concatenate([vs[pl.ds(prow, SPAN), :], vs[pl.ds(row0, SPAN), :]], axis=0)
                else:
                    k2 = ks[pl.ds(row0, SPAN), :]
                    v2 = vs[pl.ds(row0, SPAN), :]
                lg_ = lax.dot_general(qst, k2, (((1,), (1,)), ((), ())), preferred_element_type=F32) + bias2
                if nb > 1:
                    lg_ = lg_ + jnp.where(first_half, jnp.where(j % nb == 0, NEG, 0.0), 0.0)
                row0s.append(row0)
                k2s.append(k2)
                v2s.append(v2)
                logits.append(lg_)
            lg_all = jnp.concatenate(logits, axis=0)
            m = jnp.max(lg_all, axis=-1, keepdims=True)
            p16 = jnp.exp(lg_all - m).astype(BF16)
            for i in range(ATT_PAIR):
                lo = 2 * SPAN * i
                v_ext = jnp.concatenate([v2s[i], jnp.ones_like(v2s[i])], axis=1)
                acc = jnp.dot(p16[lo:lo + 2 * SPAN], v_ext, preferred_element_type=F32)
                den = acc[:, 2 * A_DH:]
                out = acc[:, :2 * A_DH] / den
                lse = m[lo:lo + 2 * SPAN] + jnp.log(den)
                to[pl.ds(row0s[i], SPAN), :] = jnp.where(head0, out[:SPAN], out[SPAN:])
                tl[pl.ds(row0s[i], SPAN), :] = jnp.where(head0, lse[:SPAN], lse[SPAN:])
            return carry

        lax.fori_loop(0, t_len // (SPAN * ATT_PAIR), block, 0)
        for r in range(dil):
            rows = pl.ds(r, length, stride=dil) if dil > 1 else pl.ds(0, length)
            src = pl.ds(r * length, length)
            og[g, rows, :] = to[src, :]
            lg[g, rows, :] = tl[src, :]
    l_1, l_2, l_3 = lg[0], lg[1], lg[2]
    mx = jnp.maximum(jnp.maximum(l_1, l_2), l_3)
    e_1, e_2, e_3 = jnp.exp(l_1 - mx), jnp.exp(l_2 - mx), jnp.exp(l_3 - mx)
    o_ref[...] = (e_1 * og[0] + e_2 * og[1] + e_3 * og[2]) / (e_1 + e_2 + e_3)


def _attn_prompt(z3, qgain, kgain, bias, layer, windows):
    b, t, _ = z3.shape
    hp_blocks = A_WIDTH // 128

    def col(base, g):
        return lambda i, hp: (i, 0, base // 128 + g * hp_blocks + hp)

    in_specs = []
    for g in range(N_GROUPS):
        for base in (Z_AQ, Z_AK, Z_AV):
            in_specs.append(pl.BlockSpec((None, t, 128), col(base, g)))
    in_specs += [pl.BlockSpec((1, 128), lambda i, hp: (0, 0)),
                 pl.BlockSpec((1, 128), lambda i, hp: (0, 0)),
                 pl.BlockSpec((N_GROUPS, 2, SPAN, 2 * SPAN), lambda i, hp: (0, hp, 0, 0))]
    args = [z3] * 9 + [qgain, kgain, bias]
    slots = DEPTH if windows is None else 1
    win_specs, win_shapes = [], []
    for win, _ in A_GROUPS:
        wp = min(win, t)
        win_specs.append(pl.BlockSpec((slots, None, 2, 128, wp), lambda i, hp: (layer // slots, i, 0, hp, 0)))
        win_shapes.append(jax.ShapeDtypeStruct((DEPTH, b, 2, A_WIDTH, wp), F32))
    aliases = {}
    if windows is not None:
        aliases = {len(args) + g: 1 + g for g in range(N_GROUPS)}
        in_specs += [pl.BlockSpec(memory_space=pl.ANY)] * N_GROUPS
        args += list(windows)
    kern = functools.partial(_attn_prompt_kernel, slots=slots)
    if windows is not None:
        kern = functools.partial(_drop_refs, kern, len(args) - N_GROUPS, N_GROUPS)
    return pl.pallas_call(
        kern,
        grid=(b, hp_blocks),
        in_specs=in_specs,
        out_specs=[pl.BlockSpec((None, t, 128), lambda i, hp: (i, 0, hp))] + win_specs,
        out_shape=[jax.ShapeDtypeStruct((b, t, A_WIDTH), F32)] + win_shapes,
        scratch_shapes=[pltpu.VMEM((t, 128), F32), pltpu.VMEM((t, 128), F32),
                        pltpu.VMEM((t, 128), BF16), pltpu.VMEM((t, 128), BF16), pltpu.VMEM((t, 128), BF16),
                        pltpu.VMEM((N_GROUPS, t, 128), F32), pltpu.VMEM((N_GROUPS, t, 128), F32),
                        pltpu.VMEM((t, 128), F32), pltpu.VMEM((t, 128), F32)],
        input_output_aliases=aliases,
        compiler_params=_cparams(("parallel", "parallel")),
        name="attn_prompt",
    )(*args)


def _drop_refs(kern, start, count, *refs):
    return kern(*refs[:start], *refs[start + count:])


def _gla_prompt_kernel(q_ref, k_ref, v_ref, sm_ref, wlr_ref, lrb_ref, o_ref, st_ref, s_scr, la_scr):
    tb = q_ref.shape[0]
    hk = B_HEADS * B_DK
    hv = B_HEADS * B_DV

    @pl.when(pl.program_id(1) == 0)
    def _():
        s_scr[...] = jnp.zeros_like(s_scr)

    la_scr[...] = _log_sigmoid(_dot(sm_ref[...], wlr_ref[...]) + lrb_ref[...]) * (1.0 / B_TAU)

    rr = lax.broadcasted_iota(jnp.int32, (CHUNK, CHUNK), 0)
    cc = lax.broadcasted_iota(jnp.int32, (CHUNK, CHUNK), 1)
    causal = rr >= cc
    ltri = causal.astype(F32)
    lane_head = lax.broadcasted_iota(jnp.int32, (1, hk), 1) // B_DK
    row = lax.broadcasted_iota(jnp.int32, (CHUNK, 1), 0)
    same_head = (lax.broadcasted_iota(jnp.int32, (hv, hk), 0) // B_DV
                 == lax.broadcasted_iota(jnp.int32, (hv, hk), 1) // B_DK)

    def intra(r0):
        b = _mask_dot(ltri, la_scr[pl.ds(r0, CHUNK), :])
        q = q_ref[pl.ds(r0, CHUNK), :] * (B_DK ** -0.5)
        k = k_ref[pl.ds(r0, CHUNK), :]
        vb = v_ref[pl.ds(r0, CHUNK), :].astype(BF16)
        b_last = b[CHUNK - 1:CHUNK, :]
        atts = []
        for i in range(CHUNK // SUB):
            b_ref0 = b[SUB * i:SUB * i + 1, :]
            qi = q[SUB * i:SUB * (i + 1), :] * jnp.exp(b[SUB * i:SUB * (i + 1), :] - b_ref0)
            qst = jnp.concatenate([jnp.where(lane_head == h, qi, 0.0) for h in range(B_HEADS)], axis=0)
            kt = k * jnp.exp(jnp.where(row < SUB * (i + 1), b_ref0 - b, 0.0))
            atts.append(_dot_nt(qst, kt))
        o_parts = []
        for h in range(B_HEADS):
            att = jnp.concatenate([a[SUB * h:SUB * (h + 1), :] for a in atts], axis=0)
            att = jnp.where(causal, att, 0.0)
            o_parts.append(_dot(att, vb[:, B_DV * h:B_DV * (h + 1)]))
        upd = jnp.where(same_head, _dot_tn(vb, k * jnp.exp(b_last - b)), 0.0)
        return (q * jnp.exp(b)).astype(BF16), jnp.concatenate(o_parts, axis=1), jnp.exp(b_last), upd

    def chunks(cc, carry):
        r0s = [pl.multiple_of((cc * GLA_PAIR + i) * CHUNK, CHUNK) for i in range(GLA_PAIR)]
        parts = [intra(r0) for r0 in r0s]
        for r0, (qe, o_intra, decay, upd) in zip(r0s, parts):
            s = s_scr[...]
            o_ref[pl.ds(r0, CHUNK), :] = _dot_nt(qe, s) + o_intra
            s_scr[...] = s * decay + upd
        return carry

    lax.fori_loop(0, tb // (CHUNK * GLA_PAIR), chunks, 0)

    @pl.when(pl.program_id(1) == pl.num_programs(1) - 1)
    def _():
        st_ref[...] = s_scr[...]


def _gla_prompt(z3, wlr_pad, lr_bias, tb):
    b, t, _ = z3.shape
    hk, hv = B_HEADS * B_DK, B_HEADS * B_DV
    return pl.pallas_call(
        _gla_prompt_kernel,
        grid=(b, t // tb),
        in_specs=[pl.BlockSpec((None, tb, hk), lambda i, j: (i, j, Z_BQ // hk)),
                  pl.BlockSpec((None, tb, hk), lambda i, j: (i, j, Z_BK // hk)),
                  pl.BlockSpec((None, tb, hv), lambda i, j: (i, j, Z_BV // hv)),
                  pl.BlockSpec((None, tb, SMALL_W), lambda i, j: (i, j, Z_SMALL // SMALL_W)),
                  pl.BlockSpec((SMALL_W, hk), lambda i, j: (0, 0)),
                  pl.BlockSpec((1, hk), lambda i, j: (0, 0))],
        out_specs=[pl.BlockSpec((None, tb, hv), lambda i, j: (i, j, 0)),
                   pl.BlockSpec((None, hv, hk), lambda i, j: (i, 0, 0))],
        out_shape=[jax.ShapeDtypeStruct((b, t, hv), F32), jax.ShapeDtypeStruct((b, hv, hk), F32)],
        scratch_shapes=[pltpu.VMEM((hv, hk), F32), pltpu.VMEM((tb, hk), F32)],
        compiler_params=_cparams(("parallel", "arbitrary")),
        name="gla_prompt",
    )(z3, z3, z3, z3, wlr_pad, lr_bias)


def _gdn_prompt_kernel(x_ref, sm_ref, cw_ref, selb_ref, sela_ref, dtb_ref, alog_ref, o_ref, s_ref,
                       xpad, cv, gsc, bsc, s_scr):
    tb = x_ref.shape[0]
    hd = C_HEADS * C_DK
    pad = 8

    @pl.when(pl.program_id(1) == 0)
    def _():
        s_scr[...] = jnp.zeros_like(s_scr)
        xpad[0:pad, :] = jnp.zeros((pad, C_CONV_CH), F32)

    xpad[pad:pad + tb, :] = x_ref[...]
    rb = 128
    for i in range(tb // rb):
        acc = None
        for j in range(CONV_W):
            lo = pad - (CONV_W - 1) + j + i * rb
            term = xpad[lo:lo + rb, :] * cw_ref[j:j + 1, :]
            acc = term if acc is None else acc + term
        y = _silu(acc)
        rows = slice(i * rb, (i + 1) * rb)
        for h in range(C_HEADS):
            qh = y[:, C_DK * h:C_DK * (h + 1)]
            cv[rows, C_DK * h:C_DK * (h + 1)] = (
                qh * lax.rsqrt(jnp.sum(qh * qh, axis=-1, keepdims=True) + EPS) * (C_DK ** -0.5))
            kh = y[:, hd + C_DK * h:hd + C_DK * (h + 1)]
            cv[rows, hd + C_DK * h:hd + C_DK * (h + 1)] = (
                kh * lax.rsqrt(jnp.sum(kh * kh, axis=-1, keepdims=True) + EPS))
        cv[rows, 2 * hd:] = y[:, 2 * hd:]
    xpad[pad - (CONV_W - 1):pad, :] = xpad[pad + tb - (CONV_W - 1):pad + tb, :]

    sm = sm_ref[...]
    gsc[...] = _dot_mask(-jnp.exp(alog_ref[...]) * _softplus(sm + dtb_ref[...]), sela_ref[...])
    bsc[...] = _dot_mask(_sigmoid(sm), selb_ref[...])

    n = C_HEADS * CHUNK
    rr = lax.broadcasted_iota(jnp.int32, (CHUNK, CHUNK), 0)
    cc = lax.broadcasted_iota(jnp.int32, (CHUNK, CHUNK), 1)
    ltri = (rr >= cc).astype(F32)
    ri = lax.broadcasted_iota(jnp.int32, (n, n), 0)
    ci = lax.broadcasted_iota(jnp.int32, (n, n), 1)
    same = (ri // CHUNK) == (ci // CHUNK)
    incl = jnp.logical_and(same, ci <= ri)
    strict = jnp.logical_and(same, ci < ri)
    eye = (ri == ci).astype(F32)

    def stack(x):
        return jnp.concatenate([x[:, C_DK * h:C_DK * (h + 1)] for h in range(C_HEADS)], axis=0)

    def chunks(cc, carry):
        pair = range(GDN_PAIR)
        r0s = [pl.multiple_of((cc * GDN_PAIR + i) * CHUNK, CHUNK) for i in pair]
        bst = [stack(_mask_dot(ltri, gsc[pl.ds(r0, CHUNK), :])) for r0 in r0s]
        beta = [stack(bsc[pl.ds(r0, CHUNK), :]) for r0 in r0s]
        qst = [stack(cv[pl.ds(r0, CHUNK), 0:hd]) for r0 in r0s]
        kst = [stack(cv[pl.ds(r0, CHUNK), hd:2 * hd]) for r0 in r0s]
        vst = [stack(cv[pl.ds(r0, CHUNK), 2 * hd:3 * hd]) for r0 in r0s]
        kb = [k.astype(BF16) for k in kst]
        nt = (((1,), (1,)), ((), ()))
        kk = [lax.dot_general(kb[i], kb[i], nt, preferred_element_type=F32) for i in pair]
        qk = [lax.dot_general(qst[i].astype(BF16), kb[i], nt, preferred_element_type=F32) for i in pair]
        dec, a, aqk = [], [], []
        for i in pair:
            bst_t = bst[i].T
            d_ = jnp.exp(jnp.where(incl, jnp.concatenate([bst[i], bst[i]], axis=1)
                                   - jnp.concatenate([bst_t, bst_t], axis=0), NEG))
            dec.append(d_)
            a.append(jnp.where(strict, jnp.concatenate([beta[i], beta[i]], axis=1) * d_ * kk[i], 0.0))
            aqk.append(d_ * qk[i])
        inv = [eye - a[i] for i in pair]
        pw = a
        for _ in range(int(math.log2(CHUNK)) - 1):
            pw = [_dot(pw[i], pw[i]) for i in pair]
            inv = [inv[i] + _dot(inv[i], pw[i]) for i in pair]
        eb = [jnp.exp(bst[i]) for i in pair]
        wu = [_dot(inv[i], jnp.concatenate([beta[i] * eb[i] * kst[i], beta[i] * vst[i]], axis=1)) for i in pair]
        for i in pair:
            w, uv = wu[i][:, :C_DK], wu[i][:, C_DK:]
            us, qss = [], []
            for h in range(C_HEADS):
                sl = slice(CHUNK * h, CHUNK * (h + 1))
                ws = _dot(jnp.concatenate([w[sl], qst[i][sl]], axis=0), s_scr[h])
                us.append(uv[sl] - ws[:CHUNK])
                qss.append(ws[CHUNK:])
            o = eb[i] * jnp.concatenate(qss, axis=0) + _dot(aqk[i], jnp.concatenate(us, axis=0))
            for h in range(C_HEADS):
                sl = slice(CHUNK * h, CHUNK * (h + 1))
                o_ref[pl.ds(r0s[i], CHUNK), C_DV * h:C_DV * (h + 1)] = o[sl]
                b_last = bst[i][CHUNK * (h + 1) - 1:CHUNK * (h + 1), :]
                s_scr[h] = (jnp.exp(b_last) * s_scr[h]
                            + _dot_tn(kst[i][sl] * jnp.exp(b_last - bst[i][sl]), us[h]))
        return carry

    lax.fori_loop(0, tb // (CHUNK * GDN_PAIR), chunks, 0)

    @pl.when(pl.program_id(1) == pl.num_programs(1) - 1)
    def _():
        s_ref[...] = s_scr[...]


def _gdn_prompt(z3, conv_w, selb, sela, dtb_bc, alog_bc, tb):
    b, t, _ = z3.shape
    hd = C_HEADS * C_DV
    const = lambda shape: pl.BlockSpec(shape, lambda i, j: (0,) * len(shape))
    return pl.pallas_call(
        _gdn_prompt_kernel,
        grid=(b, t // tb),
        in_specs=[pl.BlockSpec((None, tb, C_CONV_CH), lambda i, j: (i, j, Z_CQKV // C_CONV_CH)),
                  pl.BlockSpec((None, tb, SMALL_W), lambda i, j: (i, j, Z_SMALL // SMALL_W)),
                  const((CONV_W, C_CONV_CH)), const((SMALL_W, hd)), const((SMALL_W, hd)),
                  const((1, SMALL_W)), const((1, SMALL_W))],
        out_specs=[pl.BlockSpec((None, tb, hd), lambda i, j: (i, j, 0)),
                   pl.BlockSpec((None, C_HEADS, C_DK, C_DV), lambda i, j: (i, 0, 0, 0))],
        out_shape=[jax.ShapeDtypeStruct((b, t, hd), F32),
                   jax.ShapeDtypeStruct((b, C_HEADS, C_DK, C_DV), F32)],
        scratch_shapes=[pltpu.VMEM((tb + 8, C_CONV_CH), F32), pltpu.VMEM((tb, C_CONV_CH), F32),
                        pltpu.VMEM((tb, hd), F32), pltpu.VMEM((tb, hd), F32),
                        pltpu.VMEM((C_HEADS, C_DK, C_DV), F32)],
        compiler_params=_cparams(("parallel", "arbitrary")),
        name="gdn_prompt",
    )(z3, z3, conv_w, selb, sela, dtb_bc, alog_bc)


def _head_norm(o, gain, width):
    parts = []
    for h in range(o.shape[1] // width):
        oh = o[:, width * h:width * (h + 1)]
        parts.append(oh * lax.rsqrt(jnp.mean(oh * oh, axis=-1, keepdims=True) + EPS))
    return jnp.concatenate(parts, axis=1) * gain


def _merge_kernel(oa_ref, ob_ref, bg_ref, oc_ref, cz_ref, ga_ref, gb_ref, gc_ref, x_ref,
                  bon_ref, con_ref, wb_ref, wo_ref, y_ref):
    f32 = lambda ref: ref[...].astype(F32)
    ob = _head_norm(ob_ref[...], bon_ref[...], B_DV) * _silu(f32(bg_ref))
    oc = _head_norm(oc_ref[...], con_ref[...], C_DV) * _silu(f32(cz_ref))
    merged = (_sigmoid(f32(ga_ref)) * _dot(oa_ref[...], wb_ref[0])
              + _sigmoid(f32(gb_ref)) * _dot(ob, wb_ref[1])
              + _sigmoid(f32(gc_ref)) * _dot(oc, wb_ref[2]))
    y_ref[...] = x_ref[...] + _dot(merged, wo_ref[...])


def _merge(zg, oa, ob, oc, x, bon, con, wb, wo, tm):
    m = x.shape[0]
    w = BRANCH_W
    row = lambda width, blk: pl.BlockSpec((tm, width), lambda i: (i, blk))
    const = lambda shape: pl.BlockSpec(shape, lambda i: (0,) * len(shape))
    return pl.pallas_call(
        _merge_kernel,
        grid=(m // tm,),
        in_specs=[row(w, 0), row(w, 0), row(w, ZG_BG // w), row(w, 0), row(w, ZG_CZ // w),
                  row(D_MODEL, ZG_GATES // D_MODEL), row(D_MODEL, ZG_GATES // D_MODEL + 1),
                  row(D_MODEL, ZG_GATES // D_MODEL + 2), row(D_MODEL, 0),
                  const((1, w)), const((1, w)), const((3, w, D_MODEL)), const((D_MODEL, D_MODEL))],
        out_specs=row(D_MODEL, 0),
        out_shape=jax.ShapeDtypeStruct((m, D_MODEL), F32),
        compiler_params=_cparams(("parallel",)),
        name="merge",
    )(oa, ob, zg, oc, zg, zg, zg, zg, x, bon, con, wb, wo)


def _ffn_kernel(x_ref, g_ref, wg_ref, wu_ref, wd_ref, y_ref, h_ref, acc_ref):
    f = pl.program_id(1)

    @pl.when(f == 0)
    def _():
        h_ref[...] = _rms(x_ref[...], g_ref[...]).astype(BF16)
        acc_ref[...] = jnp.zeros_like(acc_ref)

    h = h_ref[...]
    a = _silu(jnp.dot(h, wg_ref[...], preferred_element_type=F32)) * jnp.dot(h, wu_ref[...],
                                                                              preferred_element_type=F32)
    acc_ref[...] += jnp.dot(a.astype(BF16), wd_ref[...], preferred_element_type=F32)

    @pl.when(f == pl.num_programs(1) - 1)
    def _():
        y_ref[...] = x_ref[...] + acc_ref[...]


def _ffn(x, gain, wg, wu, wd, tm, tf):
    m, d = x.shape
    ff = wg.shape[1]
    return pl.pallas_call(
        _ffn_kernel,
        grid=(m // tm, ff // tf),
        in_specs=[pl.BlockSpec((tm, d), lambda i, f: (i, 0)),
                  pl.BlockSpec((1, d), lambda i, f: (0, 0)),
                  pl.BlockSpec((d, tf), lambda i, f: (0, f)),
                  pl.BlockSpec((d, tf), lambda i, f: (0, f)),
                  pl.BlockSpec((tf, d), lambda i, f: (f, 0))],
        out_specs=pl.BlockSpec((tm, d), lambda i, f: (i, 0)),
        out_shape=jax.ShapeDtypeStruct((m, d), F32),
        scratch_shapes=[pltpu.VMEM((tm, d), BF16), pltpu.VMEM((tm, d), F32)],
        compiler_params=_cparams(("parallel", "arbitrary")),
        name="ffn",
    )(x, gain.reshape(1, d), wg, wu, wd)


def _moe_kernel(x_ref, g_ref, rt_ref, wg_ref, wu_ref, wd_ref, y_ref, h_ref, acc_ref, gate_ref):
    e = pl.program_id(1)
    f = pl.program_id(2)
    lane = lax.broadcasted_iota(jnp.int32, (1, 128), 1).astype(F32)

    @pl.when(jnp.logical_and(e == 0, f == 0))
    def _():
        hf = _rms(x_ref[...], g_ref[...])
        h_ref[...] = hf.astype(BF16)
        acc_ref[...] = jnp.zeros_like(acc_ref)
        valid = lane < N_EXPERTS
        logits = jnp.where(valid, _dot(hf, rt_ref[...]), NEG)
        ex = jnp.exp(logits - jnp.max(logits, axis=-1, keepdims=True))
        probs = ex / jnp.sum(ex, axis=-1, keepdims=True)
        m1 = jnp.max(probs, axis=-1, keepdims=True)
        i1 = jnp.min(jnp.where(jnp.logical_and(probs == m1, valid), lane, 128.0), axis=-1, keepdims=True)
        hot1 = lane == i1
        rest = jnp.where(jnp.logical_or(hot1, jnp.logical_not(valid)), -1.0, probs)
        m2 = jnp.max(rest, axis=-1, keepdims=True)
        i2 = jnp.min(jnp.where(rest == m2, lane, 128.0), axis=-1, keepdims=True)
        hot2 = lane == i2
        den = m1 + m2
        gate_ref[...] = jnp.where(hot1, m1 / den, 0.0) + jnp.where(hot2, m2 / den, 0.0)

    h = h_ref[...]
    ge = jnp.sum(jnp.where(lane == e.astype(F32), gate_ref[...], 0.0), axis=-1, keepdims=True)
    a = (_silu(jnp.dot(h, wg_ref[...], preferred_element_type=F32))
         * jnp.dot(h, wu_ref[...], preferred_element_type=F32) * ge)
    acc_ref[...] += jnp.dot(a.astype(BF16), wd_ref[...], preferred_element_type=F32)

    @pl.when(jnp.logical_and(e == pl.num_programs(1) - 1, f == pl.num_programs(2) - 1))
    def _():
        y_ref[...] = x_ref[...] + acc_ref[...]


def _moe(x, gain, router_pad, wg, wu, wd, tm, tf):
    m, d = x.shape
    ne, _, ff = wg.shape
    return pl.pallas_call(
        _moe_kernel,
        grid=(m // tm, ne, ff // tf),
        in_specs=[pl.BlockSpec((tm, d), lambda i, e, f: (i, 0)),
                  pl.BlockSpec((1, d), lambda i, e, f: (0, 0)),
                  pl.BlockSpec((d, 128), lambda i, e, f: (0, 0)),
                  pl.BlockSpec((None, d, tf), lambda i, e, f: (e, 0, f)),
                  pl.BlockSpec((None, d, tf), lambda i, e, f: (e, 0, f)),
                  pl.BlockSpec((None, tf, d), lambda i, e, f: (e, f, 0))],
        out_specs=pl.BlockSpec((tm, d), lambda i, e, f: (i, 0)),
        out_shape=jax.ShapeDtypeStruct((m, d), F32),
        scratch_shapes=[pltpu.VMEM((tm, d), BF16), pltpu.VMEM((tm, d), F32), pltpu.VMEM((tm, 128), F32)],
        compiler_params=_cparams(("parallel", "arbitrary", "arbitrary")),
        name="moe",
    )(x, gain.reshape(1, d), router_pad, wg, wu, wd)


MOE_ROWS = 288


def _moe_routed_kernel(x_ref, g_ref, rt_ref, wg_ref, wu_ref, wd_ref, y_ref,
                       h_ref, gate_scr, mask_scr, rank_scr, tri_scr):
    e = pl.program_id(1)
    tm = x_ref.shape[0]
    sub = lax.broadcasted_iota(jnp.int32, (N_EXPERTS, 1), 0).astype(F32)

    @pl.when(e == 0)
    def _():
        x = x_ref[...]
        hf = _rms(x, g_ref[...])
        h_ref[...] = hf.astype(BF16)
        y_ref[...] = x
        logits = _dot_nt(rt_ref[...], hf)
        ex = jnp.exp(logits - jnp.max(logits, axis=0, keepdims=True))
        probs = ex / jnp.sum(ex, axis=0, keepdims=True)
        m1 = jnp.max(probs, axis=0, keepdims=True)
        hot1 = sub == jnp.min(jnp.where(probs == m1, sub, float(N_EXPERTS)), axis=0, keepdims=True)
        rest = jnp.where(hot1, -1.0, probs)
        m2 = jnp.max(rest, axis=0, keepdims=True)
        hot2 = sub == jnp.min(jnp.where(rest == m2, sub, float(N_EXPERTS)), axis=0, keepdims=True)
        den = m1 + m2
        gate_scr[...] = jnp.where(hot1, m1 / den, 0.0) + jnp.where(hot2, m2 / den, 0.0)
        mask = jnp.where(jnp.logical_or(hot1, hot2), 1.0, 0.0)
        mask_scr[...] = mask
        tri_scr[...] = jnp.where(lax.broadcasted_iota(jnp.int32, (tm, tm), 0)
                                 <= lax.broadcasted_iota(jnp.int32, (tm, tm), 1), 1.0, 0.0).astype(BF16)
        rank_scr[...] = jnp.dot(mask.astype(BF16), tri_scr[...], preferred_element_type=F32) - 1.0

    mine = sub == e.astype(F32)
    row = lambda ref: jnp.sum(jnp.where(mine, ref[...], 0.0), axis=0, keepdims=True)
    gate_row, mask_row, rank_row = row(gate_scr), row(mask_scr), row(rank_scr)
    count = jnp.sum(mask_row).astype(jnp.int32)
    slot = lax.broadcasted_iota(jnp.int32, (MOE_ROWS, 1), 0).astype(F32)

    def chunk(j, carry):
        base = (j * MOE_ROWS).astype(F32)
        pick = jnp.where(jnp.logical_and(rank_row == slot + base, mask_row > 0.0), 1.0, 0.0)
        pick16 = pick.astype(BF16)
        xg = jnp.dot(pick16, h_ref[...], preferred_element_type=F32).astype(BF16)
        gate_col = jnp.sum(pick * gate_row, axis=1, keepdims=True)
        a = (_silu(jnp.dot(xg, wg_ref[...], preferred_element_type=F32))
             * jnp.dot(xg, wu_ref[...], preferred_element_type=F32) * gate_col)
        out = jnp.dot(a.astype(BF16), wd_ref[...], preferred_element_type=F32)
        y_ref[...] += _dot_tn(pick16, out)
        return carry

    lax.fori_loop(0, (count + MOE_ROWS - 1) // MOE_ROWS, chunk, 0)


def _moe_routed(x, gain, router_t, wg, wu, wd, tm):
    m, d = x.shape
    ne, _, ff = wg.shape
    return pl.pallas_call(
        _moe_routed_kernel,
        grid=(m // tm, ne),
        in_specs=[pl.BlockSpec((tm, d), lambda i, e: (i, 0)),
                  pl.BlockSpec((1, d), lambda i, e: (0, 0)),
                  pl.BlockSpec((ne, d), lambda i, e: (0, 0)),
                  pl.BlockSpec((None, d, ff), lambda i, e: (e, 0, 0)),
                  pl.BlockSpec((None, d, ff), lambda i, e: (e, 0, 0)),
                  pl.BlockSpec((None, ff, d), lambda i, e: (e, 0, 0))],
        out_specs=pl.BlockSpec((tm, d), lambda i, e: (i, 0)),
        out_shape=jax.ShapeDtypeStruct((m, d), F32),
        scratch_shapes=[pltpu.VMEM((tm, d), BF16), pltpu.VMEM((ne, tm), F32), pltpu.VMEM((ne, tm), F32),
                        pltpu.VMEM((ne, tm), F32), pltpu.VMEM((tm, tm), BF16)],
        compiler_params=_cparams(("parallel", "arbitrary")),
        name="moe_routed",
    )(x, gain.reshape(1, d), router_t, wg, wu, wd)


def _attn_sample_kernel(*refs, layer, full_roll):
    zq_ref, qg_ref, kg_ref, b0_ref, b1_ref, b2_ref, bias0_ref, c0_ref, c1_ref, c2_ref = refs[:10]
    if full_roll:
        tails = (None,) * N_GROUPS
        rest = refs[10:]
    else:
        tails = refs[10:13]
        rest = refs[13:]
    o_ref, n0_ref, n1_ref, n2_ref, p0_scr, p1_scr, p2_scr, st_scr, oacc_scr = rest
    half = pl.program_id(1)
    active = (pl.program_id(2) == layer) if full_roll else True
    aw = A_WIDTH
    sel = (lax.broadcasted_iota(jnp.int32, (A_HEADS, aw), 1) // A_DH
           == lax.broadcasted_iota(jnp.int32, (A_HEADS, aw), 0))
    last_lane = lax.broadcasted_iota(jnp.int32, (1, 128), 1) == 127
    groups = ((c0_ref, n0_ref, b0_ref, p0_scr, tails[0]), (c1_ref, n1_ref, b1_ref, p1_scr, tails[1]),
              (c2_ref, n2_ref, b2_ref, p2_scr, tails[2]))
    for g, (c_ref, n_ref, b_ref, p_scr, t_ref) in enumerate(groups):
        wlen = c_ref.shape[1]
        if full_roll:
            n_ref[...] = pltpu.roll(c_ref[...], wlen - 1, axis=1)

        def newest(col, n_ref=n_ref, t_ref=t_ref, wlen=wlen):
            old = n_ref[:, wlen - 128:] if full_roll else t_ref[...]
            dst = n_ref.at[:, wlen - 128:] if full_roll else n_ref
            dst[...] = jnp.where(last_lane, col, old)

        lo = g * aw

        @pl.when(jnp.logical_and(active, half == 0))
        def _(g=g, c_ref=c_ref, b_ref=b_ref, p_scr=p_scr, lo=lo, newest=newest):
            q = zq_ref[:, Z_AQ + lo:Z_AQ + lo + aw]
            k = zq_ref[:, Z_AK + lo:Z_AK + lo + aw]
            def head_rms(x, gain):
                s8 = jnp.sum(jnp.where(sel, jnp.broadcast_to(x * x, (A_HEADS, aw)), 0.0), axis=1, keepdims=True)
                ms = jnp.sum(jnp.where(sel, s8, 0.0), axis=0, keepdims=True) * (1.0 / A_DH)
                return x * lax.rsqrt(ms + EPS) * gain

            qn = head_rms(q, qg_ref[...]) * (A_DH ** -0.5)
            kn = head_rms(k, kg_ref[...])
            qbd = jnp.where(sel, jnp.broadcast_to(qn, (A_HEADS, aw)), 0.0)
            lc = jnp.sum((c_ref[...] * _to_col(qn, aw)).reshape(A_HEADS, A_DH, c_ref.shape[1]), axis=1) + b_ref[...]
            l0 = jnp.sum(qbd * kn, axis=1, keepdims=True) + bias0_ref[g]
            m = jnp.maximum(jnp.max(lc, axis=1, keepdims=True), l0)
            pc = jnp.exp(lc - m)
            p0 = jnp.exp(l0 - m)
            p_scr[...] = pc
            st_scr[g, 0] = jnp.broadcast_to(p0, (A_HEADS, 128))
            st_scr[g, 1] = jnp.broadcast_to(jnp.sum(pc, axis=1, keepdims=True) + p0, (A_HEADS, 128))
            st_scr[g, 2] = jnp.broadcast_to(m, (A_HEADS, 128))
            newest(_to_col(kn, aw))

        @pl.when(jnp.logical_and(active, half == 1))
        def _(g=g, c_ref=c_ref, p_scr=p_scr, lo=lo, newest=newest):
            v = zq_ref[:, Z_AV + lo:Z_AV + lo + aw]
            p0 = st_scr[g, 0][:, 0:1]
            s = st_scr[g, 1][:, 0:1]
            m = st_scr[g, 2][:, 0:1]
            wl = c_ref.shape[1]
            p_rep = jnp.broadcast_to(_r16(p_scr[...])[:, None, :], (A_HEADS, A_DH, wl)).reshape(aw, wl)
            pv = _to_row(jnp.sum(p_rep * _r16(c_ref[...]), axis=1, keepdims=True), aw)
            on_lanes = lambda x8: jnp.sum(jnp.where(sel, x8, 0.0), axis=0, keepdims=True)
            oacc_scr[g, 0:1, :] = (pv + on_lanes(_r16(p0)) * _r16(v)) / on_lanes(s)
            oacc_scr[g, 1:2, :] = on_lanes(m + jnp.log(s))
            newest(_to_col(v, aw))

    @pl.when(jnp.logical_and(active, half == 1))
    def _():
        l_1, l_2, l_3 = oacc_scr[0, 1:2, :], oacc_scr[1, 1:2, :], oacc_scr[2, 1:2, :]
        mx = jnp.maximum(jnp.maximum(l_1, l_2), l_3)
        e_1, e_2, e_3 = jnp.exp(l_1 - mx), jnp.exp(l_2 - mx), jnp.exp(l_3 - mx)
        o_ref[...] = ((e_1 * oacc_scr[0, 0:1, :] + e_2 * oacc_scr[1, 0:1, :] + e_3 * oacc_scr[2, 0:1, :])
                      / (e_1 + e_2 + e_3))


def _attn_sample(zs3, caches_t, rolled, layer, qgain, kgain, biases, bias0):
    bd = zs3.shape[0]
    full_roll = rolled is None
    aw = A_WIDTH
    if full_roll:
        grid = (bd, 2, DEPTH)
        const = lambda shape: pl.BlockSpec(shape, lambda i, h, l: (0,) * len(shape))
        zspec = pl.BlockSpec((None, 1, Z_AV + N_GROUPS * aw), lambda i, h, l: (i, 0, 0))
        ospec = pl.BlockSpec((None, 1, aw), lambda i, h, l: (i, 0, 0))
        cache_specs = [pl.BlockSpec((None, None, None, aw, c.shape[-1]), lambda i, h, l: (l, i, h, 0, 0))
                       for c in caches_t]
        new_specs = cache_specs
        extra_specs, extra_args, aliases = [], [], {}
        sem = ("parallel", "arbitrary", "arbitrary")
    else:
        grid = (bd, 2)
        const = lambda shape: pl.BlockSpec(shape, lambda i, h: (0,) * len(shape))
        zspec = pl.BlockSpec((None, 1, Z_AV + N_GROUPS * aw), lambda i, h: (i, 0, 0))
        ospec = pl.BlockSpec((None, 1, aw), lambda i, h: (i, 0, 0))
        cache_specs = [pl.BlockSpec((None, None, None, aw, c.shape[-1]), lambda i, h: (layer, i, h, 0, 0))
                       for c in caches_t]
        new_specs = [pl.BlockSpec((None, None, None, aw, 128), functools.partial(
            lambda last, i, h: (layer, i, h, 0, last), c.shape[-1] // 128 - 1)) for c in caches_t]
        extra_specs, extra_args = new_specs, list(rolled)
        aliases = {10 + g: 1 + g for g in range(N_GROUPS)}
        sem = ("parallel", "arbitrary")
    return pl.pallas_call(
        functools.partial(_attn_sample_kernel, layer=layer, full_roll=full_roll),
        grid=grid,
        in_specs=[zspec, const((1, aw)), const((1, aw))]
                 + [const((A_HEADS, c.shape[-1])) for c in caches_t] + [const((N_GROUPS, A_HEADS, 1))]
                 + cache_specs + extra_specs,
        out_specs=[ospec] + new_specs,
        out_shape=[jax.ShapeDtypeStruct((bd, 1, aw), F32)]
                  + [jax.ShapeDtypeStruct(c.shape, c.dtype) for c in caches_t],
        scratch_shapes=[pltpu.VMEM((A_HEADS, c.shape[-1]), F32) for c in caches_t]
                       + [pltpu.VMEM((N_GROUPS, 3, A_HEADS, 128), F32), pltpu.VMEM((N_GROUPS, 8, aw), F32)],
        input_output_aliases=aliases,
        compiler_params=_cparams(sem),
        name="attn_sample",
    )(zs3, qgain, kgain, *biases, bias0, *caches_t, *extra_args)


def _to_row(col, n):
    eye = lax.broadcasted_iota(jnp.int32, (n, n), 0) == lax.broadcasted_iota(jnp.int32, (n, n), 1)
    return jnp.sum(jnp.where(eye, jnp.broadcast_to(col, (n, n)), 0.0), axis=0, keepdims=True)


def _to_col(row, n):
    eye = lax.broadcasted_iota(jnp.int32, (n, n), 0) == lax.broadcasted_iota(jnp.int32, (n, n), 1)
    return jnp.sum(jnp.where(eye, jnp.broadcast_to(row, (n, n)), 0.0), axis=1, keepdims=True)


def _sample_mix_kernel(bq_ref, bk_ref, bv_ref, cqkv_ref, sm_ref, cbuf_ref, sgla_ref, sgdn_ref,
                       wlr_ref, lrb_ref, cw_ref, dtb_ref, alog_ref,
                       ob_ref, oc_ref, nconv_ref, ngla_ref, ngdn_ref):
    sm = sm_ref[...]
    la =_log_sigmoid(_dot(sm, wlr_ref[...]) + lrb_ref[...]) * (1.0 / B_TAU)
    q = bq_ref[...] * (B_DK ** -0.5)
    k = bk_ref[...]
    v = bv_ref[...]
    for h in range(B_HEADS):
        lk = slice(B_DK * h, B_DK * (h + 1))
        lv = slice(B_DV * h, B_DV * (h + 1))
        s_old = sgla_ref[h]
        decay = jnp.exp(la[:, lk])
        ngla_ref[h] = _to_col(decay, B_DK) * s_old + _to_col(k[:, lk], B_DK) * v[:, lv]
        att = jnp.sum(q[:, lk] * k[:, lk], axis=-1, keepdims=True)
        ob_ref[:, lv] = (jnp.sum(_to_col(q[:, lk] * decay, B_DK) * s_old, axis=0, keepdims=True)
                         + att * v[:, lv])

    cat = jnp.concatenate([cbuf_ref[...], cqkv_ref[...]], axis=0)
    acc = None
    for j in range(CONV_W):
        term = cat[j:j + 1, :] * cw_ref[j:j + 1, :]
        acc = term if acc is None else acc + term
    y = _silu(acc)
    nconv_ref[...] = cat[1:CONV_W, :]
    hd = C_HEADS * C_DK
    for h in range(C_HEADS):
        ld = slice(C_DK * h, C_DK * (h + 1))
        qh = y[:, C_DK * h:C_DK * (h + 1)]
        qh = qh * lax.rsqrt(jnp.sum(qh * qh, axis=-1, keepdims=True) + EPS) * (C_DK ** -0.5)
        kh = y[:, hd + C_DK * h:hd + C_DK * (h + 1)]
        kh = kh * lax.rsqrt(jnp.sum(kh * kh, axis=-1, keepdims=True) + EPS)
        vh = y[:, 2 * hd + C_DV * h:2 * hd + C_DV * (h + 1)]
        beta = _sigmoid(sm[:, SM_CB + h:SM_CB + h + 1])
        g = -jnp.exp(alog_ref[:, ld]) * _softplus(sm[:, SM_CA + h:SM_CA + h + 1] + dtb_ref[:, ld])
        eg = jnp.exp(g)
        s = sgdn_ref[h]
        kcol = _to_col(kh, C_DK)
        u = beta * (vh - eg * jnp.sum(kcol * s, axis=0, keepdims=True))
        qs = jnp.sum(_to_col(qh, C_DK) * s, axis=0, keepdims=True)
        oc_ref[:, ld] = eg * qs + jnp.sum(qh * kh, axis=-1, keepdims=True) * u
        ngdn_ref[h] = eg * s + kcol * u


def _sample_mix(zs3, cbuf, sgla, sgdn, wlr_pad, lr_bias, conv_w, dtb_bc, alog_bc):
    bd = zs3.shape[0]
    hk, hv, hd = B_HEADS * B_DK, B_HEADS * B_DV, C_HEADS * C_DV
    const = lambda shape: pl.BlockSpec(shape, lambda i: (0,) * len(shape))
    zrow = lambda width, off: pl.BlockSpec((None, 1, width), lambda i: (i, 0, off // width))
    vec = lambda width: pl.BlockSpec((None, 1, width), lambda i: (i, 0, 0))
    st = lambda shape: pl.BlockSpec((None,) + shape, lambda i: (i,) + (0,) * len(shape))
    return pl.pallas_call(
        _sample_mix_kernel,
        grid=(bd,),
        in_specs=[zrow(hk, Z_BQ), zrow(hk, Z_BK), zrow(hv, Z_BV), zrow(C_CONV_CH, Z_CQKV), zrow(SMALL_W, Z_SMALL),
                  st((CONV_W - 1, C_CONV_CH)), st((B_HEADS, B_DK, B_DV)), st((C_HEADS, C_DK, C_DV))]
                 + [const((SMALL_W, hk)), const((1, hk)), const((CONV_W, C_CONV_CH)), const((1, hd)), const((1, hd))],
        out_specs=[vec(hv), vec(hd), st((CONV_W - 1, C_CONV_CH)),
                   st((B_HEADS, B_DK, B_DV)), st((C_HEADS, C_DK, C_DV))],
        out_shape=[jax.ShapeDtypeStruct((bd, 1, hv), F32),
                   jax.ShapeDtypeStruct((bd, 1, hd), F32), jax.ShapeDtypeStruct((bd, CONV_W - 1, C_CONV_CH), F32),
                   jax.ShapeDtypeStruct((bd, B_HEADS, B_DK, B_DV), F32),
                   jax.ShapeDtypeStruct((bd, C_HEADS, C_DK, C_DV), F32)],
        compiler_params=_cparams(("parallel",)),
        name="sample_mix",
    )(zs3, zs3, zs3, zs3, zs3, cbuf, sgla, sgdn, wlr_pad, lr_bias, conv_w, dtb_bc, alog_bc)


def _t5_bucket(dist):
    max_exact = N_BUCKETS // 2
    d = jnp.maximum(dist.astype(F32), 1.0)
    large = max_exact + (jnp.log(d / max_exact) / math.log(MAX_DIST / max_exact)
                         * (N_BUCKETS - max_exact)).astype(jnp.int32)
    large = jnp.minimum(large, N_BUCKETS - 1)
    return jnp.where(dist < max_exact, dist, large).astype(jnp.int32)


def _bias_tables(rel_bias):
    i = jnp.arange(SPAN)[:, None]
    c = jnp.arange(2 * SPAN)[None, :]
    dist = i + SPAN - c
    valid = (dist >= 0) & (dist <= SPAN)

    def lookup(bias_g, idx):
        hot = (idx[..., None] == jnp.arange(N_BUCKETS)).astype(F32)
        return jnp.einsum("...k,kh->...h", hot, bias_g.astype(F32), precision=HI)

    prompt, cached, new = [], [], []
    for g, (win, dil) in enumerate(A_GROUPS):
        bias_g = rel_bias[:, g * A_HEADS:(g + 1) * A_HEADS]
        tbl = lookup(bias_g, _t5_bucket(jnp.maximum(dist, 0) * dil)).transpose(2, 0, 1)
        prompt.append(jnp.where(valid[None], tbl, NEG))
        w = jnp.arange(win)
        tbl_s = lookup(bias_g, _t5_bucket(win - w)).T
        cached.append(jnp.where((w % dil == 0)[None], tbl_s, NEG))
        new.append(lookup(bias_g, _t5_bucket(jnp.zeros((1,), jnp.int32))).T)
    return jnp.stack(prompt), cached, jnp.stack(new)


def _prep_w_in(w):
    offs = np.concatenate([[0], np.cumsum(IN_SIZES)])
    seg = lambda i: w[:, offs[i]:offs[i + 1]]
    z_order = (0, 1, 2, 8, 3, 4, 5, 7, 10, 11)
    used = sum(IN_SIZES[i] for i in z_order)
    w_z = jnp.concatenate([seg(i) for i in z_order] + [jnp.zeros((w.shape[0], NZ - used), w.dtype)], axis=1)
    w_zg = jnp.concatenate([seg(12), seg(6), seg(9)], axis=1)
    return w_z.astype(BF16), w_zg.astype(BF16)


def _selector(offset):
    sel = np.zeros((SMALL_W, C_HEADS * C_DV), np.float32)
    for h in range(C_HEADS):
        sel[offset + h, C_DV * h:C_DV * (h + 1)] = 1.0
    return jnp.asarray(sel)


def kernel(x_prompt, x_sample, cache_win128_kv, cache_win512_kv, cache_win2048_kv, state_gla, state_gdn, state_conv, norm_mix, w_in, a_q_norm, a_k_norm, rel_bias, b_w_lr, b_lr_bias, b_out_norm, c_conv, c_a_log, c_dt_bias, c_out_norm, w_branch, w_out, norm_ffn, ffn_w_gate, ffn_w_up, ffn_w_down, moe_router, moe_w_gate, moe_w_up, moe_w_down):
    bp, t, d = x_prompt.shape
    bs = x_sample.shape[0]
    mp = bp * t
    caches_t = [jnp.transpose(c, (0, 1, 3, 4, 5, 2)).reshape(c.shape[:2] + (2, A_WIDTH, c.shape[2]))
                for c in (cache_win128_kv, cache_win512_kv, cache_win2048_kv)]
    bias_p, bias_c, bias_n = _bias_tables(rel_bias)
    selb, sela = _selector(SM_CB), _selector(SM_CA)
    rolled = windows = None

    xp = x_prompt.reshape(mp, d)
    xs = x_sample.reshape(bs, d)
    outs_p = {k: [] for k in ("w0", "w1", "w2", "gla", "gdn", "conv")}
    outs_s = {k: [] for k in ("w0", "w1", "w2", "gla", "gdn", "conv")}
    for l in range(DEPTH):
        w_z, w_zg = _prep_w_in(w_in[l])
        wb = w_branch[l].astype(BF16)
        wo = w_out[l].astype(BF16)
        qg128 = jnp.tile(a_q_norm[l], 2).reshape(1, 2 * A_DH)
        kg128 = jnp.tile(a_k_norm[l], 2).reshape(1, 2 * A_DH)
        qg512 = jnp.tile(a_q_norm[l], A_HEADS).reshape(1, A_WIDTH)
        kg512 = jnp.tile(a_k_norm[l], A_HEADS).reshape(1, A_WIDTH)
        wlr_pad = jnp.zeros((SMALL_W, B_HEADS * B_DK), F32).at[:B_RANK].set(b_w_lr[l])
        lr_bias = b_lr_bias[l].reshape(1, -1)
        bon = jnp.tile(b_out_norm[l], B_HEADS).reshape(1, -1)
        con = jnp.tile(c_out_norm[l], C_HEADS).reshape(1, -1)
        dtb_bc = jnp.repeat(c_dt_bias[l], C_DV).reshape(1, -1)
        alog_bc = jnp.repeat(c_a_log[l], C_DV).reshape(1, -1)

        z = _norm_matmul(xp, norm_mix[l], w_z, tm=1024, tn=2560)
        zg = _norm_matmul(xp, norm_mix[l], w_zg, tm=1024, tn=2048, out_dtype=BF16)
        z3 = z.reshape(bp, t, NZ)
        o_a, *windows = _attn_prompt(z3, qg128, kg128, bias_p, l, windows)
        o_b, gla_t = _gla_prompt(z3, wlr_pad, lr_bias, tb=512)
        dtb_sm = jnp.zeros((1, SMALL_W), F32).at[0, SM_CA:SM_CA + C_HEADS].set(c_dt_bias[l])
        alog_sm = jnp.zeros((1, SMALL_W), F32).at[0, SM_CA:SM_CA + C_HEADS].set(c_a_log[l])
        o_c, gdn_s = _gdn_prompt(z3, c_conv[l], selb, sela, dtb_sm, alog_sm, tb=512)
        xp = _merge(zg, o_a.reshape(mp, -1), o_b.reshape(mp, -1), o_c.reshape(mp, -1), xp, bon, con, wb, wo, tm=512)
        gla = gla_t.reshape(bp, B_HEADS, B_DV, B_HEADS, B_DK)
        gla = jnp.stack([gla[:, h, :, h, :] for h in range(B_HEADS)], axis=1)
        outs_p["gla"].append(jnp.swapaxes(gla, 2, 3))
        outs_p["gdn"].append(gdn_s)
        outs_p["conv"].append(z3[:, t - (CONV_W - 1):, Z_CQKV:Z_CQKV + C_CONV_CH])

        zs = _norm_matmul(xs, norm_mix[l], w_z, tm=bs, tn=2560)
        zgs = _norm_matmul(xs, norm_mix[l], w_zg, tm=bs, tn=2048)
        zs3 = zs.reshape(bs, 1, NZ)
        oa_s, *rolled = _attn_sample(zs3, caches_t, rolled, l, qg512, kg512, bias_c, bias_n)
        ob_s, oc_s, nconv, ngla, ngdn = _sample_mix(
            zs3, state_conv[l], state_gla[l], state_gdn[l], wlr_pad, lr_bias, c_conv[l], dtb_bc, alog_bc)
        xs = _merge(zgs, oa_s.reshape(bs, -1), ob_s.reshape(bs, -1), oc_s.reshape(bs, -1), xs, bon, con, wb, wo, tm=bs)
        outs_s["gla"].append(ngla)
        outs_s["gdn"].append(ngdn)
        outs_s["conv"].append(nconv)

        i = l // 2
        if l % 2 == 0:
            wg, wu, wd = ffn_w_gate[i].astype(BF16), ffn_w_up[i].astype(BF16), ffn_w_down[i].astype(BF16)
            xp = _ffn(xp, norm_ffn[l], wg, wu, wd, tm=512, tf=1408)
            xs = _ffn(xs, norm_ffn[l], wg, wu, wd, tm=bs, tf=1408)
        else:
            wg, wu, wd = moe_w_gate[i].astype(BF16), moe_w_up[i].astype(BF16), moe_w_down[i].astype(BF16)
            router_pad = jnp.zeros((d, 128), F32).at[:, :N_EXPERTS].set(moe_router[i])
            xp = _moe_routed(xp, norm_ffn[l], moe_router[i].T, wg, wu, wd, tm=1024)
            xs = _moe(xs, norm_ffn[l], router_pad, wg, wu, wd, tm=bs, tf=896)

    st = lambda name, d_: jnp.stack(d_[name])

    def window_out(w):
        w = w.reshape(w.shape[:3] + (A_HEADS, A_DH, w.shape[-1]))
        return jnp.transpose(w, (0, 1, 5, 2, 3, 4))

    return (xp.reshape(bp, t, d), xs.reshape(bs, 1, d),
            window_out(windows[0]), window_out(rolled[0]), window_out(windows[1]), window_out(rolled[1]),
            window_out(windows[2]), window_out(rolled[2]),
            st("gla", outs_p), st("gla", outs_s), st("gdn", outs_p), st("gdn", outs_s),
            st("conv", outs_p), st("conv", outs_s))
```
